```python
import jax, jax.numpy as jnp
from jax import lax
import numpy as np

D_MODEL = 2048
BATCH = 4
SEQ = 2048
DEPTH = 2
DEC_BATCH = 128
DEC_SEQ = 1
PAST_LEN = 16384
PAGE_SIZE = 128

MIX_WIDTH = D_MODEL
N_MIXERS = 4
GROUP_WIDTH = MIX_WIDTH // N_MIXERS
POOL_WINDOWS = (2, 4, 8, 16)
POOL_GROUPS = len(POOL_WINDOWS)
POOL_CH = GROUP_WIDTH // POOL_GROUPS
POOL_BUF = max(POOL_WINDOWS) - 1
CONV_WIDTH = 3
RET_HEADS = 4
RET_HEAD_DIM = GROUP_WIDTH // RET_HEADS
RET_CHUNK = 128
ROPE_BASE = 10000.0
SGU_HEADS = 4
SGU_CH = GROUP_WIDTH // SGU_HEADS
SGU_CHUNK = 128
D_FF = ((8 * D_MODEL + 3 * 256 - 1) // (3 * 256)) * 256
IN_WIDTH = 10 * GROUP_WIDTH
NORM_EPS = 1e-6

kernel_name = 'hybrid_pool_conv_retention_sgu_step'


def rmsnorm(x, g):
    xf = x.astype(jnp.float32)
    y = xf * lax.rsqrt(jnp.mean(xf * xf, axis=-1, keepdims=True) + NORM_EPS)
    return (y * g.astype(jnp.float32)).astype(x.dtype)


def layernorm_f32(x):
    xf = x.astype(jnp.float32)
    mu = jnp.mean(xf, axis=-1, keepdims=True)
    xc = xf - mu
    return xc * lax.rsqrt(jnp.mean(xc * xc, axis=-1, keepdims=True) + NORM_EPS)


def pool_mixer(a, prev, start, w, scale):
    B, L, _ = a.shape
    ext = jnp.concatenate([prev.astype(a.dtype), a], axis=1).astype(jnp.float32)
    cs = jnp.pad(jnp.cumsum(ext, axis=1), ((0, 0), (1, 0), (0, 0)))
    end = cs[:, POOL_BUF + 1:]
    pos = start + jnp.arange(L)
    af = a.astype(jnp.float32)
    outs = []
    for gi, win in enumerate(POOL_WINDOWS):
        sl = slice(gi * POOL_CH, (gi + 1) * POOL_CH)
        lo = POOL_BUF + 1 - win
        wsum = end[..., sl] - cs[:, lo:lo + L, sl]
        cnt = jnp.minimum(pos + 1, win).astype(jnp.float32)[None, :, None]
        d = (wsum / cnt - af[..., sl]).astype(a.dtype)
        outs.append(jnp.einsum('blc,ce->ble', d, w[gi]))
    y = jnp.concatenate(outs, axis=-1) * scale
    return y, ext[:, -POOL_BUF:].astype(a.dtype)


def conv_mixer(bg, cg, hh, prev, w):
    z = cg * hh
    L = z.shape[1]
    ext = jnp.concatenate([prev.astype(z.dtype), z], axis=1)
    acc = ext[:, 0:L] * w[0]
    for kk in range(1, CONV_WIDTH):
        acc = acc + ext[:, kk:kk + L] * w[kk]
    return bg * acc, ext[:, -(CONV_WIDTH - 1):]


def rotary(x, pos):
    half = x.shape[-1] // 2
    inv = ROPE_BASE ** (-jnp.arange(half, dtype=jnp.float32) / half)
    ang = pos.astype(jnp.float32)[:, None] * inv[None, :]
    cos = jnp.cos(ang)[None, :, None, :]
    sin = jnp.sin(ang)[None, :, None, :]
    x1, x2 = x[..., :half], x[..., half:]
    return jnp.concatenate([x1 * cos - x2 * sin, x2 * cos + x1 * sin], axis=-1)


def retention(q, k, v, S0, start):
    B, L, _ = q.shape
    H, d = RET_HEADS, RET_HEAD_DIM
    pos = start + jnp.arange(L)
    f = lambda t: t.astype(jnp.float32).reshape(B, L, H, d)
    q = rotary(f(q), pos)
    k = rotary(f(k), pos) * (d ** -0.5)
    v = f(v)
    c = RET_CHUNK if L % RET_CHUNK == 0 else L
    n = L // c
    chunks = lambda t: t.reshape(B, n, c, H, d).transpose(1, 0, 3, 2, 4)
    lg = jnp.log1p(-(2.0 ** (-5.0 - jnp.arange(H, dtype=jnp.float32))))
    idx = jnp.arange(c, dtype=jnp.float32)
    diff = idx[:, None] - idx[None, :]
    decay_mask = jnp.where(diff >= 0, jnp.exp(jnp.maximum(diff, 0.0)[None] * lg[:, None, None]), 0.0)
    q_decay = jnp.exp((idx + 1.0)[None, :] * lg[:, None])[:, :, None]
    k_decay = jnp.exp((c - 1.0 - idx)[None, :] * lg[:, None])[:, :, None]
    chunk_decay = jnp.exp(c * lg)[:, None, None]

    def step(S, qkv):
        qc, kc, vc = qkv
        scores = jnp.einsum('bhid,bhjd->bhij', qc, kc) * decay_mask
        o = jnp.einsum('bhij,bhje->bhie', scores, vc) + jnp.einsum('bhid,bhde->bhie', qc * q_decay, S)
        S = chunk_decay * S + jnp.einsum('bhjd,bhje->bhde', kc * k_decay, vc)
        return S, o

    S, o = lax.scan(step, S0.astype(jnp.float32), (chunks(q), chunks(k), chunks(v)))
    o = o.transpose(1, 0, 3, 2, 4).reshape(B, L, H, d)
    return o, S


def sgu_mixer(u, v, w_s, b_s, norm_g):
    B, L, _ = u.shape
    vn = layernorm_f32(v) * norm_g.astype(jnp.float32)
    c = L if L <= SGU_CHUNK else SGU_CHUNK
    Lp = -(-L // c) * c
    vp = jnp.pad(vn, ((0, 0), (0, Lp - L), (0, 0))).reshape(B, Lp // c, c, SGU_HEADS, SGU_CH)
    tri = jnp.tril(jnp.ones((c, c), dtype=bool))
    w = jnp.where(tri[None], w_s[:, :c, :c].astype(jnp.float32), 0.0)
    bias = b_s[:, :c].astype(jnp.float32).T[None, None, :, :, None]
    mixed = jnp.einsum('hts,bnshe->bnthe', w, vp) + bias
    mixed = mixed.reshape(B, Lp, GROUP_WIDTH)[:, :L]
    y = (u.astype(jnp.float32) * mixed).astype(u.dtype)
    n_open = (L - 1) % SGU_CHUNK + 1
    return y, vn[:, L - n_open:].astype(u.dtype)


def decoder_layer(h, start, pool_prev, conv_prev, ret_prev,
                  norm1_g, w_in, pool_w, pool_scale, conv_w, ret_norm_g,
                  sgu_norm_g, sgu_w, sgu_b, w_out, norm2_g, w_gate_up, w_down):
    B, L, _ = h.shape
    xn = rmsnorm(h, norm1_g)
    p = jnp.einsum('bld,de->ble', xn, w_in)
    a, bg, cg, hh, q, k, v, g, u, vv = jnp.split(p, 10, axis=-1)
    y_a, pool_new = pool_mixer(a, pool_prev, start, pool_w, pool_scale)
    y_b, conv_new = conv_mixer(bg, cg, hh, conv_prev, conv_w)
    o, ret_new = retention(q, k, v, ret_prev, start)
    on = layernorm_f32(o).reshape(B, L, GROUP_WIDTH) * ret_norm_g.astype(jnp.float32)
    y_c = (jax.nn.silu(g.astype(jnp.float32)) * on).astype(h.dtype)
    y_d, sgu_rows = sgu_mixer(u, vv, sgu_w, sgu_b, sgu_norm_g)
    mix = jnp.concatenate([y_a, y_b, y_c, y_d], axis=-1)
    h = h + jnp.einsum('blm,md->bld', mix, w_out)
    hn = rmsnorm(h, norm2_g)
    gt, up = jnp.split(jnp.einsum('bld,df->blf', hn, w_gate_up), 2, axis=-1)
    h = h + jnp.einsum('blf,fd->bld', jax.nn.silu(gt) * up, w_down)
    return h, pool_new, conv_new, ret_new, sgu_rows


def setup_inputs(seed: int = 0) -> dict:
    key = jax.random.key(seed)
    ks = jax.random.split(key, 20)
    f32 = jnp.float32
    nrm = lambda k, shape, s: jax.random.normal(k, shape, f32) * s
    return {
        'x_prompt': nrm(ks[0], (BATCH, SEQ, D_MODEL), 1.0),
        'x_sample': nrm(ks[1], (DEC_BATCH, DEC_SEQ, D_MODEL), 1.0),
        'state_pool': nrm(ks[2], (DEPTH, DEC_BATCH, POOL_BUF, GROUP_WIDTH), 1.0),
        'state_conv': nrm(ks[3], (DEPTH, DEC_BATCH, CONV_WIDTH - 1, GROUP_WIDTH), 1.0),
        'state_ret': nrm(ks[4], (DEPTH, DEC_BATCH, RET_HEADS, RET_HEAD_DIM, RET_HEAD_DIM), 0.5),
        'norm1_g': 1.0 + nrm(ks[5], (DEPTH, D_MODEL), 0.02),
        'w_in': nrm(ks[6], (DEPTH, D_MODEL, IN_WIDTH), D_MODEL ** -0.5),
        'pool_w': nrm(ks[7], (DEPTH, POOL_GROUPS, POOL_CH, POOL_CH), POOL_CH ** -0.5),
        'pool_scale': 1.0 + nrm(ks[8], (DEPTH, GROUP_WIDTH), 0.02),
        'conv_w': nrm(ks[9], (DEPTH, CONV_WIDTH, GROUP_WIDTH), CONV_WIDTH ** -0.5),
        'ret_norm_g': 1.0 + nrm(ks[10], (DEPTH, GROUP_WIDTH), 0.02),
        'sgu_norm_g': 1.0 + nrm(ks[11], (DEPTH, GROUP_WIDTH), 0.02),
        'sgu_w': nrm(ks[12], (DEPTH, SGU_HEADS, SGU_CHUNK, SGU_CHUNK), SGU_CHUNK ** -0.5),
        'sgu_b': 1.0 + nrm(ks[13], (DEPTH, SGU_HEADS, SGU_CHUNK), 0.02),
        'w_out': nrm(ks[14], (DEPTH, MIX_WIDTH, D_MODEL), MIX_WIDTH ** -0.5),
        'norm2_g': 1.0 + nrm(ks[15], (DEPTH, D_MODEL), 0.02),
        'w_gate_up': nrm(ks[16], (DEPTH, D_MODEL, 2 * D_FF), D_MODEL ** -0.5),
        'w_down': nrm(ks[17], (DEPTH, D_FF, D_MODEL), D_FF ** -0.5),
        'final_norm_g': 1.0 + nrm(ks[18], (D_MODEL,), 0.02),
    }


def reference(x_prompt, x_sample, state_pool, state_conv, state_ret,
              norm1_g, w_in, pool_w, pool_scale, conv_w, ret_norm_g,
              sgu_norm_g, sgu_w, sgu_b, w_out, norm2_g, w_gate_up, w_down, final_norm_g):
    hp, hs = x_prompt, x_sample
    Bp = x_prompt.shape[0]
    pool_p, pool_s, conv_p, conv_s, ret_p, ret_s, sgu_p, sgu_s = [], [], [], [], [], [], [], []
    for l in range(DEPTH):
        lp = (norm1_g[l], w_in[l], pool_w[l], pool_scale[l], conv_w[l], ret_norm_g[l],
              sgu_norm_g[l], sgu_w[l], sgu_b[l], w_out[l], norm2_g[l], w_gate_up[l], w_down[l])
        hp, a1, b1, c1, d1 = decoder_layer(
            hp, 0,
            jnp.zeros((Bp, POOL_BUF, GROUP_WIDTH), x_prompt.dtype),
            jnp.zeros((Bp, CONV_WIDTH - 1, GROUP_WIDTH), x_prompt.dtype),
            jnp.zeros((Bp, RET_HEADS, RET_HEAD_DIM, RET_HEAD_DIM), jnp.float32),
            *lp)
        hs, a2, b2, c2, d2 = decoder_layer(hs, PAST_LEN, state_pool[l], state_conv[l], state_ret[l], *lp)
        pool_p.append(a1); conv_p.append(b1); ret_p.append(c1); sgu_p.append(d1)
        pool_s.append(a2); conv_s.append(b2); ret_s.append(c2); sgu_s.append(d2)
    y_prompt = rmsnorm(hp, final_norm_g)
    y_sample = rmsnorm(hs, final_norm_g)
    return (y_prompt, y_sample,
            jnp.stack(pool_p), jnp.stack(pool_s),
            jnp.stack(conv_p), jnp.stack(conv_s),
            jnp.stack(ret_p), jnp.stack(ret_s),
            jnp.stack(sgu_p), jnp.stack(sgu_s))
```

```python
import functools

import jax
import jax.numpy as jnp
from jax import lax
from jax.experimental import pallas as pl
from jax.experimental.pallas import tpu as pltpu

F32 = jnp.float32
BF16 = jnp.bfloat16

D_MODEL = 2048
DEPTH = 2
GROUP = 512
N_SPLITS = 10
IN_WIDTH = N_SPLITS * GROUP
POOL_WINDOWS = (2, 4, 8, 16)
POOL_BUF = 15
HEADS = 4
HEAD_DIM = 128
CHUNK = 128
ROPE_BASE = 10000.0
D_FF = 5632
NORM_EPS = 1e-6
PAST_LEN = 16384

_A, _BG, _CG, _HH, _Q, _K, _V, _G, _U, _VV = (i * GROUP for i in range(N_SPLITS))
_YA, _YB, _YC, _YD = (i * GROUP for i in range(4))

MIB = 1024 * 1024


def _rms(x, g):
    ms = jnp.mean(x * x, axis=-1, keepdims=True)
    return x * lax.rsqrt(ms + NORM_EPS) * g


def _layernorm(x):
    mu = jnp.mean(x, axis=-1, keepdims=True)
    xc = x - mu
    return xc * lax.rsqrt(jnp.mean(xc * xc, axis=-1, keepdims=True) + NORM_EPS)


def _silu(x):
    return x * jax.nn.sigmoid(x)


def _dot(a, b):
    return jnp.dot(a, b, preferred_element_type=F32)


def _rotate(x, cos, sin_signed):
    return x * cos + pltpu.roll(x, HEAD_DIM // 2, 1) * sin_signed


def _norm_matmul_kernel(x_ref, g_ref, w_ref, o_ref, xn_ref):
    @pl.when(pl.program_id(1) == 0)
    def _():
        xn_ref[...] = _rms(x_ref[...], g_ref[...]).astype(BF16)

    o_ref[...] = _dot(xn_ref[...], w_ref[...])


def norm_matmul(x, g, w, layer, *, tm, tn):
    m, k = x.shape
    n = w.shape[-1]
    return pl.pallas_call(
        _norm_matmul_kernel,
        grid=(m // tm, n // tn),
        in_specs=[
            pl.BlockSpec((tm, k), lambda i, j: (i, 0)),
            pl.BlockSpec((None, 1, k), lambda i, j: (layer, 0, 0)),
            pl.BlockSpec((None, k, tn), lambda i, j: (layer, 0, j)),
        ],
        out_specs=pl.BlockSpec((tm, tn), lambda i, j: (i, j)),
        out_shape=jax.ShapeDtypeStruct((m, n), F32),
        scratch_shapes=[pltpu.VMEM((tm, k), BF16)],
        compiler_params=pltpu.CompilerParams(
            dimension_semantics=("arbitrary", "arbitrary"),
            vmem_limit_bytes=48 * MIB),
        name="norm_matmul",
    )(x, g, w)


def _matmul_residual_kernel(a_ref, w_ref, r_ref, o_ref):
    o_ref[...] = r_ref[...] + _dot(a_ref[...], w_ref[...])


def matmul_residual(a, w, res, layer, *, tm):
    m, k = a.shape
    n = w.shape[-1]
    return pl.pallas_call(
        _matmul_residual_kernel,
        grid=(m // tm,),
        in_specs=[
            pl.BlockSpec((tm, k), lambda i: (i, 0)),
            pl.BlockSpec((None, k, n), lambda i: (layer, 0, 0)),
            pl.BlockSpec((tm, n), lambda i: (i, 0)),
        ],
        out_specs=pl.BlockSpec((tm, n), lambda i: (i, 0)),
        out_shape=jax.ShapeDtypeStruct((m, n), F32),
        compiler_params=pltpu.CompilerParams(
            dimension_semantics=("arbitrary",),
            vmem_limit_bytes=48 * MIB),
        name="matmul_residual",
    )(a, w, res)


def _mlp_kernel(x_ref, g_ref, wg_ref, wu_ref, wd_ref, fg_ref, o_ref, hn_ref, *, final_norm):
    f = pl.program_id(1)

    @pl.when(f == 0)
    def _():
        x = x_ref[...]
        hn_ref[...] = _rms(x, g_ref[...]).astype(BF16)
        o_ref[...] = x

    hn = hn_ref[...]
    act = (_silu(_dot(hn, wg_ref[...])) * _dot(hn, wu_ref[...])).astype(BF16)
    o_ref[...] += _dot(act, wd_ref[...])

    if final_norm:
        @pl.when(f == pl.num_programs(1) - 1)
        def _():
            o_ref[...] = _rms(o_ref[...], fg_ref[...])


def mlp(x, g, w_gate_up, w_down, final_g, layer, *, tm, tf, final_norm):
    m, d = x.shape
    nf = D_FF // tf
    return pl.pallas_call(
        functools.partial(_mlp_kernel, final_norm=final_norm),
        grid=(m // tm, nf),
        in_specs=[
            pl.BlockSpec((tm, d), lambda i, f: (i, 0)),
            pl.BlockSpec((None, 1, d), lambda i, f: (layer, 0, 0)),
            pl.BlockSpec((None, d, tf), lambda i, f: (layer, 0, f)),
            pl.BlockSpec((None, d, tf), lambda i, f: (layer, 0, nf + f)),
            pl.BlockSpec((None, tf, d), lambda i, f: (layer, f, 0)),
            pl.BlockSpec((1, d), lambda i, f: (0, 0)),
        ],
        out_specs=pl.BlockSpec((tm, d), lambda i, f: (i, 0)),
        out_shape=jax.ShapeDtypeStruct((m, d), F32),
        scratch_shapes=[pltpu.VMEM((tm, d), BF16)],
        compiler_params=pltpu.CompilerParams(
            dimension_semantics=("arbitrary", "arbitrary"),
            vmem_limit_bytes=48 * MIB),
        name="mlp",
    )(x, g, w_gate_up, w_gate_up, w_down, final_g)


_POOL_TOP = 16
_CONV_TOP = 8


def _mixer_prompt_kernel(p_ref, cos_ref, sin_ref, dmask_ref, qdec_ref, kdec_ref,
                         poolw_ref, pscale_ref, convw_ref, retg_ref, sgug_ref, sguw_ref, sgub_ref,
                         mix_ref, pool_out, conv_out, ret_out, sgu_out,
                         pool_ext, conv_ext, *, start):
    c = pl.program_id(1)
    last = pl.num_programs(1) - 1
    T = CHUNK

    @pl.when(c == 0)
    def _():
        pool_ext[0:_POOL_TOP, :] = jnp.zeros((_POOL_TOP, GROUP), F32)
        conv_ext[0:_CONV_TOP, :] = jnp.zeros((_CONV_TOP, GROUP), F32)
        ret_out[...] = jnp.zeros(ret_out.shape, F32)

    row = lax.broadcasted_iota(jnp.int32, (T, HEAD_DIM), 0)
    col = lax.broadcasted_iota(jnp.int32, (T, HEAD_DIM), 1)

    pool_ext[_POOL_TOP:_POOL_TOP + T, :] = p_ref[0, :, _A:_A + GROUP]
    pos1 = row + (start + 1) + c * T
    for gi, win in enumerate(POOL_WINDOWS):
        lanes = slice(gi * HEAD_DIM, (gi + 1) * HEAD_DIM)
        a = pool_ext[_POOL_TOP:_POOL_TOP + T, lanes]
        wsum = a
        for j in range(1, win):
            wsum = wsum + pool_ext[_POOL_TOP - j:_POOL_TOP - j + T, lanes]
        cnt = jnp.minimum(pos1, win).astype(F32)
        d = (wsum / cnt - a).astype(BF16)
        y = _dot(d, poolw_ref[gi]) * pscale_ref[:, lanes]
        mix_ref[0, :, _YA + gi * HEAD_DIM:_YA + (gi + 1) * HEAD_DIM] = y.astype(BF16)

    @pl.when(c == last)
    def _():
        pool_out[0] = pool_ext[_POOL_TOP + T - POOL_BUF:_POOL_TOP + T, :]

    pool_ext[_POOL_TOP - POOL_BUF:_POOL_TOP, :] = pool_ext[_POOL_TOP + T - POOL_BUF:_POOL_TOP + T, :]

    z = p_ref[0, :, _CG:_CG + GROUP] * p_ref[0, :, _HH:_HH + GROUP]
    conv_ext[_CONV_TOP:_CONV_TOP + T, :] = z
    acc = (conv_ext[_CONV_TOP - 2:_CONV_TOP - 2 + T, :] * convw_ref[0:1, :]
           + conv_ext[_CONV_TOP - 1:_CONV_TOP - 1 + T, :] * convw_ref[1:2, :]
           + z * convw_ref[2:3, :])
    mix_ref[0, :, _YB:_YB + GROUP] = (p_ref[0, :, _BG:_BG + GROUP] * acc).astype(BF16)

    @pl.when(c == last)
    def _():
        conv_out[0] = conv_ext[_CONV_TOP + T - 2:_CONV_TOP + T, :]

    conv_ext[_CONV_TOP - 2:_CONV_TOP, :] = conv_ext[_CONV_TOP + T - 2:_CONV_TOP + T, :]

    cos = cos_ref[...]
    sin = sin_ref[...]
    for h in range(HEADS):
        lanes = slice(h * HEAD_DIM, (h + 1) * HEAD_DIM)
        q = _rotate(p_ref[0, :, _Q + h * HEAD_DIM:_Q + (h + 1) * HEAD_DIM], cos, sin)
        k = _rotate(p_ref[0, :, _K + h * HEAD_DIM:_K + (h + 1) * HEAD_DIM], cos, sin) * (HEAD_DIM ** -0.5)
        v = p_ref[0, :, _V + h * HEAD_DIM:_V + (h + 1) * HEAD_DIM].astype(BF16)
        s_prev = ret_out[0, h]
        scores = lax.dot_general(q.astype(BF16), k.astype(BF16), (((1,), (1,)), ((), ())),
                                 preferred_element_type=F32) * dmask_ref[h]
        o = _dot(scores.astype(BF16), v) + _dot((q * qdec_ref[h]).astype(BF16), s_prev.astype(BF16))
        kv = lax.dot_general((k * kdec_ref[h]).astype(BF16), v, (((0,), (0,)), ((), ())),
                             preferred_element_type=F32)
        ret_out[0, h] = qdec_ref[h, T - 1:T, :] * s_prev + kv
        on = _layernorm(o) * retg_ref[:, lanes]
        g = p_ref[0, :, _G + h * HEAD_DIM:_G + (h + 1) * HEAD_DIM]
        mix_ref[0, :, _YC + h * HEAD_DIM:_YC + (h + 1) * HEAD_DIM] = (_silu(g) * on).astype(BF16)

    vn = _layernorm(p_ref[0, :, _VV:_VV + GROUP]) * sgug_ref[...]
    tri = row >= col
    for h in range(HEADS):
        lanes = slice(h * HEAD_DIM, (h + 1) * HEAD_DIM)
        w = jnp.where(tri, sguw_ref[h], 0.0).astype(BF16)
        mixed = _dot(w, vn[:, lanes].astype(BF16)) + sgub_ref[h]
        u = p_ref[0, :, _U + h * HEAD_DIM:_U + (h + 1) * HEAD_DIM]
        mix_ref[0, :, _YD + h * HEAD_DIM:_YD + (h + 1) * HEAD_DIM] = (u * mixed).astype(BF16)

    @pl.when(c == last)
    def _():
        sgu_out[0] = vn


def mixer_prompt(p, tables, params, layer, *, start):
    b, l, _ = p.shape
    nc = l // CHUNK
    cos, sin, dmask, qdec, kdec = tables
    poolw, pscale, convw, retg, sgug, sguw, sgub = params
    const3 = lambda shape: pl.BlockSpec(shape, lambda i, c: (0, 0, 0))
    lay3 = lambda shape: pl.BlockSpec((None,) + shape, lambda i, c: (layer, 0, 0))
    lay4 = lambda shape: pl.BlockSpec((None,) + shape, lambda i, c: (layer, 0, 0, 0))
    return pl.pallas_call(
        functools.partial(_mixer_prompt_kernel, start=start),
        grid=(b, nc),
        in_specs=[
            pl.BlockSpec((1, CHUNK, IN_WIDTH), lambda i, c: (i, c, 0)),
            pl.BlockSpec((CHUNK, HEAD_DIM), lambda i, c: (c, 0)),
            pl.BlockSpec((CHUNK, HEAD_DIM), lambda i, c: (c, 0)),
            const3((HEADS, CHUNK, CHUNK)),
            const3((HEADS, CHUNK, HEAD_DIM)),
            const3((HEADS, CHUNK, HEAD_DIM)),
            lay4((len(POOL_WINDOWS), HEAD_DIM, HEAD_DIM)),
            lay3((1, GROUP)),
            lay3((3, GROUP)),
            lay3((1, GROUP)),
            lay3((1, GROUP)),
            lay4((HEADS, CHUNK, CHUNK)),
            lay4((HEADS, CHUNK, HEAD_DIM)),
        ],
        out_specs=[
            pl.BlockSpec((1, CHUNK, 4 * GROUP), lambda i, c: (i, c, 0)),
            pl.BlockSpec((1, POOL_BUF, GROUP), lambda i, c: (i, 0, 0)),
            pl.BlockSpec((1, 2, GROUP), lambda i, c: (i, 0, 0)),
            pl.BlockSpec((1, HEADS, HEAD_DIM, HEAD_DIM), lambda i, c: (i, 0, 0, 0)),
            pl.BlockSpec((1, CHUNK, GROUP), lambda i, c: (i, 0, 0)),
        ],
        out_shape=[
            jax.ShapeDtypeStruct((b, l, 4 * GROUP), BF16),
            jax.ShapeDtypeStruct((b, POOL_BUF, GROUP), F32),
            jax.ShapeDtypeStruct((b, 2, GROUP), F32),
            jax.ShapeDtypeStruct((b, HEADS, HEAD_DIM, HEAD_DIM), F32),
            jax.ShapeDtypeStruct((b, CHUNK, GROUP), F32),
        ],
        scratch_shapes=[
            pltpu.VMEM((_POOL_TOP + CHUNK, GROUP), F32),
            pltpu.VMEM((_CONV_TOP + CHUNK, GROUP), F32),
        ],
        compiler_params=pltpu.CompilerParams(
            dimension_semantics=("arbitrary", "arbitrary"),
            vmem_limit_bytes=32 * MIB),
        name="mixer_prompt",
    )(p, cos, sin, dmask, qdec, kdec, poolw, pscale, convw, retg, sgug, sguw, sgub)


_SEQ_BLOCK = 8


def _mixer_sample_kernel(p_ref, pool_ref, conv_ref, ret_ref, cos_ref, sin_ref, gam_ref,
                         poolw_ref, pscale_ref, convw_ref, retg_ref, sgug_ref, sguw_ref, sgub_ref,
                         mix_ref, pool_out, conv_out, ret_out, sgu_out,
                         qg_s, k_s, v_s, os_s, *, start):
    TB = _SEQ_BLOCK

    a_all = p_ref[:, _A:_A + GROUP]
    for gi, win in enumerate(POOL_WINDOWS):
        lanes = slice(gi * HEAD_DIM, (gi + 1) * HEAD_DIM)
        a = a_all[:, lanes]
        wsum = a
        for j in range(1, win):
            off = (POOL_BUF - j) * GROUP + gi * HEAD_DIM
            wsum = wsum + pool_ref[:, off:off + HEAD_DIM]
        cnt = float(min(start + 1, win))
        d = (wsum / cnt - a).astype(BF16)
        y = _dot(d, poolw_ref[gi]) * pscale_ref[:, lanes]
        mix_ref[:, _YA + gi * HEAD_DIM:_YA + (gi + 1) * HEAD_DIM] = y.astype(BF16)
    pool_out[:, 0:(POOL_BUF - 1) * GROUP] = pool_ref[:, GROUP:POOL_BUF * GROUP]
    pool_out[:, (POOL_BUF - 1) * GROUP:POOL_BUF * GROUP] = a_all

    z = p_ref[:, _CG:_CG + GROUP] * p_ref[:, _HH:_HH + GROUP]
    acc = (conv_ref[:, 0:GROUP] * convw_ref[0:1, :] + conv_ref[:, GROUP:2 * GROUP] * convw_ref[1:2, :]
           + z * convw_ref[2:3, :])
    mix_ref[:, _YB:_YB + GROUP] = (p_ref[:, _BG:_BG + GROUP] * acc).astype(BF16)
    conv_out[:, 0:GROUP] = conv_ref[:, GROUP:2 * GROUP]
    conv_out[:, GROUP:2 * GROUP] = z

    cos = cos_ref[...]
    sin = sin_ref[...]
    scores = []
    for h in range(HEADS):
        lanes = slice(h * HEAD_DIM, (h + 1) * HEAD_DIM)
        q = _rotate(p_ref[:, _Q + h * HEAD_DIM:_Q + (h + 1) * HEAD_DIM], cos, sin)
        k = _rotate(p_ref[:, _K + h * HEAD_DIM:_K + (h + 1) * HEAD_DIM], cos, sin) * (HEAD_DIM ** -0.5)
        scores.append(jnp.sum(q * k, axis=-1, keepdims=True))
        qg_s[:, lanes] = q * gam_ref[:, lanes]
        k_s[:, lanes] = k
    v_s[...] = p_ref[:, _V:_V + GROUP]

    first_row = lax.broadcasted_iota(jnp.int32, (8, HEAD_DIM), 0) == 0

    for b in range(TB):
        for h in range(HEADS):
            lanes = slice(h * HEAD_DIM, (h + 1) * HEAD_DIM)
            s_prev = ret_ref[b, h]
            q8 = jnp.broadcast_to(qg_s[b:b + 1, lanes], (8, HEAD_DIM))
            os_s[b:b + 1, lanes] = _dot(q8.astype(BF16), s_prev.astype(BF16))[0:1, :]
            k8 = jnp.where(first_row, jnp.broadcast_to(k_s[b:b + 1, lanes], (8, HEAD_DIM)), 0.0)
            v8 = jnp.broadcast_to(v_s[b:b + 1, lanes], (8, HEAD_DIM))
            kv = lax.dot_general(k8.astype(BF16), v8.astype(BF16), (((0,), (0,)), ((), ())),
                                 preferred_element_type=F32)
            ret_out[b, h] = gam_ref[:, lanes] * s_prev + kv

    for h in range(HEADS):
        lanes = slice(h * HEAD_DIM, (h + 1) * HEAD_DIM)
        o = scores[h] * v_s[:, lanes] + os_s[:, lanes]
        on = _layernorm(o) * retg_ref[:, lanes]
        g = p_ref[:, _G + h * HEAD_DIM:_G + (h + 1) * HEAD_DIM]
        mix_ref[:, _YC + h * HEAD_DIM:_YC + (h + 1) * HEAD_DIM] = (_silu(g) * on).astype(BF16)

    vn = _layernorm(p_ref[:, _VV:_VV + GROUP]) * sgug_ref[...]
    mix_ref[:, _YD:_YD + GROUP] = (p_ref[:, _U:_U + GROUP] * (sguw_ref[...] * vn + sgub_ref[...])).astype(BF16)
    sgu_out[...] = vn


def mixer_sample(p, pool_state, conv_state, ret_state, tables, params, layer, *, start):
    n = p.shape[0]
    tb = _SEQ_BLOCK
    cos, sin, gam = tables
    poolw, pscale, convw, retg, sgug, sguw, sgub = params
    rows = lambda width: pl.BlockSpec((tb, width), lambda i: (i, 0))
    lay_rows = lambda width: pl.BlockSpec((None, tb, width), lambda i: (layer, i, 0))
    const2 = lambda shape: pl.BlockSpec(shape, lambda i: (0, 0))
    lay3 = lambda shape: pl.BlockSpec((None,) + shape, lambda i: (layer, 0, 0))
    lay4 = lambda shape: pl.BlockSpec((None,) + shape, lambda i: (layer, 0, 0, 0))
    return pl.pallas_call(
        functools.partial(_mixer_sample_kernel, start=start),
        grid=(n // tb,),
        in_specs=[
            rows(IN_WIDTH),
            lay_rows(POOL_BUF * GROUP),
            lay_rows(2 * GROUP),
            pl.BlockSpec((None, tb, HEADS, HEAD_DIM, HEAD_DIM), lambda i: (layer, i, 0, 0, 0)),
            const2((1, HEAD_DIM)),
            const2((1, HEAD_DIM)),
            const2((1, GROUP)),
            lay4((len(POOL_WINDOWS), HEAD_DIM, HEAD_DIM)),
            lay3((1, GROUP)),
            lay3((3, GROUP)),
            lay3((1, GROUP)),
            lay3((1, GROUP)),
            lay3((1, GROUP)),
            lay3((1, GROUP)),
        ],
        out_specs=[
            rows(4 * GROUP),
            rows(POOL_BUF * GROUP),
            rows(2 * GROUP),
            pl.BlockSpec((tb, HEADS, HEAD_DIM, HEAD_DIM), lambda i: (i, 0, 0, 0)),
            rows(GROUP),
        ],
        out_shape=[
            jax.ShapeDtypeStruct((n, 4 * GROUP), BF16),
            jax.ShapeDtypeStruct((n, POOL_BUF * GROUP), F32),
            jax.ShapeDtypeStruct((n, 2 * GROUP), F32),
            jax.ShapeDtypeStruct((n, HEADS, HEAD_DIM, HEAD_DIM), F32),
            jax.ShapeDtypeStruct((n, GROUP), F32),
        ],
        scratch_shapes=[pltpu.VMEM((tb, GROUP), F32)] * 4,
        compiler_params=pltpu.CompilerParams(
            dimension_semantics=("arbitrary",),
            vmem_limit_bytes=32 * MIB),
        name="mixer_sample",
    )(p, pool_state, conv_state, ret_state, cos, sin, gam,
      poolw, pscale, convw, retg, sgug, sguw, sgub)


def _rope_tables(start, length):
    half = HEAD_DIM // 2
    inv = ROPE_BASE ** (-jnp.arange(half, dtype=F32) / half)
    pos = start + jnp.arange(length)
    ang = pos.astype(F32)[:, None] * inv[None, :]
    cos, sin = jnp.cos(ang), jnp.sin(ang)
    return jnp.concatenate([cos, cos], axis=-1), jnp.concatenate([-sin, sin], axis=-1)


def _log_gamma():
    return jnp.log1p(-(2.0 ** (-5.0 - jnp.arange(HEADS, dtype=F32))))


def _decay_tables(c):
    lg = _log_gamma()
    idx = jnp.arange(c, dtype=F32)
    diff = idx[:, None] - idx[None, :]
    dmask = jnp.where(diff >= 0, jnp.exp(jnp.maximum(diff, 0.0)[None] * lg[:, None, None]), 0.0)
    qdec = jnp.exp((idx + 1.0)[None, :] * lg[:, None])[:, :, None]
    kdec = jnp.exp((c - 1.0 - idx)[None, :] * lg[:, None])[:, :, None]
    wide = lambda t: jnp.broadcast_to(t, (HEADS, c, HEAD_DIM))
    return dmask, wide(qdec), wide(kdec)


def _per_head_lanes(t):
    return jnp.repeat(t, HEAD_DIM, axis=-1)[..., None, :]


def kernel(x_prompt, x_sample, state_pool, state_conv, state_ret, norm1_g, w_in, pool_w, pool_scale,
           conv_w, ret_norm_g, sgu_norm_g, sgu_w, sgu_b, w_out, norm2_g, w_gate_up, w_down,
           final_norm_g):
    bp, lp, _ = x_prompt.shape
    ns = x_sample.shape[0]

    w_in_b = w_in.astype(BF16)
    w_out_b = w_out.astype(BF16)
    w_gu_b = w_gate_up.astype(BF16)
    w_dn_b = w_down.astype(BF16)
    pool_w_b = pool_w.astype(BF16)
    row3 = lambda t: t[:, None, :]
    norm1 = row3(norm1_g)
    norm2 = row3(norm2_g)
    final_g = final_norm_g[None, :]

    cos_p, sin_p = _rope_tables(0, lp)
    prompt_tables = (cos_p, sin_p) + _decay_tables(CHUNK)
    sgu_b_wide = jnp.broadcast_to(sgu_b[:, :, :, None], sgu_b.shape + (HEAD_DIM,))
    prompt_params = (pool_w_b, row3(pool_scale), conv_w, row3(ret_norm_g), row3(sgu_norm_g), sgu_w, sgu_b_wide)

    cos_s, sin_s = _rope_tables(PAST_LEN, 1)
    gam = _per_head_lanes(jnp.exp(_log_gamma()))
    sample_tables = (cos_s, sin_s, gam)
    sample_params = (pool_w_b, row3(pool_scale), conv_w, row3(ret_norm_g), row3(sgu_norm_g),
                     _per_head_lanes(sgu_w[:, :, 0, 0]), _per_head_lanes(sgu_b[:, :, 0]))
    pool_s2 = state_pool.reshape(DEPTH, ns, POOL_BUF * GROUP)
    conv_s2 = state_conv.reshape(DEPTH, ns, 2 * GROUP)

    hp = x_prompt.reshape(bp * lp, D_MODEL)
    hs = x_sample.reshape(ns, D_MODEL)
    outs = [[] for _ in range(8)]
    for l in range(DEPTH):
        fin = l == DEPTH - 1
        p = norm_matmul(hp, norm1, w_in_b, l, tm=1024, tn=1024)
        mix, pool_n, conv_n, ret_n, sgu_n = mixer_prompt(
            p.reshape(bp, lp, IN_WIDTH), prompt_tables, prompt_params, l, start=0)
        hp = matmul_residual(mix.reshape(bp * lp, 4 * GROUP), w_out_b, hp, l, tm=512)
        hp = mlp(hp, norm2, w_gu_b, w_dn_b, final_g, l, tm=512, tf=512, final_norm=fin)
        p = norm_matmul(hs, norm1, w_in_b, l, tm=ns, tn=1024)
        mix, pool_m, conv_m, ret_m, sgu_m = mixer_sample(
            p, pool_s2, conv_s2, state_ret, sample_tables, sample_params, l, start=PAST_LEN)
        hs = matmul_residual(mix, w_out_b, hs, l, tm=ns)
        hs = mlp(hs, norm2, w_gu_b, w_dn_b, final_g, l, tm=ns, tf=512, final_norm=fin)
        for lst, val in zip(outs, (pool_n, pool_m.reshape(ns, POOL_BUF, GROUP), conv_n,
                                   conv_m.reshape(ns, 2, GROUP), ret_n, ret_m, sgu_n,
                                   sgu_m.reshape(ns, 1, GROUP))):
            lst.append(val)
    return (hp.reshape(bp, lp, D_MODEL), hs.reshape(ns, 1, D_MODEL)) + tuple(jnp.stack(o) for o in outs)
```

```python
import functools

import jax
import jax.numpy as jnp
from jax import lax
from jax.experimental import pallas as pl
from jax.experimental.pallas import tpu as pltpu

F32 = jnp.float32
BF16 = jnp.bfloat16

D_MODEL = 2048
DEPTH = 2
GROUP = 512
N_SPLITS = 10
IN_WIDTH = N_SPLITS * GROUP
POOL_WINDOWS = (2, 4, 8, 16)
POOL_BUF = 15
HEADS = 4
HEAD_DIM = 128
CHUNK = 128
ROPE_BASE = 10000.0
D_FF = 5632
NORM_EPS = 1e-6
PAST_LEN = 16384

_A, _BG, _CG, _HH, _Q, _K, _V, _G, _U, _VV = (i * GROUP for i in range(N_SPLITS))
_YA, _YB, _YC, _YD = (i * GROUP for i in range(4))

MIB = 1024 * 1024
VMEM_BYTES_V7X = 64 * MIB


def _vmem_limit(estimate):
    return min(estimate + 4 * MIB, VMEM_BYTES_V7X - 4 * MIB)


def _rms(x, g):
    ms = jnp.mean(x * x, axis=-1, keepdims=True)
    return x * lax.rsqrt(ms + NORM_EPS) * g


def _layernorm(x):
    mu = jnp.mean(x, axis=-1, keepdims=True)
    xc = x - mu
    return xc * lax.rsqrt(jnp.mean(xc * xc, axis=-1, keepdims=True) + NORM_EPS)


def _silu(x):
    return x * jax.nn.sigmoid(x)


def _dot(a, b):
    return jnp.dot(a, b, preferred_element_type=F32)


def _rotate(x, cos, sin_signed):
    return x * cos + pltpu.roll(x, HEAD_DIM // 2, 1) * sin_signed


def _norm_matmul_kernel(x_ref, g_ref, w_ref, o_ref, xn_ref):
    @pl.when(pl.program_id(1) == 0)
    def _():
        xn_ref[...] = _rms(x_ref[...], g_ref[...]).astype(BF16)

    o_ref[...] = _dot(xn_ref[...], w_ref[...])


def norm_matmul(x, g, w, layer, *, tm, tn):
    m, k = x.shape
    n = w.shape[-1]
    return pl.pallas_call(
        _norm_matmul_kernel,
        grid=(m // tm, n // tn),
        in_specs=[
            pl.BlockSpec((tm, k), lambda i, j: (i, 0)),
            pl.BlockSpec((None, 1, k), lambda i, j: (layer, 0, 0)),
            pl.BlockSpec((None, k, tn), lambda i, j: (layer, 0, j)),
        ],
        out_specs=pl.BlockSpec((tm, tn), lambda i, j: (i, j)),
        out_shape=jax.ShapeDtypeStruct((m, n), F32),
        scratch_shapes=[pltpu.VMEM((tm, k), BF16)],
        compiler_params=pltpu.CompilerParams(
            dimension_semantics=("arbitrary", "arbitrary"),
            vmem_limit_bytes=48 * MIB),
        name="norm_matmul",
    )(x, g, w)


def _matmul_residual_kernel(a_ref, w_ref, r_ref, o_ref):
    o_ref[...] = r_ref[...] + _dot(a_ref[...], w_ref[...])


def matmul_residual(a, w, res, layer, *, tm):
    m, k = a.shape
    n = w.shape[-1]
    return pl.pallas_call(
        _matmul_residual_kernel,
        grid=(m // tm,),
        in_specs=[
            pl.BlockSpec((tm, k), lambda i: (i, 0)),
            pl.BlockSpec((None, k, n), lambda i: (layer, 0, 0)),
            pl.BlockSpec((tm, n), lambda i: (i, 0)),
        ],
        out_specs=pl.BlockSpec((tm, n), lambda i: (i, 0)),
        out_shape=jax.ShapeDtypeStruct((m, n), F32),
        compiler_params=pltpu.CompilerParams(
            dimension_semantics=("arbitrary",),
            vmem_limit_bytes=48 * MIB),
        name="matmul_residual",
    )(a, w, res)


def _mlp_kernel(x_ref, g_ref, wg_ref, wu_ref, wd_ref, fg_ref, o_ref, hn_ref, *, final_norm):
    f = pl.program_id(1)

    @pl.when(f == 0)
    def _():
        x = x_ref[...]
        hn_ref[...] = _rms(x, g_ref[...]).astype(BF16)
        o_ref[...] = x

    hn = hn_ref[...]
    act = (_silu(_dot(hn, wg_ref[...])) * _dot(hn, wu_ref[...])).astype(BF16)
    o_ref[...] += _dot(act, wd_ref[...])

    if final_norm:
        @pl.when(f == pl.num_programs(1) - 1)
        def _():
            o_ref[...] = _rms(o_ref[...], fg_ref[...])


def mlp(x, g, w_gate_up, w_down, final_g, layer, *, tm, tf, final_norm):
    m, d = x.shape
    nf = D_FF // tf
    vmem = 2 * 2 * tm * d * 4 + tm * d * 2 + 2 * 3 * d * tf * 2 + 3 * tm * tf * 4
    return pl.pallas_call(
        functools.partial(_mlp_kernel, final_norm=final_norm),
        grid=(m // tm, nf),
        in_specs=[
            pl.BlockSpec((tm, d), lambda i, f: (i, 0)),
            pl.BlockSpec((None, 1, d), lambda i, f: (layer, 0, 0)),
            pl.BlockSpec((None, d, tf), lambda i, f: (layer, 0, f)),
            pl.BlockSpec((None, d, tf), lambda i, f: (layer, 0, nf + f)),
            pl.BlockSpec((None, tf, d), lambda i, f: (layer, f, 0)),
            pl.BlockSpec((1, d), lambda i, f: (0, 0)),
        ],
        out_specs=pl.BlockSpec((tm, d), lambda i, f: (i, 0)),
        out_shape=jax.ShapeDtypeStruct((m, d), F32),
        scratch_shapes=[pltpu.VMEM((tm, d), BF16)],
        compiler_params=pltpu.CompilerParams(
            dimension_semantics=("arbitrary", "arbitrary"),
            vmem_limit_bytes=_vmem_limit(vmem)),
        name="mlp",
    )(x, g, w_gate_up, w_gate_up, w_down, final_g)


_POOL_TOP = 16
_CONV_TOP = 8


def _mixer_prompt_kernel(p_ref, cos_ref, sin_ref, dmask_ref, qdec_ref, kdec_ref,
                         poolw_ref, pscale_ref, convw_ref, retg_ref, sgug_ref, sguw_ref, sgub_ref,
                         mix_ref, pool_out, conv_out, ret_out, sgu_out,
                         pool_ext, conv_ext, *, start):
    c = pl.program_id(1)
    last = pl.num_programs(1) - 1
    T = CHUNK

    @pl.when(c == 0)
    def _():
        pool_ext[0:_POOL_TOP, :] = jnp.zeros((_POOL_TOP, GROUP), F32)
        conv_ext[0:_CONV_TOP, :] = jnp.zeros((_CONV_TOP, GROUP), F32)
        ret_out[...] = jnp.zeros(ret_out.shape, F32)

    row = lax.broadcasted_iota(jnp.int32, (T, HEAD_DIM), 0)
    col = lax.broadcasted_iota(jnp.int32, (T, HEAD_DIM), 1)

    pool_ext[_POOL_TOP:_POOL_TOP + T, :] = p_ref[0, :, _A:_A + GROUP]
    pos1 = row + (start + 1) + c * T
    for gi, win in enumerate(POOL_WINDOWS):
        lanes = slice(gi * HEAD_DIM, (gi + 1) * HEAD_DIM)
        a = pool_ext[_POOL_TOP:_POOL_TOP + T, lanes]
        wsum = a
        for j in range(1, win):
            wsum = wsum + pool_ext[_POOL_TOP - j:_POOL_TOP - j + T, lanes]
        cnt = jnp.minimum(pos1, win).astype(F32)
        d = (wsum / cnt - a).astype(BF16)
        y = _dot(d, poolw_ref[gi]) * pscale_ref[:, lanes]
        mix_ref[0, :, _YA + gi * HEAD_DIM:_YA + (gi + 1) * HEAD_DIM] = y.astype(BF16)

    @pl.when(c == last)
    def _():
        pool_out[0] = pool_ext[_POOL_TOP + T - POOL_BUF:_POOL_TOP + T, :]

    pool_ext[_POOL_TOP - POOL_BUF:_POOL_TOP, :] = pool_ext[_POOL_TOP + T - POOL_BUF:_POOL_TOP + T, :]

    z = p_ref[0, :, _CG:_CG + GROUP] * p_ref[0, :, _HH:_HH + GROUP]
    conv_ext[_CONV_TOP:_CONV_TOP + T, :] = z
    acc = (conv_ext[_CONV_TOP - 2:_CONV_TOP - 2 + T, :] * convw_ref[0:1, :]
           + conv_ext[_CONV_TOP - 1:_CONV_TOP - 1 + T, :] * convw_ref[1:2, :]
           + z * convw_ref[2:3, :])
    mix_ref[0, :, _YB:_YB + GROUP] = (p_ref[0, :, _BG:_BG + GROUP] * acc).astype(BF16)

    @pl.when(c == last)
    def _():
        conv_out[0] = conv_ext[_CONV_TOP + T - 2:_CONV_TOP + T, :]

    conv_ext[_CONV_TOP - 2:_CONV_TOP, :] = conv_ext[_CONV_TOP + T - 2:_CONV_TOP + T, :]

    cos = cos_ref[...]
    sin = sin_ref[...]
    for h in range(HEADS):
        lanes = slice(h * HEAD_DIM, (h + 1) * HEAD_DIM)
        q = _rotate(p_ref[0, :, _Q + h * HEAD_DIM:_Q + (h + 1) * HEAD_DIM], cos, sin)
        k = _rotate(p_ref[0, :, _K + h * HEAD_DIM:_K + (h + 1) * HEAD_DIM], cos, sin) * (HEAD_DIM ** -0.5)
        v = p_ref[0, :, _V + h * HEAD_DIM:_V + (h + 1) * HEAD_DIM].astype(BF16)
        s_prev = ret_out[0, h]
        scores = lax.dot_general(q.astype(BF16), k.astype(BF16), (((1,), (1,)), ((), ())),
                                 preferred_element_type=F32) * dmask_ref[h]
        o = _dot(scores.astype(BF16), v) + _dot((q * qdec_ref[h]).astype(BF16), s_prev.astype(BF16))
        kv = lax.dot_general((k * kdec_ref[h]).astype(BF16), v, (((0,), (0,)), ((), ())),
                             preferred_element_type=F32)
        ret_out[0, h] = qdec_ref[h, T - 1:T, :] * s_prev + kv
        on = _layernorm(o) * retg_ref[:, lanes]
        g = p_ref[0, :, _G + h * HEAD_DIM:_G + (h + 1) * HEAD_DIM]
        mix_ref[0, :, _YC + h * HEAD_DIM:_YC + (h + 1) * HEAD_DIM] = (_silu(g) * on).astype(BF16)

    vn = _layernorm(p_ref[0, :, _VV:_VV + GROUP]) * sgug_ref[...]
    tri = row >= col
    for h in range(HEADS):
        lanes = slice(h * HEAD_DIM, (h + 1) * HEAD_DIM)
        w = jnp.where(tri, sguw_ref[h], 0.0).astype(BF16)
        mixed = _dot(w, vn[:, lanes].astype(BF16)) + sgub_ref[h]
        u = p_ref[0, :, _U + h * HEAD_DIM:_U + (h + 1) * HEAD_DIM]
        mix_ref[0, :, _YD + h * HEAD_DIM:_YD + (h + 1) * HEAD_DIM] = (u * mixed).astype(BF16)

    @pl.when(c == last)
    def _():
        sgu_out[0] = vn


def mixer_prompt(p, tables, params, layer, *, start):
    b, l, _ = p.shape
    nc = l // CHUNK
    cos, sin, dmask, qdec, kdec = tables
    poolw, pscale, convw, retg, sgug, sguw, sgub = params
    const3 = lambda shape: pl.BlockSpec(shape, lambda i, c: (0, 0, 0))
    lay3 = lambda shape: pl.BlockSpec((None,) + shape, lambda i, c: (layer, 0, 0))
    lay4 = lambda shape: pl.BlockSpec((None,) + shape, lambda i, c: (layer, 0, 0, 0))
    return pl.pallas_call(
        functools.partial(_mixer_prompt_kernel, start=start),
        grid=(b, nc),
        in_specs=[
            pl.BlockSpec((1, CHUNK, IN_WIDTH), lambda i, c: (i, c, 0)),
            pl.BlockSpec((CHUNK, HEAD_DIM), lambda i, c: (c, 0)),
            pl.BlockSpec((CHUNK, HEAD_DIM), lambda i, c: (c, 0)),
            const3((HEADS, CHUNK, CHUNK)),
            const3((HEADS, CHUNK, HEAD_DIM)),
            const3((HEADS, CHUNK, HEAD_DIM)),
            lay4((len(POOL_WINDOWS), HEAD_DIM, HEAD_DIM)),
            lay3((1, GROUP)),
            lay3((3, GROUP)),
            lay3((1, GROUP)),
            lay3((1, GROUP)),
            lay4((HEADS, CHUNK, CHUNK)),
            lay4((HEADS, CHUNK, HEAD_DIM)),
        ],
        out_specs=[
            pl.BlockSpec((1, CHUNK, 4 * GROUP), lambda i, c: (i, c, 0)),
            pl.BlockSpec((1, POOL_BUF, GROUP), lambda i, c: (i, 0, 0)),
            pl.BlockSpec((1, 2, GROUP), lambda i, c: (i, 0, 0)),
            pl.BlockSpec((1, HEADS, HEAD_DIM, HEAD_DIM), lambda i, c: (i, 0, 0, 0)),
            pl.BlockSpec((1, CHUNK, GROUP), lambda i, c: (i, 0, 0)),
        ],
        out_shape=[
            jax.ShapeDtypeStruct((b, l, 4 * GROUP), BF16),
            jax.ShapeDtypeStruct((b, POOL_BUF, GROUP), F32),
            jax.ShapeDtypeStruct((b, 2, GROUP), F32),
            jax.ShapeDtypeStruct((b, HEADS, HEAD_DIM, HEAD_DIM), F32),
            jax.ShapeDtypeStruct((b, CHUNK, GROUP), F32),
        ],
        scratch_shapes=[
            pltpu.VMEM((_POOL_TOP + CHUNK, GROUP), F32),
            pltpu.VMEM((_CONV_TOP + CHUNK, GROUP), F32),
        ],
        compiler_params=pltpu.CompilerParams(
            dimension_semantics=("arbitrary", "arbitrary"),
            vmem_limit_bytes=32 * MIB),
        name="mixer_prompt",
    )(p, cos, sin, dmask, qdec, kdec, poolw, pscale, convw, retg, sgug, sguw, sgub)


_SEQ_BLOCK = 8


_N_SAMPLE_INPUTS = 14


def _mixer_sample_kernel(*refs, start):
    (p_ref, pool_ref, conv_ref, ret_ref, cos_ref, sin_ref, gam_ref,
     poolw_ref, pscale_ref, convw_ref, retg_ref, sgug_ref, sguw_ref, sgub_ref) = refs[:_N_SAMPLE_INPUTS]
    mix_ref, pool_out, conv_out, ret_out, sgu_out, qg_s, k_s, v_s, os_s = refs[-9:]
    TB = _SEQ_BLOCK

    a_all = p_ref[:, _A:_A + GROUP]
    for gi, win in enumerate(POOL_WINDOWS):
        lanes = slice(gi * HEAD_DIM, (gi + 1) * HEAD_DIM)
        a = a_all[:, lanes]
        wsum = a
        for j in range(1, win):
            off = (POOL_BUF - j) * GROUP + gi * HEAD_DIM
            wsum = wsum + pool_ref[:, off:off + HEAD_DIM]
        cnt = float(min(start + 1, win))
        d = (wsum / cnt - a).astype(BF16)
        y = _dot(d, poolw_ref[gi]) * pscale_ref[:, lanes]
        mix_ref[:, _YA + gi * HEAD_DIM:_YA + (gi + 1) * HEAD_DIM] = y.astype(BF16)
    pool_out[:, 0:(POOL_BUF - 1) * GROUP] = pool_ref[:, GROUP:POOL_BUF * GROUP]
    pool_out[:, (POOL_BUF - 1) * GROUP:POOL_BUF * GROUP] = a_all

    z = p_ref[:, _CG:_CG + GROUP] * p_ref[:, _HH:_HH + GROUP]
    acc = (conv_ref[:, 0:GROUP] * convw_ref[0:1, :] + conv_ref[:, GROUP:2 * GROUP] * convw_ref[1:2, :]
           + z * convw_ref[2:3, :])
    mix_ref[:, _YB:_YB + GROUP] = (p_ref[:, _BG:_BG + GROUP] * acc).astype(BF16)
    conv_out[:, 0:GROUP] = conv_ref[:, GROUP:2 * GROUP]
    conv_out[:, GROUP:2 * GROUP] = z

    cos = cos_ref[...]
    sin = sin_ref[...]
    scores = []
    for h in range(HEADS):
        lanes = slice(h * HEAD_DIM, (h + 1) * HEAD_DIM)
        q = _rotate(p_ref[:, _Q + h * HEAD_DIM:_Q + (h + 1) * HEAD_DIM], cos, sin)
        k = _rotate(p_ref[:, _K + h * HEAD_DIM:_K + (h + 1) * HEAD_DIM], cos, sin) * (HEAD_DIM ** -0.5)
        scores.append(jnp.sum(q * k, axis=-1, keepdims=True))
        qg_s[:, lanes] = q * gam_ref[:, lanes]
        k_s[:, lanes] = k
    v_s[...] = p_ref[:, _V:_V + GROUP]

    first_row = lax.broadcasted_iota(jnp.int32, (8, HEAD_DIM), 0) == 0

    for b in range(TB):
        for h in range(HEADS):
            lanes = slice(h * HEAD_DIM, (h + 1) * HEAD_DIM)
            s_prev = ret_ref[b, h]
            q8 = jnp.broadcast_to(qg_s[b:b + 1, lanes], (8, HEAD_DIM))
            os_s[b:b + 1, lanes] = _dot(q8.astype(BF16), s_prev.astype(BF16))[0:1, :]
            k8 = jnp.where(first_row, jnp.broadcast_to(k_s[b:b + 1, lanes], (8, HEAD_DIM)), 0.0)
            v8 = jnp.broadcast_to(v_s[b:b + 1, lanes], (8, HEAD_DIM))
            kv = lax.dot_general(k8.astype(BF16), v8.astype(BF16), (((0,), (0,)), ((), ())),
                                 preferred_element_type=F32)
            ret_out[b, h] = gam_ref[:, lanes] * s_prev + kv

    for h in range(HEADS):
        lanes = slice(h * HEAD_DIM, (h + 1) * HEAD_DIM)
        o = scores[h] * v_s[:, lanes] + os_s[:, lanes]
        on = _layernorm(o) * retg_ref[:, lanes]
        g = p_ref[:, _G + h * HEAD_DIM:_G + (h + 1) * HEAD_DIM]
        mix_ref[:, _YC + h * HEAD_DIM:_YC + (h + 1) * HEAD_DIM] = (_silu(g) * on).astype(BF16)

    vn = _layernorm(p_ref[:, _VV:_VV + GROUP]) * sgug_ref[...]
    mix_ref[:, _YD:_YD + GROUP] = (p_ref[:, _U:_U + GROUP] * (sguw_ref[...] * vn + sgub_ref[...])).astype(BF16)
    sgu_out[...] = vn


def mixer_sample(p, pool_state, conv_state, ret_state, tables, params, layer, stacked, *, start):
    n = p.shape[0]
    tb = _SEQ_BLOCK
    cos, sin, gam = tables
    poolw, pscale, convw, retg, sgug, sguw, sgub = params
    rows = lambda width: pl.BlockSpec((tb, width), lambda i: (i, 0))
    lay_rows = lambda width: pl.BlockSpec((None, tb, width), lambda i: (layer, i, 0))
    lay_ret = pl.BlockSpec((None, tb, HEADS, HEAD_DIM, HEAD_DIM), lambda i: (layer, i, 0, 0, 0))
    const2 = lambda shape: pl.BlockSpec(shape, lambda i: (0, 0))
    lay3 = lambda shape: pl.BlockSpec((None,) + shape, lambda i: (layer, 0, 0))
    lay4 = lambda shape: pl.BlockSpec((None,) + shape, lambda i: (layer, 0, 0, 0))
    in_specs = [
        rows(IN_WIDTH),
        lay_rows(POOL_BUF * GROUP),
        lay_rows(2 * GROUP),
        lay_ret,
        const2((1, HEAD_DIM)),
        const2((1, HEAD_DIM)),
        const2((1, GROUP)),
        lay4((len(POOL_WINDOWS), HEAD_DIM, HEAD_DIM)),
        lay3((1, GROUP)),
        lay3((3, GROUP)),
        lay3((1, GROUP)),
        lay3((1, GROUP)),
        lay3((1, GROUP)),
        lay3((1, GROUP)),
    ]
    args = (p, pool_state, conv_state, ret_state, cos, sin, gam,
            poolw, pscale, convw, retg, sgug, sguw, sgub)
    assert len(args) == _N_SAMPLE_INPUTS
    aliases = {}
    if stacked is not None:
        aliases = {len(args) + k: 1 + k for k in range(len(stacked))}
        in_specs = in_specs + [pl.BlockSpec(memory_space=pl.ANY)] * len(stacked)
        args = args + tuple(stacked)
    outs = pl.pallas_call(
        functools.partial(_mixer_sample_kernel, start=start),
        grid=(n // tb,),
        in_specs=in_specs,
        out_specs=[rows(4 * GROUP), lay_rows(POOL_BUF * GROUP), lay_rows(2 * GROUP), lay_ret,
                   lay_rows(GROUP)],
        out_shape=[
            jax.ShapeDtypeStruct((n, 4 * GROUP), BF16),
            jax.ShapeDtypeStruct((DEPTH, n, POOL_BUF * GROUP), F32),
            jax.ShapeDtypeStruct((DEPTH, n, 2 * GROUP), F32),
            jax.ShapeDtypeStruct((DEPTH, n, HEADS, HEAD_DIM, HEAD_DIM), F32),
            jax.ShapeDtypeStruct((DEPTH, n, GROUP), F32),
        ],
        scratch_shapes=[pltpu.VMEM((tb, GROUP), F32)] * 4,
        input_output_aliases=aliases,
        compiler_params=pltpu.CompilerParams(
            dimension_semantics=("arbitrary",),
            vmem_limit_bytes=32 * MIB),
        name="mixer_sample",
    )(*args)
    return outs[0], tuple(outs[1:])


def _rope_tables(start, length):
    half = HEAD_DIM // 2
    inv = ROPE_BASE ** (-jnp.arange(half, dtype=F32) / half)
    pos = start + jnp.arange(length)
    ang = pos.astype(F32)[:, None] * inv[None, :]
    cos, sin = jnp.cos(ang), jnp.sin(ang)
    return jnp.concatenate([cos, cos], axis=-1), jnp.concatenate([-sin, sin], axis=-1)


def _log_gamma():
    return jnp.log1p(-(2.0 ** (-5.0 - jnp.arange(HEADS, dtype=F32))))


def _decay_tables(c):
    lg = _log_gamma()
    idx = jnp.arange(c, dtype=F32)
    diff = idx[:, None] - idx[None, :]
    dmask = jnp.where(diff >= 0, jnp.exp(jnp.maximum(diff, 0.0)[None] * lg[:, None, None]), 0.0)
    qdec = jnp.exp((idx + 1.0)[None, :] * lg[:, None])[:, :, None]
    kdec = jnp.exp((c - 1.0 - idx)[None, :] * lg[:, None])[:, :, None]
    wide = lambda t: jnp.broadcast_to(t, (HEADS, c, HEAD_DIM))
    return dmask, wide(qdec), wide(kdec)


def _per_head_lanes(t):
    return jnp.repeat(t, HEAD_DIM, axis=-1)[..., None, :]


def kernel(x_prompt, x_sample, state_pool, state_conv, state_ret, norm1_g, w_in, pool_w, pool_scale,
           conv_w, ret_norm_g, sgu_norm_g, sgu_w, sgu_b, w_out, norm2_g, w_gate_up, w_down,
           final_norm_g):
    bp, lp, _ = x_prompt.shape
    ns = x_sample.shape[0]

    w_in_b = w_in.astype(BF16)
    w_out_b = w_out.astype(BF16)
    w_gu_b = w_gate_up.astype(BF16)
    w_dn_b = w_down.astype(BF16)
    pool_w_b = pool_w.astype(BF16)
    row3 = lambda t: t[:, None, :]
    norm1 = row3(norm1_g)
    norm2 = row3(norm2_g)
    final_g = final_norm_g[None, :]

    cos_p, sin_p = _rope_tables(0, lp)
    prompt_tables = (cos_p, sin_p) + _decay_tables(CHUNK)
    sgu_b_wide = jnp.broadcast_to(sgu_b[:, :, :, None], sgu_b.shape + (HEAD_DIM,))
    prompt_params = (pool_w_b, row3(pool_scale), conv_w, row3(ret_norm_g), row3(sgu_norm_g), sgu_w, sgu_b_wide)

    cos_s, sin_s = _rope_tables(PAST_LEN, 1)
    gam = _per_head_lanes(jnp.exp(_log_gamma()))
    sample_tables = (cos_s, sin_s, gam)
    sample_params = (pool_w_b, row3(pool_scale), conv_w, row3(ret_norm_g), row3(sgu_norm_g),
                     _per_head_lanes(sgu_w[:, :, 0, 0]), _per_head_lanes(sgu_b[:, :, 0]))
    pool_s2 = state_pool.reshape(DEPTH, ns, POOL_BUF * GROUP)
    conv_s2 = state_conv.reshape(DEPTH, ns, 2 * GROUP)

    hp = x_prompt.reshape(bp * lp, D_MODEL)
    hs = x_sample.reshape(ns, D_MODEL)
    prompt_states = [[] for _ in range(4)]
    sample_states = None
    for l in range(DEPTH):
        fin = l == DEPTH - 1
        p = norm_matmul(hp, norm1, w_in_b, l, tm=1024, tn=1024)
        mix, *states = mixer_prompt(p.reshape(bp, lp, IN_WIDTH), prompt_tables, prompt_params, l, start=0)
        hp = matmul_residual(mix.reshape(bp * lp, 4 * GROUP), w_out_b, hp, l, tm=512)
        hp = mlp(hp, norm2, w_gu_b, w_dn_b, final_g, l, tm=1024, tf=512, final_norm=fin)
        for lst, val in zip(prompt_states, states):
            lst.append(val)
        p = norm_matmul(hs, norm1, w_in_b, l, tm=ns, tn=1024)
        mix, sample_states = mixer_sample(p, pool_s2, conv_s2, state_ret, sample_tables, sample_params,
                                          l, sample_states, start=PAST_LEN)
        hs = matmul_residual(mix, w_out_b, hs, l, tm=ns)
        hs = mlp(hs, norm2, w_gu_b, w_dn_b, final_g, l, tm=ns, tf=512, final_norm=fin)
    pool_p, conv_p, ret_p, sgu_p = (jnp.stack(s) for s in prompt_states)
    pool_s, conv_s, ret_s, sgu_s = sample_states
    return (hp.reshape(bp, lp, D_MODEL), hs.reshape(ns, 1, D_MODEL),
            pool_p, pool_s.reshape(DEPTH, ns, POOL_BUF, GROUP),
            conv_p, conv_s.reshape(DEPTH, ns, 2, GROUP),
            ret_p, ret_s,
            sgu_p, sgu_s.reshape(DEPTH, ns, 1, GROUP))
```

```python
import functools

import jax
import jax.numpy as jnp
from jax import lax
from jax.experimental import pallas as pl
from jax.experimental.pallas import tpu as pltpu

F32 = jnp.float32
BF16 = jnp.bfloat16

D_MODEL = 2048
DEPTH = 2
GROUP = 512
N_SPLITS = 10
IN_WIDTH = N_SPLITS * GROUP
POOL_WINDOWS = (2, 4, 8, 16)
POOL_BUF = 15
HEADS = 4
HEAD_DIM = 128
CHUNK = 128
ROPE_BASE = 10000.0
D_FF = 5632
NORM_EPS = 1e-6
PAST_LEN = 16384

_A, _BG, _CG, _HH, _Q, _K, _V, _G, _U, _VV = (i * GROUP for i in range(N_SPLITS))
_YA, _YB, _YC, _YD = (i * GROUP for i in range(4))

MIB = 1024 * 1024
VMEM_BYTES_V7X = 64 * MIB


def _vmem_limit(estimate):
    return min(estimate + 4 * MIB, VMEM_BYTES_V7X - 4 * MIB)


def _rms(x, g):
    ms = jnp.mean(x * x, axis=-1, keepdims=True)
    return x * lax.rsqrt(ms + NORM_EPS) * g


def _layernorm(x):
    mu = jnp.mean(x, axis=-1, keepdims=True)
    xc = x - mu
    return xc * lax.rsqrt(jnp.mean(xc * xc, axis=-1, keepdims=True) + NORM_EPS)


def _silu(x):
    return x * jax.nn.sigmoid(x)


def _dot(a, b):
    return jnp.dot(a, b, preferred_element_type=F32)


def _rotate(x, cos, sin_signed):
    return x * cos + pltpu.roll(x, HEAD_DIM // 2, 1) * sin_signed


def _norm_matmul_kernel(x_ref, g_ref, w_ref, o_ref, xn_ref):
    @pl.when(pl.program_id(1) == 0)
    def _():
        xn_ref[...] = _rms(x_ref[...], g_ref[...]).astype(BF16)

    o_ref[...] = _dot(xn_ref[...], w_ref[...])


def norm_matmul(x, g, w, layer, *, tm, tn):
    m, k = x.shape
    n = w.shape[-1]
    return pl.pallas_call(
        _norm_matmul_kernel,
        grid=(m // tm, n // tn),
        in_specs=[
            pl.BlockSpec((tm, k), lambda i, j: (i, 0)),
            pl.BlockSpec((None, 1, k), lambda i, j: (layer, 0, 0)),
            pl.BlockSpec((None, k, tn), lambda i, j: (layer, 0, j)),
        ],
        out_specs=pl.BlockSpec((tm, tn), lambda i, j: (i, j)),
        out_shape=jax.ShapeDtypeStruct((m, n), F32),
        scratch_shapes=[pltpu.VMEM((tm, k), BF16)],
        compiler_params=pltpu.CompilerParams(
            dimension_semantics=("arbitrary", "arbitrary"),
            vmem_limit_bytes=48 * MIB),
        name="norm_matmul",
    )(x, g, w)


def _matmul_residual_kernel(a_ref, w_ref, r_ref, o_ref):
    o_ref[...] = r_ref[...] + _dot(a_ref[...], w_ref[...])


def matmul_residual(a, w, res, layer, *, tm):
    m, k = a.shape
    n = w.shape[-1]
    return pl.pallas_call(
        _matmul_residual_kernel,
        grid=(m // tm,),
        in_specs=[
            pl.BlockSpec((tm, k), lambda i: (i, 0)),
            pl.BlockSpec((None, k, n), lambda i: (layer, 0, 0)),
            pl.BlockSpec((tm, n), lambda i: (i, 0)),
        ],
        out_specs=pl.BlockSpec((tm, n), lambda i: (i, 0)),
        out_shape=jax.ShapeDtypeStruct((m, n), F32),
        compiler_params=pltpu.CompilerParams(
            dimension_semantics=("arbitrary",),
            vmem_limit_bytes=48 * MIB),
        name="matmul_residual",
    )(a, w, res)


def _mlp_kernel(x_ref, g_ref, wg_ref, wu_ref, wd_ref, fg_ref, o_ref, hn_ref, *, final_norm):
    f = pl.program_id(1)

    @pl.when(f == 0)
    def _():
        x = x_ref[...]
        hn_ref[...] = _rms(x, g_ref[...]).astype(BF16)
        o_ref[...] = x

    hn = hn_ref[...]
    act = (_silu(_dot(hn, wg_ref[...])) * _dot(hn, wu_ref[...])).astype(BF16)
    o_ref[...] += _dot(act, wd_ref[...])

    if final_norm:
        @pl.when(f == pl.num_programs(1) - 1)
        def _():
            o_ref[...] = _rms(o_ref[...], fg_ref[...])


def mlp(x, g, w_gate_up, w_down, final_g, layer, *, tm, tf, final_norm):
    m, d = x.shape
    nf = D_FF // tf
    vmem = 2 * 2 * tm * d * 4 + tm * d * 2 + 2 * 3 * d * tf * 2 + 3 * tm * tf * 4
    return pl.pallas_call(
        functools.partial(_mlp_kernel, final_norm=final_norm),
        grid=(m // tm, nf),
        in_specs=[
            pl.BlockSpec((tm, d), lambda i, f: (i, 0)),
            pl.BlockSpec((None, 1, d), lambda i, f: (layer, 0, 0)),
            pl.BlockSpec((None, d, tf), lambda i, f: (layer, 0, f)),
            pl.BlockSpec((None, d, tf), lambda i, f: (layer, 0, nf + f)),
            pl.BlockSpec((None, tf, d), lambda i, f: (layer, f, 0)),
            pl.BlockSpec((1, d), lambda i, f: (0, 0)),
        ],
        out_specs=pl.BlockSpec((tm, d), lambda i, f: (i, 0)),
        out_shape=jax.ShapeDtypeStruct((m, d), F32),
        scratch_shapes=[pltpu.VMEM((tm, d), BF16)],
        compiler_params=pltpu.CompilerParams(
            dimension_semantics=("arbitrary", "arbitrary"),
            vmem_limit_bytes=_vmem_limit(vmem)),
        name="mlp",
    )(x, g, w_gate_up, w_gate_up, w_down, final_g)


_POOL_TOP = 16
_CONV_TOP = 8


def _block_prompt_kernel(x_ref, g1_ref, win_ref, wout_ref, cos_ref, sin_ref, dmask_ref, qdec_ref, kdec_ref,
                         poolw_ref, pscale_ref, convw_ref, retg_ref, sgug_ref, sguw_ref, sgub_ref,
                         h_out, pool_out, conv_out, ret_out, sgu_out,
                         pool_ext, conv_ext, *, start, tm):
    c = pl.program_id(1)
    T = CHUNK
    n_chunks = tm // T

    @pl.when(c == 0)
    def _():
        pool_ext[0:_POOL_TOP, :] = jnp.zeros((_POOL_TOP, GROUP), F32)
        conv_ext[0:_CONV_TOP, :] = jnp.zeros((_CONV_TOP, GROUP), F32)
        ret_out[...] = jnp.zeros(ret_out.shape, F32)

    x = x_ref[0]
    xn = _rms(x, g1_ref[...]).astype(BF16)

    def proj(col):
        return _dot(xn, win_ref[:, col:col + GROUP])

    def out_proj(y, col):
        return _dot(y, wout_ref[col:col + GROUP, :])

    pool_ext[_POOL_TOP:_POOL_TOP + tm, :] = proj(_A)
    pos1 = lax.broadcasted_iota(jnp.int32, (tm, HEAD_DIM), 0) + (start + 1) + c * tm
    y_a = []
    for gi, win in enumerate(POOL_WINDOWS):
        lanes = slice(gi * HEAD_DIM, (gi + 1) * HEAD_DIM)
        a = pool_ext[_POOL_TOP:_POOL_TOP + tm, lanes]
        wsum = a
        for j in range(1, win):
            wsum = wsum + pool_ext[_POOL_TOP - j:_POOL_TOP - j + tm, lanes]
        cnt = jnp.minimum(pos1, win).astype(F32)
        d = (wsum / cnt - a).astype(BF16)
        y_a.append((_dot(d, poolw_ref[gi]) * pscale_ref[:, lanes]).astype(BF16))
    h = x + out_proj(jnp.concatenate(y_a, axis=-1), _YA)
    pool_out[0] = pool_ext[_POOL_TOP + tm - POOL_BUF:_POOL_TOP + tm, :]
    pool_ext[_POOL_TOP - POOL_BUF:_POOL_TOP, :] = pool_ext[_POOL_TOP + tm - POOL_BUF:_POOL_TOP + tm, :]

    z = proj(_CG) * proj(_HH)
    conv_ext[_CONV_TOP:_CONV_TOP + tm, :] = z
    acc = (conv_ext[_CONV_TOP - 2:_CONV_TOP - 2 + tm, :] * convw_ref[0:1, :]
           + conv_ext[_CONV_TOP - 1:_CONV_TOP - 1 + tm, :] * convw_ref[1:2, :]
           + z * convw_ref[2:3, :])
    h = h + out_proj((proj(_BG) * acc).astype(BF16), _YB)
    conv_out[0] = conv_ext[_CONV_TOP + tm - 2:_CONV_TOP + tm, :]
    conv_ext[_CONV_TOP - 2:_CONV_TOP, :] = conv_ext[_CONV_TOP + tm - 2:_CONV_TOP + tm, :]

    q_all, k_all, v_all, g_all = proj(_Q), proj(_K), proj(_V), proj(_G)
    y_c = []
    for j in range(n_chunks):
        rows = slice(j * T, (j + 1) * T)
        cos = cos_ref[rows, :]
        sin = sin_ref[rows, :]
        y_heads = []
        for hd in range(HEADS):
            lanes = slice(hd * HEAD_DIM, (hd + 1) * HEAD_DIM)
            q = _rotate(q_all[rows, lanes], cos, sin)
            k = _rotate(k_all[rows, lanes], cos, sin) * (HEAD_DIM ** -0.5)
            v = v_all[rows, lanes].astype(BF16)
            s_prev = ret_out[0, hd]
            scores = lax.dot_general(q.astype(BF16), k.astype(BF16), (((1,), (1,)), ((), ())),
                                     preferred_element_type=F32) * dmask_ref[hd]
            o = _dot(scores.astype(BF16), v) + _dot((q * qdec_ref[hd]).astype(BF16), s_prev.astype(BF16))
            kv = lax.dot_general((k * kdec_ref[hd]).astype(BF16), v, (((0,), (0,)), ((), ())),
                                 preferred_element_type=F32)
            ret_out[0, hd] = qdec_ref[hd, T - 1:T, :] * s_prev + kv
            on = _layernorm(o) * retg_ref[:, lanes]
            y_heads.append((_silu(g_all[rows, lanes]) * on).astype(BF16))
        y_c.append(jnp.concatenate(y_heads, axis=-1))
    h = h + out_proj(jnp.concatenate(y_c, axis=0), _YC)

    vn = _layernorm(proj(_VV)) * sgug_ref[...]
    u_all = proj(_U)
    tri = lax.broadcasted_iota(jnp.int32, (T, T), 0) >= lax.broadcasted_iota(jnp.int32, (T, T), 1)
    w_tril = [jnp.where(tri, sguw_ref[hd], 0.0).astype(BF16) for hd in range(HEADS)]
    y_d = []
    for j in range(n_chunks):
        rows = slice(j * T, (j + 1) * T)
        y_heads = []
        for hd in range(HEADS):
            lanes = slice(hd * HEAD_DIM, (hd + 1) * HEAD_DIM)
            mixed = _dot(w_tril[hd], vn[rows, lanes].astype(BF16)) + sgub_ref[hd]
            y_heads.append((u_all[rows, lanes] * mixed).astype(BF16))
        y_d.append(jnp.concatenate(y_heads, axis=-1))
    h_out[0] = h + out_proj(jnp.concatenate(y_d, axis=0), _YD)
    sgu_out[0] = vn[tm - T:tm, :]


def block_prompt(x, norm_g, w_in, w_out, tables, params, layer, *, start, tm):
    b, l, d = x.shape
    cos, sin, dmask, qdec, kdec = tables
    poolw, pscale, convw, retg, sgug, sguw, sgub = params
    const3 = lambda shape: pl.BlockSpec(shape, lambda i, c: (0, 0, 0))
    lay3 = lambda shape: pl.BlockSpec((None,) + shape, lambda i, c: (layer, 0, 0))
    lay4 = lambda shape: pl.BlockSpec((None,) + shape, lambda i, c: (layer, 0, 0, 0))
    resident = lambda shape: pl.BlockSpec((None,) + shape, lambda i, c: (layer, 0, 0),
                                          pipeline_mode=pl.Buffered(1))
    vmem = (d * IN_WIDTH + 4 * GROUP * d) * 2 + 2 * 2 * tm * d * 4 + 2 * tm * IN_WIDTH * 4
    return pl.pallas_call(
        functools.partial(_block_prompt_kernel, start=start, tm=tm),
        grid=(b, l // tm),
        in_specs=[
            pl.BlockSpec((1, tm, d), lambda i, c: (i, c, 0)),
            lay3((1, d)),
            resident((d, IN_WIDTH)),
            resident((4 * GROUP, d)),
            pl.BlockSpec((tm, HEAD_DIM), lambda i, c: (c, 0)),
            pl.BlockSpec((tm, HEAD_DIM), lambda i, c: (c, 0)),
            const3((HEADS, CHUNK, CHUNK)),
            const3((HEADS, CHUNK, HEAD_DIM)),
            const3((HEADS, CHUNK, HEAD_DIM)),
            lay4((len(POOL_WINDOWS), HEAD_DIM, HEAD_DIM)),
            lay3((1, GROUP)),
            lay3((3, GROUP)),
            lay3((1, GROUP)),
            lay3((1, GROUP)),
            lay4((HEADS, CHUNK, CHUNK)),
            lay4((HEADS, CHUNK, HEAD_DIM)),
        ],
        out_specs=[
            pl.BlockSpec((1, tm, d), lambda i, c: (i, c, 0)),
            pl.BlockSpec((1, POOL_BUF, GROUP), lambda i, c: (i, 0, 0)),
            pl.BlockSpec((1, 2, GROUP), lambda i, c: (i, 0, 0)),
            pl.BlockSpec((1, HEADS, HEAD_DIM, HEAD_DIM), lambda i, c: (i, 0, 0, 0)),
            pl.BlockSpec((1, CHUNK, GROUP), lambda i, c: (i, 0, 0)),
        ],
        out_shape=[
            jax.ShapeDtypeStruct((b, l, d), F32),
            jax.ShapeDtypeStruct((b, POOL_BUF, GROUP), F32),
            jax.ShapeDtypeStruct((b, 2, GROUP), F32),
            jax.ShapeDtypeStruct((b, HEADS, HEAD_DIM, HEAD_DIM), F32),
            jax.ShapeDtypeStruct((b, CHUNK, GROUP), F32),
        ],
        scratch_shapes=[
            pltpu.VMEM((_POOL_TOP + tm, GROUP), F32),
            pltpu.VMEM((_CONV_TOP + tm, GROUP), F32),
        ],
        compiler_params=pltpu.CompilerParams(
            dimension_semantics=("arbitrary", "arbitrary"),
            vmem_limit_bytes=_vmem_limit(vmem)),
        name="block_prompt",
    )(x, norm_g, w_in, w_out, cos, sin, dmask, qdec, kdec, poolw, pscale, convw, retg, sgug, sguw, sgub)


_SEQ_BLOCK = 8


_N_SAMPLE_INPUTS = 14


def _mixer_sample_kernel(*refs, start):
    (p_ref, pool_ref, conv_ref, ret_ref, cos_ref, sin_ref, gam_ref,
     poolw_ref, pscale_ref, convw_ref, retg_ref, sgug_ref, sguw_ref, sgub_ref) = refs[:_N_SAMPLE_INPUTS]
    mix_ref, pool_out, conv_out, ret_out, sgu_out, qg_s, k_s, v_s, os_s = refs[-9:]
    TB = _SEQ_BLOCK

    a_all = p_ref[:, _A:_A + GROUP]
    for gi, win in enumerate(POOL_WINDOWS):
        lanes = slice(gi * HEAD_DIM, (gi + 1) * HEAD_DIM)
        a = a_all[:, lanes]
        wsum = a
        for j in range(1, win):
            off = (POOL_BUF - j) * GROUP + gi * HEAD_DIM
            wsum = wsum + pool_ref[:, off:off + HEAD_DIM]
        cnt = float(min(start + 1, win))
        d = (wsum / cnt - a).astype(BF16)
        y = _dot(d, poolw_ref[gi]) * pscale_ref[:, lanes]
        mix_ref[:, _YA + gi * HEAD_DIM:_YA + (gi + 1) * HEAD_DIM] = y.astype(BF16)
    pool_out[:, 0:(POOL_BUF - 1) * GROUP] = pool_ref[:, GROUP:POOL_BUF * GROUP]
    pool_out[:, (POOL_BUF - 1) * GROUP:POOL_BUF * GROUP] = a_all

    z = p_ref[:, _CG:_CG + GROUP] * p_ref[:, _HH:_HH + GROUP]
    acc = (conv_ref[:, 0:GROUP] * convw_ref[0:1, :] + conv_ref[:, GROUP:2 * GROUP] * convw_ref[1:2, :]
           + z * convw_ref[2:3, :])
    mix_ref[:, _YB:_YB + GROUP] = (p_ref[:, _BG:_BG + GROUP] * acc).astype(BF16)
    conv_out[:, 0:GROUP] = conv_ref[:, GROUP:2 * GROUP]
    conv_out[:, GROUP:2 * GROUP] = z

    cos = cos_ref[...]
    sin = sin_ref[...]
    scores = []
    for h in range(HEADS):
        lanes = slice(h * HEAD_DIM, (h + 1) * HEAD_DIM)
        q = _rotate(p_ref[:, _Q + h * HEAD_DIM:_Q + (h + 1) * HEAD_DIM], cos, sin)
        k = _rotate(p_ref[:, _K + h * HEAD_DIM:_K + (h + 1) * HEAD_DIM], cos, sin) * (HEAD_DIM ** -0.5)
        scores.append(jnp.sum(q * k, axis=-1, keepdims=True))
        qg_s[:, lanes] = q * gam_ref[:, lanes]
        k_s[:, lanes] = k
    v_s[...] = p_ref[:, _V:_V + GROUP]

    first_row = lax.broadcasted_iota(jnp.int32, (8, HEAD_DIM), 0) == 0

    for b in range(TB):
        for h in range(HEADS):
            lanes = slice(h * HEAD_DIM, (h + 1) * HEAD_DIM)
            s_prev = ret_ref[b, h]
            q8 = jnp.broadcast_to(qg_s[b:b + 1, lanes], (8, HEAD_DIM))
            os_s[b:b + 1, lanes] = _dot(q8.astype(BF16), s_prev.astype(BF16))[0:1, :]
            k8 = jnp.where(first_row, jnp.broadcast_to(k_s[b:b + 1, lanes], (8, HEAD_DIM)), 0.0)
            v8 = jnp.broadcast_to(v_s[b:b + 1, lanes], (8, HEAD_DIM))
            kv = lax.dot_general(k8.astype(BF16), v8.astype(BF16), (((0,), (0,)), ((), ())),
                                 preferred_element_type=F32)
            ret_out[b, h] = gam_ref[:, lanes] * s_prev + kv

    for h in range(HEADS):
        lanes = slice(h * HEAD_DIM, (h + 1) * HEAD_DIM)
        o = scores[h] * v_s[:, lanes] + os_s[:, lanes]
        on = _layernorm(o) * retg_ref[:, lanes]
        g = p_ref[:, _G + h * HEAD_DIM:_G + (h + 1) * HEAD_DIM]
        mix_ref[:, _YC + h * HEAD_DIM:_YC + (h + 1) * HEAD_DIM] = (_silu(g) * on).astype(BF16)

    vn = _layernorm(p_ref[:, _VV:_VV + GROUP]) * sgug_ref[...]
    mix_ref[:, _YD:_YD + GROUP] = (p_ref[:, _U:_U + GROUP] * (sguw_ref[...] * vn + sgub_ref[...])).astype(BF16)
    sgu_out[...] = vn


def mixer_sample(p, pool_state, conv_state, ret_state, tables, params, layer, stacked, *, start):
    n = p.shape[0]
    tb = _SEQ_BLOCK
    cos, sin, gam = tables
    poolw, pscale, convw, retg, sgug, sguw, sgub = params
    rows = lambda width: pl.BlockSpec((tb, width), lambda i: (i, 0))
    lay_rows = lambda width: pl.BlockSpec((None, tb, width), lambda i: (layer, i, 0))
    lay_ret = pl.BlockSpec((None, tb, HEADS, HEAD_DIM, HEAD_DIM), lambda i: (layer, i, 0, 0, 0))
    const2 = lambda shape: pl.BlockSpec(shape, lambda i: (0, 0))
    lay3 = lambda shape: pl.BlockSpec((None,) + shape, lambda i: (layer, 0, 0))
    lay4 = lambda shape: pl.BlockSpec((None,) + shape, lambda i: (layer, 0, 0, 0))
    in_specs = [
        rows(IN_WIDTH),
        lay_rows(POOL_BUF * GROUP),
        lay_rows(2 * GROUP),
        lay_ret,
        const2((1, HEAD_DIM)),
        const2((1, HEAD_DIM)),
        const2((1, GROUP)),
        lay4((len(POOL_WINDOWS), HEAD_DIM, HEAD_DIM)),
        lay3((1, GROUP)),
        lay3((3, GROUP)),
        lay3((1, GROUP)),
        lay3((1, GROUP)),
        lay3((1, GROUP)),
        lay3((1, GROUP)),
    ]
    args = (p, pool_state, conv_state, ret_state, cos, sin, gam,
            poolw, pscale, convw, retg, sgug, sguw, sgub)
    assert len(args) == _N_SAMPLE_INPUTS
    aliases = {}
    if stacked is not None:
        aliases = {len(args) + k: 1 + k for k in range(len(stacked))}
        in_specs = in_specs + [pl.BlockSpec(memory_space=pl.ANY)] * len(stacked)
        args = args + tuple(stacked)
    outs = pl.pallas_call(
        functools.partial(_mixer_sample_kernel, start=start),
        grid=(n // tb,),
        in_specs=in_specs,
        out_specs=[rows(4 * GROUP), lay_rows(POOL_BUF * GROUP), lay_rows(2 * GROUP), lay_ret,
                   lay_rows(GROUP)],
        out_shape=[
            jax.ShapeDtypeStruct((n, 4 * GROUP), BF16),
            jax.ShapeDtypeStruct((DEPTH, n, POOL_BUF * GROUP), F32),
            jax.ShapeDtypeStruct((DEPTH, n, 2 * GROUP), F32),
            jax.ShapeDtypeStruct((DEPTH, n, HEADS, HEAD_DIM, HEAD_DIM), F32),
            jax.ShapeDtypeStruct((DEPTH, n, GROUP), F32),
        ],
        scratch_shapes=[pltpu.VMEM((tb, GROUP), F32)] * 4,
        input_output_aliases=aliases,
        compiler_params=pltpu.CompilerParams(
            dimension_semantics=("arbitrary",),
            vmem_limit_bytes=32 * MIB),
        name="mixer_sample",
    )(*args)
    return outs[0], tuple(outs[1:])


def _rope_tables(start, length):
    half = HEAD_DIM // 2
    inv = ROPE_BASE ** (-jnp.arange(half, dtype=F32) / half)
    pos = start + jnp.arange(length)
    ang = pos.astype(F32)[:, None] * inv[None, :]
    cos, sin = jnp.cos(ang), jnp.sin(ang)
    return jnp.concatenate([cos, cos], axis=-1), jnp.concatenate([-sin, sin], axis=-1)


def _log_gamma():
    return jnp.log1p(-(2.0 ** (-5.0 - jnp.arange(HEADS, dtype=F32))))


def _decay_tables(c):
    lg = _log_gamma()
    idx = jnp.arange(c, dtype=F32)
    diff = idx[:, None] - idx[None, :]
    dmask = jnp.where(diff >= 0, jnp.exp(jnp.maximum(diff, 0.0)[None] * lg[:, None, None]), 0.0)
    qdec = jnp.exp((idx + 1.0)[None, :] * lg[:, None])[:, :, None]
    kdec = jnp.exp((c - 1.0 - idx)[None, :] * lg[:, None])[:, :, None]
    wide = lambda t: jnp.broadcast_to(t, (HEADS, c, HEAD_DIM))
    return dmask, wide(qdec), wide(kdec)


def _per_head_lanes(t):
    return jnp.repeat(t, HEAD_DIM, axis=-1)[..., None, :]


def kernel(x_prompt, x_sample, state_pool, state_conv, state_ret, norm1_g, w_in, pool_w, pool_scale,
           conv_w, ret_norm_g, sgu_norm_g, sgu_w, sgu_b, w_out, norm2_g, w_gate_up, w_down,
           final_norm_g):
    bp, lp, _ = x_prompt.shape
    ns = x_sample.shape[0]

    w_in_b = w_in.astype(BF16)
    w_out_b = w_out.astype(BF16)
    w_gu_b = w_gate_up.astype(BF16)
    w_dn_b = w_down.astype(BF16)
    pool_w_b = pool_w.astype(BF16)
    row3 = lambda t: t[:, None, :]
    norm1 = row3(norm1_g)
    norm2 = row3(norm2_g)
    final_g = final_norm_g[None, :]

    cos_p, sin_p = _rope_tables(0, lp)
    prompt_tables = (cos_p, sin_p) + _decay_tables(CHUNK)
    sgu_b_wide = jnp.broadcast_to(sgu_b[:, :, :, None], sgu_b.shape + (HEAD_DIM,))
    prompt_params = (pool_w_b, row3(pool_scale), conv_w, row3(ret_norm_g), row3(sgu_norm_g), sgu_w, sgu_b_wide)

    cos_s, sin_s = _rope_tables(PAST_LEN, 1)
    gam = _per_head_lanes(jnp.exp(_log_gamma()))
    sample_tables = (cos_s, sin_s, gam)
    sample_params = (pool_w_b, row3(pool_scale), conv_w, row3(ret_norm_g), row3(sgu_norm_g),
                     _per_head_lanes(sgu_w[:, :, 0, 0]), _per_head_lanes(sgu_b[:, :, 0]))
    pool_s2 = state_pool.reshape(DEPTH, ns, POOL_BUF * GROUP)
    conv_s2 = state_conv.reshape(DEPTH, ns, 2 * GROUP)

    hp = x_prompt.reshape(bp * lp, D_MODEL)
    hs = x_sample.reshape(ns, D_MODEL)
    prompt_states = [[] for _ in range(4)]
    sample_states = None
    for l in range(DEPTH):
        fin = l == DEPTH - 1
        hp, *states = block_prompt(hp.reshape(bp, lp, D_MODEL), norm1, w_in_b, w_out_b, prompt_tables,
                                   prompt_params, l, start=0, tm=256)
        hp = mlp(hp.reshape(bp * lp, D_MODEL), norm2, w_gu_b, w_dn_b, final_g, l, tm=1024, tf=512,
                 final_norm=fin)
        for lst, val in zip(prompt_states, states):
            lst.append(val)
        p = norm_matmul(hs, norm1, w_in_b, l, tm=ns, tn=1024)
        mix, sample_states = mixer_sample(p, pool_s2, conv_s2, state_ret, sample_tables, sample_params,
                                          l, sample_states, start=PAST_LEN)
        hs = matmul_residual(mix, w_out_b, hs, l, tm=ns)
        hs = mlp(hs, norm2, w_gu_b, w_dn_b, final_g, l, tm=ns, tf=512, final_norm=fin)
    pool_p, conv_p, ret_p, sgu_p = (jnp.stack(s) for s in prompt_states)
    pool_s, conv_s, ret_s, sgu_s = sample_states
    return (hp.reshape(bp, lp, D_MODEL), hs.reshape(ns, 1, D_MODEL),
            pool_p, pool_s.reshape(DEPTH, ns, POOL_BUF, GROUP),
            conv_p, conv_s.reshape(DEPTH, ns, 2, GROUP),
            ret_p, ret_s,
            sgu_p, sgu_s.reshape(DEPTH, ns, 1, GROUP))
```

```python
import functools

import jax
import jax.numpy as jnp
from jax import lax
from jax.experimental import pallas as pl
from jax.experimental.pallas import tpu as pltpu

F32 = jnp.float32
BF16 = jnp.bfloat16

D_MODEL = 2048
DEPTH = 2
GROUP = 512
N_SPLITS = 10
IN_WIDTH = N_SPLITS * GROUP
POOL_WINDOWS = (2, 4, 8, 16)
POOL_BUF = 15
HEADS = 4
HEAD_DIM = 128
CHUNK = 128
ROPE_BASE = 10000.0
D_FF = 5632
NORM_EPS = 1e-6
PAST_LEN = 16384

_A, _BG, _CG, _HH, _Q, _K, _V, _G, _U, _VV = (i * GROUP for i in range(N_SPLITS))
_YA, _YB, _YC, _YD = (i * GROUP for i in range(4))

MIB = 1024 * 1024
VMEM_BYTES_V7X = 64 * MIB


def _vmem_limit(estimate):
    return min(estimate + 4 * MIB, VMEM_BYTES_V7X - 4 * MIB)


def _rms(x, g):
    ms = jnp.mean(x * x, axis=-1, keepdims=True)
    return x * lax.rsqrt(ms + NORM_EPS) * g


def _layernorm(x):
    mu = jnp.mean(x, axis=-1, keepdims=True)
    xc = x - mu
    return xc * lax.rsqrt(jnp.mean(xc * xc, axis=-1, keepdims=True) + NORM_EPS)


def _silu(x):
    return x * jax.nn.sigmoid(x)


def _dot(a, b):
    return jnp.dot(a, b, preferred_element_type=F32)


def _rotate(x, cos, sin_signed):
    return x * cos + pltpu.roll(x, HEAD_DIM // 2, 1) * sin_signed


def _norm_matmul_kernel(x_ref, g_ref, w_ref, o_ref, wb_ref, xn_ref):
    @pl.when(pl.program_id(0) == 0)
    def _():
        xn_ref[...] = _rms(x_ref[...], g_ref[...]).astype(BF16)

    wb = w_ref[...].astype(BF16)
    wb_ref[...] = wb
    o_ref[...] = _dot(xn_ref[...], wb)


def norm_matmul(x, g, w, layer, *, tn):
    m, k = x.shape
    n = w.shape[-1]
    vmem = 2 * k * tn * (4 + 2) + 2 * m * (k + tn) * 4 + m * k * 2
    return pl.pallas_call(
        _norm_matmul_kernel,
        grid=(n // tn,),
        in_specs=[
            pl.BlockSpec((m, k), lambda j: (0, 0)),
            pl.BlockSpec((None, 1, k), lambda j: (layer, 0, 0)),
            pl.BlockSpec((None, k, tn), lambda j: (layer, 0, j)),
        ],
        out_specs=[pl.BlockSpec((m, tn), lambda j: (0, j)),
                   pl.BlockSpec((k, tn), lambda j: (0, j))],
        out_shape=[jax.ShapeDtypeStruct((m, n), F32),
                   jax.ShapeDtypeStruct((k, n), BF16)],
        scratch_shapes=[pltpu.VMEM((m, k), BF16)],
        compiler_params=pltpu.CompilerParams(
            dimension_semantics=("arbitrary",),
            vmem_limit_bytes=_vmem_limit(vmem)),
        name="norm_matmul",
    )(x, g, w)


def _matmul_residual_kernel(a_ref, w_ref, r_ref, o_ref, wb_ref):
    wb = w_ref[...].astype(BF16)
    wb_ref[...] = wb
    o_ref[...] = r_ref[...] + _dot(a_ref[...], wb)


def matmul_residual(a, w, res, layer, *, tn):
    m, k = a.shape
    n = w.shape[-1]
    vmem = 2 * k * tn * (4 + 2) + 2 * m * (k * 2 + 2 * tn * 4)
    return pl.pallas_call(
        _matmul_residual_kernel,
        grid=(n // tn,),
        in_specs=[
            pl.BlockSpec((m, k), lambda j: (0, 0)),
            pl.BlockSpec((None, k, tn), lambda j: (layer, 0, j)),
            pl.BlockSpec((m, tn), lambda j: (0, j)),
        ],
        out_specs=[pl.BlockSpec((m, tn), lambda j: (0, j)),
                   pl.BlockSpec((k, tn), lambda j: (0, j))],
        out_shape=[jax.ShapeDtypeStruct((m, n), F32),
                   jax.ShapeDtypeStruct((k, n), BF16)],
        compiler_params=pltpu.CompilerParams(
            dimension_semantics=("arbitrary",),
            vmem_limit_bytes=_vmem_limit(vmem)),
        name="matmul_residual",
    )(a, w, res)


def _mlp_kernel(x_ref, g_ref, wg_ref, wu_ref, wd_ref, fg_ref, o_ref, *rest, final_norm, emit_bf16):
    hn_ref = rest[-1]
    f = pl.program_id(1)

    @pl.when(f == 0)
    def _():
        x = x_ref[...]
        hn_ref[...] = _rms(x, g_ref[...]).astype(BF16)
        o_ref[...] = x

    wg, wu, wd = wg_ref[...], wu_ref[...], wd_ref[...]
    if emit_bf16:
        wg, wu, wd = wg.astype(BF16), wu.astype(BF16), wd.astype(BF16)
        rest[0][...] = wg
        rest[1][...] = wu
        rest[2][...] = wd
    hn = hn_ref[...]
    act = (_silu(_dot(hn, wg)) * _dot(hn, wu)).astype(BF16)
    o_ref[...] += _dot(act, wd)

    if final_norm:
        @pl.when(f == pl.num_programs(1) - 1)
        def _():
            o_ref[...] = _rms(o_ref[...], fg_ref[...])


def mlp(x, g, weights, final_g, layer, *, tm, tf, final_norm):
    m, d = x.shape
    nf = D_FF // tf
    emit_bf16 = len(weights) == 2
    if emit_bf16:
        assert m == tm
        w_gate_up, w_down = weights
        w_args = (w_gate_up, w_gate_up, w_down)
        w_specs = [
            pl.BlockSpec((None, d, tf), lambda i, f: (layer, 0, f)),
            pl.BlockSpec((None, d, tf), lambda i, f: (layer, 0, nf + f)),
            pl.BlockSpec((None, tf, d), lambda i, f: (layer, f, 0)),
        ]
        w_bytes = 4 + 2
    else:
        w_args = weights
        w_specs = [
            pl.BlockSpec((d, tf), lambda i, f: (0, f)),
            pl.BlockSpec((d, tf), lambda i, f: (0, f)),
            pl.BlockSpec((tf, d), lambda i, f: (f, 0)),
        ]
        w_bytes = 2
    out_specs = [pl.BlockSpec((tm, d), lambda i, f: (i, 0))]
    out_shape = [jax.ShapeDtypeStruct((m, d), F32)]
    if emit_bf16:
        out_specs += [pl.BlockSpec((d, tf), lambda i, f: (0, f)),
                      pl.BlockSpec((d, tf), lambda i, f: (0, f)),
                      pl.BlockSpec((tf, d), lambda i, f: (f, 0))]
        out_shape += [jax.ShapeDtypeStruct((d, D_FF), BF16), jax.ShapeDtypeStruct((d, D_FF), BF16),
                      jax.ShapeDtypeStruct((D_FF, d), BF16)]
    vmem = 2 * 2 * tm * d * 4 + tm * d * 2 + 2 * 3 * d * tf * w_bytes + 3 * tm * tf * 4
    outs = pl.pallas_call(
        functools.partial(_mlp_kernel, final_norm=final_norm, emit_bf16=emit_bf16),
        grid=(m // tm, nf),
        in_specs=[
            pl.BlockSpec((tm, d), lambda i, f: (i, 0)),
            pl.BlockSpec((None, 1, d), lambda i, f: (layer, 0, 0)),
            *w_specs,
            pl.BlockSpec((1, d), lambda i, f: (0, 0)),
        ],
        out_specs=out_specs,
        out_shape=out_shape,
        scratch_shapes=[pltpu.VMEM((tm, d), BF16)],
        compiler_params=pltpu.CompilerParams(
            dimension_semantics=("arbitrary", "arbitrary"),
            vmem_limit_bytes=_vmem_limit(vmem)),
        name="mlp",
    )(x, g, *w_args, final_g)
    return (outs[0], tuple(outs[1:])) if emit_bf16 else outs[0]


_POOL_TOP = 16
_CONV_TOP = 8


def _block_prompt_kernel(x_ref, g1_ref, win_ref, wout_ref, cos_ref, sin_ref, dmask_ref, qdec_ref, kdec_ref,
                         poolw_ref, pscale_ref, convw_ref, retg_ref, sgug_ref, sguw_ref, sgub_ref,
                         h_out, pool_out, conv_out, ret_out, sgu_out,
                         pool_ext, conv_ext, *, start, tm):
    c = pl.program_id(1)
    T = CHUNK
    n_chunks = tm // T

    @pl.when(c == 0)
    def _():
        pool_ext[0:_POOL_TOP, :] = jnp.zeros((_POOL_TOP, GROUP), F32)
        conv_ext[0:_CONV_TOP, :] = jnp.zeros((_CONV_TOP, GROUP), F32)
        ret_out[...] = jnp.zeros(ret_out.shape, F32)

    x = x_ref[0]
    xn = _rms(x, g1_ref[...]).astype(BF16)

    def proj(col):
        return _dot(xn, win_ref[:, col:col + GROUP])

    def out_proj(y, col):
        return _dot(y, wout_ref[col:col + GROUP, :])

    pool_ext[_POOL_TOP:_POOL_TOP + tm, :] = proj(_A)
    pos1 = lax.broadcasted_iota(jnp.int32, (tm, HEAD_DIM), 0) + (start + 1) + c * tm
    y_a = []
    for gi, win in enumerate(POOL_WINDOWS):
        lanes = slice(gi * HEAD_DIM, (gi + 1) * HEAD_DIM)
        a = pool_ext[_POOL_TOP:_POOL_TOP + tm, lanes]
        wsum = a
        for j in range(1, win):
            wsum = wsum + pool_ext[_POOL_TOP - j:_POOL_TOP - j + tm, lanes]
        cnt = jnp.minimum(pos1, win).astype(F32)
        d = (wsum / cnt - a).astype(BF16)
        y_a.append((_dot(d, poolw_ref[gi]) * pscale_ref[:, lanes]).astype(BF16))
    h = x + out_proj(jnp.concatenate(y_a, axis=-1), _YA)
    pool_out[0] = pool_ext[_POOL_TOP + tm - POOL_BUF:_POOL_TOP + tm, :]
    pool_ext[_POOL_TOP - POOL_BUF:_POOL_TOP, :] = pool_ext[_POOL_TOP + tm - POOL_BUF:_POOL_TOP + tm, :]

    z = proj(_CG) * proj(_HH)
    conv_ext[_CONV_TOP:_CONV_TOP + tm, :] = z
    acc = (conv_ext[_CONV_TOP - 2:_CONV_TOP - 2 + tm, :] * convw_ref[0:1, :]
           + conv_ext[_CONV_TOP - 1:_CONV_TOP - 1 + tm, :] * convw_ref[1:2, :]
           + z * convw_ref[2:3, :])
    h = h + out_proj((proj(_BG) * acc).astype(BF16), _YB)
    conv_out[0] = conv_ext[_CONV_TOP + tm - 2:_CONV_TOP + tm, :]
    conv_ext[_CONV_TOP - 2:_CONV_TOP, :] = conv_ext[_CONV_TOP + tm - 2:_CONV_TOP + tm, :]

    q_all, k_all, v_all, g_all = proj(_Q), proj(_K), proj(_V), proj(_G)
    y_c = []
    for j in range(n_chunks):
        rows = slice(j * T, (j + 1) * T)
        cos = cos_ref[rows, :]
        sin = sin_ref[rows, :]
        y_heads = []
        for hd in range(HEADS):
            lanes = slice(hd * HEAD_DIM, (hd + 1) * HEAD_DIM)
            q = _rotate(q_all[rows, lanes], cos, sin)
            k = _rotate(k_all[rows, lanes], cos, sin) * (HEAD_DIM ** -0.5)
            v = v_all[rows, lanes].astype(BF16)
            s_prev = ret_out[0, hd]
            scores = lax.dot_general(q.astype(BF16), k.astype(BF16), (((1,), (1,)), ((), ())),
                                     preferred_element_type=F32) * dmask_ref[hd]
            o = _dot(scores.astype(BF16), v) + _dot((q * qdec_ref[hd]).astype(BF16), s_prev.astype(BF16))
            kv = lax.dot_general((k * kdec_ref[hd]).astype(BF16), v, (((0,), (0,)), ((), ())),
                                 preferred_element_type=F32)
            ret_out[0, hd] = qdec_ref[hd, T - 1:T, :] * s_prev + kv
            on = _layernorm(o) * retg_ref[:, lanes]
            y_heads.append((_silu(g_all[rows, lanes]) * on).astype(BF16))
        y_c.append(jnp.concatenate(y_heads, axis=-1))
    h = h + out_proj(jnp.concatenate(y_c, axis=0), _YC)

    vn = _layernorm(proj(_VV)) * sgug_ref[...]
    u_all = proj(_U)
    tri = lax.broadcasted_iota(jnp.int32, (T, T), 0) >= lax.broadcasted_iota(jnp.int32, (T, T), 1)
    w_tril = [jnp.where(tri, sguw_ref[hd], 0.0).astype(BF16) for hd in range(HEADS)]
    y_d = []
    for j in range(n_chunks):
        rows = slice(j * T, (j + 1) * T)
        y_heads = []
        for hd in range(HEADS):
            lanes = slice(hd * HEAD_DIM, (hd + 1) * HEAD_DIM)
            mixed = _dot(w_tril[hd], vn[rows, lanes].astype(BF16)) + sgub_ref[hd]
            y_heads.append((u_all[rows, lanes] * mixed).astype(BF16))
        y_d.append(jnp.concatenate(y_heads, axis=-1))
    h_out[0] = h + out_proj(jnp.concatenate(y_d, axis=0), _YD)
    sgu_out[0] = vn[tm - T:tm, :]


def block_prompt(x, norm_g, w_in, w_out, tables, params, layer, *, start, tm):
    b, l, d = x.shape
    cos, sin, dmask, qdec, kdec = tables
    poolw, pscale, convw, retg, sgug, sguw, sgub = params
    const3 = lambda shape: pl.BlockSpec(shape, lambda i, c: (0, 0, 0))
    lay3 = lambda shape: pl.BlockSpec((None,) + shape, lambda i, c: (layer, 0, 0))
    lay4 = lambda shape: pl.BlockSpec((None,) + shape, lambda i, c: (layer, 0, 0, 0))
    resident = lambda shape: pl.BlockSpec(shape, lambda i, c: (0, 0), pipeline_mode=pl.Buffered(1))
    vmem = (d * IN_WIDTH + 4 * GROUP * d) * 2 + 2 * 2 * tm * d * 4 + 2 * tm * IN_WIDTH * 4
    return pl.pallas_call(
        functools.partial(_block_prompt_kernel, start=start, tm=tm),
        grid=(b, l // tm),
        in_specs=[
            pl.BlockSpec((1, tm, d), lambda i, c: (i, c, 0)),
            lay3((1, d)),
            resident((d, IN_WIDTH)),
            resident((4 * GROUP, d)),
            pl.BlockSpec((tm, HEAD_DIM), lambda i, c: (c, 0)),
            pl.BlockSpec((tm, HEAD_DIM), lambda i, c: (c, 0)),
            const3((HEADS, CHUNK, CHUNK)),
            const3((HEADS, CHUNK, HEAD_DIM)),
            const3((HEADS, CHUNK, HEAD_DIM)),
            lay4((len(POOL_WINDOWS), HEAD_DIM, HEAD_DIM)),
            lay3((1, GROUP)),
            lay3((3, GROUP)),
            lay3((1, GROUP)),
            lay3((1, GROUP)),
            lay4((HEADS, CHUNK, CHUNK)),
            lay4((HEADS, CHUNK, HEAD_DIM)),
        ],
        out_specs=[
            pl.BlockSpec((1, tm, d), lambda i, c: (i, c, 0)),
            pl.BlockSpec((1, POOL_BUF, GROUP), lambda i, c: (i, 0, 0)),
            pl.BlockSpec((1, 2, GROUP), lambda i, c: (i, 0, 0)),
            pl.BlockSpec((1, HEADS, HEAD_DIM, HEAD_DIM), lambda i, c: (i, 0, 0, 0)),
            pl.BlockSpec((1, CHUNK, GROUP), lambda i, c: (i, 0, 0)),
        ],
        out_shape=[
            jax.ShapeDtypeStruct((b, l, d), F32),
            jax.ShapeDtypeStruct((b, POOL_BUF, GROUP), F32),
            jax.ShapeDtypeStruct((b, 2, GROUP), F32),
            jax.ShapeDtypeStruct((b, HEADS, HEAD_DIM, HEAD_DIM), F32),
            jax.ShapeDtypeStruct((b, CHUNK, GROUP), F32),
        ],
        scratch_shapes=[
            pltpu.VMEM((_POOL_TOP + tm, GROUP), F32),
            pltpu.VMEM((_CONV_TOP + tm, GROUP), F32),
        ],
        compiler_params=pltpu.CompilerParams(
            dimension_semantics=("arbitrary", "arbitrary"),
            vmem_limit_bytes=_vmem_limit(vmem)),
        name="block_prompt",
    )(x, norm_g, w_in, w_out, cos, sin, dmask, qdec, kdec, poolw, pscale, convw, retg, sgug, sguw, sgub)


_SEQ_BLOCK = 8


_N_SAMPLE_INPUTS = 14


def _mixer_sample_kernel(*refs, start):
    (p_ref, pool_ref, conv_ref, ret_ref, cos_ref, sin_ref, gam_ref,
     poolw_ref, pscale_ref, convw_ref, retg_ref, sgug_ref, sguw_ref, sgub_ref) = refs[:_N_SAMPLE_INPUTS]
    mix_ref, pool_out, conv_out, ret_out, sgu_out, qg_s, k_s, v_s, os_s = refs[-9:]
    TB = _SEQ_BLOCK

    a_all = p_ref[:, _A:_A + GROUP]
    for gi, win in enumerate(POOL_WINDOWS):
        lanes = slice(gi * HEAD_DIM, (gi + 1) * HEAD_DIM)
        a = a_all[:, lanes]
        wsum = a
        for j in range(1, win):
            wsum = wsum + pool_ref[:, POOL_BUF - j, lanes]
        cnt = float(min(start + 1, win))
        d = (wsum / cnt - a).astype(BF16)
        y = _dot(d, poolw_ref[gi]) * pscale_ref[:, lanes]
        mix_ref[:, _YA + gi * HEAD_DIM:_YA + (gi + 1) * HEAD_DIM] = y.astype(BF16)
    pool_out[:, 0:POOL_BUF - 1, :] = pool_ref[:, 1:POOL_BUF, :]
    pool_out[:, POOL_BUF - 1, :] = a_all

    z = p_ref[:, _CG:_CG + GROUP] * p_ref[:, _HH:_HH + GROUP]
    acc = (conv_ref[:, 0, :] * convw_ref[0:1, :] + conv_ref[:, 1, :] * convw_ref[1:2, :]
           + z * convw_ref[2:3, :])
    mix_ref[:, _YB:_YB + GROUP] = (p_ref[:, _BG:_BG + GROUP] * acc).astype(BF16)
    conv_out[:, 0, :] = conv_ref[:, 1, :]
    conv_out[:, 1, :] = z

    cos = cos_ref[...]
    sin = sin_ref[...]
    scores = []
    for h in range(HEADS):
        lanes = slice(h * HEAD_DIM, (h + 1) * HEAD_DIM)
        q = _rotate(p_ref[:, _Q + h * HEAD_DIM:_Q + (h + 1) * HEAD_DIM], cos, sin)
        k = _rotate(p_ref[:, _K + h * HEAD_DIM:_K + (h + 1) * HEAD_DIM], cos, sin) * (HEAD_DIM ** -0.5)
        scores.append(jnp.sum(q * k, axis=-1, keepdims=True))
        qg_s[:, lanes] = q * gam_ref[:, lanes]
        k_s[:, lanes] = k
    v_s[...] = p_ref[:, _V:_V + GROUP]

    first_row = lax.broadcasted_iota(jnp.int32, (8, HEAD_DIM), 0) == 0

    for b in range(TB):
        for h in range(HEADS):
            lanes = slice(h * HEAD_DIM, (h + 1) * HEAD_DIM)
            s_prev = ret_ref[b, h]
            q8 = jnp.broadcast_to(qg_s[b:b + 1, lanes], (8, HEAD_DIM))
            os_s[b:b + 1, lanes] = _dot(q8.astype(BF16), s_prev.astype(BF16))[0:1, :]
            k8 = jnp.where(first_row, jnp.broadcast_to(k_s[b:b + 1, lanes], (8, HEAD_DIM)), 0.0)
            v8 = jnp.broadcast_to(v_s[b:b + 1, lanes], (8, HEAD_DIM))
            kv = lax.dot_general(k8.astype(BF16), v8.astype(BF16), (((0,), (0,)), ((), ())),
                                 preferred_element_type=F32)
            ret_out[b, h] = gam_ref[:, lanes] * s_prev + kv

    for h in range(HEADS):
        lanes = slice(h * HEAD_DIM, (h + 1) * HEAD_DIM)
        o = scores[h] * v_s[:, lanes] + os_s[:, lanes]
        on = _layernorm(o) * retg_ref[:, lanes]
        g = p_ref[:, _G + h * HEAD_DIM:_G + (h + 1) * HEAD_DIM]
        mix_ref[:, _YC + h * HEAD_DIM:_YC + (h + 1) * HEAD_DIM] = (_silu(g) * on).astype(BF16)

    vn = _layernorm(p_ref[:, _VV:_VV + GROUP]) * sgug_ref[...]
    mix_ref[:, _YD:_YD + GROUP] = (p_ref[:, _U:_U + GROUP] * (sguw_ref[...] * vn + sgub_ref[...])).astype(BF16)
    sgu_out[:, 0, :] = vn


def mixer_sample(p, pool_state, conv_state, ret_state, tables, params, layer, stacked, *, start):
    n = p.shape[0]
    tb = _SEQ_BLOCK
    cos, sin, gam = tables
    poolw, pscale, convw, retg, sgug, sguw, sgub = params
    rows = lambda width: pl.BlockSpec((tb, width), lambda i: (i, 0))
    lay_rows = lambda r: pl.BlockSpec((None, tb, r, GROUP), lambda i: (layer, i, 0, 0))
    lay_ret = pl.BlockSpec((None, tb, HEADS, HEAD_DIM, HEAD_DIM), lambda i: (layer, i, 0, 0, 0))
    const2 = lambda shape: pl.BlockSpec(shape, lambda i: (0, 0))
    lay3 = lambda shape: pl.BlockSpec((None,) + shape, lambda i: (layer, 0, 0))
    lay4 = lambda shape: pl.BlockSpec((None,) + shape, lambda i: (layer, 0, 0, 0))
    in_specs = [
        rows(IN_WIDTH),
        lay_rows(POOL_BUF),
        lay_rows(2),
        lay_ret,
        const2((1, HEAD_DIM)),
        const2((1, HEAD_DIM)),
        const2((1, GROUP)),
        lay4((len(POOL_WINDOWS), HEAD_DIM, HEAD_DIM)),
        lay3((1, GROUP)),
        lay3((3, GROUP)),
        lay3((1, GROUP)),
        lay3((1, GROUP)),
        lay3((1, GROUP)),
        lay3((1, GROUP)),
    ]
    args = (p, pool_state, conv_state, ret_state, cos, sin, gam,
            poolw, pscale, convw, retg, sgug, sguw, sgub)
    assert len(args) == _N_SAMPLE_INPUTS
    aliases = {}
    if stacked is not None:
        aliases = {len(args) + k: 1 + k for k in range(len(stacked))}
        in_specs = in_specs + [pl.BlockSpec(memory_space=pl.ANY)] * len(stacked)
        args = args + tuple(stacked)
    outs = pl.pallas_call(
        functools.partial(_mixer_sample_kernel, start=start),
        grid=(n // tb,),
        in_specs=in_specs,
        out_specs=[rows(4 * GROUP), lay_rows(POOL_BUF), lay_rows(2), lay_ret, lay_rows(1)],
        out_shape=[
            jax.ShapeDtypeStruct((n, 4 * GROUP), BF16),
            jax.ShapeDtypeStruct((DEPTH, n, POOL_BUF, GROUP), F32),
            jax.ShapeDtypeStruct((DEPTH, n, 2, GROUP), F32),
            jax.ShapeDtypeStruct((DEPTH, n, HEADS, HEAD_DIM, HEAD_DIM), F32),
            jax.ShapeDtypeStruct((DEPTH, n, 1, GROUP), F32),
        ],
        scratch_shapes=[pltpu.VMEM((tb, GROUP), F32)] * 4,
        input_output_aliases=aliases,
        compiler_params=pltpu.CompilerParams(
            dimension_semantics=("arbitrary",),
            vmem_limit_bytes=32 * MIB),
        name="mixer_sample",
    )(*args)
    return outs[0], tuple(outs[1:])


def _rope_tables(start, length):
    half = HEAD_DIM // 2
    inv = ROPE_BASE ** (-jnp.arange(half, dtype=F32) / half)
    pos = start + jnp.arange(length)
    ang = pos.astype(F32)[:, None] * inv[None, :]
    cos, sin = jnp.cos(ang), jnp.sin(ang)
    return jnp.concatenate([cos, cos], axis=-1), jnp.concatenate([-sin, sin], axis=-1)


def _log_gamma():
    return jnp.log1p(-(2.0 ** (-5.0 - jnp.arange(HEADS, dtype=F32))))


def _decay_tables(c):
    lg = _log_gamma()
    idx = jnp.arange(c, dtype=F32)
    diff = idx[:, None] - idx[None, :]
    dmask = jnp.where(diff >= 0, jnp.exp(jnp.maximum(diff, 0.0)[None] * lg[:, None, None]), 0.0)
    qdec = jnp.exp((idx + 1.0)[None, :] * lg[:, None])[:, :, None]
    kdec = jnp.exp((c - 1.0 - idx)[None, :] * lg[:, None])[:, :, None]
    wide = lambda t: jnp.broadcast_to(t, (HEADS, c, HEAD_DIM))
    return dmask, wide(qdec), wide(kdec)


def _per_head_lanes(t):
    return jnp.repeat(t, HEAD_DIM, axis=-1)[..., None, :]


def kernel(x_prompt, x_sample, state_pool, state_conv, state_ret, norm1_g, w_in, pool_w, pool_scale,
           conv_w, ret_norm_g, sgu_norm_g, sgu_w, sgu_b, w_out, norm2_g, w_gate_up, w_down,
           final_norm_g):
    bp, lp, _ = x_prompt.shape
    ns = x_sample.shape[0]

    pool_w_b = pool_w.astype(BF16)
    row3 = lambda t: t[:, None, :]
    norm1 = row3(norm1_g)
    norm2 = row3(norm2_g)
    final_g = final_norm_g[None, :]

    cos_p, sin_p = _rope_tables(0, lp)
    prompt_tables = (cos_p, sin_p) + _decay_tables(CHUNK)
    sgu_b_wide = jnp.broadcast_to(sgu_b[:, :, :, None], sgu_b.shape + (HEAD_DIM,))
    prompt_params = (pool_w_b, row3(pool_scale), conv_w, row3(ret_norm_g), row3(sgu_norm_g), sgu_w, sgu_b_wide)

    cos_s, sin_s = _rope_tables(PAST_LEN, 1)
    gam = _per_head_lanes(jnp.exp(_log_gamma()))
    sample_tables = (cos_s, sin_s, gam)
    sample_params = (pool_w_b, row3(pool_scale), conv_w, row3(ret_norm_g), row3(sgu_norm_g),
                     _per_head_lanes(sgu_w[:, :, 0, 0]), _per_head_lanes(sgu_b[:, :, 0]))
    hp = x_prompt
    hs = x_sample.reshape(ns, D_MODEL)
    prompt_states = [[] for _ in range(4)]
    sample_states = None
    for l in range(DEPTH):
        fin = l == DEPTH - 1
        p, w_in_b = norm_matmul(hs, norm1, w_in, l, tn=1024)
        mix, sample_states = mixer_sample(p, state_pool, state_conv, state_ret, sample_tables,
                                          sample_params, l, sample_states, start=PAST_LEN)
        hs, w_out_b = matmul_residual(mix, w_out, hs, l, tn=1024)
        hs, w_mlp_b = mlp(hs, norm2, (w_gate_up, w_down), final_g, l, tm=ns, tf=512, final_norm=fin)
        hp, *states = block_prompt(hp, norm1, w_in_b, w_out_b, prompt_tables, prompt_params, l,
                                   start=0, tm=256)
        hp = mlp(hp.reshape(bp * lp, D_MODEL), norm2, w_mlp_b, final_g, l, tm=1024, tf=512,
                 final_norm=fin).reshape(bp, lp, D_MODEL)
        for lst, val in zip(prompt_states, states):
            lst.append(val)
    pool_p, conv_p, ret_p, sgu_p = (jnp.stack(s) for s in prompt_states)
    pool_s, conv_s, ret_s, sgu_s = sample_states
    return (hp, hs.reshape(ns, 1, D_MODEL), pool_p, pool_s, conv_p, conv_s, ret_p, ret_s, sgu_p, sgu_s)
```

```python
import functools

import jax
import jax.numpy as jnp
from jax import lax
from jax.experimental import pallas as pl
from jax.experimental.pallas import tpu as pltpu

F32 = jnp.float32
BF16 = jnp.bfloat16

D_MODEL = 2048
DEPTH = 2
GROUP = 512
N_SPLITS = 10
IN_WIDTH = N_SPLITS * GROUP
POOL_WINDOWS = (2, 4, 8, 16)
POOL_BUF = 15
HEADS = 4
HEAD_DIM = 128
CHUNK = 128
ROPE_BASE = 10000.0
D_FF = 5632
NORM_EPS = 1e-6
PAST_LEN = 16384

_A, _BG, _CG, _HH, _Q, _K, _V, _G, _U, _VV = (i * GROUP for i in range(N_SPLITS))
_YA, _YB, _YC, _YD = (i * GROUP for i in range(4))

MIB = 1024 * 1024
VMEM_BYTES_V7X = 64 * MIB


def _vmem_limit(estimate):
    return min(estimate + 4 * MIB, VMEM_BYTES_V7X - 4 * MIB)


def _rms(x, g):
    ms = jnp.mean(x * x, axis=-1, keepdims=True)
    return x * lax.rsqrt(ms + NORM_EPS) * g


def _layernorm(x):
    mu = jnp.mean(x, axis=-1, keepdims=True)
    xc = x - mu
    return xc * lax.rsqrt(jnp.mean(xc * xc, axis=-1, keepdims=True) + NORM_EPS)


def _silu(x):
    return x * jax.nn.sigmoid(x)


def _dot(a, b):
    return jnp.dot(a, b, preferred_element_type=F32)


def _rotate(x, cos, sin_signed):
    return x * cos + pltpu.roll(x, HEAD_DIM // 2, 1) * sin_signed


def _norm_matmul_kernel(x_ref, g_ref, w_ref, o_ref, wb_ref, xn_ref):
    @pl.when(pl.program_id(0) == 0)
    def _():
        xn_ref[...] = _rms(x_ref[...], g_ref[...]).astype(BF16)

    wb = w_ref[...].astype(BF16)
    wb_ref[...] = wb
    o_ref[...] = _dot(xn_ref[...], wb)


def norm_matmul(x, g, w, layer, *, tn):
    m, k = x.shape
    n = w.shape[-1]
    vmem = 2 * k * tn * (4 + 2) + 2 * m * (k + tn) * 4 + m * k * 2
    return pl.pallas_call(
        _norm_matmul_kernel,
        grid=(n // tn,),
        in_specs=[
            pl.BlockSpec((m, k), lambda j: (0, 0)),
            pl.BlockSpec((None, 1, k), lambda j: (layer, 0, 0)),
            pl.BlockSpec((None, k, tn), lambda j: (layer, 0, j)),
        ],
        out_specs=[pl.BlockSpec((m, tn), lambda j: (0, j)),
                   pl.BlockSpec((k, tn), lambda j: (0, j))],
        out_shape=[jax.ShapeDtypeStruct((m, n), F32),
                   jax.ShapeDtypeStruct((k, n), BF16)],
        scratch_shapes=[pltpu.VMEM((m, k), BF16)],
        compiler_params=pltpu.CompilerParams(
            dimension_semantics=("arbitrary",),
            vmem_limit_bytes=_vmem_limit(vmem)),
        name="norm_matmul",
    )(x, g, w)


def _matmul_residual_kernel(a_ref, w_ref, r_ref, o_ref, wb_ref):
    wb = w_ref[...].astype(BF16)
    wb_ref[...] = wb
    o_ref[...] = r_ref[...] + _dot(a_ref[...], wb)


def matmul_residual(a, w, res, layer, *, tn):
    m, k = a.shape
    n = w.shape[-1]
    vmem = 2 * k * tn * (4 + 2) + 2 * m * (k * 2 + 2 * tn * 4)
    return pl.pallas_call(
        _matmul_residual_kernel,
        grid=(n // tn,),
        in_specs=[
            pl.BlockSpec((m, k), lambda j: (0, 0)),
            pl.BlockSpec((None, k, tn), lambda j: (layer, 0, j)),
            pl.BlockSpec((m, tn), lambda j: (0, j)),
        ],
        out_specs=[pl.BlockSpec((m, tn), lambda j: (0, j)),
                   pl.BlockSpec((k, tn), lambda j: (0, j))],
        out_shape=[jax.ShapeDtypeStruct((m, n), F32),
                   jax.ShapeDtypeStruct((k, n), BF16)],
        compiler_params=pltpu.CompilerParams(
            dimension_semantics=("arbitrary",),
            vmem_limit_bytes=_vmem_limit(vmem)),
        name="matmul_residual",
    )(a, w, res)


def _mlp_kernel(x_ref, g_ref, wg_ref, wu_ref, wd_ref, fg_ref, o_ref, *rest, final_norm, emit_bf16):
    hn_ref = rest[-1]
    f = pl.program_id(1)

    @pl.when(f == 0)
    def _():
        x = x_ref[...]
        hn_ref[...] = _rms(x, g_ref[...]).astype(BF16)
        o_ref[...] = x

    wg, wu, wd = wg_ref[...], wu_ref[...], wd_ref[...]
    if emit_bf16:
        wg, wu, wd = wg.astype(BF16), wu.astype(BF16), wd.astype(BF16)
        rest[0][...] = wg
        rest[1][...] = wu
        rest[2][...] = wd
    hn = hn_ref[...]
    act = (_silu(_dot(hn, wg)) * _dot(hn, wu)).astype(BF16)
    o_ref[...] += _dot(act, wd)

    if final_norm:
        @pl.when(f == pl.num_programs(1) - 1)
        def _():
            o_ref[...] = _rms(o_ref[...], fg_ref[...])


def mlp(x, g, weights, final_g, layer, *, tm, tf, final_norm):
    m, d = x.shape
    nf = D_FF // tf
    emit_bf16 = len(weights) == 2
    if emit_bf16:
        assert m == tm
        w_gate_up, w_down = weights
        w_args = (w_gate_up, w_gate_up, w_down)
        w_specs = [
            pl.BlockSpec((None, d, tf), lambda i, f: (layer, 0, f)),
            pl.BlockSpec((None, d, tf), lambda i, f: (layer, 0, nf + f)),
            pl.BlockSpec((None, tf, d), lambda i, f: (layer, f, 0)),
        ]
        w_bytes = 4 + 2
    else:
        w_args = weights
        w_specs = [
            pl.BlockSpec((d, tf), lambda i, f: (0, f)),
            pl.BlockSpec((d, tf), lambda i, f: (0, f)),
            pl.BlockSpec((tf, d), lambda i, f: (f, 0)),
        ]
        w_bytes = 2
    out_specs = [pl.BlockSpec((tm, d), lambda i, f: (i, 0))]
    out_shape = [jax.ShapeDtypeStruct((m, d), F32)]
    if emit_bf16:
        out_specs += [pl.BlockSpec((d, tf), lambda i, f: (0, f)),
                      pl.BlockSpec((d, tf), lambda i, f: (0, f)),
                      pl.BlockSpec((tf, d), lambda i, f: (f, 0))]
        out_shape += [jax.ShapeDtypeStruct((d, D_FF), BF16), jax.ShapeDtypeStruct((d, D_FF), BF16),
                      jax.ShapeDtypeStruct((D_FF, d), BF16)]
    vmem = 2 * 2 * tm * d * 4 + tm * d * 2 + 2 * 3 * d * tf * w_bytes + 3 * tm * tf * 4
    outs = pl.pallas_call(
        functools.partial(_mlp_kernel, final_norm=final_norm, emit_bf16=emit_bf16),
        grid=(m // tm, nf),
        in_specs=[
            pl.BlockSpec((tm, d), lambda i, f: (i, 0)),
            pl.BlockSpec((None, 1, d), lambda i, f: (layer, 0, 0)),
            *w_specs,
            pl.BlockSpec((1, d), lambda i, f: (0, 0)),
        ],
        out_specs=out_specs,
        out_shape=out_shape,
        scratch_shapes=[pltpu.VMEM((tm, d), BF16)],
        compiler_params=pltpu.CompilerParams(
            dimension_semantics=("arbitrary", "arbitrary"),
            vmem_limit_bytes=_vmem_limit(vmem)),
        name="mlp",
    )(x, g, *w_args, final_g)
    return (outs[0], tuple(outs[1:])) if emit_bf16 else outs[0]


_POOL_TOP = 16
_CONV_TOP = 8


def _block_prompt_kernel(x_ref, g1_ref, win_ref, wout_ref, cos_ref, sin_ref, dmask_ref, qdec_ref, kdec_ref,
                         poolw_ref, pscale_ref, convw_ref, retg_ref, sgug_ref, sguw_ref, sgub_ref,
                         h_out, pool_out, conv_out, ret_out, sgu_out,
                         pool_ext, conv_ext, *, start, tm):
    c = pl.program_id(1)
    T = CHUNK
    n_chunks = tm // T

    @pl.when(c == 0)
    def _():
        pool_ext[0:_POOL_TOP, :] = jnp.zeros((_POOL_TOP, GROUP), F32)
        conv_ext[0:_CONV_TOP, :] = jnp.zeros((_CONV_TOP, GROUP), F32)
        ret_out[...] = jnp.zeros(ret_out.shape, F32)

    x = x_ref[0]
    xn = _rms(x, g1_ref[...]).astype(BF16)

    def proj(col):
        return _dot(xn, win_ref[:, col:col + GROUP])

    def out_proj(y, col):
        return _dot(y, wout_ref[col:col + GROUP, :])

    head_lanes = [slice(hd * HEAD_DIM, (hd + 1) * HEAD_DIM) for hd in range(HEADS)]
    chunk_rows = [slice(j * T, (j + 1) * T) for j in range(n_chunks)]

    def pool_mixer(a_proj):
        pool_ext[_POOL_TOP:_POOL_TOP + tm, :] = a_proj
        pos1 = lax.broadcasted_iota(jnp.int32, (tm, HEAD_DIM), 0) + (start + 1) + c * tm
        y = []
        for gi, win in enumerate(POOL_WINDOWS):
            lanes = head_lanes[gi]
            a = pool_ext[_POOL_TOP:_POOL_TOP + tm, lanes]
            wsum = a
            for j in range(1, win):
                wsum = wsum + pool_ext[_POOL_TOP - j:_POOL_TOP - j + tm, lanes]
            cnt = jnp.minimum(pos1, win).astype(F32)
            d = (wsum / cnt - a).astype(BF16)
            y.append((_dot(d, poolw_ref[gi]) * pscale_ref[:, lanes]).astype(BF16))
        pool_out[0] = pool_ext[_POOL_TOP + tm - POOL_BUF:_POOL_TOP + tm, :]
        pool_ext[_POOL_TOP - POOL_BUF:_POOL_TOP, :] = pool_ext[_POOL_TOP + tm - POOL_BUF:_POOL_TOP + tm, :]
        return jnp.concatenate(y, axis=-1)

    def conv_mixer(bg, cg, hh):
        z = cg * hh
        conv_ext[_CONV_TOP:_CONV_TOP + tm, :] = z
        acc = (conv_ext[_CONV_TOP - 2:_CONV_TOP - 2 + tm, :] * convw_ref[0:1, :]
               + conv_ext[_CONV_TOP - 1:_CONV_TOP - 1 + tm, :] * convw_ref[1:2, :]
               + z * convw_ref[2:3, :])
        conv_out[0] = conv_ext[_CONV_TOP + tm - 2:_CONV_TOP + tm, :]
        conv_ext[_CONV_TOP - 2:_CONV_TOP, :] = conv_ext[_CONV_TOP + tm - 2:_CONV_TOP + tm, :]
        return (bg * acc).astype(BF16)

    def retention(q_all, k_all, v_all, g_all, j, hd):
        rows, lanes = chunk_rows[j], head_lanes[hd]
        cos = cos_ref[rows, :]
        sin = sin_ref[rows, :]
        q = _rotate(q_all[rows, lanes], cos, sin)
        k = _rotate(k_all[rows, lanes], cos, sin) * (HEAD_DIM ** -0.5)
        v = v_all[rows, lanes].astype(BF16)
        s_prev = ret_out[0, hd]
        scores = lax.dot_general(q.astype(BF16), k.astype(BF16), (((1,), (1,)), ((), ())),
                                 preferred_element_type=F32) * dmask_ref[hd]
        o = _dot(scores.astype(BF16), v) + _dot((q * qdec_ref[hd]).astype(BF16), s_prev.astype(BF16))
        kv = lax.dot_general((k * kdec_ref[hd]).astype(BF16), v, (((0,), (0,)), ((), ())),
                             preferred_element_type=F32)
        ret_out[0, hd] = qdec_ref[hd, T - 1:T, :] * s_prev + kv
        on = _layernorm(o) * retg_ref[:, lanes]
        return (_silu(g_all[rows, lanes]) * on).astype(BF16)

    def sgu_mixer(u_all, vn, j):
        rows = chunk_rows[j]
        tri = lax.broadcasted_iota(jnp.int32, (T, T), 0) >= lax.broadcasted_iota(jnp.int32, (T, T), 1)
        y = []
        for hd in range(HEADS):
            lanes = head_lanes[hd]
            w = jnp.where(tri, sguw_ref[hd], 0.0).astype(BF16)
            mixed = _dot(w, vn[rows, lanes].astype(BF16)) + sgub_ref[hd]
            y.append((u_all[rows, lanes] * mixed).astype(BF16))
        return jnp.concatenate(y, axis=-1)

    q_all, k_all, v_all, g_all = proj(_Q), proj(_K), proj(_V), proj(_G)
    ret = functools.partial(retention, q_all, k_all, v_all, g_all)
    y_c = [[None] * HEADS for _ in range(n_chunks)]
    y_c[0][0] = ret(0, 0)
    a_proj = proj(_A)
    y_c[0][1] = ret(0, 1)
    cg = proj(_CG)
    y_c[0][2] = ret(0, 2)
    hh = proj(_HH)
    y_c[0][3] = ret(0, 3)
    bg = proj(_BG)
    later = iter([lambda: proj(_VV), lambda: proj(_U), lambda: pool_mixer(a_proj),
                  lambda: conv_mixer(bg, cg, hh)])
    fills = []
    for j in range(1, n_chunks):
        for hd in range(HEADS):
            y_c[j][hd] = ret(j, hd)
            nxt = next(later, None)
            if nxt is not None:
                fills.append(nxt())
    fills += [f() for f in later]
    vv, u_all, y_a, y_b = fills
    h = x + out_proj(jnp.concatenate([jnp.concatenate(r, axis=-1) for r in y_c], axis=0), _YC)
    vn = _layernorm(vv) * sgug_ref[...]
    y_d = [sgu_mixer(u_all, vn, 0)]
    h = h + out_proj(y_a, _YA)
    y_d += [sgu_mixer(u_all, vn, j) for j in range(1, n_chunks)]
    h = h + out_proj(y_b, _YB)
    h_out[0] = h + out_proj(jnp.concatenate(y_d, axis=0), _YD)
    sgu_out[0] = vn[tm - T:tm, :]


def block_prompt(x, norm_g, w_in, w_out, tables, params, layer, *, start, tm):
    b, l, d = x.shape
    cos, sin, dmask, qdec, kdec = tables
    poolw, pscale, convw, retg, sgug, sguw, sgub = params
    const3 = lambda shape: pl.BlockSpec(shape, lambda i, c: (0, 0, 0))
    lay3 = lambda shape: pl.BlockSpec((None,) + shape, lambda i, c: (layer, 0, 0))
    lay4 = lambda shape: pl.BlockSpec((None,) + shape, lambda i, c: (layer, 0, 0, 0))
    resident = lambda shape: pl.BlockSpec(shape, lambda i, c: (0, 0), pipeline_mode=pl.Buffered(1))
    vmem = (d * IN_WIDTH + 4 * GROUP * d) * 2 + 2 * 2 * tm * d * 4 + 2 * tm * IN_WIDTH * 4
    return pl.pallas_call(
        functools.partial(_block_prompt_kernel, start=start, tm=tm),
        grid=(b, l // tm),
        in_specs=[
            pl.BlockSpec((1, tm, d), lambda i, c: (i, c, 0)),
            lay3((1, d)),
            resident((d, IN_WIDTH)),
            resident((4 * GROUP, d)),
            pl.BlockSpec((tm, HEAD_DIM), lambda i, c: (c, 0)),
            pl.BlockSpec((tm, HEAD_DIM), lambda i, c: (c, 0)),
            const3((HEADS, CHUNK, CHUNK)),
            const3((HEADS, CHUNK, HEAD_DIM)),
            const3((HEADS, CHUNK, HEAD_DIM)),
            lay4((len(POOL_WINDOWS), HEAD_DIM, HEAD_DIM)),
            lay3((1, GROUP)),
            lay3((3, GROUP)),
            lay3((1, GROUP)),
            lay3((1, GROUP)),
            lay4((HEADS, CHUNK, CHUNK)),
            lay4((HEADS, CHUNK, HEAD_DIM)),
        ],
        out_specs=[
            pl.BlockSpec((1, tm, d), lambda i, c: (i, c, 0)),
            pl.BlockSpec((1, POOL_BUF, GROUP), lambda i, c: (i, 0, 0)),
            pl.BlockSpec((1, 2, GROUP), lambda i, c: (i, 0, 0)),
            pl.BlockSpec((1, HEADS, HEAD_DIM, HEAD_DIM), lambda i, c: (i, 0, 0, 0)),
            pl.BlockSpec((1, CHUNK, GROUP), lambda i, c: (i, 0, 0)),
        ],
        out_shape=[
            jax.ShapeDtypeStruct((b, l, d), F32),
            jax.ShapeDtypeStruct((b, POOL_BUF, GROUP), F32),
            jax.ShapeDtypeStruct((b, 2, GROUP), F32),
            jax.ShapeDtypeStruct((b, HEADS, HEAD_DIM, HEAD_DIM), F32),
            jax.ShapeDtypeStruct((b, CHUNK, GROUP), F32),
        ],
        scratch_shapes=[
            pltpu.VMEM((_POOL_TOP + tm, GROUP), F32),
            pltpu.VMEM((_CONV_TOP + tm, GROUP), F32),
        ],
        compiler_params=pltpu.CompilerParams(
            dimension_semantics=("arbitrary", "arbitrary"),
            vmem_limit_bytes=_vmem_limit(vmem)),
        name="block_prompt",
    )(x, norm_g, w_in, w_out, cos, sin, dmask, qdec, kdec, poolw, pscale, convw, retg, sgug, sguw, sgub)


_SEQ_BLOCK = 8


_N_SAMPLE_INPUTS = 14


def _mixer_sample_kernel(*refs, start):
    (p_ref, pool_ref, conv_ref, ret_ref, cos_ref, sin_ref, gam_ref,
     poolw_ref, pscale_ref, convw_ref, retg_ref, sgug_ref, sguw_ref, sgub_ref) = refs[:_N_SAMPLE_INPUTS]
    mix_ref, pool_out, conv_out, ret_out, sgu_out, qg_s, k_s, v_s, os_s = refs[-9:]
    TB = _SEQ_BLOCK

    a_all = p_ref[:, _A:_A + GROUP]
    for gi, win in enumerate(POOL_WINDOWS):
        lanes = slice(gi * HEAD_DIM, (gi + 1) * HEAD_DIM)
        a = a_all[:, lanes]
        wsum = a
        for j in range(1, win):
            wsum = wsum + pool_ref[:, POOL_BUF - j, lanes]
        cnt = float(min(start + 1, win))
        d = (wsum / cnt - a).astype(BF16)
        y = _dot(d, poolw_ref[gi]) * pscale_ref[:, lanes]
        mix_ref[:, _YA + gi * HEAD_DIM:_YA + (gi + 1) * HEAD_DIM] = y.astype(BF16)
    pool_out[:, 0:POOL_BUF - 1, :] = pool_ref[:, 1:POOL_BUF, :]
    pool_out[:, POOL_BUF - 1, :] = a_all

    z = p_ref[:, _CG:_CG + GROUP] * p_ref[:, _HH:_HH + GROUP]
    acc = (conv_ref[:, 0, :] * convw_ref[0:1, :] + conv_ref[:, 1, :] * convw_ref[1:2, :]
           + z * convw_ref[2:3, :])
    mix_ref[:, _YB:_YB + GROUP] = (p_ref[:, _BG:_BG + GROUP] * acc).astype(BF16)
    conv_out[:, 0, :] = conv_ref[:, 1, :]
    conv_out[:, 1, :] = z

    cos = cos_ref[...]
    sin = sin_ref[...]
    scores = []
    for h in range(HEADS):
        lanes = slice(h * HEAD_DIM, (h + 1) * HEAD_DIM)
        q = _rotate(p_ref[:, _Q + h * HEAD_DIM:_Q + (h + 1) * HEAD_DIM], cos, sin)
        k = _rotate(p_ref[:, _K + h * HEAD_DIM:_K + (h + 1) * HEAD_DIM], cos, sin) * (HEAD_DIM ** -0.5)
        scores.append(jnp.sum(q * k, axis=-1, keepdims=True))
        qg_s[:, lanes] = q * gam_ref[:, lanes]
        k_s[:, lanes] = k
    v_s[...] = p_ref[:, _V:_V + GROUP]

    first_row = lax.broadcasted_iota(jnp.int32, (8, HEAD_DIM), 0) == 0

    for b in range(TB):
        for h in range(HEADS):
            lanes = slice(h * HEAD_DIM, (h + 1) * HEAD_DIM)
            s_prev = ret_ref[b, h]
            q8 = jnp.broadcast_to(qg_s[b:b + 1, lanes], (8, HEAD_DIM))
            os_s[b:b + 1, lanes] = _dot(q8.astype(BF16), s_prev.astype(BF16))[0:1, :]
            k8 = jnp.where(first_row, jnp.broadcast_to(k_s[b:b + 1, lanes], (8, HEAD_DIM)), 0.0)
            v8 = jnp.broadcast_to(v_s[b:b + 1, lanes], (8, HEAD_DIM))
            kv = lax.dot_general(k8.astype(BF16), v8.astype(BF16), (((0,), (0,)), ((), ())),
                                 preferred_element_type=F32)
            ret_out[b, h] = gam_ref[:, lanes] * s_prev + kv

    for h in range(HEADS):
        lanes = slice(h * HEAD_DIM, (h + 1) * HEAD_DIM)
        o = scores[h] * v_s[:, lanes] + os_s[:, lanes]
        on = _layernorm(o) * retg_ref[:, lanes]
        g = p_ref[:, _G + h * HEAD_DIM:_G + (h + 1) * HEAD_DIM]
        mix_ref[:, _YC + h * HEAD_DIM:_YC + (h + 1) * HEAD_DIM] = (_silu(g) * on).astype(BF16)

    vn = _layernorm(p_ref[:, _VV:_VV + GROUP]) * sgug_ref[...]
    mix_ref[:, _YD:_YD + GROUP] = (p_ref[:, _U:_U + GROUP] * (sguw_ref[...] * vn + sgub_ref[...])).astype(BF16)
    sgu_out[:, 0, :] = vn


def mixer_sample(p, pool_state, conv_state, ret_state, tables, params, layer, stacked, *, start):
    n = p.shape[0]
    tb = _SEQ_BLOCK
    cos, sin, gam = tables
    poolw, pscale, convw, retg, sgug, sguw, sgub = params
    rows = lambda width: pl.BlockSpec((tb, width), lambda i: (i, 0))
    lay_rows = lambda r: pl.BlockSpec((None, tb, r, GROUP), lambda i: (layer, i, 0, 0))
    lay_ret = pl.BlockSpec((None, tb, HEADS, HEAD_DIM, HEAD_DIM), lambda i: (layer, i, 0, 0, 0))
    const2 = lambda shape: pl.BlockSpec(shape, lambda i: (0, 0))
    lay3 = lambda shape: pl.BlockSpec((None,) + shape, lambda i: (layer, 0, 0))
    lay4 = lambda shape: pl.BlockSpec((None,) + shape, lambda i: (layer, 0, 0, 0))
    in_specs = [
        rows(IN_WIDTH),
        lay_rows(POOL_BUF),
        lay_rows(2),
        lay_ret,
        const2((1, HEAD_DIM)),
        const2((1, HEAD_DIM)),
        const2((1, GROUP)),
        lay4((len(POOL_WINDOWS), HEAD_DIM, HEAD_DIM)),
        lay3((1, GROUP)),
        lay3((3, GROUP)),
        lay3((1, GROUP)),
        lay3((1, GROUP)),
        lay3((1, GROUP)),
        lay3((1, GROUP)),
    ]
    args = (p, pool_state, conv_state, ret_state, cos, sin, gam,
            poolw, pscale, convw, retg, sgug, sguw, sgub)
    assert len(args) == _N_SAMPLE_INPUTS
    aliases = {}
    if stacked is not None:
        aliases = {len(args) + k: 1 + k for k in range(len(stacked))}
        in_specs = in_specs + [pl.BlockSpec(memory_space=pl.ANY)] * len(stacked)
        args = args + tuple(stacked)
    outs = pl.pallas_call(
        functools.partial(_mixer_sample_kernel, start=start),
        grid=(n // tb,),
        in_specs=in_specs,
        out_specs=[rows(4 * GROUP), lay_rows(POOL_BUF), lay_rows(2), lay_ret, lay_rows(1)],
        out_shape=[
            jax.ShapeDtypeStruct((n, 4 * GROUP), BF16),
            jax.ShapeDtypeStruct((DEPTH, n, POOL_BUF, GROUP), F32),
            jax.ShapeDtypeStruct((DEPTH, n, 2, GROUP), F32),
            jax.ShapeDtypeStruct((DEPTH, n, HEADS, HEAD_DIM, HEAD_DIM), F32),
            jax.ShapeDtypeStruct((DEPTH, n, 1, GROUP), F32),
        ],
        scratch_shapes=[pltpu.VMEM((tb, GROUP), F32)] * 4,
        input_output_aliases=aliases,
        compiler_params=pltpu.CompilerParams(
            dimension_semantics=("arbitrary",),
            vmem_limit_bytes=32 * MIB),
        name="mixer_sample",
    )(*args)
    return outs[0], tuple(outs[1:])


def _rope_tables(start, length):
    half = HEAD_DIM // 2
    inv = ROPE_BASE ** (-jnp.arange(half, dtype=F32) / half)
    pos = start + jnp.arange(length)
    ang = pos.astype(F32)[:, None] * inv[None, :]
    cos, sin = jnp.cos(ang), jnp.sin(ang)
    return jnp.concatenate([cos, cos], axis=-1), jnp.concatenate([-sin, sin], axis=-1)


def _log_gamma():
    return jnp.log1p(-(2.0 ** (-5.0 - jnp.arange(HEADS, dtype=F32))))


def _decay_tables(c):
    lg = _log_gamma()
    idx = jnp.arange(c, dtype=F32)
    diff = idx[:, None] - idx[None, :]
    dmask = jnp.where(diff >= 0, jnp.exp(jnp.maximum(diff, 0.0)[None] * lg[:, None, None]), 0.0)
    qdec = jnp.exp((idx + 1.0)[None, :] * lg[:, None])[:, :, None]
    kdec = jnp.exp((c - 1.0 - idx)[None, :] * lg[:, None])[:, :, None]
    wide = lambda t: jnp.broadcast_to(t, (HEADS, c, HEAD_DIM))
    return dmask, wide(qdec), wide(kdec)


def _per_head_lanes(t):
    return jnp.repeat(t, HEAD_DIM, axis=-1)[..., None, :]


def kernel(x_prompt, x_sample, state_pool, state_conv, state_ret, norm1_g, w_in, pool_w, pool_scale,
           conv_w, ret_norm_g, sgu_norm_g, sgu_w, sgu_b, w_out, norm2_g, w_gate_up, w_down,
           final_norm_g):
    bp, lp, _ = x_prompt.shape
    ns = x_sample.shape[0]

    pool_w_b = pool_w.astype(BF16)
    row3 = lambda t: t[:, None, :]
    norm1 = row3(norm1_g)
    norm2 = row3(norm2_g)
    final_g = final_norm_g[None, :]

    cos_p, sin_p = _rope_tables(0, lp)
    prompt_tables = (cos_p, sin_p) + _decay_tables(CHUNK)
    sgu_b_wide = jnp.broadcast_to(sgu_b[:, :, :, None], sgu_b.shape + (HEAD_DIM,))
    prompt_params = (pool_w_b, row3(pool_scale), conv_w, row3(ret_norm_g), row3(sgu_norm_g), sgu_w, sgu_b_wide)

    cos_s, sin_s = _rope_tables(PAST_LEN, 1)
    gam = _per_head_lanes(jnp.exp(_log_gamma()))
    sample_tables = (cos_s, sin_s, gam)
    sample_params = (pool_w_b, row3(pool_scale), conv_w, row3(ret_norm_g), row3(sgu_norm_g),
                     _per_head_lanes(sgu_w[:, :, 0, 0]), _per_head_lanes(sgu_b[:, :, 0]))
    hp = x_prompt
    hs = x_sample.reshape(ns, D_MODEL)
    prompt_states = [[] for _ in range(4)]
    sample_states = None
    for l in range(DEPTH):
        fin = l == DEPTH - 1
        p, w_in_b = norm_matmul(hs, norm1, w_in, l, tn=1024)
        mix, sample_states = mixer_sample(p, state_pool, state_conv, state_ret, sample_tables,
                                          sample_params, l, sample_states, start=PAST_LEN)
        hs, w_out_b = matmul_residual(mix, w_out, hs, l, tn=1024)
        hs, w_mlp_b = mlp(hs, norm2, (w_gate_up, w_down), final_g, l, tm=ns, tf=512, final_norm=fin)
        hp, *states = block_prompt(hp, norm1, w_in_b, w_out_b, prompt_tables, prompt_params, l,
                                   start=0, tm=256)
        hp = mlp(hp.reshape(bp * lp, D_MODEL), norm2, w_mlp_b, final_g, l, tm=1024, tf=512,
                 final_norm=fin).reshape(bp, lp, D_MODEL)
        for lst, val in zip(prompt_states, states):
            lst.append(val)
    pool_p, conv_p, ret_p, sgu_p = (jnp.stack(s) for s in prompt_states)
    pool_s, conv_s, ret_s, sgu_s = sample_states
    return (hp, hs.reshape(ns, 1, D_MODEL), pool_p, pool_s, conv_p, conv_s, ret_p, ret_s, sgu_p, sgu_s)
```

```python
import functools

import jax
import jax.numpy as jnp
from jax import lax
from jax.experimental import pallas as pl
from jax.experimental.pallas import tpu as pltpu

F32 = jnp.float32
BF16 = jnp.bfloat16

D_MODEL = 2048
DEPTH = 2
GROUP = 512
N_SPLITS = 10
IN_WIDTH = N_SPLITS * GROUP
POOL_WINDOWS = (2, 4, 8, 16)
POOL_BUF = 15
HEADS = 4
HEAD_DIM = 128
CHUNK = 128
ROPE_BASE = 10000.0
D_FF = 5632
NORM_EPS = 1e-6
PAST_LEN = 16384

_A, _BG, _CG, _HH, _Q, _K, _V, _G, _U, _VV = (i * GROUP for i in range(N_SPLITS))
_YA, _YB, _YC, _YD = (i * GROUP for i in range(4))

MIB = 1024 * 1024
VMEM_BYTES_V7X = 64 * MIB


VMEM_LIMIT_MAX = VMEM_BYTES_V7X - 4 * MIB


def _vmem_limit(estimate):
    return min(estimate + 4 * MIB, VMEM_LIMIT_MAX)


def _rms(x, g):
    ms = jnp.mean(x * x, axis=-1, keepdims=True)
    return x * lax.rsqrt(ms + NORM_EPS) * g


def _layernorm(x):
    mu = jnp.mean(x, axis=-1, keepdims=True)
    xc = x - mu
    return xc * lax.rsqrt(jnp.mean(xc * xc, axis=-1, keepdims=True) + NORM_EPS)


def _silu(x):
    return x * jax.nn.sigmoid(x)


def _dot(a, b):
    return jnp.dot(a, b, preferred_element_type=F32)


def _rotate(x, cos, sin_signed):
    return x * cos + pltpu.roll(x, HEAD_DIM // 2, 1) * sin_signed


def _norm_matmul_kernel(x_ref, g_ref, w_ref, o_ref, wb_ref, xn_ref):
    @pl.when(pl.program_id(0) == 0)
    def _():
        xn_ref[...] = _rms(x_ref[...], g_ref[...]).astype(BF16)

    wb = w_ref[...].astype(BF16)
    wb_ref[...] = wb
    o_ref[...] = _dot(xn_ref[...], wb)


def norm_matmul(x, g, w, layer, *, tn):
    m, k = x.shape
    n = w.shape[-1]
    vmem = 2 * k * tn * (4 + 2) + 2 * m * (k + tn) * 4 + m * k * 2
    return pl.pallas_call(
        _norm_matmul_kernel,
        grid=(n // tn,),
        in_specs=[
            pl.BlockSpec((m, k), lambda j: (0, 0)),
            pl.BlockSpec((None, 1, k), lambda j: (layer, 0, 0)),
            pl.BlockSpec((None, k, tn), lambda j: (layer, 0, j)),
        ],
        out_specs=[pl.BlockSpec((m, tn), lambda j: (0, j)),
                   pl.BlockSpec((k, tn), lambda j: (0, j))],
        out_shape=[jax.ShapeDtypeStruct((m, n), F32),
                   jax.ShapeDtypeStruct((k, n), BF16)],
        scratch_shapes=[pltpu.VMEM((m, k), BF16)],
        compiler_params=pltpu.CompilerParams(
            dimension_semantics=("arbitrary",),
            vmem_limit_bytes=_vmem_limit(vmem)),
        name="norm_matmul",
    )(x, g, w)


def _matmul_residual_kernel(a_ref, w_ref, r_ref, o_ref, wb_ref):
    wb = w_ref[...].astype(BF16)
    wb_ref[...] = wb
    o_ref[...] = r_ref[...] + _dot(a_ref[...], wb)


def matmul_residual(a, w, res, layer, *, tn):
    m, k = a.shape
    n = w.shape[-1]
    return pl.pallas_call(
        _matmul_residual_kernel,
        grid=(n // tn,),
        in_specs=[
            pl.BlockSpec((m, k), lambda j: (0, 0)),
            pl.BlockSpec((None, k, tn), lambda j: (layer, 0, j)),
            pl.BlockSpec((m, tn), lambda j: (0, j)),
        ],
        out_specs=[pl.BlockSpec((m, tn), lambda j: (0, j)),
                   pl.BlockSpec((k, tn), lambda j: (0, j))],
        out_shape=[jax.ShapeDtypeStruct((m, n), F32),
                   jax.ShapeDtypeStruct((k, n), BF16)],
        compiler_params=pltpu.CompilerParams(
            dimension_semantics=("arbitrary",),
            vmem_limit_bytes=VMEM_LIMIT_MAX),
        name="matmul_residual",
    )(a, w, res)


def _mlp_kernel(x_ref, g_ref, wg_ref, wu_ref, wd_ref, fg_ref, o_ref, *rest, final_norm, emit_bf16):
    hn_ref = rest[-1]
    f = pl.program_id(1)

    @pl.when(f == 0)
    def _():
        x = x_ref[...]
        hn_ref[...] = _rms(x, g_ref[...]).astype(BF16)
        o_ref[...] = x

    wg, wu, wd = wg_ref[...], wu_ref[...], wd_ref[...]
    if emit_bf16:
        wg, wu, wd = wg.astype(BF16), wu.astype(BF16), wd.astype(BF16)
        rest[0][...] = wg
        rest[1][...] = wu
        rest[2][...] = wd
    hn = hn_ref[...]
    act = (_silu(_dot(hn, wg)) * _dot(hn, wu)).astype(BF16)
    o_ref[...] += _dot(act, wd)

    if final_norm:
        @pl.when(f == pl.num_programs(1) - 1)
        def _():
            o_ref[...] = _rms(o_ref[...], fg_ref[...])


def mlp(x, g, weights, final_g, layer, *, tm, tf, final_norm):
    m, d = x.shape
    nf = D_FF // tf
    emit_bf16 = len(weights) == 2
    if emit_bf16:
        assert m == tm
        w_gate_up, w_down = weights
        w_args = (w_gate_up, w_gate_up, w_down)
        w_specs = [
            pl.BlockSpec((None, d, tf), lambda i, f: (layer, 0, f)),
            pl.BlockSpec((None, d, tf), lambda i, f: (layer, 0, nf + f)),
            pl.BlockSpec((None, tf, d), lambda i, f: (layer, f, 0)),
        ]
        w_bytes = 4 + 2
    else:
        w_args = weights
        w_specs = [
            pl.BlockSpec((d, tf), lambda i, f: (0, f)),
            pl.BlockSpec((d, tf), lambda i, f: (0, f)),
            pl.BlockSpec((tf, d), lambda i, f: (f, 0)),
        ]
        w_bytes = 2
    out_specs = [pl.BlockSpec((tm, d), lambda i, f: (i, 0))]
    out_shape = [jax.ShapeDtypeStruct((m, d), F32)]
    if emit_bf16:
        out_specs += [pl.BlockSpec((d, tf), lambda i, f: (0, f)),
                      pl.BlockSpec((d, tf), lambda i, f: (0, f)),
                      pl.BlockSpec((tf, d), lambda i, f: (f, 0))]
        out_shape += [jax.ShapeDtypeStruct((d, D_FF), BF16), jax.ShapeDtypeStruct((d, D_FF), BF16),
                      jax.ShapeDtypeStruct((D_FF, d), BF16)]
    vmem = 2 * 2 * tm * d * 4 + tm * d * 2 + 2 * 3 * d * tf * w_bytes + 3 * tm * tf * 4
    outs = pl.pallas_call(
        functools.partial(_mlp_kernel, final_norm=final_norm, emit_bf16=emit_bf16),
        grid=(m // tm, nf),
        in_specs=[
            pl.BlockSpec((tm, d), lambda i, f: (i, 0)),
            pl.BlockSpec((None, 1, d), lambda i, f: (layer, 0, 0)),
            *w_specs,
            pl.BlockSpec((1, d), lambda i, f: (0, 0)),
        ],
        out_specs=out_specs,
        out_shape=out_shape,
        scratch_shapes=[pltpu.VMEM((tm, d), BF16)],
        compiler_params=pltpu.CompilerParams(
            dimension_semantics=("arbitrary", "arbitrary"),
            vmem_limit_bytes=_vmem_limit(vmem)),
        name="mlp",
    )(x, g, *w_args, final_g)
    return (outs[0], tuple(outs[1:])) if emit_bf16 else outs[0]


_POOL_TOP = 16
_CONV_TOP = 8


def _block_prompt_kernel(x_ref, g1_ref, win_ref, wout_ref, cos_ref, sin_ref, dmask_ref, qdec_ref, kdec_ref,
                         poolw_ref, pscale_ref, convw_ref, retg_ref, sgug_ref, sguw_ref, sgub_ref,
                         h_out, pool_out, conv_out, ret_out, sgu_out,
                         pool_ext, conv_ext, *, start, tm):
    c = pl.program_id(1)
    T = CHUNK
    n_chunks = tm // T

    @pl.when(c == 0)
    def _():
        pool_ext[0:_POOL_TOP, :] = jnp.zeros((_POOL_TOP, GROUP), F32)
        conv_ext[0:_CONV_TOP, :] = jnp.zeros((_CONV_TOP, GROUP), F32)
        ret_out[...] = jnp.zeros(ret_out.shape, F32)

    x = x_ref[0]
    xn = _rms(x, g1_ref[...]).astype(BF16)

    def proj(col):
        return _dot(xn, win_ref[:, col:col + GROUP])

    def out_proj(y, col):
        return _dot(y, wout_ref[col:col + GROUP, :])

    head_lanes = [slice(hd * HEAD_DIM, (hd + 1) * HEAD_DIM) for hd in range(HEADS)]
    chunk_rows = [slice(j * T, (j + 1) * T) for j in range(n_chunks)]

    def pool_mixer(a_proj):
        pool_ext[_POOL_TOP:_POOL_TOP + tm, :] = a_proj
        pos1 = lax.broadcasted_iota(jnp.int32, (tm, HEAD_DIM), 0) + (start + 1) + c * tm
        y = []
        for gi, win in enumerate(POOL_WINDOWS):
            lanes = head_lanes[gi]
            a = pool_ext[_POOL_TOP:_POOL_TOP + tm, lanes]
            wsum = a
            for j in range(1, win):
                wsum = wsum + pool_ext[_POOL_TOP - j:_POOL_TOP - j + tm, lanes]
            cnt = jnp.minimum(pos1, win).astype(F32)
            d = (wsum / cnt - a).astype(BF16)
            y.append((_dot(d, poolw_ref[gi]) * pscale_ref[:, lanes]).astype(BF16))
        pool_out[0] = pool_ext[_POOL_TOP + tm - POOL_BUF:_POOL_TOP + tm, :]
        pool_ext[_POOL_TOP - POOL_BUF:_POOL_TOP, :] = pool_ext[_POOL_TOP + tm - POOL_BUF:_POOL_TOP + tm, :]
        return jnp.concatenate(y, axis=-1)

    def conv_mixer(bg, cg, hh):
        z = cg * hh
        conv_ext[_CONV_TOP:_CONV_TOP + tm, :] = z
        acc = (conv_ext[_CONV_TOP - 2:_CONV_TOP - 2 + tm, :] * convw_ref[0:1, :]
               + conv_ext[_CONV_TOP - 1:_CONV_TOP - 1 + tm, :] * convw_ref[1:2, :]
               + z * convw_ref[2:3, :])
        conv_out[0] = conv_ext[_CONV_TOP + tm - 2:_CONV_TOP + tm, :]
        conv_ext[_CONV_TOP - 2:_CONV_TOP, :] = conv_ext[_CONV_TOP + tm - 2:_CONV_TOP + tm, :]
        return (bg * acc).astype(BF16)

    def retention(q_all, k_all, v_all, g_all, j, hd):
        rows, lanes = chunk_rows[j], head_lanes[hd]
        cos = cos_ref[rows, :]
        sin = sin_ref[rows, :]
        q = _rotate(q_all[rows, lanes], cos, sin)
        k = _rotate(k_all[rows, lanes], cos, sin) * (HEAD_DIM ** -0.5)
        v = v_all[rows, lanes].astype(BF16)
        s_prev = ret_out[0, hd]
        scores = lax.dot_general(q.astype(BF16), k.astype(BF16), (((1,), (1,)), ((), ())),
                                 preferred_element_type=F32) * dmask_ref[hd]
        o = _dot(scores.astype(BF16), v) + _dot((q * qdec_ref[hd]).astype(BF16), s_prev.astype(BF16))
        kv = lax.dot_general((k * kdec_ref[hd]).astype(BF16), v, (((0,), (0,)), ((), ())),
                             preferred_element_type=F32)
        ret_out[0, hd] = qdec_ref[hd, T - 1:T, :] * s_prev + kv
        on = _layernorm(o) * retg_ref[:, lanes]
        return (_silu(g_all[rows, lanes]) * on).astype(BF16)

    def sgu_mixer(u_all, vn, j):
        rows = chunk_rows[j]
        tri = lax.broadcasted_iota(jnp.int32, (T, T), 0) >= lax.broadcasted_iota(jnp.int32, (T, T), 1)
        y = []
        for hd in range(HEADS):
            lanes = head_lanes[hd]
            w = jnp.where(tri, sguw_ref[hd], 0.0).astype(BF16)
            mixed = _dot(w, vn[rows, lanes].astype(BF16)) + sgub_ref[hd]
            y.append((u_all[rows, lanes] * mixed).astype(BF16))
        return jnp.concatenate(y, axis=-1)

    q_all, k_all, v_all, g_all = proj(_Q), proj(_K), proj(_V), proj(_G)
    ret = functools.partial(retention, q_all, k_all, v_all, g_all)
    y_c = [[None] * HEADS for _ in range(n_chunks)]
    y_c[0][0] = ret(0, 0)
    a_proj = proj(_A)
    y_c[0][1] = ret(0, 1)
    cg = proj(_CG)
    y_c[0][2] = ret(0, 2)
    hh = proj(_HH)
    y_c[0][3] = ret(0, 3)
    bg = proj(_BG)
    later = iter([lambda: proj(_VV), lambda: proj(_U), lambda: pool_mixer(a_proj),
                  lambda: conv_mixer(bg, cg, hh)])
    fills = []
    for j in range(1, n_chunks):
        for hd in range(HEADS):
            y_c[j][hd] = ret(j, hd)
            nxt = next(later, None)
            if nxt is not None:
                fills.append(nxt())
    fills += [f() for f in later]
    vv, u_all, y_a, y_b = fills
    h = x + out_proj(jnp.concatenate([jnp.concatenate(r, axis=-1) for r in y_c], axis=0), _YC)
    vn = _layernorm(vv) * sgug_ref[...]
    y_d = [sgu_mixer(u_all, vn, 0)]
    h = h + out_proj(y_a, _YA)
    y_d += [sgu_mixer(u_all, vn, j) for j in range(1, n_chunks)]
    h = h + out_proj(y_b, _YB)
    h_out[0] = h + out_proj(jnp.concatenate(y_d, axis=0), _YD)
    sgu_out[0] = vn[tm - T:tm, :]


def block_prompt(x, norm_g, w_in, w_out, tables, params, layer, *, start, tm):
    b, l, d = x.shape
    cos, sin, dmask, qdec, kdec = tables
    poolw, pscale, convw, retg, sgug, sguw, sgub = params
    const3 = lambda shape: pl.BlockSpec(shape, lambda i, c: (0, 0, 0))
    lay3 = lambda shape: pl.BlockSpec((None,) + shape, lambda i, c: (layer, 0, 0))
    lay4 = lambda shape: pl.BlockSpec((None,) + shape, lambda i, c: (layer, 0, 0, 0))
    resident = lambda shape: pl.BlockSpec(shape, lambda i, c: (0, 0), pipeline_mode=pl.Buffered(1))
    vmem = (d * IN_WIDTH + 4 * GROUP * d) * 2 + 2 * 2 * tm * d * 4 + 2 * tm * IN_WIDTH * 4
    return pl.pallas_call(
        functools.partial(_block_prompt_kernel, start=start, tm=tm),
        grid=(b, l // tm),
        in_specs=[
            pl.BlockSpec((1, tm, d), lambda i, c: (i, c, 0)),
            lay3((1, d)),
            resident((d, IN_WIDTH)),
            resident((4 * GROUP, d)),
            pl.BlockSpec((tm, HEAD_DIM), lambda i, c: (c, 0)),
            pl.BlockSpec((tm, HEAD_DIM), lambda i, c: (c, 0)),
            const3((HEADS, CHUNK, CHUNK)),
            const3((HEADS, CHUNK, HEAD_DIM)),
            const3((HEADS, CHUNK, HEAD_DIM)),
            lay4((len(POOL_WINDOWS), HEAD_DIM, HEAD_DIM)),
            lay3((1, GROUP)),
            lay3((3, GROUP)),
            lay3((1, GROUP)),
            lay3((1, GROUP)),
            lay4((HEADS, CHUNK, CHUNK)),
            lay4((HEADS, CHUNK, HEAD_DIM)),
        ],
        out_specs=[
            pl.BlockSpec((1, tm, d), lambda i, c: (i, c, 0)),
            pl.BlockSpec((1, POOL_BUF, GROUP), lambda i, c: (i, 0, 0)),
            pl.BlockSpec((1, 2, GROUP), lambda i, c: (i, 0, 0)),
            pl.BlockSpec((1, HEADS, HEAD_DIM, HEAD_DIM), lambda i, c: (i, 0, 0, 0)),
            pl.BlockSpec((1, CHUNK, GROUP), lambda i, c: (i, 0, 0)),
        ],
        out_shape=[
            jax.ShapeDtypeStruct((b, l, d), F32),
            jax.ShapeDtypeStruct((b, POOL_BUF, GROUP), F32),
            jax.ShapeDtypeStruct((b, 2, GROUP), F32),
            jax.ShapeDtypeStruct((b, HEADS, HEAD_DIM, HEAD_DIM), F32),
            jax.ShapeDtypeStruct((b, CHUNK, GROUP), F32),
        ],
        scratch_shapes=[
            pltpu.VMEM((_POOL_TOP + tm, GROUP), F32),
            pltpu.VMEM((_CONV_TOP + tm, GROUP), F32),
        ],
        compiler_params=pltpu.CompilerParams(
            dimension_semantics=("arbitrary", "arbitrary"),
            vmem_limit_bytes=_vmem_limit(vmem)),
        name="block_prompt",
    )(x, norm_g, w_in, w_out, cos, sin, dmask, qdec, kdec, poolw, pscale, convw, retg, sgug, sguw, sgub)


_SEQ_BLOCK = 8


_N_SAMPLE_INPUTS = 14


def _mixer_sample_kernel(*refs, start, layer, all_layers):
    (p_ref, pool_ref, conv_ref, ret_ref, cos_ref, sin_ref, gam_ref,
     poolw_ref, pscale_ref, convw_ref, retg_ref, sgug_ref, sguw_ref, sgub_ref) = refs[:_N_SAMPLE_INPUTS]
    mix_ref, pool_out, conv_out, ret_out, sgu_out, qg_s, k_s, v_s, os_s = refs[-9:]
    TB = _SEQ_BLOCK

    if all_layers:
        stacked = (pool_out, conv_out, ret_out, sgu_out)
        for ref in stacked:
            for other in range(DEPTH):
                if other != layer:
                    ref[other] = jnp.zeros(ref.shape[1:], F32)
        pool_out, conv_out, ret_out, sgu_out = (ref.at[layer] for ref in stacked)

    a_all = p_ref[:, _A:_A + GROUP]
    for gi, win in enumerate(POOL_WINDOWS):
        lanes = slice(gi * HEAD_DIM, (gi + 1) * HEAD_DIM)
        a = a_all[:, lanes]
        wsum = a
        for j in range(1, win):
            wsum = wsum + pool_ref[:, POOL_BUF - j, lanes]
        cnt = float(min(start + 1, win))
        d = (wsum / cnt - a).astype(BF16)
        y = _dot(d, poolw_ref[gi]) * pscale_ref[:, lanes]
        mix_ref[:, _YA + gi * HEAD_DIM:_YA + (gi + 1) * HEAD_DIM] = y.astype(BF16)
    pool_out[:, 0:POOL_BUF - 1, :] = pool_ref[:, 1:POOL_BUF, :]
    pool_out[:, POOL_BUF - 1, :] = a_all

    z = p_ref[:, _CG:_CG + GROUP] * p_ref[:, _HH:_HH + GROUP]
    acc = (conv_ref[:, 0, :] * convw_ref[0:1, :] + conv_ref[:, 1, :] * convw_ref[1:2, :]
           + z * convw_ref[2:3, :])
    mix_ref[:, _YB:_YB + GROUP] = (p_ref[:, _BG:_BG + GROUP] * acc).astype(BF16)
    conv_out[:, 0, :] = conv_ref[:, 1, :]
    conv_out[:, 1, :] = z

    cos = cos_ref[...]
    sin = sin_ref[...]
    scores = []
    for h in range(HEADS):
        lanes = slice(h * HEAD_DIM, (h + 1) * HEAD_DIM)
        q = _rotate(p_ref[:, _Q + h * HEAD_DIM:_Q + (h + 1) * HEAD_DIM], cos, sin)
        k = _rotate(p_ref[:, _K + h * HEAD_DIM:_K + (h + 1) * HEAD_DIM], cos, sin) * (HEAD_DIM ** -0.5)
        scores.append(jnp.sum(q * k, axis=-1, keepdims=True))
        qg_s[:, lanes] = q * gam_ref[:, lanes]
        k_s[:, lanes] = k
    v_s[...] = p_ref[:, _V:_V + GROUP]

    first_row = lax.broadcasted_iota(jnp.int32, (8, HEAD_DIM), 0) == 0

    for b in range(TB):
        for h in range(HEADS):
            lanes = slice(h * HEAD_DIM, (h + 1) * HEAD_DIM)
            s_prev = ret_ref[b, h]
            q8 = jnp.broadcast_to(qg_s[b:b + 1, lanes], (8, HEAD_DIM))
            os_s[b:b + 1, lanes] = _dot(q8.astype(BF16), s_prev.astype(BF16))[0:1, :]
            k8 = jnp.where(first_row, jnp.broadcast_to(k_s[b:b + 1, lanes], (8, HEAD_DIM)), 0.0)
            v8 = jnp.broadcast_to(v_s[b:b + 1, lanes], (8, HEAD_DIM))
            kv = lax.dot_general(k8.astype(BF16), v8.astype(BF16), (((0,), (0,)), ((), ())),
                                 preferred_element_type=F32)
            ret_out[b, h] = gam_ref[:, lanes] * s_prev + kv

    for h in range(HEADS):
        lanes = slice(h * HEAD_DIM, (h + 1) * HEAD_DIM)
        o = scores[h] * v_s[:, lanes] + os_s[:, lanes]
        on = _layernorm(o) * retg_ref[:, lanes]
        g = p_ref[:, _G + h * HEAD_DIM:_G + (h + 1) * HEAD_DIM]
        mix_ref[:, _YC + h * HEAD_DIM:_YC + (h + 1) * HEAD_DIM] = (_silu(g) * on).astype(BF16)

    vn = _layernorm(p_ref[:, _VV:_VV + GROUP]) * sgug_ref[...]
    mix_ref[:, _YD:_YD + GROUP] = (p_ref[:, _U:_U + GROUP] * (sguw_ref[...] * vn + sgub_ref[...])).astype(BF16)
    sgu_out[:, 0, :] = vn


def mixer_sample(p, pool_state, conv_state, ret_state, tables, params, layer, stacked, *, start):
    n = p.shape[0]
    tb = _SEQ_BLOCK
    cos, sin, gam = tables
    poolw, pscale, convw, retg, sgug, sguw, sgub = params
    rows = lambda width: pl.BlockSpec((tb, width), lambda i: (i, 0))
    lay_rows = lambda r: pl.BlockSpec((None, tb, r, GROUP), lambda i: (layer, i, 0, 0))
    lay_ret = pl.BlockSpec((None, tb, HEADS, HEAD_DIM, HEAD_DIM), lambda i: (layer, i, 0, 0, 0))
    const2 = lambda shape: pl.BlockSpec(shape, lambda i: (0, 0))
    lay3 = lambda shape: pl.BlockSpec((None,) + shape, lambda i: (layer, 0, 0))
    lay4 = lambda shape: pl.BlockSpec((None,) + shape, lambda i: (layer, 0, 0, 0))
    in_specs = [
        rows(IN_WIDTH),
        lay_rows(POOL_BUF),
        lay_rows(2),
        lay_ret,
        const2((1, HEAD_DIM)),
        const2((1, HEAD_DIM)),
        const2((1, GROUP)),
        lay4((len(POOL_WINDOWS), HEAD_DIM, HEAD_DIM)),
        lay3((1, GROUP)),
        lay3((3, GROUP)),
        lay3((1, GROUP)),
        lay3((1, GROUP)),
        lay3((1, GROUP)),
        lay3((1, GROUP)),
    ]
    args = (p, pool_state, conv_state, ret_state, cos, sin, gam,
            poolw, pscale, convw, retg, sgug, sguw, sgub)
    assert len(args) == _N_SAMPLE_INPUTS
    aliases = {}
    first = stacked is None
    if first:
        all_rows = lambda r: pl.BlockSpec((DEPTH, tb, r, GROUP), lambda i: (0, i, 0, 0))
        state_specs = [all_rows(POOL_BUF), all_rows(2),
                       pl.BlockSpec((DEPTH, tb, HEADS, HEAD_DIM, HEAD_DIM), lambda i: (0, i, 0, 0, 0)),
                       all_rows(1)]
    else:
        state_specs = [lay_rows(POOL_BUF), lay_rows(2), lay_ret, lay_rows(1)]
        aliases = {len(args) + k: 1 + k for k in range(len(stacked))}
        in_specs = in_specs + [pl.BlockSpec(memory_space=pl.ANY)] * len(stacked)
        args = args + tuple(stacked)
    outs = pl.pallas_call(
        functools.partial(_mixer_sample_kernel, start=start, layer=layer, all_layers=first),
        grid=(n // tb,),
        in_specs=in_specs,
        out_specs=[rows(4 * GROUP)] + state_specs,
        out_shape=[
            jax.ShapeDtypeStruct((n, 4 * GROUP), BF16),
            jax.ShapeDtypeStruct((DEPTH, n, POOL_BUF, GROUP), F32),
            jax.ShapeDtypeStruct((DEPTH, n, 2, GROUP), F32),
            jax.ShapeDtypeStruct((DEPTH, n, HEADS, HEAD_DIM, HEAD_DIM), F32),
            jax.ShapeDtypeStruct((DEPTH, n, 1, GROUP), F32),
        ],
        scratch_shapes=[pltpu.VMEM((tb, GROUP), F32)] * 4,
        input_output_aliases=aliases,
        compiler_params=pltpu.CompilerParams(
            dimension_semantics=("arbitrary",),
            vmem_limit_bytes=VMEM_LIMIT_MAX),
        name="mixer_sample",
    )(*args)
    return outs[0], tuple(outs[1:])


def _rope_tables(start, length):
    half = HEAD_DIM // 2
    inv = ROPE_BASE ** (-jnp.arange(half, dtype=F32) / half)
    pos = start + jnp.arange(length)
    ang = pos.astype(F32)[:, None] * inv[None, :]
    cos, sin = jnp.cos(ang), jnp.sin(ang)
    return jnp.concatenate([cos, cos], axis=-1), jnp.concatenate([-sin, sin], axis=-1)


def _log_gamma():
    return jnp.log1p(-(2.0 ** (-5.0 - jnp.arange(HEADS, dtype=F32))))


def _decay_tables(c):
    lg = _log_gamma()
    idx = jnp.arange(c, dtype=F32)
    diff = idx[:, None] - idx[None, :]
    dmask = jnp.where(diff >= 0, jnp.exp(jnp.maximum(diff, 0.0)[None] * lg[:, None, None]), 0.0)
    qdec = jnp.exp((idx + 1.0)[None, :] * lg[:, None])[:, :, None]
    kdec = jnp.exp((c - 1.0 - idx)[None, :] * lg[:, None])[:, :, None]
    wide = lambda t: jnp.broadcast_to(t, (HEADS, c, HEAD_DIM))
    return dmask, wide(qdec), wide(kdec)


def _per_head_lanes(t):
    return jnp.repeat(t, HEAD_DIM, axis=-1)[..., None, :]


def kernel(x_prompt, x_sample, state_pool, state_conv, state_ret, norm1_g, w_in, pool_w, pool_scale,
           conv_w, ret_norm_g, sgu_norm_g, sgu_w, sgu_b, w_out, norm2_g, w_gate_up, w_down,
           final_norm_g):
    bp, lp, _ = x_prompt.shape
    ns = x_sample.shape[0]

    pool_w_b = pool_w.astype(BF16)
    row3 = lambda t: t[:, None, :]
    norm1 = row3(norm1_g)
    norm2 = row3(norm2_g)
    final_g = final_norm_g[None, :]

    cos_p, sin_p = _rope_tables(0, lp)
    prompt_tables = (cos_p, sin_p) + _decay_tables(CHUNK)
    sgu_b_wide = jnp.broadcast_to(sgu_b[:, :, :, None], sgu_b.shape + (HEAD_DIM,))
    prompt_params = (pool_w_b, row3(pool_scale), conv_w, row3(ret_norm_g), row3(sgu_norm_g), sgu_w, sgu_b_wide)

    cos_s, sin_s = _rope_tables(PAST_LEN, 1)
    gam = _per_head_lanes(jnp.exp(_log_gamma()))
    sample_tables = (cos_s, sin_s, gam)
    sample_params = (pool_w_b, row3(pool_scale), conv_w, row3(ret_norm_g), row3(sgu_norm_g),
                     _per_head_lanes(sgu_w[:, :, 0, 0]), _per_head_lanes(sgu_b[:, :, 0]))
    hp = x_prompt
    hs = x_sample.reshape(ns, D_MODEL)
    prompt_states = [[] for _ in range(4)]
    sample_states = None
    for l in range(DEPTH):
        fin = l == DEPTH - 1
        p, w_in_b = norm_matmul(hs, norm1, w_in, l, tn=1024)
        mix, sample_states = mixer_sample(p, state_pool, state_conv, state_ret, sample_tables,
                                          sample_params, l, sample_states, start=PAST_LEN)
        hs, w_out_b = matmul_residual(mix, w_out, hs, l, tn=1024)
        hs, w_mlp_b = mlp(hs, norm2, (w_gate_up, w_down), final_g, l, tm=ns, tf=512, final_norm=fin)
        hp, *states = block_prompt(hp, norm1, w_in_b, w_out_b, prompt_tables, prompt_params, l,
                                   start=0, tm=256)
        hp = mlp(hp.reshape(bp * lp, D_MODEL), norm2, w_mlp_b, final_g, l, tm=1024, tf=512,
                 final_norm=fin).reshape(bp, lp, D_MODEL)
        for lst, val in zip(prompt_states, states):
            lst.append(val)
    pool_p, conv_p, ret_p, sgu_p = (jnp.stack(s) for s in prompt_states)
    pool_s, conv_s, ret_s, sgu_s = sample_states
    return (hp, hs.reshape(ns, 1, D_MODEL), pool_p, pool_s, conv_p, conv_s, ret_p, ret_s, sgu_p, sgu_s)
```

```python
import functools

import jax
import jax.numpy as jnp
from jax import lax
from jax.experimental import pallas as pl
from jax.experimental.pallas import tpu as pltpu

F32 = jnp.float32
BF16 = jnp.bfloat16

D_MODEL = 2048
DEPTH = 2
GROUP = 512
N_SPLITS = 10
IN_WIDTH = N_SPLITS * GROUP
POOL_WINDOWS = (2, 4, 8, 16)
POOL_BUF = 15
HEADS = 4
HEAD_DIM = 128
CHUNK = 128
ROPE_BASE = 10000.0
D_FF = 5632
NORM_EPS = 1e-6
PAST_LEN = 16384

_A, _BG, _CG, _HH, _Q, _K, _V, _G, _U, _VV = (i * GROUP for i in range(N_SPLITS))
_YA, _YB, _YC, _YD = (i * GROUP for i in range(4))

MIB = 1024 * 1024
VMEM_BYTES_V7X = 64 * MIB


VMEM_LIMIT_MAX = VMEM_BYTES_V7X - 4 * MIB


def _vmem_limit(estimate, streaming=False):
    return VMEM_LIMIT_MAX if streaming else min(estimate + 4 * MIB, VMEM_LIMIT_MAX)


def _rms(x, g):
    ms = jnp.mean(x * x, axis=-1, keepdims=True)
    return x * lax.rsqrt(ms + NORM_EPS) * g


def _layernorm(x):
    mu = jnp.mean(x, axis=-1, keepdims=True)
    xc = x - mu
    return xc * lax.rsqrt(jnp.mean(xc * xc, axis=-1, keepdims=True) + NORM_EPS)


def _silu(x):
    return x * jax.nn.sigmoid(x)


def _dot(a, b):
    return jnp.dot(a, b, preferred_element_type=F32)


def _rotate(x, cos, sin_signed):
    return x * cos + pltpu.roll(x, HEAD_DIM // 2, 1) * sin_signed


def _norm_matmul_kernel(x_ref, g_ref, w_ref, o_ref, wb_ref, xn_ref):
    @pl.when(pl.program_id(0) == 0)
    def _():
        xn_ref[...] = _rms(x_ref[...], g_ref[...]).astype(BF16)

    wb = w_ref[...].astype(BF16)
    wb_ref[...] = wb
    o_ref[...] = _dot(xn_ref[...], wb)


def norm_matmul(x, g, w, layer, *, tn):
    m, k = x.shape
    n = w.shape[-1]
    vmem = 2 * k * tn * (4 + 2) + 2 * m * (k + tn) * 4 + m * k * 2
    return pl.pallas_call(
        _norm_matmul_kernel,
        grid=(n // tn,),
        in_specs=[
            pl.BlockSpec((m, k), lambda j: (0, 0)),
            pl.BlockSpec((None, 1, k), lambda j: (layer, 0, 0)),
            pl.BlockSpec((None, k, tn), lambda j: (layer, 0, j)),
        ],
        out_specs=[pl.BlockSpec((m, tn), lambda j: (0, j)),
                   pl.BlockSpec((k, tn), lambda j: (0, j))],
        out_shape=[jax.ShapeDtypeStruct((m, n), F32),
                   jax.ShapeDtypeStruct((k, n), BF16)],
        scratch_shapes=[pltpu.VMEM((m, k), BF16)],
        compiler_params=pltpu.CompilerParams(
            dimension_semantics=("arbitrary",),
            vmem_limit_bytes=_vmem_limit(vmem, streaming=True)),
        name="norm_matmul",
    )(x, g, w)


def _matmul_residual_kernel(a_ref, w_ref, r_ref, o_ref, wb_ref):
    wb = w_ref[...].astype(BF16)
    wb_ref[...] = wb
    o_ref[...] = r_ref[...] + _dot(a_ref[...], wb)


def matmul_residual(a, w, res, layer, *, tn):
    m, k = a.shape
    n = w.shape[-1]
    vmem = 2 * k * tn * (4 + 2) + 2 * m * (k * 2 + 2 * tn * 4)
    return pl.pallas_call(
        _matmul_residual_kernel,
        grid=(n // tn,),
        in_specs=[
            pl.BlockSpec((m, k), lambda j: (0, 0)),
            pl.BlockSpec((None, k, tn), lambda j: (layer, 0, j)),
            pl.BlockSpec((m, tn), lambda j: (0, j)),
        ],
        out_specs=[pl.BlockSpec((m, tn), lambda j: (0, j)),
                   pl.BlockSpec((k, tn), lambda j: (0, j))],
        out_shape=[jax.ShapeDtypeStruct((m, n), F32),
                   jax.ShapeDtypeStruct((k, n), BF16)],
        compiler_params=pltpu.CompilerParams(
            dimension_semantics=("arbitrary",),
            vmem_limit_bytes=_vmem_limit(vmem, streaming=True)),
        name="matmul_residual",
    )(a, w, res)


_MLP_TILE = 1024


def _mlp_start(first, x_refs, g_ref, o_ref, hn_ref):
    @pl.when(first)
    def _():
        row = 0
        for x_ref in x_refs:
            x = x_ref[...]
            rows = slice(row, row + x.shape[0])
            hn_ref[rows, :] = _rms(x, g_ref[...]).astype(BF16)
            o_ref[rows, :] = x
            row += x.shape[0]


def _mlp_accumulate(wg, wu, wd, o_ref, hn_ref):
    hn = hn_ref[...]
    act = (_silu(_dot(hn, wg)) * _dot(hn, wu)).astype(BF16)
    o_ref[...] += _dot(act, wd)


def _mlp_finish(last, fg_ref, o_ref, final_norm):
    if final_norm:
        @pl.when(last)
        def _():
            o_ref[...] = _rms(o_ref[...], fg_ref[...])


def _mlp_cast_kernel(xs_ref, xp_ref, g_ref, wg_ref, wu_ref, wd_ref, fg_ref,
                     o_ref, wgb_ref, wub_ref, wdb_ref, hn_ref, *, final_norm):
    f = pl.program_id(0)
    _mlp_start(f == 0, (xs_ref, xp_ref), g_ref, o_ref, hn_ref)
    wg, wu, wd = wg_ref[...].astype(BF16), wu_ref[...].astype(BF16), wd_ref[...].astype(BF16)
    wgb_ref[...] = wg
    wub_ref[...] = wu
    wdb_ref[...] = wd
    _mlp_accumulate(wg, wu, wd, o_ref, hn_ref)
    _mlp_finish(f == pl.num_programs(0) - 1, fg_ref, o_ref, final_norm)


def _mlp_rest_kernel(first_hbm, x_ref, g_ref, wg_ref, wu_ref, wd_ref, fg_ref, o_ref, hn_ref, sem,
                     *, nf, skip_rows, final_norm):
    s = pl.program_id(0)
    tm = o_ref.shape[0]

    @pl.when(s == 0)
    def _():
        copy = pltpu.make_async_copy(first_hbm.at[pl.ds(skip_rows, tm), :], o_ref, sem)
        copy.start()
        copy.wait()

    @pl.when(s > 0)
    def _():
        f = (s - 1) % nf
        _mlp_start(f == 0, (x_ref,), g_ref, o_ref, hn_ref)
        _mlp_accumulate(wg_ref[...], wu_ref[...], wd_ref[...], o_ref, hn_ref)
        _mlp_finish(f == nf - 1, fg_ref, o_ref, final_norm)


def mlp_cast(xs, xp, g, w_gate_up, w_down, final_g, layer, *, tp, tf, final_norm):
    ns, d = xs.shape
    m = ns + tp
    nf = D_FF // tf
    once = pl.Buffered(1)
    vmem = 2 * m * d * 4 + m * d * 2 + 2 * 3 * d * tf * (4 + 2) + 3 * m * tf * 4
    outs = pl.pallas_call(
        functools.partial(_mlp_cast_kernel, final_norm=final_norm),
        grid=(nf,),
        in_specs=[
            pl.BlockSpec((ns, d), lambda f: (0, 0), pipeline_mode=once),
            pl.BlockSpec((tp, d), lambda f: (0, 0), pipeline_mode=once),
            pl.BlockSpec((None, 1, d), lambda f: (layer, 0, 0)),
            pl.BlockSpec((None, d, tf), lambda f: (layer, 0, f)),
            pl.BlockSpec((None, d, tf), lambda f: (layer, 0, nf + f)),
            pl.BlockSpec((None, tf, d), lambda f: (layer, f, 0)),
            pl.BlockSpec((1, d), lambda f: (0, 0)),
        ],
        out_specs=[
            pl.BlockSpec((m, d), lambda f: (0, 0), pipeline_mode=once),
            pl.BlockSpec((d, tf), lambda f: (0, f)),
            pl.BlockSpec((d, tf), lambda f: (0, f)),
            pl.BlockSpec((tf, d), lambda f: (f, 0)),
        ],
        out_shape=[jax.ShapeDtypeStruct((m, d), F32),
                   jax.ShapeDtypeStruct((d, D_FF), BF16), jax.ShapeDtypeStruct((d, D_FF), BF16),
                   jax.ShapeDtypeStruct((D_FF, d), BF16)],
        scratch_shapes=[pltpu.VMEM((m, d), BF16)],
        compiler_params=pltpu.CompilerParams(
            dimension_semantics=("arbitrary",),
            vmem_limit_bytes=_vmem_limit(vmem)),
        name="mlp_cast",
    )(xs, xp, g, w_gate_up, w_gate_up, w_down, final_g)
    return outs[0], tuple(outs[1:])


def mlp_rest(first, x, g, weights, final_g, layer, *, skip_rows, tm, tf, final_norm):
    m, d = x.shape
    nf = D_FF // tf
    step = lambda s: jnp.maximum(s - 1, 0)
    x_tile = lambda s: 1 + step(s) // nf
    f_tile = lambda s: step(s) % nf
    vmem = 2 * 2 * tm * d * 4 + tm * d * 2 + 2 * 3 * d * tf * 2 + 3 * tm * tf * 4
    return pl.pallas_call(
        functools.partial(_mlp_rest_kernel, nf=nf, skip_rows=skip_rows, final_norm=final_norm),
        grid=(1 + (m // tm - 1) * nf,),
        in_specs=[
            pl.BlockSpec(memory_space=pl.ANY),
            pl.BlockSpec((tm, d), lambda s: (x_tile(s), 0)),
            pl.BlockSpec((None, 1, d), lambda s: (layer, 0, 0)),
            pl.BlockSpec((d, tf), lambda s: (0, f_tile(s))),
            pl.BlockSpec((d, tf), lambda s: (0, f_tile(s))),
            pl.BlockSpec((tf, d), lambda s: (f_tile(s), 0)),
            pl.BlockSpec((1, d), lambda s: (0, 0)),
        ],
        out_specs=pl.BlockSpec((tm, d), lambda s: (jnp.where(s == 0, 0, x_tile(s)), 0)),
        out_shape=jax.ShapeDtypeStruct((m, d), F32),
        scratch_shapes=[pltpu.VMEM((tm, d), BF16), pltpu.SemaphoreType.DMA(())],
        compiler_params=pltpu.CompilerParams(
            dimension_semantics=("arbitrary",),
            vmem_limit_bytes=_vmem_limit(vmem)),
        name="mlp_rest",
    )(first, x, g, *weights, final_g)


_POOL_TOP = 16
_CONV_TOP = 8


def _block_prompt_kernel(x_ref, g1_ref, win_ref, wout_ref, cos_ref, sin_ref, dmask_ref, qdec_ref, kdec_ref,
                         poolw_ref, pscale_ref, convw_ref, retg_ref, sgug_ref, sguw_ref, sgub_ref,
                         h_out, pool_out, conv_out, ret_out, sgu_out,
                         pool_ext, conv_ext, *, start, tm):
    c = pl.program_id(1)
    T = CHUNK
    n_chunks = tm // T

    @pl.when(c == 0)
    def _():
        pool_ext[0:_POOL_TOP, :] = jnp.zeros((_POOL_TOP, GROUP), F32)
        conv_ext[0:_CONV_TOP, :] = jnp.zeros((_CONV_TOP, GROUP), F32)
        ret_out[...] = jnp.zeros(ret_out.shape, F32)

    x = x_ref[0]
    xn = _rms(x, g1_ref[...]).astype(BF16)

    def proj(col):
        return _dot(xn, win_ref[:, col:col + GROUP])

    def out_proj(y, col):
        return _dot(y, wout_ref[col:col + GROUP, :])

    head_lanes = [slice(hd * HEAD_DIM, (hd + 1) * HEAD_DIM) for hd in range(HEADS)]
    chunk_rows = [slice(j * T, (j + 1) * T) for j in range(n_chunks)]

    def pool_mixer(a_proj):
        pool_ext[_POOL_TOP:_POOL_TOP + tm, :] = a_proj
        pos1 = lax.broadcasted_iota(jnp.int32, (tm, HEAD_DIM), 0) + (start + 1) + c * tm
        y = []
        for gi, win in enumerate(POOL_WINDOWS):
            lanes = head_lanes[gi]
            a = pool_ext[_POOL_TOP:_POOL_TOP + tm, lanes]
            wsum = a
            for j in range(1, win):
                wsum = wsum + pool_ext[_POOL_TOP - j:_POOL_TOP - j + tm, lanes]
            cnt = jnp.minimum(pos1, win).astype(F32)
            d = (wsum / cnt - a).astype(BF16)
            y.append((_dot(d, poolw_ref[gi]) * pscale_ref[:, lanes]).astype(BF16))
        pool_out[0] = pool_ext[_POOL_TOP + tm - POOL_BUF:_POOL_TOP + tm, :]
        pool_ext[_POOL_TOP - POOL_BUF:_POOL_TOP, :] = pool_ext[_POOL_TOP + tm - POOL_BUF:_POOL_TOP + tm, :]
        return jnp.concatenate(y, axis=-1)

    def conv_mixer(bg, cg, hh):
        z = cg * hh
        conv_ext[_CONV_TOP:_CONV_TOP + tm, :] = z
        acc = (conv_ext[_CONV_TOP - 2:_CONV_TOP - 2 + tm, :] * convw_ref[0:1, :]
               + conv_ext[_CONV_TOP - 1:_CONV_TOP - 1 + tm, :] * convw_ref[1:2, :]
               + z * convw_ref[2:3, :])
        conv_out[0] = conv_ext[_CONV_TOP + tm - 2:_CONV_TOP + tm, :]
        conv_ext[_CONV_TOP - 2:_CONV_TOP, :] = conv_ext[_CONV_TOP + tm - 2:_CONV_TOP + tm, :]
        return (bg * acc).astype(BF16)

    def retention(q_all, k_all, v_all, g_all, j, hd):
        rows, lanes = chunk_rows[j], head_lanes[hd]
        cos = cos_ref[rows, :]
        sin = sin_ref[rows, :]
        q = _rotate(q_all[rows, lanes], cos, sin)
        k = _rotate(k_all[rows, lanes], cos, sin) * (HEAD_DIM ** -0.5)
        v = v_all[rows, lanes].astype(BF16)
        s_prev = ret_out[0, hd]
        scores = lax.dot_general(q.astype(BF16), k.astype(BF16), (((1,), (1,)), ((), ())),
                                 preferred_element_type=F32) * dmask_ref[hd]
        o = _dot(scores.astype(BF16), v) + _dot((q * qdec_ref[hd]).astype(BF16), s_prev.astype(BF16))
        kv = lax.dot_general((k * kdec_ref[hd]).astype(BF16), v, (((0,), (0,)), ((), ())),
                             preferred_element_type=F32)
        ret_out[0, hd] = qdec_ref[hd, T - 1:T, :] * s_prev + kv
        on = _layernorm(o) * retg_ref[:, lanes]
        return (_silu(g_all[rows, lanes]) * on).astype(BF16)

    def sgu_mixer(u_all, vn, j):
        rows = chunk_rows[j]
        tri = lax.broadcasted_iota(jnp.int32, (T, T), 0) >= lax.broadcasted_iota(jnp.int32, (T, T), 1)
        y = []
        for hd in range(HEADS):
            lanes = head_lanes[hd]
            w = jnp.where(tri, sguw_ref[hd], 0.0).astype(BF16)
            mixed = _dot(w, vn[rows, lanes].astype(BF16)) + sgub_ref[hd]
            y.append((u_all[rows, lanes] * mixed).astype(BF16))
        return jnp.concatenate(y, axis=-1)

    q_all, k_all, v_all, g_all = proj(_Q), proj(_K), proj(_V), proj(_G)
    ret = functools.partial(retention, q_all, k_all, v_all, g_all)
    y_c = [[None] * HEADS for _ in range(n_chunks)]
    y_c[0][0] = ret(0, 0)
    a_proj = proj(_A)
    y_c[0][1] = ret(0, 1)
    cg = proj(_CG)
    y_c[0][2] = ret(0, 2)
    hh = proj(_HH)
    y_c[0][3] = ret(0, 3)
    bg = proj(_BG)
    later = iter([lambda: proj(_VV), lambda: proj(_U), lambda: pool_mixer(a_proj),
                  lambda: conv_mixer(bg, cg, hh)])
    fills = []
    for j in range(1, n_chunks):
        for hd in range(HEADS):
            y_c[j][hd] = ret(j, hd)
            nxt = next(later, None)
            if nxt is not None:
                fills.append(nxt())
    fills += [f() for f in later]
    vv, u_all, y_a, y_b = fills
    h = x + out_proj(jnp.concatenate([jnp.concatenate(r, axis=-1) for r in y_c], axis=0), _YC)
    vn = _layernorm(vv) * sgug_ref[...]
    y_d = [sgu_mixer(u_all, vn, 0)]
    h = h + out_proj(y_a, _YA)
    y_d += [sgu_mixer(u_all, vn, j) for j in range(1, n_chunks)]
    h = h + out_proj(y_b, _YB)
    h_out[0] = h + out_proj(jnp.concatenate(y_d, axis=0), _YD)
    sgu_out[0] = vn[tm - T:tm, :]


def block_prompt(x, norm_g, w_in, w_out, tables, params, layer, *, start, tm):
    b, l, d = x.shape
    cos, sin, dmask, qdec, kdec = tables
    poolw, pscale, convw, retg, sgug, sguw, sgub = params
    const3 = lambda shape: pl.BlockSpec(shape, lambda i, c: (0, 0, 0))
    lay3 = lambda shape: pl.BlockSpec((None,) + shape, lambda i, c: (layer, 0, 0))
    lay4 = lambda shape: pl.BlockSpec((None,) + shape, lambda i, c: (layer, 0, 0, 0))
    resident = lambda shape: pl.BlockSpec(shape, lambda i, c: (0, 0), pipeline_mode=pl.Buffered(1))
    vmem = (d * IN_WIDTH + 4 * GROUP * d) * 2 + 2 * 2 * tm * d * 4 + 2 * tm * IN_WIDTH * 4
    return pl.pallas_call(
        functools.partial(_block_prompt_kernel, start=start, tm=tm),
        grid=(b, l // tm),
        in_specs=[
            pl.BlockSpec((1, tm, d), lambda i, c: (i, c, 0)),
            lay3((1, d)),
            resident((d, IN_WIDTH)),
            resident((4 * GROUP, d)),
            pl.BlockSpec((tm, HEAD_DIM), lambda i, c: (c, 0)),
            pl.BlockSpec((tm, HEAD_DIM), lambda i, c: (c, 0)),
            const3((HEADS, CHUNK, CHUNK)),
            const3((HEADS, CHUNK, HEAD_DIM)),
            const3((HEADS, CHUNK, HEAD_DIM)),
            lay4((len(POOL_WINDOWS), HEAD_DIM, HEAD_DIM)),
            lay3((1, GROUP)),
            lay3((3, GROUP)),
            lay3((1, GROUP)),
            lay3((1, GROUP)),
            lay4((HEADS, CHUNK, CHUNK)),
            lay4((HEADS, CHUNK, HEAD_DIM)),
        ],
        out_specs=[
            pl.BlockSpec((1, tm, d), lambda i, c: (i, c, 0)),
            pl.BlockSpec((1, POOL_BUF, GROUP), lambda i, c: (i, 0, 0)),
            pl.BlockSpec((1, 2, GROUP), lambda i, c: (i, 0, 0)),
            pl.BlockSpec((1, HEADS, HEAD_DIM, HEAD_DIM), lambda i, c: (i, 0, 0, 0)),
            pl.BlockSpec((1, CHUNK, GROUP), lambda i, c: (i, 0, 0)),
        ],
        out_shape=[
            jax.ShapeDtypeStruct((b, l, d), F32),
            jax.ShapeDtypeStruct((b, POOL_BUF, GROUP), F32),
            jax.ShapeDtypeStruct((b, 2, GROUP), F32),
            jax.ShapeDtypeStruct((b, HEADS, HEAD_DIM, HEAD_DIM), F32),
            jax.ShapeDtypeStruct((b, CHUNK, GROUP), F32),
        ],
        scratch_shapes=[
            pltpu.VMEM((_POOL_TOP + tm, GROUP), F32),
            pltpu.VMEM((_CONV_TOP + tm, GROUP), F32),
        ],
        compiler_params=pltpu.CompilerParams(
            dimension_semantics=("arbitrary", "arbitrary"),
            vmem_limit_bytes=_vmem_limit(vmem)),
        name="block_prompt",
    )(x, norm_g, w_in, w_out, cos, sin, dmask, qdec, kdec, poolw, pscale, convw, retg, sgug, sguw, sgub)


_SEQ_BLOCK = 16


_N_SAMPLE_INPUTS = 14


def _mixer_sample_kernel(*refs, start, layer, all_layers):
    (p_ref, pool_ref, conv_ref, ret_ref, cos_ref, sin_ref, gam_ref,
     poolw_ref, pscale_ref, convw_ref, retg_ref, sgug_ref, sguw_ref, sgub_ref) = refs[:_N_SAMPLE_INPUTS]
    mix_ref, pool_out, conv_out, ret_out, sgu_out, qg_s, k_s, v_s, os_s = refs[-9:]
    TB = _SEQ_BLOCK

    if all_layers:
        stacked = (pool_out, conv_out, ret_out, sgu_out)
        for ref in stacked:
            for other in range(DEPTH):
                if other != layer:
                    ref[other] = jnp.zeros(ref.shape[1:], F32)
        pool_out, conv_out, ret_out, sgu_out = (ref.at[layer] for ref in stacked)

    a_all = p_ref[:, _A:_A + GROUP]
    for gi, win in enumerate(POOL_WINDOWS):
        lanes = slice(gi * HEAD_DIM, (gi + 1) * HEAD_DIM)
        a = a_all[:, lanes]
        wsum = a
        for j in range(1, win):
            wsum = wsum + pool_ref[:, POOL_BUF - j, lanes]
        cnt = float(min(start + 1, win))
        d = (wsum / cnt - a).astype(BF16)
        y = _dot(d, poolw_ref[gi]) * pscale_ref[:, lanes]
        mix_ref[:, _YA + gi * HEAD_DIM:_YA + (gi + 1) * HEAD_DIM] = y.astype(BF16)
    pool_out[:, 0:POOL_BUF - 1, :] = pool_ref[:, 1:POOL_BUF, :]
    pool_out[:, POOL_BUF - 1, :] = a_all

    z = p_ref[:, _CG:_CG + GROUP] * p_ref[:, _HH:_HH + GROUP]
    acc = (conv_ref[:, 0, :] * convw_ref[0:1, :] + conv_ref[:, 1, :] * convw_ref[1:2, :]
           + z * convw_ref[2:3, :])
    mix_ref[:, _YB:_YB + GROUP] = (p_ref[:, _BG:_BG + GROUP] * acc).astype(BF16)
    conv_out[:, 0, :] = conv_ref[:, 1, :]
    conv_out[:, 1, :] = z

    cos = cos_ref[...]
    sin = sin_ref[...]
    scores = []
    for h in range(HEADS):
        lanes = slice(h * HEAD_DIM, (h + 1) * HEAD_DIM)
        q = _rotate(p_ref[:, _Q + h * HEAD_DIM:_Q + (h + 1) * HEAD_DIM], cos, sin)
        k = _rotate(p_ref[:, _K + h * HEAD_DIM:_K + (h + 1) * HEAD_DIM], cos, sin) * (HEAD_DIM ** -0.5)
        scores.append(jnp.sum(q * k, axis=-1, keepdims=True))
        qg_s[:, lanes] = q * gam_ref[:, lanes]
        k_s[:, lanes] = k
    v_s[...] = p_ref[:, _V:_V + GROUP]

    first_row = lax.broadcasted_iota(jnp.int32, (8, HEAD_DIM), 0) == 0

    for b in range(TB):
        for h in range(HEADS):
            lanes = slice(h * HEAD_DIM, (h + 1) * HEAD_DIM)
            s_prev = ret_ref[b, h]
            q8 = jnp.broadcast_to(qg_s[b:b + 1, lanes], (8, HEAD_DIM))
            os_s[b:b + 1, lanes] = _dot(q8.astype(BF16), s_prev.astype(BF16))[0:1, :]
            k8 = jnp.where(first_row, jnp.broadcast_to(k_s[b:b + 1, lanes], (8, HEAD_DIM)), 0.0)
            v8 = jnp.broadcast_to(v_s[b:b + 1, lanes], (8, HEAD_DIM))
            kv = lax.dot_general(k8.astype(BF16), v8.astype(BF16), (((0,), (0,)), ((), ())),
                                 preferred_element_type=F32)
            ret_out[b, h] = gam_ref[:, lanes] * s_prev + kv

    for h in range(HEADS):
        lanes = slice(h * HEAD_DIM, (h + 1) * HEAD_DIM)
        o = scores[h] * v_s[:, lanes] + os_s[:, lanes]
        on = _layernorm(o) * retg_ref[:, lanes]
        g = p_ref[:, _G + h * HEAD_DIM:_G + (h + 1) * HEAD_DIM]
        mix_ref[:, _YC + h * HEAD_DIM:_YC + (h + 1) * HEAD_DIM] = (_silu(g) * on).astype(BF16)

    vn = _layernorm(p_ref[:, _VV:_VV + GROUP]) * sgug_ref[...]
    mix_ref[:, _YD:_YD + GROUP] = (p_ref[:, _U:_U + GROUP] * (sguw_ref[...] * vn + sgub_ref[...])).astype(BF16)
    sgu_out[:, 0, :] = vn


def mixer_sample(p, pool_state, conv_state, ret_state, tables, params, layer, stacked, *, start):
    n = p.shape[0]
    tb = _SEQ_BLOCK
    cos, sin, gam = tables
    poolw, pscale, convw, retg, sgug, sguw, sgub = params
    rows = lambda width: pl.BlockSpec((tb, width), lambda i: (i, 0))
    lay_rows = lambda r: pl.BlockSpec((None, tb, r, GROUP), lambda i: (layer, i, 0, 0))
    lay_ret = pl.BlockSpec((None, tb, HEADS, HEAD_DIM, HEAD_DIM), lambda i: (layer, i, 0, 0, 0))
    const2 = lambda shape: pl.BlockSpec(shape, lambda i: (0, 0))
    lay3 = lambda shape: pl.BlockSpec((None,) + shape, lambda i: (layer, 0, 0))
    lay4 = lambda shape: pl.BlockSpec((None,) + shape, lambda i: (layer, 0, 0, 0))
    in_specs = [
        rows(IN_WIDTH),
        lay_rows(POOL_BUF),
        lay_rows(2),
        lay_ret,
        const2((1, HEAD_DIM)),
        const2((1, HEAD_DIM)),
        const2((1, GROUP)),
        lay4((len(POOL_WINDOWS), HEAD_DIM, HEAD_DIM)),
        lay3((1, GROUP)),
        lay3((3, GROUP)),
        lay3((1, GROUP)),
        lay3((1, GROUP)),
        lay3((1, GROUP)),
        lay3((1, GROUP)),
    ]
    args = (p, pool_state, conv_state, ret_state, cos, sin, gam,
            poolw, pscale, convw, retg, sgug, sguw, sgub)
    assert len(args) == _N_SAMPLE_INPUTS
    aliases = {}
    first = stacked is None
    state_block = tb * (16 + 8 + 8 + HEAD_DIM) * GROUP * 4
    vmem = 2 * (tb * IN_WIDTH * 4 + state_block * (1 + (DEPTH if first else 1)))
    if first:
        all_rows = lambda r: pl.BlockSpec((DEPTH, tb, r, GROUP), lambda i: (0, i, 0, 0))
        state_specs = [all_rows(POOL_BUF), all_rows(2),
                       pl.BlockSpec((DEPTH, tb, HEADS, HEAD_DIM, HEAD_DIM), lambda i: (0, i, 0, 0, 0)),
                       all_rows(1)]
    else:
        state_specs = [lay_rows(POOL_BUF), lay_rows(2), lay_ret, lay_rows(1)]
        aliases = {len(args) + k: 1 + k for k in range(len(stacked))}
        in_specs = in_specs + [pl.BlockSpec(memory_space=pl.ANY)] * len(stacked)
        args = args + tuple(stacked)
    outs = pl.pallas_call(
        functools.partial(_mixer_sample_kernel, start=start, layer=layer, all_layers=first),
        grid=(n // tb,),
        in_specs=in_specs,
        out_specs=[rows(4 * GROUP)] + state_specs,
        out_shape=[
            jax.ShapeDtypeStruct((n, 4 * GROUP), BF16),
            jax.ShapeDtypeStruct((DEPTH, n, POOL_BUF, GROUP), F32),
            jax.ShapeDtypeStruct((DEPTH, n, 2, GROUP), F32),
            jax.ShapeDtypeStruct((DEPTH, n, HEADS, HEAD_DIM, HEAD_DIM), F32),
            jax.ShapeDtypeStruct((DEPTH, n, 1, GROUP), F32),
        ],
        scratch_shapes=[pltpu.VMEM((tb, GROUP), F32)] * 4,
        input_output_aliases=aliases,
        compiler_params=pltpu.CompilerParams(
            dimension_semantics=("arbitrary",),
            vmem_limit_bytes=_vmem_limit(vmem, streaming=True)),
        name="mixer_sample",
    )(*args)
    return outs[0], tuple(outs[1:])


def _rope_tables(start, length):
    half = HEAD_DIM // 2
    inv = ROPE_BASE ** (-jnp.arange(half, dtype=F32) / half)
    pos = start + jnp.arange(length)
    ang = pos.astype(F32)[:, None] * inv[None, :]
    cos, sin = jnp.cos(ang), jnp.sin(ang)
    return jnp.concatenate([cos, cos], axis=-1), jnp.concatenate([-sin, sin], axis=-1)


def _log_gamma():
    return jnp.log1p(-(2.0 ** (-5.0 - jnp.arange(HEADS, dtype=F32))))


def _decay_tables(c):
    lg = _log_gamma()
    idx = jnp.arange(c, dtype=F32)
    diff = idx[:, None] - idx[None, :]
    dmask = jnp.where(diff >= 0, jnp.exp(jnp.maximum(diff, 0.0)[None] * lg[:, None, None]), 0.0)
    qdec = jnp.exp((idx + 1.0)[None, :] * lg[:, None])[:, :, None]
    kdec = jnp.exp((c - 1.0 - idx)[None, :] * lg[:, None])[:, :, None]
    wide = lambda t: jnp.broadcast_to(t, (HEADS, c, HEAD_DIM))
    return dmask, wide(qdec), wide(kdec)


def _per_head_lanes(t):
    return jnp.repeat(t, HEAD_DIM, axis=-1)[..., None, :]


def kernel(x_prompt, x_sample, state_pool, state_conv, state_ret, norm1_g, w_in, pool_w, pool_scale,
           conv_w, ret_norm_g, sgu_norm_g, sgu_w, sgu_b, w_out, norm2_g, w_gate_up, w_down,
           final_norm_g):
    bp, lp, _ = x_prompt.shape
    ns = x_sample.shape[0]

    pool_w_b = pool_w.astype(BF16)
    row3 = lambda t: t[:, None, :]
    norm1 = row3(norm1_g)
    norm2 = row3(norm2_g)
    final_g = final_norm_g[None, :]

    cos_p, sin_p = _rope_tables(0, lp)
    prompt_tables = (cos_p, sin_p) + _decay_tables(CHUNK)
    sgu_b_wide = jnp.broadcast_to(sgu_b[:, :, :, None], sgu_b.shape + (HEAD_DIM,))
    prompt_params = (pool_w_b, row3(pool_scale), conv_w, row3(ret_norm_g), row3(sgu_norm_g), sgu_w, sgu_b_wide)

    cos_s, sin_s = _rope_tables(PAST_LEN, 1)
    gam = _per_head_lanes(jnp.exp(_log_gamma()))
    sample_tables = (cos_s, sin_s, gam)
    sample_params = (pool_w_b, row3(pool_scale), conv_w, row3(ret_norm_g), row3(sgu_norm_g),
                     _per_head_lanes(sgu_w[:, :, 0, 0]), _per_head_lanes(sgu_b[:, :, 0]))
    hp = x_prompt
    hs = x_sample.reshape(ns, D_MODEL)
    prompt_states = [[] for _ in range(4)]
    sample_states = None
    for l in range(DEPTH):
        fin = l == DEPTH - 1
        p, w_in_b = norm_matmul(hs, norm1, w_in, l, tn=512)
        mix, sample_states = mixer_sample(p, state_pool, state_conv, state_ret, sample_tables,
                                          sample_params, l, sample_states, start=PAST_LEN)
        hs, w_out_b = matmul_residual(mix, w_out, hs, l, tn=1024)
        hp, *states = block_prompt(hp, norm1, w_in_b, w_out_b, prompt_tables, prompt_params, l,
                                   start=0, tm=256)
        for lst, val in zip(prompt_states, states):
            lst.append(val)
        hp = hp.reshape(bp * lp, D_MODEL)
        first, w_mlp_b = mlp_cast(hs, hp, norm2, w_gate_up, w_down, final_g, l, tp=_MLP_TILE, tf=256,
                                  final_norm=fin)
        hs = first[:ns]
        hp = mlp_rest(first, hp, norm2, w_mlp_b, final_g, l, skip_rows=ns, tm=_MLP_TILE, tf=512,
                      final_norm=fin).reshape(bp, lp, D_MODEL)
    pool_p, conv_p, ret_p, sgu_p = (jnp.stack(s) for s in prompt_states)
    pool_s, conv_s, ret_s, sgu_s = sample_states
    return (hp, hs.reshape(ns, 1, D_MODEL), pool_p, pool_s, conv_p, conv_s, ret_p, ret_s, sgu_p, sgu_s)
```

```python
import functools

import jax
import jax.numpy as jnp
from jax import lax
from jax.experimental import pallas as pl
from jax.experimental.pallas import tpu as pltpu

F32 = jnp.float32
BF16 = jnp.bfloat16

D_MODEL = 2048
DEPTH = 2
GROUP = 512
N_SPLITS = 10
IN_WIDTH = N_SPLITS * GROUP
POOL_WINDOWS = (2, 4, 8, 16)
POOL_BUF = 15
HEADS = 4
HEAD_DIM = 128
CHUNK = 128
ROPE_BASE = 10000.0
D_FF = 5632
NORM_EPS = 1e-6
PAST_LEN = 16384

_A, _BG, _CG, _HH, _Q, _K, _V, _G, _U, _VV = (i * GROUP for i in range(N_SPLITS))
_YA, _YB, _YC, _YD = (i * GROUP for i in range(4))

MIB = 1024 * 1024
VMEM_BYTES_V7X = 64 * MIB


VMEM_LIMIT_MAX = VMEM_BYTES_V7X - 4 * MIB


def _vmem_limit(estimate):
    return min(estimate + 4 * MIB, VMEM_LIMIT_MAX)


def _rms(x, g):
    ms = jnp.mean(x * x, axis=-1, keepdims=True)
    return x * lax.rsqrt(ms + NORM_EPS) * g


def _layernorm(x):
    mu = jnp.mean(x, axis=-1, keepdims=True)
    xc = x - mu
    return xc * lax.rsqrt(jnp.mean(xc * xc, axis=-1, keepdims=True) + NORM_EPS)


def _silu(x):
    return x * jax.nn.sigmoid(x)


def _dot(a, b):
    return jnp.dot(a, b, preferred_element_type=F32)


def _rotate(x, cos, sin_signed):
    return x * cos + pltpu.roll(x, HEAD_DIM // 2, 1) * sin_signed


def _norm_matmul_kernel(x_ref, g_ref, w_ref, o_ref, wb_ref, xn_ref):
    @pl.when(pl.program_id(0) == 0)
    def _():
        xn_ref[...] = _rms(x_ref[...], g_ref[...]).astype(BF16)

    wb = w_ref[...].astype(BF16)
    wb_ref[...] = wb
    o_ref[...] = _dot(xn_ref[...], wb)


def norm_matmul(x, g, w, layer, *, tn):
    m, k = x.shape
    n = w.shape[-1]
    vmem = 2 * k * tn * (4 + 2) + 2 * m * (k + tn) * 4 + m * k * 2
    return pl.pallas_call(
        _norm_matmul_kernel,
        grid=(n // tn,),
        in_specs=[
            pl.BlockSpec((m, k), lambda j: (0, 0)),
            pl.BlockSpec((None, 1, k), lambda j: (layer, 0, 0)),
            pl.BlockSpec((None, k, tn), lambda j: (layer, 0, j)),
        ],
        out_specs=[pl.BlockSpec((m, tn), lambda j: (0, j)),
                   pl.BlockSpec((k, tn), lambda j: (0, j))],
        out_shape=[jax.ShapeDtypeStruct((m, n), F32),
                   jax.ShapeDtypeStruct((k, n), BF16)],
        scratch_shapes=[pltpu.VMEM((m, k), BF16)],
        compiler_params=pltpu.CompilerParams(
            dimension_semantics=("arbitrary",),
            vmem_limit_bytes=_vmem_limit(vmem)),
        name="norm_matmul",
    )(x, g, w)


def _matmul_residual_kernel(a_ref, w_ref, r_ref, o_ref, wb_ref):
    wb = w_ref[...].astype(BF16)
    wb_ref[...] = wb
    o_ref[...] = r_ref[...] + _dot(a_ref[...], wb)


def matmul_residual(a, w, res, layer, *, tn):
    m, k = a.shape
    n = w.shape[-1]
    vmem = 2 * k * tn * (4 + 2) + 2 * m * (k * 2 + 2 * tn * 4)
    return pl.pallas_call(
        _matmul_residual_kernel,
        grid=(n // tn,),
        in_specs=[
            pl.BlockSpec((m, k), lambda j: (0, 0)),
            pl.BlockSpec((None, k, tn), lambda j: (layer, 0, j)),
            pl.BlockSpec((m, tn), lambda j: (0, j)),
        ],
        out_specs=[pl.BlockSpec((m, tn), lambda j: (0, j)),
                   pl.BlockSpec((k, tn), lambda j: (0, j))],
        out_shape=[jax.ShapeDtypeStruct((m, n), F32),
                   jax.ShapeDtypeStruct((k, n), BF16)],
        compiler_params=pltpu.CompilerParams(
            dimension_semantics=("arbitrary",),
            vmem_limit_bytes=_vmem_limit(vmem)),
        name="matmul_residual",
    )(a, w, res)


_MLP_TILE = 1024


def _mlp_start(first, x_refs, g_ref, o_ref, hn_ref):
    @pl.when(first)
    def _():
        row = 0
        for x_ref in x_refs:
            x = x_ref[...]
            rows = slice(row, row + x.shape[0])
            hn_ref[rows, :] = _rms(x, g_ref[...]).astype(BF16)
            o_ref[rows, :] = x
            row += x.shape[0]


def _mlp_accumulate(wg, wu, wd, o_ref, hn_ref):
    hn = hn_ref[...]
    act = (_silu(_dot(hn, wg)) * _dot(hn, wu)).astype(BF16)
    o_ref[...] += _dot(act, wd)


def _mlp_finish(last, fg_ref, o_ref, final_norm):
    if final_norm:
        @pl.when(last)
        def _():
            o_ref[...] = _rms(o_ref[...], fg_ref[...])


def _mlp_cast_kernel(xs_ref, xp_ref, g_ref, wg_ref, wu_ref, wd_ref, fg_ref,
                     o_ref, wgb_ref, wub_ref, wdb_ref, hn_ref, *, final_norm):
    f = pl.program_id(0)
    _mlp_start(f == 0, (xs_ref, xp_ref), g_ref, o_ref, hn_ref)
    wg, wu, wd = wg_ref[...].astype(BF16), wu_ref[...].astype(BF16), wd_ref[...].astype(BF16)
    wgb_ref[...] = wg
    wub_ref[...] = wu
    wdb_ref[...] = wd
    _mlp_accumulate(wg, wu, wd, o_ref, hn_ref)
    _mlp_finish(f == pl.num_programs(0) - 1, fg_ref, o_ref, final_norm)


def _mlp_rest_kernel(first_hbm, x_ref, g_ref, wg_ref, wu_ref, wd_ref, fg_ref, o_ref, hn_ref, sem,
                     *, nf, skip_rows, final_norm):
    s = pl.program_id(0)
    tm = o_ref.shape[0]

    @pl.when(s == 0)
    def _():
        copy = pltpu.make_async_copy(first_hbm.at[pl.ds(skip_rows, tm), :], o_ref, sem)
        copy.start()
        copy.wait()

    @pl.when(s > 0)
    def _():
        f = (s - 1) % nf
        _mlp_start(f == 0, (x_ref,), g_ref, o_ref, hn_ref)
        _mlp_accumulate(wg_ref[...], wu_ref[...], wd_ref[...], o_ref, hn_ref)
        _mlp_finish(f == nf - 1, fg_ref, o_ref, final_norm)


def mlp_cast(xs, xp, g, w_gate_up, w_down, final_g, layer, *, tp, tf, final_norm):
    ns, d = xs.shape
    m = ns + tp
    nf = D_FF // tf
    once = pl.Buffered(1)
    vmem = 2 * m * d * 4 + m * d * 2 + 2 * 3 * d * tf * (4 + 2) + 3 * m * tf * 4
    outs = pl.pallas_call(
        functools.partial(_mlp_cast_kernel, final_norm=final_norm),
        grid=(nf,),
        in_specs=[
            pl.BlockSpec((ns, d), lambda f: (0, 0), pipeline_mode=once),
            pl.BlockSpec((tp, d), lambda f: (0, 0), pipeline_mode=once),
            pl.BlockSpec((None, 1, d), lambda f: (layer, 0, 0)),
            pl.BlockSpec((None, d, tf), lambda f: (layer, 0, f)),
            pl.BlockSpec((None, d, tf), lambda f: (layer, 0, nf + f)),
            pl.BlockSpec((None, tf, d), lambda f: (layer, f, 0)),
            pl.BlockSpec((1, d), lambda f: (0, 0)),
        ],
        out_specs=[
            pl.BlockSpec((m, d), lambda f: (0, 0), pipeline_mode=once),
            pl.BlockSpec((d, tf), lambda f: (0, f)),
            pl.BlockSpec((d, tf), lambda f: (0, f)),
            pl.BlockSpec((tf, d), lambda f: (f, 0)),
        ],
        out_shape=[jax.ShapeDtypeStruct((m, d), F32),
                   jax.ShapeDtypeStruct((d, D_FF), BF16), jax.ShapeDtypeStruct((d, D_FF), BF16),
                   jax.ShapeDtypeStruct((D_FF, d), BF16)],
        scratch_shapes=[pltpu.VMEM((m, d), BF16)],
        compiler_params=pltpu.CompilerParams(
            dimension_semantics=("arbitrary",),
            vmem_limit_bytes=_vmem_limit(vmem)),
        name="mlp_cast",
    )(xs, xp, g, w_gate_up, w_gate_up, w_down, final_g)
    return outs[0], tuple(outs[1:])


def mlp_rest(first, x, g, weights, final_g, layer, *, skip_rows, tm, tf, final_norm):
    m, d = x.shape
    nf = D_FF // tf
    step = lambda s: jnp.maximum(s - 1, 0)
    x_tile = lambda s: 1 + step(s) // nf
    f_tile = lambda s: step(s) % nf
    vmem = 2 * 2 * tm * d * 4 + tm * d * 2 + 2 * 3 * d * tf * 2 + 3 * tm * tf * 4
    return pl.pallas_call(
        functools.partial(_mlp_rest_kernel, nf=nf, skip_rows=skip_rows, final_norm=final_norm),
        grid=(1 + (m // tm - 1) * nf,),
        in_specs=[
            pl.BlockSpec(memory_space=pl.ANY),
            pl.BlockSpec((tm, d), lambda s: (x_tile(s), 0)),
            pl.BlockSpec((None, 1, d), lambda s: (layer, 0, 0)),
            pl.BlockSpec((d, tf), lambda s: (0, f_tile(s))),
            pl.BlockSpec((d, tf), lambda s: (0, f_tile(s))),
            pl.BlockSpec((tf, d), lambda s: (f_tile(s), 0)),
            pl.BlockSpec((1, d), lambda s: (0, 0)),
        ],
        out_specs=pl.BlockSpec((tm, d), lambda s: (jnp.where(s == 0, 0, x_tile(s)), 0)),
        out_shape=jax.ShapeDtypeStruct((m, d), F32),
        scratch_shapes=[pltpu.VMEM((tm, d), BF16), pltpu.SemaphoreType.DMA(())],
        compiler_params=pltpu.CompilerParams(
            dimension_semantics=("arbitrary",),
            vmem_limit_bytes=_vmem_limit(vmem)),
        name="mlp_rest",
    )(first, x, g, *weights, final_g)


_POOL_TOP = 16
_CONV_TOP = 8


def _block_prompt_kernel(x_ref, g1_ref, win_ref, wout_ref, cos_ref, sin_ref, dmask_ref, qdec_ref, kdec_ref,
                         poolw_ref, pscale_ref, convw_ref, retg_ref, sgug_ref, sguw_ref, sgub_ref,
                         h_out, pool_out, conv_out, ret_out, sgu_out,
                         pool_ext, conv_ext, *, start, tm):
    c = pl.program_id(1)
    T = CHUNK
    n_chunks = tm // T

    @pl.when(c == 0)
    def _():
        pool_ext[0:_POOL_TOP, :] = jnp.zeros((_POOL_TOP, GROUP), F32)
        conv_ext[0:_CONV_TOP, :] = jnp.zeros((_CONV_TOP, GROUP), F32)
        ret_out[...] = jnp.zeros(ret_out.shape, F32)

    x = x_ref[0]
    xn = _rms(x, g1_ref[...]).astype(BF16)

    def proj(col):
        return _dot(xn, win_ref[:, col:col + GROUP])

    def out_proj(y, col):
        return _dot(y, wout_ref[col:col + GROUP, :])

    head_lanes = [slice(hd * HEAD_DIM, (hd + 1) * HEAD_DIM) for hd in range(HEADS)]
    chunk_rows = [slice(j * T, (j + 1) * T) for j in range(n_chunks)]

    def pool_mixer(a_proj):
        pool_ext[_POOL_TOP:_POOL_TOP + tm, :] = a_proj
        pos1 = lax.broadcasted_iota(jnp.int32, (tm, HEAD_DIM), 0) + (start + 1) + c * tm
        y = []
        for gi, win in enumerate(POOL_WINDOWS):
            lanes = head_lanes[gi]
            a = pool_ext[_POOL_TOP:_POOL_TOP + tm, lanes]
            wsum = a
            for j in range(1, win):
                wsum = wsum + pool_ext[_POOL_TOP - j:_POOL_TOP - j + tm, lanes]
            cnt = jnp.minimum(pos1, win).astype(F32)
            d = (wsum / cnt - a).astype(BF16)
            y.append((_dot(d, poolw_ref[gi]) * pscale_ref[:, lanes]).astype(BF16))
        pool_out[0] = pool_ext[_POOL_TOP + tm - POOL_BUF:_POOL_TOP + tm, :]
        pool_ext[_POOL_TOP - POOL_BUF:_POOL_TOP, :] = pool_ext[_POOL_TOP + tm - POOL_BUF:_POOL_TOP + tm, :]
        return jnp.concatenate(y, axis=-1)

    def conv_mixer(bg, cg, hh):
        z = cg * hh
        conv_ext[_CONV_TOP:_CONV_TOP + tm, :] = z
        acc = (conv_ext[_CONV_TOP - 2:_CONV_TOP - 2 + tm, :] * convw_ref[0:1, :]
               + conv_ext[_CONV_TOP - 1:_CONV_TOP - 1 + tm, :] * convw_ref[1:2, :]
               + z * convw_ref[2:3, :])
        conv_out[0] = conv_ext[_CONV_TOP + tm - 2:_CONV_TOP + tm, :]
        conv_ext[_CONV_TOP - 2:_CONV_TOP, :] = conv_ext[_CONV_TOP + tm - 2:_CONV_TOP + tm, :]
        return (bg * acc).astype(BF16)

    def retention(q_all, k_all, v_all, g_all, j, hd):
        rows, lanes = chunk_rows[j], head_lanes[hd]
        cos = cos_ref[rows, :]
        sin = sin_ref[rows, :]
        q = _rotate(q_all[rows, lanes], cos, sin)
        k = _rotate(k_all[rows, lanes], cos, sin) * (HEAD_DIM ** -0.5)
        v = v_all[rows, lanes].astype(BF16)
        s_prev = ret_out[0, hd]
        scores = lax.dot_general(q.astype(BF16), k.astype(BF16), (((1,), (1,)), ((), ())),
                                 preferred_element_type=F32) * dmask_ref[hd]
        o = _dot(scores.astype(BF16), v) + _dot((q * qdec_ref[hd]).astype(BF16), s_prev.astype(BF16))
        kv = lax.dot_general((k * kdec_ref[hd]).astype(BF16), v, (((0,), (0,)), ((), ())),
                             preferred_element_type=F32)
        ret_out[0, hd] = qdec_ref[hd, T - 1:T, :] * s_prev + kv
        on = _layernorm(o) * retg_ref[:, lanes]
        return (_silu(g_all[rows, lanes]) * on).astype(BF16)

    def sgu_mixer(u_all, vn, j):
        rows = chunk_rows[j]
        tri = lax.broadcasted_iota(jnp.int32, (T, T), 0) >= lax.broadcasted_iota(jnp.int32, (T, T), 1)
        y = []
        for hd in range(HEADS):
            lanes = head_lanes[hd]
            w = jnp.where(tri, sguw_ref[hd], 0.0).astype(BF16)
            mixed = _dot(w, vn[rows, lanes].astype(BF16)) + sgub_ref[hd]
            y.append((u_all[rows, lanes] * mixed).astype(BF16))
        return jnp.concatenate(y, axis=-1)

    q_all, k_all, v_all, g_all = proj(_Q), proj(_K), proj(_V), proj(_G)
    ret = functools.partial(retention, q_all, k_all, v_all, g_all)
    y_c = [[None] * HEADS for _ in range(n_chunks)]
    y_c[0][0] = ret(0, 0)
    a_proj = proj(_A)
    y_c[0][1] = ret(0, 1)
    cg = proj(_CG)
    y_c[0][2] = ret(0, 2)
    hh = proj(_HH)
    y_c[0][3] = ret(0, 3)
    bg = proj(_BG)
    later = iter([lambda: proj(_VV), lambda: proj(_U), lambda: pool_mixer(a_proj),
                  lambda: conv_mixer(bg, cg, hh)])
    fills = []
    for j in range(1, n_chunks):
        for hd in range(HEADS):
            y_c[j][hd] = ret(j, hd)
            nxt = next(later, None)
            if nxt is not None:
                fills.append(nxt())
    fills += [f() for f in later]
    vv, u_all, y_a, y_b = fills
    h = x + out_proj(jnp.concatenate([jnp.concatenate(r, axis=-1) for r in y_c], axis=0), _YC)
    vn = _layernorm(vv) * sgug_ref[...]
    y_d = [sgu_mixer(u_all, vn, 0)]
    h = h + out_proj(y_a, _YA)
    y_d += [sgu_mixer(u_all, vn, j) for j in range(1, n_chunks)]
    h = h + out_proj(y_b, _YB)
    h_out[0] = h + out_proj(jnp.concatenate(y_d, axis=0), _YD)
    sgu_out[0] = vn[tm - T:tm, :]


def block_prompt(x, norm_g, w_in, w_out, tables, params, layer, *, start, tm):
    b, l, d = x.shape
    cos, sin, dmask, qdec, kdec = tables
    poolw, pscale, convw, retg, sgug, sguw, sgub = params
    const3 = lambda shape: pl.BlockSpec(shape, lambda i, c: (0, 0, 0))
    lay3 = lambda shape: pl.BlockSpec((None,) + shape, lambda i, c: (layer, 0, 0))
    lay4 = lambda shape: pl.BlockSpec((None,) + shape, lambda i, c: (layer, 0, 0, 0))
    resident = lambda shape: pl.BlockSpec(shape, lambda i, c: (0, 0), pipeline_mode=pl.Buffered(1))
    vmem = (d * IN_WIDTH + 4 * GROUP * d) * 2 + 2 * 2 * tm * d * 4 + 2 * tm * IN_WIDTH * 4
    return pl.pallas_call(
        functools.partial(_block_prompt_kernel, start=start, tm=tm),
        grid=(b, l // tm),
        in_specs=[
            pl.BlockSpec((1, tm, d), lambda i, c: (i, c, 0)),
            lay3((1, d)),
            resident((d, IN_WIDTH)),
            resident((4 * GROUP, d)),
            pl.BlockSpec((tm, HEAD_DIM), lambda i, c: (c, 0)),
            pl.BlockSpec((tm, HEAD_DIM), lambda i, c: (c, 0)),
            const3((HEADS, CHUNK, CHUNK)),
            const3((HEADS, CHUNK, HEAD_DIM)),
            const3((HEADS, CHUNK, HEAD_DIM)),
            lay4((len(POOL_WINDOWS), HEAD_DIM, HEAD_DIM)),
            lay3((1, GROUP)),
            lay3((3, GROUP)),
            lay3((1, GROUP)),
            lay3((1, GROUP)),
            lay4((HEADS, CHUNK, CHUNK)),
            lay4((HEADS, CHUNK, HEAD_DIM)),
        ],
        out_specs=[
            pl.BlockSpec((1, tm, d), lambda i, c: (i, c, 0)),
            pl.BlockSpec((1, POOL_BUF, GROUP), lambda i, c: (i, 0, 0)),
            pl.BlockSpec((1, 2, GROUP), lambda i, c: (i, 0, 0)),
            pl.BlockSpec((1, HEADS, HEAD_DIM, HEAD_DIM), lambda i, c: (i, 0, 0, 0)),
            pl.BlockSpec((1, CHUNK, GROUP), lambda i, c: (i, 0, 0)),
        ],
        out_shape=[
            jax.ShapeDtypeStruct((b, l, d), F32),
            jax.ShapeDtypeStruct((b, POOL_BUF, GROUP), F32),
            jax.ShapeDtypeStruct((b, 2, GROUP), F32),
            jax.ShapeDtypeStruct((b, HEADS, HEAD_DIM, HEAD_DIM), F32),
            jax.ShapeDtypeStruct((b, CHUNK, GROUP), F32),
        ],
        scratch_shapes=[
            pltpu.VMEM((_POOL_TOP + tm, GROUP), F32),
            pltpu.VMEM((_CONV_TOP + tm, GROUP), F32),
        ],
        compiler_params=pltpu.CompilerParams(
            dimension_semantics=("arbitrary", "arbitrary"),
            vmem_limit_bytes=_vmem_limit(vmem)),
        name="block_prompt",
    )(x, norm_g, w_in, w_out, cos, sin, dmask, qdec, kdec, poolw, pscale, convw, retg, sgug, sguw, sgub)


_SEQ_BLOCK = 16


_N_SAMPLE_INPUTS = 14


def _mixer_sample_kernel(*refs, start, layer, all_layers):
    (p_ref, pool_ref, conv_ref, ret_ref, cos_ref, sin_ref, gam_ref,
     poolw_ref, pscale_ref, convw_ref, retg_ref, sgug_ref, sguw_ref, sgub_ref) = refs[:_N_SAMPLE_INPUTS]
    mix_ref, pool_out, conv_out, ret_out, sgu_out, qg_s, k_s, v_s, os_s = refs[-9:]
    TB = _SEQ_BLOCK

    if all_layers:
        stacked = (pool_out, conv_out, ret_out, sgu_out)
        for ref in stacked:
            for other in range(DEPTH):
                if other != layer:
                    ref[other] = jnp.zeros(ref.shape[1:], F32)
        pool_out, conv_out, ret_out, sgu_out = (ref.at[layer] for ref in stacked)

    a_all = p_ref[:, _A:_A + GROUP]
    for gi, win in enumerate(POOL_WINDOWS):
        lanes = slice(gi * HEAD_DIM, (gi + 1) * HEAD_DIM)
        a = a_all[:, lanes]
        wsum = a
        for j in range(1, win):
            wsum = wsum + pool_ref[POOL_BUF - j, :, lanes]
        cnt = float(min(start + 1, win))
        d = (wsum / cnt - a).astype(BF16)
        y = _dot(d, poolw_ref[gi]) * pscale_ref[:, lanes]
        mix_ref[:, _YA + gi * HEAD_DIM:_YA + (gi + 1) * HEAD_DIM] = y.astype(BF16)
    pool_out[0:POOL_BUF - 1] = pool_ref[1:POOL_BUF]
    pool_out[POOL_BUF - 1] = a_all

    z = p_ref[:, _CG:_CG + GROUP] * p_ref[:, _HH:_HH + GROUP]
    acc = (conv_ref[:, 0, :] * convw_ref[0:1, :] + conv_ref[:, 1, :] * convw_ref[1:2, :]
           + z * convw_ref[2:3, :])
    mix_ref[:, _YB:_YB + GROUP] = (p_ref[:, _BG:_BG + GROUP] * acc).astype(BF16)
    conv_out[:, 0, :] = conv_ref[:, 1, :]
    conv_out[:, 1, :] = z

    cos = cos_ref[...]
    sin = sin_ref[...]
    scores = []
    for h in range(HEADS):
        lanes = slice(h * HEAD_DIM, (h + 1) * HEAD_DIM)
        q = _rotate(p_ref[:, _Q + h * HEAD_DIM:_Q + (h + 1) * HEAD_DIM], cos, sin)
        k = _rotate(p_ref[:, _K + h * HEAD_DIM:_K + (h + 1) * HEAD_DIM], cos, sin) * (HEAD_DIM ** -0.5)
        scores.append(jnp.sum(q * k, axis=-1, keepdims=True))
        qg_s[:, lanes] = q * gam_ref[:, lanes]
        k_s[:, lanes] = k
    v_s[...] = p_ref[:, _V:_V + GROUP]

    first_row = lax.broadcasted_iota(jnp.int32, (8, HEAD_DIM), 0) == 0

    for b in range(TB):
        for h in range(HEADS):
            lanes = slice(h * HEAD_DIM, (h + 1) * HEAD_DIM)
            s_prev = ret_ref[b, h]
            q8 = jnp.broadcast_to(qg_s[b:b + 1, lanes], (8, HEAD_DIM))
            os_s[b:b + 1, lanes] = _dot(q8.astype(BF16), s_prev.astype(BF16))[0:1, :]
            k8 = jnp.where(first_row, jnp.broadcast_to(k_s[b:b + 1, lanes], (8, HEAD_DIM)), 0.0)
            v8 = jnp.broadcast_to(v_s[b:b + 1, lanes], (8, HEAD_DIM))
            kv = lax.dot_general(k8.astype(BF16), v8.astype(BF16), (((0,), (0,)), ((), ())),
                                 preferred_element_type=F32)
            ret_out[b, h] = gam_ref[:, lanes] * s_prev + kv

    for h in range(HEADS):
        lanes = slice(h * HEAD_DIM, (h + 1) * HEAD_DIM)
        o = scores[h] * v_s[:, lanes] + os_s[:, lanes]
        on = _layernorm(o) * retg_ref[:, lanes]
        g = p_ref[:, _G + h * HEAD_DIM:_G + (h + 1) * HEAD_DIM]
        mix_ref[:, _YC + h * HEAD_DIM:_YC + (h + 1) * HEAD_DIM] = (_silu(g) * on).astype(BF16)

    vn = _layernorm(p_ref[:, _VV:_VV + GROUP]) * sgug_ref[...]
    mix_ref[:, _YD:_YD + GROUP] = (p_ref[:, _U:_U + GROUP] * (sguw_ref[...] * vn + sgub_ref[...])).astype(BF16)
    sgu_out[:, 0, :] = vn


def mixer_sample(p, pool_state, conv_state, ret_state, tables, params, layer, stacked, *, start):
    n = p.shape[0]
    tb = _SEQ_BLOCK
    cos, sin, gam = tables
    poolw, pscale, convw, retg, sgug, sguw, sgub = params
    rows = lambda width: pl.BlockSpec((tb, width), lambda i: (i, 0))
    lay_rows = lambda r: pl.BlockSpec((None, tb, r, GROUP), lambda i: (layer, i, 0, 0))
    lay_pool = pl.BlockSpec((None, POOL_BUF, tb, GROUP), lambda i: (layer, 0, i, 0))
    lay_ret = pl.BlockSpec((None, tb, HEADS, HEAD_DIM, HEAD_DIM), lambda i: (layer, i, 0, 0, 0))
    const2 = lambda shape: pl.BlockSpec(shape, lambda i: (0, 0))
    lay3 = lambda shape: pl.BlockSpec((None,) + shape, lambda i: (layer, 0, 0))
    lay4 = lambda shape: pl.BlockSpec((None,) + shape, lambda i: (layer, 0, 0, 0))
    in_specs = [
        rows(IN_WIDTH),
        lay_pool,
        lay_rows(2),
        lay_ret,
        const2((1, HEAD_DIM)),
        const2((1, HEAD_DIM)),
        const2((1, GROUP)),
        lay4((len(POOL_WINDOWS), HEAD_DIM, HEAD_DIM)),
        lay3((1, GROUP)),
        lay3((3, GROUP)),
        lay3((1, GROUP)),
        lay3((1, GROUP)),
        lay3((1, GROUP)),
        lay3((1, GROUP)),
    ]
    args = (p, pool_state, conv_state, ret_state, cos, sin, gam,
            poolw, pscale, convw, retg, sgug, sguw, sgub)
    assert len(args) == _N_SAMPLE_INPUTS
    aliases = {}
    first = stacked is None
    state_block = tb * (16 + 8 + 8 + HEAD_DIM) * GROUP * 4
    vmem = 2 * (tb * IN_WIDTH * 4 + state_block * (1 + (DEPTH if first else 1)))
    if first:
        all_rows = lambda r: pl.BlockSpec((DEPTH, tb, r, GROUP), lambda i: (0, i, 0, 0))
        state_specs = [pl.BlockSpec((DEPTH, POOL_BUF, tb, GROUP), lambda i: (0, 0, i, 0)), all_rows(2),
                       pl.BlockSpec((DEPTH, tb, HEADS, HEAD_DIM, HEAD_DIM), lambda i: (0, i, 0, 0, 0)),
                       all_rows(1)]
    else:
        state_specs = [lay_pool, lay_rows(2), lay_ret, lay_rows(1)]
        aliases = {len(args) + k: 1 + k for k in range(len(stacked))}
        in_specs = in_specs + [pl.BlockSpec(memory_space=pl.ANY)] * len(stacked)
        args = args + tuple(stacked)
    outs = pl.pallas_call(
        functools.partial(_mixer_sample_kernel, start=start, layer=layer, all_layers=first),
        grid=(n // tb,),
        in_specs=in_specs,
        out_specs=[rows(4 * GROUP)] + state_specs,
        out_shape=[
            jax.ShapeDtypeStruct((n, 4 * GROUP), BF16),
            jax.ShapeDtypeStruct((DEPTH, POOL_BUF, n, GROUP), F32),
            jax.ShapeDtypeStruct((DEPTH, n, 2, GROUP), F32),
            jax.ShapeDtypeStruct((DEPTH, n, HEADS, HEAD_DIM, HEAD_DIM), F32),
            jax.ShapeDtypeStruct((DEPTH, n, 1, GROUP), F32),
        ],
        scratch_shapes=[pltpu.VMEM((tb, GROUP), F32)] * 4,
        input_output_aliases=aliases,
        compiler_params=pltpu.CompilerParams(
            dimension_semantics=("arbitrary",),
            vmem_limit_bytes=_vmem_limit(vmem)),
        name="mixer_sample",
    )(*args)
    return outs[0], tuple(outs[1:])


def _rope_tables(start, length):
    half = HEAD_DIM // 2
    inv = ROPE_BASE ** (-jnp.arange(half, dtype=F32) / half)
    pos = start + jnp.arange(length)
    ang = pos.astype(F32)[:, None] * inv[None, :]
    cos, sin = jnp.cos(ang), jnp.sin(ang)
    return jnp.concatenate([cos, cos], axis=-1), jnp.concatenate([-sin, sin], axis=-1)


def _log_gamma():
    return jnp.log1p(-(2.0 ** (-5.0 - jnp.arange(HEADS, dtype=F32))))


def _decay_tables(c):
    lg = _log_gamma()
    idx = jnp.arange(c, dtype=F32)
    diff = idx[:, None] - idx[None, :]
    dmask = jnp.where(diff >= 0, jnp.exp(jnp.maximum(diff, 0.0)[None] * lg[:, None, None]), 0.0)
    qdec = jnp.exp((idx + 1.0)[None, :] * lg[:, None])[:, :, None]
    kdec = jnp.exp((c - 1.0 - idx)[None, :] * lg[:, None])[:, :, None]
    wide = lambda t: jnp.broadcast_to(t, (HEADS, c, HEAD_DIM))
    return dmask, wide(qdec), wide(kdec)


def _per_head_lanes(t):
    return jnp.repeat(t, HEAD_DIM, axis=-1)[..., None, :]


def kernel(x_prompt, x_sample, state_pool, state_conv, state_ret, norm1_g, w_in, pool_w, pool_scale,
           conv_w, ret_norm_g, sgu_norm_g, sgu_w, sgu_b, w_out, norm2_g, w_gate_up, w_down,
           final_norm_g):
    bp, lp, _ = x_prompt.shape
    ns = x_sample.shape[0]

    pool_w_b = pool_w.astype(BF16)
    row3 = lambda t: t[:, None, :]
    norm1 = row3(norm1_g)
    norm2 = row3(norm2_g)
    final_g = final_norm_g[None, :]

    cos_p, sin_p = _rope_tables(0, lp)
    prompt_tables = (cos_p, sin_p) + _decay_tables(CHUNK)
    sgu_b_wide = jnp.broadcast_to(sgu_b[:, :, :, None], sgu_b.shape + (HEAD_DIM,))
    prompt_params = (pool_w_b, row3(pool_scale), conv_w, row3(ret_norm_g), row3(sgu_norm_g), sgu_w, sgu_b_wide)

    cos_s, sin_s = _rope_tables(PAST_LEN, 1)
    gam = _per_head_lanes(jnp.exp(_log_gamma()))
    sample_tables = (cos_s, sin_s, gam)
    sample_params = (pool_w_b, row3(pool_scale), conv_w, row3(ret_norm_g), row3(sgu_norm_g),
                     _per_head_lanes(sgu_w[:, :, 0, 0]), _per_head_lanes(sgu_b[:, :, 0]))
    pool_rows = jnp.transpose(state_pool, (0, 2, 1, 3))
    hp = x_prompt
    hs = x_sample.reshape(ns, D_MODEL)
    prompt_states = [[] for _ in range(4)]
    sample_states = None
    for l in range(DEPTH):
        fin = l == DEPTH - 1
        p, w_in_b = norm_matmul(hs, norm1, w_in, l, tn=512)
        mix, sample_states = mixer_sample(p, pool_rows, state_conv, state_ret, sample_tables,
                                          sample_params, l, sample_states, start=PAST_LEN)
        hs, w_out_b = matmul_residual(mix, w_out, hs, l, tn=1024)
        hp, *states = block_prompt(hp, norm1, w_in_b, w_out_b, prompt_tables, prompt_params, l,
                                   start=0, tm=256)
        for lst, val in zip(prompt_states, states):
            lst.append(val)
        hp = hp.reshape(bp * lp, D_MODEL)
        first, w_mlp_b = mlp_cast(hs, hp, norm2, w_gate_up, w_down, final_g, l, tp=_MLP_TILE, tf=256,
                                  final_norm=fin)
        hs = first[:ns]
        hp = mlp_rest(first, hp, norm2, w_mlp_b, final_g, l, skip_rows=ns, tm=_MLP_TILE, tf=512,
                      final_norm=fin).reshape(bp, lp, D_MODEL)
    pool_p, conv_p, ret_p, sgu_p = (jnp.stack(s) for s in prompt_states)
    pool_s, conv_s, ret_s, sgu_s = sample_states
    pool_s = jnp.transpose(pool_s, (0, 2, 1, 3))
    return (hp, hs.reshape(ns, 1, D_MODEL), pool_p, pool_s, conv_p, conv_s, ret_p, ret_s, sgu_p, sgu_s)
```

```python
import functools

import jax
import jax.numpy as jnp
from jax import lax
from jax.experimental import pallas as pl
from jax.experimental.pallas import tpu as pltpu

F32 = jnp.float32
BF16 = jnp.bfloat16

D_MODEL = 2048
DEPTH = 2
GROUP = 512
N_SPLITS = 10
IN_WIDTH = N_SPLITS * GROUP
POOL_WINDOWS = (2, 4, 8, 16)
POOL_BUF = 15
HEADS = 4
HEAD_DIM = 128
CHUNK = 128
ROPE_BASE = 10000.0
D_FF = 5632
NORM_EPS = 1e-6
PAST_LEN = 16384

_A, _BG, _CG, _HH, _Q, _K, _V, _G, _U, _VV = (i * GROUP for i in range(N_SPLITS))
_YA, _YB, _YC, _YD = (i * GROUP for i in range(4))

MIB = 1024 * 1024
VMEM_BYTES_V7X = 64 * MIB


VMEM_LIMIT_MAX = VMEM_BYTES_V7X - 4 * MIB


def _vmem_limit(estimate):
    return min(estimate + 4 * MIB, VMEM_LIMIT_MAX)


def _rms(x, g):
    ms = jnp.mean(x * x, axis=-1, keepdims=True)
    return x * lax.rsqrt(ms + NORM_EPS) * g


def _layernorm(x):
    mu = jnp.mean(x, axis=-1, keepdims=True)
    xc = x - mu
    return xc * lax.rsqrt(jnp.mean(xc * xc, axis=-1, keepdims=True) + NORM_EPS)


def _silu(x):
    return x * jax.nn.sigmoid(x)


def _dot(a, b):
    return jnp.dot(a, b, preferred_element_type=F32)


def _rotate(x, cos, sin_signed):
    return x * cos + pltpu.roll(x, HEAD_DIM // 2, 1) * sin_signed


def _weight_specs(w, layer, k, tn):
    if w.ndim == 3:
        return pl.BlockSpec((None, k, tn), lambda j: (layer, 0, j)), True
    return pl.BlockSpec((k, tn), lambda j: (0, j)), False


def _norm_matmul_kernel(x_ref, g_ref, w_ref, o_ref, *rest, emit):
    xn_ref = rest[-1]

    @pl.when(pl.program_id(0) == 0)
    def _():
        xn_ref[...] = _rms(x_ref[...], g_ref[...]).astype(BF16)

    wb = w_ref[...].astype(BF16)
    if emit:
        rest[0][...] = wb
    o_ref[...] = _dot(xn_ref[...], wb)


def norm_matmul(x, g, w, layer, *, tn):
    m, k = x.shape
    n = w.shape[-1]
    w_spec, emit = _weight_specs(w, layer, k, tn)
    vmem = 2 * k * tn * (4 + 2) + 2 * m * (k + tn) * 4 + m * k * 2
    outs = pl.pallas_call(
        functools.partial(_norm_matmul_kernel, emit=emit),
        grid=(n // tn,),
        in_specs=[
            pl.BlockSpec((m, k), lambda j: (0, 0)),
            pl.BlockSpec((None, 1, k), lambda j: (layer, 0, 0)),
            w_spec,
        ],
        out_specs=[pl.BlockSpec((m, tn), lambda j: (0, j))] + [pl.BlockSpec((k, tn), lambda j: (0, j))] * emit,
        out_shape=[jax.ShapeDtypeStruct((m, n), F32)] + [jax.ShapeDtypeStruct((k, n), BF16)] * emit,
        scratch_shapes=[pltpu.VMEM((m, k), BF16)],
        compiler_params=pltpu.CompilerParams(
            dimension_semantics=("arbitrary",),
            vmem_limit_bytes=_vmem_limit(vmem)),
        name="norm_matmul",
    )(x, g, w)
    return tuple(outs) if emit else (outs[0], w)


def _matmul_residual_kernel(a_ref, w_ref, r_ref, o_ref, *rest, emit):
    wb = w_ref[...].astype(BF16)
    if emit:
        rest[0][...] = wb
    o_ref[...] = r_ref[...] + _dot(a_ref[...], wb)


def matmul_residual(a, w, res, layer, *, tn):
    m, k = a.shape
    n = w.shape[-1]
    w_spec, emit = _weight_specs(w, layer, k, tn)
    vmem = 2 * k * tn * (4 + 2) + 2 * m * (k * 2 + 2 * tn * 4)
    outs = pl.pallas_call(
        functools.partial(_matmul_residual_kernel, emit=emit),
        grid=(n // tn,),
        in_specs=[
            pl.BlockSpec((m, k), lambda j: (0, 0)),
            w_spec,
            pl.BlockSpec((m, tn), lambda j: (0, j)),
        ],
        out_specs=[pl.BlockSpec((m, tn), lambda j: (0, j))] + [pl.BlockSpec((k, tn), lambda j: (0, j))] * emit,
        out_shape=[jax.ShapeDtypeStruct((m, n), F32)] + [jax.ShapeDtypeStruct((k, n), BF16)] * emit,
        compiler_params=pltpu.CompilerParams(
            dimension_semantics=("arbitrary",),
            vmem_limit_bytes=_vmem_limit(vmem)),
        name="matmul_residual",
    )(a, w, res)
    return tuple(outs) if emit else (outs[0], w)


_MLP_TILE = 1024


def _mlp_start(first, x_refs, g_ref, o_ref, hn_ref):
    @pl.when(first)
    def _():
        row = 0
        for x_ref in x_refs:
            x = x_ref[...]
            rows = slice(row, row + x.shape[0])
            hn_ref[rows, :] = _rms(x, g_ref[...]).astype(BF16)
            o_ref[rows, :] = x
            row += x.shape[0]


def _mlp_accumulate(wg, wu, wd, o_ref, hn_ref):
    hn = hn_ref[...]
    act = (_silu(_dot(hn, wg)) * _dot(hn, wu)).astype(BF16)
    o_ref[...] += _dot(act, wd)


def _mlp_finish(last, fg_ref, o_ref, final_norm):
    if final_norm:
        @pl.when(last)
        def _():
            o_ref[...] = _rms(o_ref[...], fg_ref[...])


def _mlp_cast_kernel(xs_ref, xp_ref, g_ref, wg_ref, wu_ref, wd_ref, fg_ref,
                     o_ref, wgb_ref, wub_ref, wdb_ref, hn_ref, *, final_norm):
    f = pl.program_id(0)
    _mlp_start(f == 0, (xs_ref, xp_ref), g_ref, o_ref, hn_ref)
    wg, wu, wd = wg_ref[...].astype(BF16), wu_ref[...].astype(BF16), wd_ref[...].astype(BF16)
    wgb_ref[...] = wg
    wub_ref[...] = wu
    wdb_ref[...] = wd
    _mlp_accumulate(wg, wu, wd, o_ref, hn_ref)
    _mlp_finish(f == pl.num_programs(0) - 1, fg_ref, o_ref, final_norm)


def _mlp_rest_kernel(first_hbm, x_ref, g_ref, wg_ref, wu_ref, wd_ref, fg_ref, o_ref, hn_ref, sem,
                     *, nf, skip_rows, final_norm):
    s = pl.program_id(0)
    tm = o_ref.shape[0]

    @pl.when(s == 0)
    def _():
        copy = pltpu.make_async_copy(first_hbm.at[pl.ds(skip_rows, tm), :], o_ref, sem)
        copy.start()
        copy.wait()

    @pl.when(s > 0)
    def _():
        f = (s - 1) % nf
        _mlp_start(f == 0, (x_ref,), g_ref, o_ref, hn_ref)
        _mlp_accumulate(wg_ref[...], wu_ref[...], wd_ref[...], o_ref, hn_ref)
        _mlp_finish(f == nf - 1, fg_ref, o_ref, final_norm)


def mlp_cast(xs, xp, g, w_gate_up, w_down, final_g, layer, *, tp, tf, final_norm):
    ns, d = xs.shape
    m = ns + tp
    nf = D_FF // tf
    once = pl.Buffered(1)
    vmem = 2 * m * d * 4 + m * d * 2 + 2 * 3 * d * tf * (4 + 2) + 3 * m * tf * 4
    outs = pl.pallas_call(
        functools.partial(_mlp_cast_kernel, final_norm=final_norm),
        grid=(nf,),
        in_specs=[
            pl.BlockSpec((ns, d), lambda f: (0, 0), pipeline_mode=once),
            pl.BlockSpec((tp, d), lambda f: (0, 0), pipeline_mode=once),
            pl.BlockSpec((None, 1, d), lambda f: (layer, 0, 0)),
            pl.BlockSpec((None, d, tf), lambda f: (layer, 0, f)),
            pl.BlockSpec((None, d, tf), lambda f: (layer, 0, nf + f)),
            pl.BlockSpec((None, tf, d), lambda f: (layer, f, 0)),
            pl.BlockSpec((1, d), lambda f: (0, 0)),
        ],
        out_specs=[
            pl.BlockSpec((m, d), lambda f: (0, 0), pipeline_mode=once),
            pl.BlockSpec((d, tf), lambda f: (0, f)),
            pl.BlockSpec((d, tf), lambda f: (0, f)),
            pl.BlockSpec((tf, d), lambda f: (f, 0)),
        ],
        out_shape=[jax.ShapeDtypeStruct((m, d), F32),
                   jax.ShapeDtypeStruct((d, D_FF), BF16), jax.ShapeDtypeStruct((d, D_FF), BF16),
                   jax.ShapeDtypeStruct((D_FF, d), BF16)],
        scratch_shapes=[pltpu.VMEM((m, d), BF16)],
        compiler_params=pltpu.CompilerParams(
            dimension_semantics=("arbitrary",),
            vmem_limit_bytes=_vmem_limit(vmem)),
        name="mlp_cast",
    )(xs, xp, g, w_gate_up, w_gate_up, w_down, final_g)
    return outs[0], tuple(outs[1:])


def mlp_rest(first, x, g, weights, final_g, layer, *, skip_rows, tm, tf, final_norm):
    m, d = x.shape
    nf = D_FF // tf
    step = lambda s: jnp.maximum(s - 1, 0)
    x_tile = lambda s: 1 + step(s) // nf
    f_tile = lambda s: step(s) % nf
    vmem = 2 * 2 * tm * d * 4 + tm * d * 2 + 2 * 3 * d * tf * 2 + 3 * tm * tf * 4
    return pl.pallas_call(
        functools.partial(_mlp_rest_kernel, nf=nf, skip_rows=skip_rows, final_norm=final_norm),
        grid=(1 + (m // tm - 1) * nf,),
        in_specs=[
            pl.BlockSpec(memory_space=pl.ANY),
            pl.BlockSpec((tm, d), lambda s: (x_tile(s), 0)),
            pl.BlockSpec((None, 1, d), lambda s: (layer, 0, 0)),
            pl.BlockSpec((d, tf), lambda s: (0, f_tile(s))),
            pl.BlockSpec((d, tf), lambda s: (0, f_tile(s))),
            pl.BlockSpec((tf, d), lambda s: (f_tile(s), 0)),
            pl.BlockSpec((1, d), lambda s: (0, 0)),
        ],
        out_specs=pl.BlockSpec((tm, d), lambda s: (jnp.where(s == 0, 0, x_tile(s)), 0)),
        out_shape=jax.ShapeDtypeStruct((m, d), F32),
        scratch_shapes=[pltpu.VMEM((tm, d), BF16), pltpu.SemaphoreType.DMA(())],
        compiler_params=pltpu.CompilerParams(
            dimension_semantics=("arbitrary",),
            vmem_limit_bytes=_vmem_limit(vmem)),
        name="mlp_rest",
    )(first, x, g, *weights, final_g)


_POOL_TOP = 16
_CONV_TOP = 8


def _block_prompt_kernel(x_ref, g1_ref, win_ref, wout_ref, cos_ref, sin_ref, dmask_ref, qdec_ref, kdec_ref,
                         poolw_ref, pscale_ref, convw_ref, retg_ref, sgug_ref, sguw_ref, sgub_ref,
                         *rest, start, tm, cast_next):
    if cast_next:
        win_next_ref, wout_next_ref, *rest = rest
        *rest, win_next_out, wout_next_out, pool_ext, conv_ext = rest
        win_next_out[...] = win_next_ref[...].astype(BF16)
        wout_next_out[...] = wout_next_ref[...].astype(BF16)
    else:
        *rest, pool_ext, conv_ext = rest
    h_out, pool_out, conv_out, ret_out, sgu_out = rest
    c = pl.program_id(1)
    T = CHUNK
    n_chunks = tm // T

    @pl.when(c == 0)
    def _():
        pool_ext[0:_POOL_TOP, :] = jnp.zeros((_POOL_TOP, GROUP), F32)
        conv_ext[0:_CONV_TOP, :] = jnp.zeros((_CONV_TOP, GROUP), F32)
        ret_out[...] = jnp.zeros(ret_out.shape, F32)

    x = x_ref[0]
    xn = _rms(x, g1_ref[...]).astype(BF16)

    def proj(col):
        return _dot(xn, win_ref[:, col:col + GROUP])

    def out_proj(y, col):
        return _dot(y, wout_ref[col:col + GROUP, :])

    head_lanes = [slice(hd * HEAD_DIM, (hd + 1) * HEAD_DIM) for hd in range(HEADS)]
    chunk_rows = [slice(j * T, (j + 1) * T) for j in range(n_chunks)]

    def pool_mixer(a_proj):
        pool_ext[_POOL_TOP:_POOL_TOP + tm, :] = a_proj
        pos1 = lax.broadcasted_iota(jnp.int32, (tm, HEAD_DIM), 0) + (start + 1) + c * tm
        y = []
        for gi, win in enumerate(POOL_WINDOWS):
            lanes = head_lanes[gi]
            a = pool_ext[_POOL_TOP:_POOL_TOP + tm, lanes]
            wsum = a
            for j in range(1, win):
                wsum = wsum + pool_ext[_POOL_TOP - j:_POOL_TOP - j + tm, lanes]
            cnt = jnp.minimum(pos1, win).astype(F32)
            d = (wsum / cnt - a).astype(BF16)
            y.append((_dot(d, poolw_ref[gi]) * pscale_ref[:, lanes]).astype(BF16))
        pool_out[0] = pool_ext[_POOL_TOP + tm - POOL_BUF:_POOL_TOP + tm, :]
        pool_ext[_POOL_TOP - POOL_BUF:_POOL_TOP, :] = pool_ext[_POOL_TOP + tm - POOL_BUF:_POOL_TOP + tm, :]
        return jnp.concatenate(y, axis=-1)

    def conv_mixer(bg, cg, hh):
        z = cg * hh
        conv_ext[_CONV_TOP:_CONV_TOP + tm, :] = z
        acc = (conv_ext[_CONV_TOP - 2:_CONV_TOP - 2 + tm, :] * convw_ref[0:1, :]
               + conv_ext[_CONV_TOP - 1:_CONV_TOP - 1 + tm, :] * convw_ref[1:2, :]
               + z * convw_ref[2:3, :])
        conv_out[0] = conv_ext[_CONV_TOP + tm - 2:_CONV_TOP + tm, :]
        conv_ext[_CONV_TOP - 2:_CONV_TOP, :] = conv_ext[_CONV_TOP + tm - 2:_CONV_TOP + tm, :]
        return (bg * acc).astype(BF16)

    def retention(q_all, k_all, v_all, g_all, j, hd):
        rows, lanes = chunk_rows[j], head_lanes[hd]
        cos = cos_ref[rows, :]
        sin = sin_ref[rows, :]
        q = _rotate(q_all[rows, lanes], cos, sin)
        k = _rotate(k_all[rows, lanes], cos, sin) * (HEAD_DIM ** -0.5)
        v = v_all[rows, lanes].astype(BF16)
        s_prev = ret_out[0, hd]
        scores = lax.dot_general(q.astype(BF16), k.astype(BF16), (((1,), (1,)), ((), ())),
                                 preferred_element_type=F32) * dmask_ref[hd]
        o = _dot(scores.astype(BF16), v) + _dot((q * qdec_ref[hd]).astype(BF16), s_prev.astype(BF16))
        kv = lax.dot_general((k * kdec_ref[hd]).astype(BF16), v, (((0,), (0,)), ((), ())),
                             preferred_element_type=F32)
        ret_out[0, hd] = qdec_ref[hd, T - 1:T, :] * s_prev + kv
        on = _layernorm(o) * retg_ref[:, lanes]
        return (_silu(g_all[rows, lanes]) * on).astype(BF16)

    def sgu_mixer(u_all, vn, j):
        rows = chunk_rows[j]
        tri = lax.broadcasted_iota(jnp.int32, (T, T), 0) >= lax.broadcasted_iota(jnp.int32, (T, T), 1)
        y = []
        for hd in range(HEADS):
            lanes = head_lanes[hd]
            w = jnp.where(tri, sguw_ref[hd], 0.0).astype(BF16)
            mixed = _dot(w, vn[rows, lanes].astype(BF16)) + sgub_ref[hd]
            y.append((u_all[rows, lanes] * mixed).astype(BF16))
        return jnp.concatenate(y, axis=-1)

    q_all, k_all, v_all, g_all = proj(_Q), proj(_K), proj(_V), proj(_G)
    ret = functools.partial(retention, q_all, k_all, v_all, g_all)
    y_c = [[None] * HEADS for _ in range(n_chunks)]
    y_c[0][0] = ret(0, 0)
    a_proj = proj(_A)
    y_c[0][1] = ret(0, 1)
    cg = proj(_CG)
    y_c[0][2] = ret(0, 2)
    hh = proj(_HH)
    y_c[0][3] = ret(0, 3)
    bg = proj(_BG)
    later = iter([lambda: proj(_VV), lambda: proj(_U), lambda: pool_mixer(a_proj),
                  lambda: conv_mixer(bg, cg, hh)])
    fills = []
    for j in range(1, n_chunks):
        for hd in range(HEADS):
            y_c[j][hd] = ret(j, hd)
            nxt = next(later, None)
            if nxt is not None:
                fills.append(nxt())
    fills += [f() for f in later]
    vv, u_all, y_a, y_b = fills
    h = x + out_proj(jnp.concatenate([jnp.concatenate(r, axis=-1) for r in y_c], axis=0), _YC)
    vn = _layernorm(vv) * sgug_ref[...]
    y_d = [sgu_mixer(u_all, vn, 0)]
    h = h + out_proj(y_a, _YA)
    y_d += [sgu_mixer(u_all, vn, j) for j in range(1, n_chunks)]
    h = h + out_proj(y_b, _YB)
    h_out[0] = h + out_proj(jnp.concatenate(y_d, axis=0), _YD)
    sgu_out[0] = vn[tm - T:tm, :]


def block_prompt(x, norm_g, w_in, w_out, tables, params, layer, next_weights, *, start, tm):
    b, l, d = x.shape
    nc = l // tm
    cos, sin, dmask, qdec, kdec = tables
    poolw, pscale, convw, retg, sgug, sguw, sgub = params
    const3 = lambda shape: pl.BlockSpec(shape, lambda i, c: (0, 0, 0))
    lay3 = lambda shape: pl.BlockSpec((None,) + shape, lambda i, c: (layer, 0, 0))
    lay4 = lambda shape: pl.BlockSpec((None,) + shape, lambda i, c: (layer, 0, 0, 0))
    resident = lambda shape: pl.BlockSpec(shape, lambda i, c: (0, 0), pipeline_mode=pl.Buffered(1))
    cast_next = next_weights is not None
    next_in_specs, next_out_specs, next_out_shape, next_args = [], [], [], ()
    next_bytes = 0
    if cast_next:
        for w in next_weights:
            rows, cols = w.shape[1] // (b * nc), w.shape[2]
            assert rows * b * nc == w.shape[1] and rows % 16 == 0
            next_in_specs.append(pl.BlockSpec((None, rows, cols), lambda i, c: (layer + 1, i * nc + c, 0)))
            next_out_specs.append(pl.BlockSpec((rows, cols), lambda i, c: (i * nc + c, 0)))
            next_out_shape.append(jax.ShapeDtypeStruct(w.shape[1:], BF16))
            next_bytes += 2 * rows * cols * (4 + 2)
        next_args = tuple(next_weights)
    vmem = (d * IN_WIDTH + 4 * GROUP * d) * 2 + 2 * 2 * tm * d * 4 + 2 * tm * IN_WIDTH * 4 + next_bytes
    outs = pl.pallas_call(
        functools.partial(_block_prompt_kernel, start=start, tm=tm, cast_next=cast_next),
        grid=(b, nc),
        in_specs=[
            pl.BlockSpec((1, tm, d), lambda i, c: (i, c, 0)),
            lay3((1, d)),
            resident((d, IN_WIDTH)),
            resident((4 * GROUP, d)),
            pl.BlockSpec((tm, HEAD_DIM), lambda i, c: (c, 0)),
            pl.BlockSpec((tm, HEAD_DIM), lambda i, c: (c, 0)),
            const3((HEADS, CHUNK, CHUNK)),
            const3((HEADS, CHUNK, HEAD_DIM)),
            const3((HEADS, CHUNK, HEAD_DIM)),
            lay4((len(POOL_WINDOWS), HEAD_DIM, HEAD_DIM)),
            lay3((1, GROUP)),
            lay3((3, GROUP)),
            lay3((1, GROUP)),
            lay3((1, GROUP)),
            lay4((HEADS, CHUNK, CHUNK)),
            lay4((HEADS, CHUNK, HEAD_DIM)),
            *next_in_specs,
        ],
        out_specs=[
            pl.BlockSpec((1, tm, d), lambda i, c: (i, c, 0)),
            pl.BlockSpec((1, POOL_BUF, GROUP), lambda i, c: (i, 0, 0)),
            pl.BlockSpec((1, 2, GROUP), lambda i, c: (i, 0, 0)),
            pl.BlockSpec((1, HEADS, HEAD_DIM, HEAD_DIM), lambda i, c: (i, 0, 0, 0)),
            pl.BlockSpec((1, CHUNK, GROUP), lambda i, c: (i, 0, 0)),
            *next_out_specs,
        ],
        out_shape=[
            jax.ShapeDtypeStruct((b, l, d), F32),
            jax.ShapeDtypeStruct((b, POOL_BUF, GROUP), F32),
            jax.ShapeDtypeStruct((b, 2, GROUP), F32),
            jax.ShapeDtypeStruct((b, HEADS, HEAD_DIM, HEAD_DIM), F32),
            jax.ShapeDtypeStruct((b, CHUNK, GROUP), F32),
            *next_out_shape,
        ],
        scratch_shapes=[
            pltpu.VMEM((_POOL_TOP + tm, GROUP), F32),
            pltpu.VMEM((_CONV_TOP + tm, GROUP), F32),
        ],
        compiler_params=pltpu.CompilerParams(
            dimension_semantics=("arbitrary", "arbitrary"),
            vmem_limit_bytes=_vmem_limit(vmem)),
        name="block_prompt",
    )(x, norm_g, w_in, w_out, cos, sin, dmask, qdec, kdec, poolw, pscale, convw, retg, sgug, sguw, sgub,
      *next_args)
    return outs[0], tuple(outs[1:5]), (tuple(outs[5:]) if cast_next else None)


_SEQ_BLOCK = 16


_N_SAMPLE_INPUTS = 14


def _mixer_sample_kernel(*refs, start, layer, all_layers):
    (p_ref, pool_ref, conv_ref, ret_ref, cos_ref, sin_ref, gam_ref,
     poolw_ref, pscale_ref, convw_ref, retg_ref, sgug_ref, sguw_ref, sgub_ref) = refs[:_N_SAMPLE_INPUTS]
    mix_ref, pool_out, conv_out, ret_out, sgu_out, qg_s, k_s, v_s, os_s = refs[-9:]
    TB = _SEQ_BLOCK

    if all_layers:
        stacked = (pool_out, conv_out, ret_out, sgu_out)
        for ref in stacked:
            for other in range(DEPTH):
                if other != layer:
                    ref[other] = jnp.zeros(ref.shape[1:], F32)
        pool_out, conv_out, ret_out, sgu_out = (ref.at[layer] for ref in stacked)

    a_all = p_ref[:, _A:_A + GROUP]
    for gi, win in enumerate(POOL_WINDOWS):
        lanes = slice(gi * HEAD_DIM, (gi + 1) * HEAD_DIM)
        a = a_all[:, lanes]
        wsum = a
        for j in range(1, win):
            wsum = wsum + pool_ref[POOL_BUF - j, :, lanes]
        cnt = float(min(start + 1, win))
        d = (wsum / cnt - a).astype(BF16)
        y = _dot(d, poolw_ref[gi]) * pscale_ref[:, lanes]
        mix_ref[:, _YA + gi * HEAD_DIM:_YA + (gi + 1) * HEAD_DIM] = y.astype(BF16)
    pool_out[0:POOL_BUF - 1] = pool_ref[1:POOL_BUF]
    pool_out[POOL_BUF - 1] = a_all

    z = p_ref[:, _CG:_CG + GROUP] * p_ref[:, _HH:_HH + GROUP]
    acc = (conv_ref[:, 0, :] * convw_ref[0:1, :] + conv_ref[:, 1, :] * convw_ref[1:2, :]
           + z * convw_ref[2:3, :])
    mix_ref[:, _YB:_YB + GROUP] = (p_ref[:, _BG:_BG + GROUP] * acc).astype(BF16)
    conv_out[:, 0, :] = conv_ref[:, 1, :]
    conv_out[:, 1, :] = z

    cos = cos_ref[...]
    sin = sin_ref[...]
    scores = []
    for h in range(HEADS):
        lanes = slice(h * HEAD_DIM, (h + 1) * HEAD_DIM)
        q = _rotate(p_ref[:, _Q + h * HEAD_DIM:_Q + (h + 1) * HEAD_DIM], cos, sin)
        k = _rotate(p_ref[:, _K + h * HEAD_DIM:_K + (h + 1) * HEAD_DIM], cos, sin) * (HEAD_DIM ** -0.5)
        scores.append(jnp.sum(q * k, axis=-1, keepdims=True))
        qg_s[:, lanes] = q * gam_ref[:, lanes]
        k_s[:, lanes] = k
    v_s[...] = p_ref[:, _V:_V + GROUP]

    first_row = lax.broadcasted_iota(jnp.int32, (8, HEAD_DIM), 0) == 0

    for b in range(TB):
        for h in range(HEADS):
            lanes = slice(h * HEAD_DIM, (h + 1) * HEAD_DIM)
            s_prev = ret_ref[b, h]
            q8 = jnp.broadcast_to(qg_s[b:b + 1, lanes], (8, HEAD_DIM))
            os_s[b:b + 1, lanes] = _dot(q8.astype(BF16), s_prev.astype(BF16))[0:1, :]
            k8 = jnp.where(first_row, jnp.broadcast_to(k_s[b:b + 1, lanes], (8, HEAD_DIM)), 0.0)
            v8 = jnp.broadcast_to(v_s[b:b + 1, lanes], (8, HEAD_DIM))
            kv = lax.dot_general(k8.astype(BF16), v8.astype(BF16), (((0,), (0,)), ((), ())),
                                 preferred_element_type=F32)
            ret_out[b, h] = gam_ref[:, lanes] * s_prev + kv

    for h in range(HEADS):
        lanes = slice(h * HEAD_DIM, (h + 1) * HEAD_DIM)
        o = scores[h] * v_s[:, lanes] + os_s[:, lanes]
        on = _layernorm(o) * retg_ref[:, lanes]
        g = p_ref[:, _G + h * HEAD_DIM:_G + (h + 1) * HEAD_DIM]
        mix_ref[:, _YC + h * HEAD_DIM:_YC + (h + 1) * HEAD_DIM] = (_silu(g) * on).astype(BF16)

    vn = _layernorm(p_ref[:, _VV:_VV + GROUP]) * sgug_ref[...]
    mix_ref[:, _YD:_YD + GROUP] = (p_ref[:, _U:_U + GROUP] * (sguw_ref[...] * vn + sgub_ref[...])).astype(BF16)
    sgu_out[:, 0, :] = vn


def mixer_sample(p, pool_state, conv_state, ret_state, tables, params, layer, stacked, *, start):
    n = p.shape[0]
    tb = _SEQ_BLOCK
    cos, sin, gam = tables
    poolw, pscale, convw, retg, sgug, sguw, sgub = params
    rows = lambda width: pl.BlockSpec((tb, width), lambda i: (i, 0))
    lay_rows = lambda r: pl.BlockSpec((None, tb, r, GROUP), lambda i: (layer, i, 0, 0))
    lay_pool = pl.BlockSpec((None, POOL_BUF, tb, GROUP), lambda i: (layer, 0, i, 0))
    lay_ret = pl.BlockSpec((None, tb, HEADS, HEAD_DIM, HEAD_DIM), lambda i: (layer, i, 0, 0, 0))
    const2 = lambda shape: pl.BlockSpec(shape, lambda i: (0, 0))
    lay3 = lambda shape: pl.BlockSpec((None,) + shape, lambda i: (layer, 0, 0))
    lay4 = lambda shape: pl.BlockSpec((None,) + shape, lambda i: (layer, 0, 0, 0))
    in_specs = [
        rows(IN_WIDTH),
        lay_pool,
        lay_rows(2),
        lay_ret,
        const2((1, HEAD_DIM)),
        const2((1, HEAD_DIM)),
        const2((1, GROUP)),
        lay4((len(POOL_WINDOWS), HEAD_DIM, HEAD_DIM)),
        lay3((1, GROUP)),
        lay3((3, GROUP)),
        lay3((1, GROUP)),
        lay3((1, GROUP)),
        lay3((1, GROUP)),
        lay3((1, GROUP)),
    ]
    args = (p, pool_state, conv_state, ret_state, cos, sin, gam,
            poolw, pscale, convw, retg, sgug, sguw, sgub)
    assert len(args) == _N_SAMPLE_INPUTS
    aliases = {}
    first = stacked is None
    state_block = tb * (16 + 8 + 8 + HEAD_DIM) * GROUP * 4
    vmem = 2 * (tb * IN_WIDTH * 4 + state_block * (1 + (DEPTH if first else 1)))
    if first:
        all_rows = lambda r: pl.BlockSpec((DEPTH, tb, r, GROUP), lambda i: (0, i, 0, 0))
        state_specs = [pl.BlockSpec((DEPTH, POOL_BUF, tb, GROUP), lambda i: (0, 0, i, 0)), all_rows(2),
                       pl.BlockSpec((DEPTH, tb, HEADS, HEAD_DIM, HEAD_DIM), lambda i: (0, i, 0, 0, 0)),
                       all_rows(1)]
    else:
        state_specs = [lay_pool, lay_rows(2), lay_ret, lay_rows(1)]
        aliases = {len(args) + k: 1 + k for k in range(len(stacked))}
        in_specs = in_specs + [pl.BlockSpec(memory_space=pl.ANY)] * len(stacked)
        args = args + tuple(stacked)
    outs = pl.pallas_call(
        functools.partial(_mixer_sample_kernel, start=start, layer=layer, all_layers=first),
        grid=(n // tb,),
        in_specs=in_specs,
        out_specs=[rows(4 * GROUP)] + state_specs,
        out_shape=[
            jax.ShapeDtypeStruct((n, 4 * GROUP), BF16),
            jax.ShapeDtypeStruct((DEPTH, POOL_BUF, n, GROUP), F32),
            jax.ShapeDtypeStruct((DEPTH, n, 2, GROUP), F32),
            jax.ShapeDtypeStruct((DEPTH, n, HEADS, HEAD_DIM, HEAD_DIM), F32),
            jax.ShapeDtypeStruct((DEPTH, n, 1, GROUP), F32),
        ],
        scratch_shapes=[pltpu.VMEM((tb, GROUP), F32)] * 4,
        input_output_aliases=aliases,
        compiler_params=pltpu.CompilerParams(
            dimension_semantics=("arbitrary",),
            vmem_limit_bytes=_vmem_limit(vmem)),
        name="mixer_sample",
    )(*args)
    return outs[0], tuple(outs[1:])


def _rope_tables(start, length):
    half = HEAD_DIM // 2
    inv = ROPE_BASE ** (-jnp.arange(half, dtype=F32) / half)
    pos = start + jnp.arange(length)
    ang = pos.astype(F32)[:, None] * inv[None, :]
    cos, sin = jnp.cos(ang), jnp.sin(ang)
    return jnp.concatenate([cos, cos], axis=-1), jnp.concatenate([-sin, sin], axis=-1)


def _log_gamma():
    return jnp.log1p(-(2.0 ** (-5.0 - jnp.arange(HEADS, dtype=F32))))


def _decay_tables(c):
    lg = _log_gamma()
    idx = jnp.arange(c, dtype=F32)
    diff = idx[:, None] - idx[None, :]
    dmask = jnp.where(diff >= 0, jnp.exp(jnp.maximum(diff, 0.0)[None] * lg[:, None, None]), 0.0)
    qdec = jnp.exp((idx + 1.0)[None, :] * lg[:, None])[:, :, None]
    kdec = jnp.exp((c - 1.0 - idx)[None, :] * lg[:, None])[:, :, None]
    wide = lambda t: jnp.broadcast_to(t, (HEADS, c, HEAD_DIM))
    return dmask, wide(qdec), wide(kdec)


def _per_head_lanes(t):
    return jnp.repeat(t, HEAD_DIM, axis=-1)[..., None, :]


def kernel(x_prompt, x_sample, state_pool, state_conv, state_ret, norm1_g, w_in, pool_w, pool_scale,
           conv_w, ret_norm_g, sgu_norm_g, sgu_w, sgu_b, w_out, norm2_g, w_gate_up, w_down,
           final_norm_g):
    bp, lp, _ = x_prompt.shape
    ns = x_sample.shape[0]

    pool_w_b = pool_w.astype(BF16)
    row3 = lambda t: t[:, None, :]
    norm1 = row3(norm1_g)
    norm2 = row3(norm2_g)
    final_g = final_norm_g[None, :]

    cos_p, sin_p = _rope_tables(0, lp)
    prompt_tables = (cos_p, sin_p) + _decay_tables(CHUNK)
    sgu_b_wide = jnp.broadcast_to(sgu_b[:, :, :, None], sgu_b.shape + (HEAD_DIM,))
    prompt_params = (pool_w_b, row3(pool_scale), conv_w, row3(ret_norm_g), row3(sgu_norm_g), sgu_w, sgu_b_wide)

    cos_s, sin_s = _rope_tables(PAST_LEN, 1)
    gam = _per_head_lanes(jnp.exp(_log_gamma()))
    sample_tables = (cos_s, sin_s, gam)
    sample_params = (pool_w_b, row3(pool_scale), conv_w, row3(ret_norm_g), row3(sgu_norm_g),
                     _per_head_lanes(sgu_w[:, :, 0, 0]), _per_head_lanes(sgu_b[:, :, 0]))
    pool_rows = jnp.transpose(state_pool, (0, 2, 1, 3))
    hp = x_prompt
    hs = x_sample.reshape(ns, D_MODEL)
    prompt_states = [[] for _ in range(4)]
    sample_states = None
    next_b = None
    for l in range(DEPTH):
        fin = l == DEPTH - 1
        w_in_l, w_out_l = (w_in, w_out) if next_b is None else next_b
        p, w_in_b = norm_matmul(hs, norm1, w_in_l, l, tn=512)
        mix, sample_states = mixer_sample(p, pool_rows, state_conv, state_ret, sample_tables,
                                          sample_params, l, sample_states, start=PAST_LEN)
        hs, w_out_b = matmul_residual(mix, w_out_l, hs, l, tn=1024)
        hp, states, next_b = block_prompt(hp, norm1, w_in_b, w_out_b, prompt_tables, prompt_params, l,
                                          None if fin else (w_in, w_out), start=0, tm=256)
        for lst, val in zip(prompt_states, states):
            lst.append(val)
        hp = hp.reshape(bp * lp, D_MODEL)
        first, w_mlp_b = mlp_cast(hs, hp, norm2, w_gate_up, w_down, final_g, l, tp=_MLP_TILE, tf=256,
                                  final_norm=fin)
        hs = first[:ns]
        hp = mlp_rest(first, hp, norm2, w_mlp_b, final_g, l, skip_rows=ns, tm=_MLP_TILE, tf=512,
                      final_norm=fin).reshape(bp, lp, D_MODEL)
    pool_p, conv_p, ret_p, sgu_p = (jnp.stack(s) for s in prompt_states)
    pool_s, conv_s, ret_s, sgu_s = sample_states
    pool_s = jnp.transpose(pool_s, (0, 2, 1, 3))
    return (hp, hs.reshape(ns, 1, D_MODEL), pool_p, pool_s, conv_p, conv_s, ret_p, ret_s, sgu_p, sgu_s)
```

```python
import functools

import jax
import jax.numpy as jnp
from jax import lax
from jax.experimental import pallas as pl
from jax.experimental.pallas import tpu as pltpu

F32 = jnp.float32
BF16 = jnp.bfloat16

D_MODEL = 2048
DEPTH = 2
GROUP = 512
N_SPLITS = 10
IN_WIDTH = N_SPLITS * GROUP
POOL_WINDOWS = (2, 4, 8, 16)
POOL_BUF = 15
HEADS = 4
HEAD_DIM = 128
CHUNK = 128
ROPE_BASE = 10000.0
D_FF = 5632
NORM_EPS = 1e-6
PAST_LEN = 16384

_A, _BG, _CG, _HH, _Q, _K, _V, _G, _U, _VV = (i * GROUP for i in range(N_SPLITS))
_YA, _YB, _YC, _YD = (i * GROUP for i in range(4))

MIB = 1024 * 1024
VMEM_BYTES_V7X = 64 * MIB


VMEM_LIMIT_MAX = VMEM_BYTES_V7X - 4 * MIB


def _vmem_limit(estimate):
    return min(estimate + 4 * MIB, VMEM_LIMIT_MAX)


def _rms(x, g):
    ms = jnp.mean(x * x, axis=-1, keepdims=True)
    return x * lax.rsqrt(ms + NORM_EPS) * g


def _layernorm(x):
    mu = jnp.mean(x, axis=-1, keepdims=True)
    xc = x - mu
    return xc * lax.rsqrt(jnp.mean(xc * xc, axis=-1, keepdims=True) + NORM_EPS)


def _silu(x):
    return x * jax.nn.sigmoid(x)


def _dot(a, b):
    return jnp.dot(a, b, preferred_element_type=F32)


def _rotate(x, cos, sin_signed):
    return x * cos + pltpu.roll(x, HEAD_DIM // 2, 1) * sin_signed


def _weight_specs(w, layer, k, tn):
    if w.ndim == 3:
        return pl.BlockSpec((None, k, tn), lambda j: (layer, 0, j)), True
    return pl.BlockSpec((k, tn), lambda j: (0, j)), False


_STREAMING_VMEM_LIMIT = VMEM_LIMIT_MAX


def _norm_matmul_kernel(x_ref, g_ref, w_ref, o_ref, *rest, emit):
    xn_ref = rest[-1]

    @pl.when(pl.program_id(0) == 0)
    def _():
        xn_ref[...] = _rms(x_ref[...], g_ref[...]).astype(BF16)

    wb = w_ref[...].astype(BF16)
    if emit:
        rest[0][...] = wb
    o_ref[...] = _dot(xn_ref[...], wb)


def norm_matmul(x, g, w, layer, *, tn):
    m, k = x.shape
    n = w.shape[-1]
    w_spec, emit = _weight_specs(w, layer, k, tn)
    outs = pl.pallas_call(
        functools.partial(_norm_matmul_kernel, emit=emit),
        grid=(n // tn,),
        in_specs=[
            pl.BlockSpec((m, k), lambda j: (0, 0)),
            pl.BlockSpec((None, 1, k), lambda j: (layer, 0, 0)),
            w_spec,
        ],
        out_specs=[pl.BlockSpec((m, tn), lambda j: (0, j))] + [pl.BlockSpec((k, tn), lambda j: (0, j))] * emit,
        out_shape=[jax.ShapeDtypeStruct((m, n), F32)] + [jax.ShapeDtypeStruct((k, n), BF16)] * emit,
        scratch_shapes=[pltpu.VMEM((m, k), BF16)],
        compiler_params=pltpu.CompilerParams(
            dimension_semantics=("arbitrary",),
            vmem_limit_bytes=_STREAMING_VMEM_LIMIT),
        name="norm_matmul",
    )(x, g, w)
    return tuple(outs) if emit else (outs[0], w)


def _matmul_residual_kernel(a_ref, w_ref, r_ref, o_ref, *rest, emit):
    wb = w_ref[...].astype(BF16)
    if emit:
        rest[0][...] = wb
    o_ref[...] = r_ref[...] + _dot(a_ref[...], wb)


def matmul_residual(a, w, res, layer, *, tn):
    m, k = a.shape
    n = w.shape[-1]
    w_spec, emit = _weight_specs(w, layer, k, tn)
    outs = pl.pallas_call(
        functools.partial(_matmul_residual_kernel, emit=emit),
        grid=(n // tn,),
        in_specs=[
            pl.BlockSpec((m, k), lambda j: (0, 0)),
            w_spec,
            pl.BlockSpec((m, tn), lambda j: (0, j)),
        ],
        out_specs=[pl.BlockSpec((m, tn), lambda j: (0, j))] + [pl.BlockSpec((k, tn), lambda j: (0, j))] * emit,
        out_shape=[jax.ShapeDtypeStruct((m, n), F32)] + [jax.ShapeDtypeStruct((k, n), BF16)] * emit,
        compiler_params=pltpu.CompilerParams(
            dimension_semantics=("arbitrary",),
            vmem_limit_bytes=_STREAMING_VMEM_LIMIT),
        name="matmul_residual",
    )(a, w, res)
    return tuple(outs) if emit else (outs[0], w)


_MLP_TILE = 1024


def _mlp_start(first, x_refs, g_ref, o_ref, hn_ref):
    @pl.when(first)
    def _():
        row = 0
        for x_ref in x_refs:
            x = x_ref[...]
            rows = slice(row, row + x.shape[0])
            hn_ref[rows, :] = _rms(x, g_ref[...]).astype(BF16)
            o_ref[rows, :] = x
            row += x.shape[0]


def _mlp_accumulate(wg, wu, wd, o_ref, hn_ref):
    hn = hn_ref[...]
    act = (_silu(_dot(hn, wg)) * _dot(hn, wu)).astype(BF16)
    o_ref[...] += _dot(act, wd)


def _mlp_finish(last, fg_ref, o_ref, final_norm):
    if final_norm:
        @pl.when(last)
        def _():
            o_ref[...] = _rms(o_ref[...], fg_ref[...])


def _mlp_cast_kernel(xs_ref, xp_ref, g_ref, wg_ref, wu_ref, wd_ref, fg_ref,
                     o_ref, wgb_ref, wub_ref, wdb_ref, hn_ref, *, final_norm):
    f = pl.program_id(0)
    _mlp_start(f == 0, (xs_ref, xp_ref), g_ref, o_ref, hn_ref)
    wg, wu, wd = wg_ref[...].astype(BF16), wu_ref[...].astype(BF16), wd_ref[...].astype(BF16)
    wgb_ref[...] = wg
    wub_ref[...] = wu
    wdb_ref[...] = wd
    _mlp_accumulate(wg, wu, wd, o_ref, hn_ref)
    _mlp_finish(f == pl.num_programs(0) - 1, fg_ref, o_ref, final_norm)


def _mlp_rest_kernel(first_hbm, x_ref, g_ref, wg_ref, wu_ref, wd_ref, fg_ref, o_ref, hn_ref, sem,
                     *, nf, skip_rows, final_norm):
    s = pl.program_id(0)
    tm = o_ref.shape[0]

    @pl.when(s == 0)
    def _():
        copy = pltpu.make_async_copy(first_hbm.at[pl.ds(skip_rows, tm), :], o_ref, sem)
        copy.start()
        copy.wait()

    @pl.when(s > 0)
    def _():
        f = (s - 1) % nf
        _mlp_start(f == 0, (x_ref,), g_ref, o_ref, hn_ref)
        _mlp_accumulate(wg_ref[...], wu_ref[...], wd_ref[...], o_ref, hn_ref)
        _mlp_finish(f == nf - 1, fg_ref, o_ref, final_norm)


def mlp_cast(xs, xp, g, w_gate_up, w_down, final_g, layer, *, tp, tf, final_norm):
    ns, d = xs.shape
    m = ns + tp
    nf = D_FF // tf
    once = pl.Buffered(1)
    vmem = 2 * m * d * 4 + m * d * 2 + 2 * 3 * d * tf * (4 + 2) + 3 * m * tf * 4
    outs = pl.pallas_call(
        functools.partial(_mlp_cast_kernel, final_norm=final_norm),
        grid=(nf,),
        in_specs=[
            pl.BlockSpec((ns, d), lambda f: (0, 0), pipeline_mode=once),
            pl.BlockSpec((tp, d), lambda f: (0, 0), pipeline_mode=once),
            pl.BlockSpec((None, 1, d), lambda f: (layer, 0, 0)),
            pl.BlockSpec((None, d, tf), lambda f: (layer, 0, f)),
            pl.BlockSpec((None, d, tf), lambda f: (layer, 0, nf + f)),
            pl.BlockSpec((None, tf, d), lambda f: (layer, f, 0)),
            pl.BlockSpec((1, d), lambda f: (0, 0)),
        ],
        out_specs=[
            pl.BlockSpec((m, d), lambda f: (0, 0), pipeline_mode=once),
            pl.BlockSpec((d, tf), lambda f: (0, f)),
            pl.BlockSpec((d, tf), lambda f: (0, f)),
            pl.BlockSpec((tf, d), lambda f: (f, 0)),
        ],
        out_shape=[jax.ShapeDtypeStruct((m, d), F32),
                   jax.ShapeDtypeStruct((d, D_FF), BF16), jax.ShapeDtypeStruct((d, D_FF), BF16),
                   jax.ShapeDtypeStruct((D_FF, d), BF16)],
        scratch_shapes=[pltpu.VMEM((m, d), BF16)],
        compiler_params=pltpu.CompilerParams(
            dimension_semantics=("arbitrary",),
            vmem_limit_bytes=_vmem_limit(vmem)),
        name="mlp_cast",
    )(xs, xp, g, w_gate_up, w_gate_up, w_down, final_g)
    return outs[0], tuple(outs[1:])


def mlp_rest(first, x, g, weights, final_g, layer, *, skip_rows, tm, tf, final_norm):
    m, d = x.shape
    nf = D_FF // tf
    step = lambda s: jnp.maximum(s - 1, 0)
    x_tile = lambda s: 1 + step(s) // nf
    f_tile = lambda s: step(s) % nf
    vmem = 2 * 2 * tm * d * 4 + tm * d * 2 + 2 * 3 * d * tf * 2 + 3 * tm * tf * 4
    return pl.pallas_call(
        functools.partial(_mlp_rest_kernel, nf=nf, skip_rows=skip_rows, final_norm=final_norm),
        grid=(1 + (m // tm - 1) * nf,),
        in_specs=[
            pl.BlockSpec(memory_space=pl.ANY),
            pl.BlockSpec((tm, d), lambda s: (x_tile(s), 0)),
            pl.BlockSpec((None, 1, d), lambda s: (layer, 0, 0)),
            pl.BlockSpec((d, tf), lambda s: (0, f_tile(s))),
            pl.BlockSpec((d, tf), lambda s: (0, f_tile(s))),
            pl.BlockSpec((tf, d), lambda s: (f_tile(s), 0)),
            pl.BlockSpec((1, d), lambda s: (0, 0)),
        ],
        out_specs=pl.BlockSpec((tm, d), lambda s: (jnp.where(s == 0, 0, x_tile(s)), 0)),
        out_shape=jax.ShapeDtypeStruct((m, d), F32),
        scratch_shapes=[pltpu.VMEM((tm, d), BF16), pltpu.SemaphoreType.DMA(())],
        compiler_params=pltpu.CompilerParams(
            dimension_semantics=("arbitrary",),
            vmem_limit_bytes=_vmem_limit(vmem)),
        name="mlp_rest",
    )(first, x, g, *weights, final_g)


_POOL_TOP = 16
_CONV_TOP = 8


def _block_prompt_kernel(x_ref, g1_ref, win_ref, wout_ref, cos_ref, sin_ref, dmask_ref, qdec_ref, kdec_ref,
                         poolw_ref, pscale_ref, convw_ref, retg_ref, sgug_ref, sguw_ref, sgub_ref,
                         *rest, start, tm, cast_next):
    if cast_next:
        win_next_ref, wout_next_ref, *rest = rest
        *rest, win_next_out, wout_next_out, pool_ext, conv_ext = rest
        win_next_out[...] = win_next_ref[...].astype(BF16)
        wout_next_out[...] = wout_next_ref[...].astype(BF16)
    else:
        *rest, pool_ext, conv_ext = rest
    h_out, pool_out, conv_out, ret_out, sgu_out = rest
    c = pl.program_id(1)
    T = CHUNK
    n_chunks = tm // T

    @pl.when(c == 0)
    def _():
        pool_ext[0:_POOL_TOP, :] = jnp.zeros((_POOL_TOP, GROUP), F32)
        conv_ext[0:_CONV_TOP, :] = jnp.zeros((_CONV_TOP, GROUP), F32)
        ret_out[...] = jnp.zeros(ret_out.shape, F32)

    x = x_ref[0]
    xn = _rms(x, g1_ref[...]).astype(BF16)

    def proj(col):
        return _dot(xn, win_ref[:, col:col + GROUP])

    def out_proj(y, col):
        return _dot(y, wout_ref[col:col + GROUP, :])

    head_lanes = [slice(hd * HEAD_DIM, (hd + 1) * HEAD_DIM) for hd in range(HEADS)]
    chunk_rows = [slice(j * T, (j + 1) * T) for j in range(n_chunks)]

    def pool_mixer(a_proj):
        pool_ext[_POOL_TOP:_POOL_TOP + tm, :] = a_proj
        pos1 = lax.broadcasted_iota(jnp.int32, (tm, HEAD_DIM), 0) + (start + 1) + c * tm
        y = []
        for gi, win in enumerate(POOL_WINDOWS):
            lanes = head_lanes[gi]
            a = pool_ext[_POOL_TOP:_POOL_TOP + tm, lanes]
            wsum = a
            for j in range(1, win):
                wsum = wsum + pool_ext[_POOL_TOP - j:_POOL_TOP - j + tm, lanes]
            cnt = jnp.minimum(pos1, win).astype(F32)
            d = (wsum / cnt - a).astype(BF16)
            y.append((_dot(d, poolw_ref[gi]) * pscale_ref[:, lanes]).astype(BF16))
        pool_out[0] = pool_ext[_POOL_TOP + tm - POOL_BUF:_POOL_TOP + tm, :]
        pool_ext[_POOL_TOP - POOL_BUF:_POOL_TOP, :] = pool_ext[_POOL_TOP + tm - POOL_BUF:_POOL_TOP + tm, :]
        return jnp.concatenate(y, axis=-1)

    def conv_mixer(bg, cg, hh):
        z = cg * hh
        conv_ext[_CONV_TOP:_CONV_TOP + tm, :] = z
        acc = (conv_ext[_CONV_TOP - 2:_CONV_TOP - 2 + tm, :] * convw_ref[0:1, :]
               + conv_ext[_CONV_TOP - 1:_CONV_TOP - 1 + tm, :] * convw_ref[1:2, :]
               + z * convw_ref[2:3, :])
        conv_out[0] = conv_ext[_CONV_TOP + tm - 2:_CONV_TOP + tm, :]
        conv_ext[_CONV_TOP - 2:_CONV_TOP, :] = conv_ext[_CONV_TOP + tm - 2:_CONV_TOP + tm, :]
        return (bg * acc).astype(BF16)

    def retention(q_all, k_all, v_all, g_all, j, hd):
        rows, lanes = chunk_rows[j], head_lanes[hd]
        cos = cos_ref[rows, :]
        sin = sin_ref[rows, :]
        q = _rotate(q_all[rows, lanes], cos, sin)
        k = _rotate(k_all[rows, lanes], cos, sin) * (HEAD_DIM ** -0.5)
        v = v_all[rows, lanes].astype(BF16)
        s_prev = ret_out[0, hd]
        scores = lax.dot_general(q.astype(BF16), k.astype(BF16), (((1,), (1,)), ((), ())),
                                 preferred_element_type=F32) * dmask_ref[hd]
        o = _dot(scores.astype(BF16), v) + _dot((q * qdec_ref[hd]).astype(BF16), s_prev.astype(BF16))
        kv = lax.dot_general((k * kdec_ref[hd]).astype(BF16), v, (((0,), (0,)), ((), ())),
                             preferred_element_type=F32)
        ret_out[0, hd] = qdec_ref[hd, T - 1:T, :] * s_prev + kv
        on = _layernorm(o) * retg_ref[:, lanes]
        return (_silu(g_all[rows, lanes]) * on).astype(BF16)

    def sgu_mixer(u_all, vn, j):
        rows = chunk_rows[j]
        tri = lax.broadcasted_iota(jnp.int32, (T, T), 0) >= lax.broadcasted_iota(jnp.int32, (T, T), 1)
        y = []
        for hd in range(HEADS):
            lanes = head_lanes[hd]
            w = jnp.where(tri, sguw_ref[hd], 0.0).astype(BF16)
            mixed = _dot(w, vn[rows, lanes].astype(BF16)) + sgub_ref[hd]
            y.append((u_all[rows, lanes] * mixed).astype(BF16))
        return jnp.concatenate(y, axis=-1)

    q_all, k_all, v_all, g_all = proj(_Q), proj(_K), proj(_V), proj(_G)
    ret = functools.partial(retention, q_all, k_all, v_all, g_all)
    y_c = [[None] * HEADS for _ in range(n_chunks)]
    y_c[0][0] = ret(0, 0)
    a_proj = proj(_A)
    y_c[0][1] = ret(0, 1)
    cg = proj(_CG)
    y_c[0][2] = ret(0, 2)
    hh = proj(_HH)
    y_c[0][3] = ret(0, 3)
    bg = proj(_BG)
    later = iter([lambda: proj(_VV), lambda: proj(_U), lambda: pool_mixer(a_proj),
                  lambda: conv_mixer(bg, cg, hh)])
    fills = []
    for j in range(1, n_chunks):
        for hd in range(HEADS):
            y_c[j][hd] = ret(j, hd)
            nxt = next(later, None)
            if nxt is not None:
                fills.append(nxt())
    fills += [f() for f in later]
    vv, u_all, y_a, y_b = fills
    h = x + out_proj(jnp.concatenate([jnp.concatenate(r, axis=-1) for r in y_c], axis=0), _YC)
    vn = _layernorm(vv) * sgug_ref[...]
    y_d = [sgu_mixer(u_all, vn, 0)]
    h = h + out_proj(y_a, _YA)
    y_d += [sgu_mixer(u_all, vn, j) for j in range(1, n_chunks)]
    h = h + out_proj(y_b, _YB)
    h_out[0] = h + out_proj(jnp.concatenate(y_d, axis=0), _YD)
    sgu_out[0] = vn[tm - T:tm, :]


def block_prompt(x, norm_g, w_in, w_out, tables, params, layer, next_weights, *, start, tm):
    b, l, d = x.shape
    nc = l // tm
    cos, sin, dmask, qdec, kdec = tables
    poolw, pscale, convw, retg, sgug, sguw, sgub = params
    const3 = lambda shape: pl.BlockSpec(shape, lambda i, c: (0, 0, 0))
    lay3 = lambda shape: pl.BlockSpec((None,) + shape, lambda i, c: (layer, 0, 0))
    lay4 = lambda shape: pl.BlockSpec((None,) + shape, lambda i, c: (layer, 0, 0, 0))
    resident = lambda shape: pl.BlockSpec(shape, lambda i, c: (0, 0), pipeline_mode=pl.Buffered(1))
    cast_next = next_weights is not None
    next_in_specs, next_out_specs, next_out_shape, next_args = [], [], [], ()
    next_bytes = 0
    if cast_next:
        for w in next_weights:
            rows, cols = w.shape[1] // (b * nc), w.shape[2]
            assert rows * b * nc == w.shape[1] and rows % 16 == 0
            next_in_specs.append(pl.BlockSpec((None, rows, cols), lambda i, c: (layer + 1, i * nc + c, 0)))
            next_out_specs.append(pl.BlockSpec((rows, cols), lambda i, c: (i * nc + c, 0)))
            next_out_shape.append(jax.ShapeDtypeStruct(w.shape[1:], BF16))
            next_bytes += 2 * rows * cols * (4 + 2)
        next_args = tuple(next_weights)
    vmem = (d * IN_WIDTH + 4 * GROUP * d) * 2 + 2 * 2 * tm * d * 4 + 2 * tm * IN_WIDTH * 4 + next_bytes
    outs = pl.pallas_call(
        functools.partial(_block_prompt_kernel, start=start, tm=tm, cast_next=cast_next),
        grid=(b, nc),
        in_specs=[
            pl.BlockSpec((1, tm, d), lambda i, c: (i, c, 0)),
            lay3((1, d)),
            resident((d, IN_WIDTH)),
            resident((4 * GROUP, d)),
            pl.BlockSpec((tm, HEAD_DIM), lambda i, c: (c, 0)),
            pl.BlockSpec((tm, HEAD_DIM), lambda i, c: (c, 0)),
            const3((HEADS, CHUNK, CHUNK)),
            const3((HEADS, CHUNK, HEAD_DIM)),
            const3((HEADS, CHUNK, HEAD_DIM)),
            lay4((len(POOL_WINDOWS), HEAD_DIM, HEAD_DIM)),
            lay3((1, GROUP)),
            lay3((3, GROUP)),
            lay3((1, GROUP)),
            lay3((1, GROUP)),
            lay4((HEADS, CHUNK, CHUNK)),
            lay4((HEADS, CHUNK, HEAD_DIM)),
            *next_in_specs,
        ],
        out_specs=[
            pl.BlockSpec((1, tm, d), lambda i, c: (i, c, 0)),
            pl.BlockSpec((1, POOL_BUF, GROUP), lambda i, c: (i, 0, 0)),
            pl.BlockSpec((1, 2, GROUP), lambda i, c: (i, 0, 0)),
            pl.BlockSpec((1, HEADS, HEAD_DIM, HEAD_DIM), lambda i, c: (i, 0, 0, 0)),
            pl.BlockSpec((1, CHUNK, GROUP), lambda i, c: (i, 0, 0)),
            *next_out_specs,
        ],
        out_shape=[
            jax.ShapeDtypeStruct((b, l, d), F32),
            jax.ShapeDtypeStruct((b, POOL_BUF, GROUP), F32),
            jax.ShapeDtypeStruct((b, 2, GROUP), F32),
            jax.ShapeDtypeStruct((b, HEADS, HEAD_DIM, HEAD_DIM), F32),
            jax.ShapeDtypeStruct((b, CHUNK, GROUP), F32),
            *next_out_shape,
        ],
        scratch_shapes=[
            pltpu.VMEM((_POOL_TOP + tm, GROUP), F32),
            pltpu.VMEM((_CONV_TOP + tm, GROUP), F32),
        ],
        compiler_params=pltpu.CompilerParams(
            dimension_semantics=("arbitrary", "arbitrary"),
            vmem_limit_bytes=_vmem_limit(vmem)),
        name="block_prompt",
    )(x, norm_g, w_in, w_out, cos, sin, dmask, qdec, kdec, poolw, pscale, convw, retg, sgug, sguw, sgub,
      *next_args)
    return outs[0], tuple(outs[1:5]), (tuple(outs[5:]) if cast_next else None)


_SEQ_BLOCK = 16


_N_SAMPLE_INPUTS = 14


def _mixer_sample_kernel(*refs, start, layer, all_layers):
    (p_ref, pool_ref, conv_ref, ret_ref, cos_ref, sin_ref, gam_ref,
     poolw_ref, pscale_ref, convw_ref, retg_ref, sgug_ref, sguw_ref, sgub_ref) = refs[:_N_SAMPLE_INPUTS]
    mix_ref, pool_out, conv_out, ret_out, sgu_out, qg_s, k_s, v_s, os_s = refs[-9:]
    TB = _SEQ_BLOCK

    if all_layers:
        stacked = (pool_out, conv_out, ret_out, sgu_out)
        for ref in stacked:
            for other in range(DEPTH):
                if other != layer:
                    ref[other] = jnp.zeros(ref.shape[1:], F32)
        pool_out, conv_out, ret_out, sgu_out = (ref.at[layer] for ref in stacked)

    a_all = p_ref[:, _A:_A + GROUP]
    for gi, win in enumerate(POOL_WINDOWS):
        lanes = slice(gi * HEAD_DIM, (gi + 1) * HEAD_DIM)
        a = a_all[:, lanes]
        wsum = a
        for j in range(1, win):
            wsum = wsum + pool_ref[POOL_BUF - j, :, lanes]
        cnt = float(min(start + 1, win))
        d = (wsum / cnt - a).astype(BF16)
        y = _dot(d, poolw_ref[gi]) * pscale_ref[:, lanes]
        mix_ref[:, _YA + gi * HEAD_DIM:_YA + (gi + 1) * HEAD_DIM] = y.astype(BF16)
    pool_out[0:POOL_BUF - 1] = pool_ref[1:POOL_BUF]
    pool_out[POOL_BUF - 1] = a_all

    z = p_ref[:, _CG:_CG + GROUP] * p_ref[:, _HH:_HH + GROUP]
    acc = (conv_ref[:, 0, :] * convw_ref[0:1, :] + conv_ref[:, 1, :] * convw_ref[1:2, :]
           + z * convw_ref[2:3, :])
    mix_ref[:, _YB:_YB + GROUP] = (p_ref[:, _BG:_BG + GROUP] * acc).astype(BF16)
    conv_out[:, 0, :] = conv_ref[:, 1, :]
    conv_out[:, 1, :] = z

    cos = cos_ref[...]
    sin = sin_ref[...]
    scores = []
    for h in range(HEADS):
        lanes = slice(h * HEAD_DIM, (h + 1) * HEAD_DIM)
        q = _rotate(p_ref[:, _Q + h * HEAD_DIM:_Q + (h + 1) * HEAD_DIM], cos, sin)
        k = _rotate(p_ref[:, _K + h * HEAD_DIM:_K + (h + 1) * HEAD_DIM], cos, sin) * (HEAD_DIM ** -0.5)
        scores.append(jnp.sum(q * k, axis=-1, keepdims=True))
        qg_s[:, lanes] = q * gam_ref[:, lanes]
        k_s[:, lanes] = k
    v_s[...] = p_ref[:, _V:_V + GROUP]

    first_row = lax.broadcasted_iota(jnp.int32, (8, HEAD_DIM), 0) == 0

    for b in range(TB):
        for h in range(HEADS):
            lanes = slice(h * HEAD_DIM, (h + 1) * HEAD_DIM)
            s_prev = ret_ref[b, h]
            q8 = jnp.broadcast_to(qg_s[b:b + 1, lanes], (8, HEAD_DIM))
            os_s[b:b + 1, lanes] = _dot(q8.astype(BF16), s_prev.astype(BF16))[0:1, :]
            k8 = jnp.where(first_row, jnp.broadcast_to(k_s[b:b + 1, lanes], (8, HEAD_DIM)), 0.0)
            v8 = jnp.broadcast_to(v_s[b:b + 1, lanes], (8, HEAD_DIM))
            kv = lax.dot_general(k8.astype(BF16), v8.astype(BF16), (((0,), (0,)), ((), ())),
                                 preferred_element_type=F32)
            ret_out[b, h] = gam_ref[:, lanes] * s_prev + kv

    for h in range(HEADS):
        lanes = slice(h * HEAD_DIM, (h + 1) * HEAD_DIM)
        o = scores[h] * v_s[:, lanes] + os_s[:, lanes]
        on = _layernorm(o) * retg_ref[:, lanes]
        g = p_ref[:, _G + h * HEAD_DIM:_G + (h + 1) * HEAD_DIM]
        mix_ref[:, _YC + h * HEAD_DIM:_YC + (h + 1) * HEAD_DIM] = (_silu(g) * on).astype(BF16)

    vn = _layernorm(p_ref[:, _VV:_VV + GROUP]) * sgug_ref[...]
    mix_ref[:, _YD:_YD + GROUP] = (p_ref[:, _U:_U + GROUP] * (sguw_ref[...] * vn + sgub_ref[...])).astype(BF16)
    sgu_out[:, 0, :] = vn


def mixer_sample(p, pool_state, conv_state, ret_state, tables, params, layer, stacked, *, start):
    n = p.shape[0]
    tb = _SEQ_BLOCK
    cos, sin, gam = tables
    poolw, pscale, convw, retg, sgug, sguw, sgub = params
    rows = lambda width: pl.BlockSpec((tb, width), lambda i: (i, 0))
    lay_rows = lambda r: pl.BlockSpec((None, tb, r, GROUP), lambda i: (layer, i, 0, 0))
    lay_pool = pl.BlockSpec((None, POOL_BUF, tb, GROUP), lambda i: (layer, 0, i, 0))
    lay_ret = pl.BlockSpec((None, tb, HEADS, HEAD_DIM, HEAD_DIM), lambda i: (layer, i, 0, 0, 0))
    const2 = lambda shape: pl.BlockSpec(shape, lambda i: (0, 0))
    lay3 = lambda shape: pl.BlockSpec((None,) + shape, lambda i: (layer, 0, 0))
    lay4 = lambda shape: pl.BlockSpec((None,) + shape, lambda i: (layer, 0, 0, 0))
    in_specs = [
        rows(IN_WIDTH),
        lay_pool,
        lay_rows(2),
        lay_ret,
        const2((1, HEAD_DIM)),
        const2((1, HEAD_DIM)),
        const2((1, GROUP)),
        lay4((len(POOL_WINDOWS), HEAD_DIM, HEAD_DIM)),
        lay3((1, GROUP)),
        lay3((3, GROUP)),
        lay3((1, GROUP)),
        lay3((1, GROUP)),
        lay3((1, GROUP)),
        lay3((1, GROUP)),
    ]
    args = (p, pool_state, conv_state, ret_state, cos, sin, gam,
            poolw, pscale, convw, retg, sgug, sguw, sgub)
    assert len(args) == _N_SAMPLE_INPUTS
    aliases = {}
    first = stacked is None
    state_block = tb * (16 + 8 + 8 + HEAD_DIM) * GROUP * 4
    vmem = 2 * (tb * IN_WIDTH * 4 + state_block * (1 + (DEPTH if first else 1)))
    if first:
        all_rows = lambda r: pl.BlockSpec((DEPTH, tb, r, GROUP), lambda i: (0, i, 0, 0))
        state_specs = [pl.BlockSpec((DEPTH, POOL_BUF, tb, GROUP), lambda i: (0, 0, i, 0)), all_rows(2),
                       pl.BlockSpec((DEPTH, tb, HEADS, HEAD_DIM, HEAD_DIM), lambda i: (0, i, 0, 0, 0)),
                       all_rows(1)]
    else:
        state_specs = [lay_pool, lay_rows(2), lay_ret, lay_rows(1)]
        aliases = {len(args) + k: 1 + k for k in range(len(stacked))}
        in_specs = in_specs + [pl.BlockSpec(memory_space=pl.ANY)] * len(stacked)
        args = args + tuple(stacked)
    outs = pl.pallas_call(
        functools.partial(_mixer_sample_kernel, start=start, layer=layer, all_layers=first),
        grid=(n // tb,),
        in_specs=in_specs,
        out_specs=[rows(4 * GROUP)] + state_specs,
        out_shape=[
            jax.ShapeDtypeStruct((n, 4 * GROUP), BF16),
            jax.ShapeDtypeStruct((DEPTH, POOL_BUF, n, GROUP), F32),
            jax.ShapeDtypeStruct((DEPTH, n, 2, GROUP), F32),
            jax.ShapeDtypeStruct((DEPTH, n, HEADS, HEAD_DIM, HEAD_DIM), F32),
            jax.ShapeDtypeStruct((DEPTH, n, 1, GROUP), F32),
        ],
        scratch_shapes=[pltpu.VMEM((tb, GROUP), F32)] * 4,
        input_output_aliases=aliases,
        compiler_params=pltpu.CompilerParams(
            dimension_semantics=("arbitrary",),
            vmem_limit_bytes=_vmem_limit(vmem)),
        name="mixer_sample",
    )(*args)
    return outs[0], tuple(outs[1:])


def _rope_tables(start, length):
    half = HEAD_DIM // 2
    inv = ROPE_BASE ** (-jnp.arange(half, dtype=F32) / half)
    pos = start + jnp.arange(length)
    ang = pos.astype(F32)[:, None] * inv[None, :]
    cos, sin = jnp.cos(ang), jnp.sin(ang)
    return jnp.concatenate([cos, cos], axis=-1), jnp.concatenate([-sin, sin], axis=-1)


def _log_gamma():
    return jnp.log1p(-(2.0 ** (-5.0 - jnp.arange(HEADS, dtype=F32))))


def _decay_tables(c):
    lg = _log_gamma()
    idx = jnp.arange(c, dtype=F32)
    diff = idx[:, None] - idx[None, :]
    dmask = jnp.where(diff >= 0, jnp.exp(jnp.maximum(diff, 0.0)[None] * lg[:, None, None]), 0.0)
    qdec = jnp.exp((idx + 1.0)[None, :] * lg[:, None])[:, :, None]
    kdec = jnp.exp((c - 1.0 - idx)[None, :] * lg[:, None])[:, :, None]
    wide = lambda t: jnp.broadcast_to(t, (HEADS, c, HEAD_DIM))
    return dmask, wide(qdec), wide(kdec)


def _per_head_lanes(t):
    return jnp.repeat(t, HEAD_DIM, axis=-1)[..., None, :]


def kernel(x_prompt, x_sample, state_pool, state_conv, state_ret, norm1_g, w_in, pool_w, pool_scale,
           conv_w, ret_norm_g, sgu_norm_g, sgu_w, sgu_b, w_out, norm2_g, w_gate_up, w_down,
           final_norm_g):
    bp, lp, _ = x_prompt.shape
    ns = x_sample.shape[0]

    pool_w_b = pool_w.astype(BF16)
    row3 = lambda t: t[:, None, :]
    norm1 = row3(norm1_g)
    norm2 = row3(norm2_g)
    final_g = final_norm_g[None, :]

    cos_p, sin_p = _rope_tables(0, lp)
    prompt_tables = (cos_p, sin_p) + _decay_tables(CHUNK)
    sgu_b_wide = jnp.broadcast_to(sgu_b[:, :, :, None], sgu_b.shape + (HEAD_DIM,))
    prompt_params = (pool_w_b, row3(pool_scale), conv_w, row3(ret_norm_g), row3(sgu_norm_g), sgu_w, sgu_b_wide)

    cos_s, sin_s = _rope_tables(PAST_LEN, 1)
    gam = _per_head_lanes(jnp.exp(_log_gamma()))
    sample_tables = (cos_s, sin_s, gam)
    sample_params = (pool_w_b, row3(pool_scale), conv_w, row3(ret_norm_g), row3(sgu_norm_g),
                     _per_head_lanes(sgu_w[:, :, 0, 0]), _per_head_lanes(sgu_b[:, :, 0]))
    pool_rows = jnp.transpose(state_pool, (0, 2, 1, 3))
    hp = x_prompt
    hs = x_sample.reshape(ns, D_MODEL)
    prompt_states = [[] for _ in range(4)]
    sample_states = None
    next_b = None
    for l in range(DEPTH):
        fin = l == DEPTH - 1
        w_in_l, w_out_l = (w_in, w_out) if next_b is None else next_b
        p, w_in_b = norm_matmul(hs, norm1, w_in_l, l, tn=512)
        mix, sample_states = mixer_sample(p, pool_rows, state_conv, state_ret, sample_tables,
                                          sample_params, l, sample_states, start=PAST_LEN)
        hs, w_out_b = matmul_residual(mix, w_out_l, hs, l, tn=1024)
        hp, states, next_b = block_prompt(hp, norm1, w_in_b, w_out_b, prompt_tables, prompt_params, l,
                                          None if fin else (w_in, w_out), start=0, tm=256)
        for lst, val in zip(prompt_states, states):
            lst.append(val)
        hp = hp.reshape(bp * lp, D_MODEL)
        first, w_mlp_b = mlp_cast(hs, hp, norm2, w_gate_up, w_down, final_g, l, tp=_MLP_TILE, tf=256,
                                  final_norm=fin)
        hs = first[:ns]
        hp = mlp_rest(first, hp, norm2, w_mlp_b, final_g, l, skip_rows=ns, tm=_MLP_TILE, tf=512,
                      final_norm=fin).reshape(bp, lp, D_MODEL)
    pool_p, conv_p, ret_p, sgu_p = (jnp.stack(s) for s in prompt_states)
    pool_s, conv_s, ret_s, sgu_s = sample_states
    pool_s = jnp.transpose(pool_s, (0, 2, 1, 3))
    return (hp, hs.reshape(ns, 1, D_MODEL), pool_p, pool_s, conv_p, conv_s, ret_p, ret_s, sgu_p, sgu_s)
```

```python
import functools

import jax
import jax.numpy as jnp
from jax import lax
from jax.experimental import pallas as pl
from jax.experimental.pallas import tpu as pltpu

F32 = jnp.float32
BF16 = jnp.bfloat16

D_MODEL = 2048
DEPTH = 2
GROUP = 512
N_SPLITS = 10
IN_WIDTH = N_SPLITS * GROUP
POOL_WINDOWS = (2, 4, 8, 16)
POOL_BUF = 15
HEADS = 4
HEAD_DIM = 128
CHUNK = 128
ROPE_BASE = 10000.0
D_FF = 5632
NORM_EPS = 1e-6
PAST_LEN = 16384

_A, _BG, _CG, _HH, _Q, _K, _V, _G, _U, _VV = (i * GROUP for i in range(N_SPLITS))
_YA, _YB, _YC, _YD = (i * GROUP for i in range(4))

MIB = 1024 * 1024
VMEM_BYTES_V7X = 64 * MIB


VMEM_LIMIT_MAX = VMEM_BYTES_V7X - 4 * MIB


def _vmem_limit(estimate):
    return min(estimate + 4 * MIB, VMEM_LIMIT_MAX)


def _rms(x, g):
    ms = jnp.mean(x * x, axis=-1, keepdims=True)
    return x * lax.rsqrt(ms + NORM_EPS) * g


def _layernorm(x):
    mu = jnp.mean(x, axis=-1, keepdims=True)
    xc = x - mu
    return xc * lax.rsqrt(jnp.mean(xc * xc, axis=-1, keepdims=True) + NORM_EPS)


def _silu(x):
    return x * jax.nn.sigmoid(x)


def _dot(a, b):
    return jnp.dot(a, b, preferred_element_type=F32)


def _rotate(x, cos, sin_signed):
    return x * cos + pltpu.roll(x, HEAD_DIM // 2, 1) * sin_signed


def _weight_specs(w, layer, k, tn):
    if w.ndim == 3:
        return pl.BlockSpec((None, k, tn), lambda j: (layer, 0, j)), True
    return pl.BlockSpec((k, tn), lambda j: (0, j)), False


def _norm_matmul_kernel(x_ref, g_ref, w_ref, o_ref, *rest, emit):
    xn_ref = rest[-1]

    @pl.when(pl.program_id(0) == 0)
    def _():
        xn_ref[...] = _rms(x_ref[...], g_ref[...]).astype(BF16)

    wb = w_ref[...].astype(BF16)
    if emit:
        rest[0][...] = wb
    o_ref[...] = _dot(xn_ref[...], wb)


def norm_matmul(x, g, w, layer, *, tn):
    m, k = x.shape
    n = w.shape[-1]
    w_spec, emit = _weight_specs(w, layer, k, tn)
    vmem = 2 * k * tn * (4 + 2) + 2 * m * (k + tn) * 4 + m * k * 2
    outs = pl.pallas_call(
        functools.partial(_norm_matmul_kernel, emit=emit),
        grid=(n // tn,),
        in_specs=[
            pl.BlockSpec((m, k), lambda j: (0, 0)),
            pl.BlockSpec((None, 1, k), lambda j: (layer, 0, 0)),
            w_spec,
        ],
        out_specs=[pl.BlockSpec((m, tn), lambda j: (0, j))] + [pl.BlockSpec((k, tn), lambda j: (0, j))] * emit,
        out_shape=[jax.ShapeDtypeStruct((m, n), F32)] + [jax.ShapeDtypeStruct((k, n), BF16)] * emit,
        scratch_shapes=[pltpu.VMEM((m, k), BF16)],
        compiler_params=pltpu.CompilerParams(
            dimension_semantics=("arbitrary",),
            vmem_limit_bytes=_vmem_limit(vmem)),
        name="norm_matmul",
    )(x, g, w)
    return tuple(outs) if emit else (outs[0], w)


def _matmul_residual_kernel(a_ref, w_ref, r_ref, o_ref, *rest, emit):
    wb = w_ref[...].astype(BF16)
    if emit:
        rest[0][...] = wb
    o_ref[...] = r_ref[...] + _dot(a_ref[...], wb)


def matmul_residual(a, w, res, layer, *, tn):
    m, k = a.shape
    n = w.shape[-1]
    w_spec, emit = _weight_specs(w, layer, k, tn)
    vmem = 2 * k * tn * (4 + 2) + 2 * m * (k * 2 + 2 * tn * 4)
    outs = pl.pallas_call(
        functools.partial(_matmul_residual_kernel, emit=emit),
        grid=(n // tn,),
        in_specs=[
            pl.BlockSpec((m, k), lambda j: (0, 0)),
            w_spec,
            pl.BlockSpec((m, tn), lambda j: (0, j)),
        ],
        out_specs=[pl.BlockSpec((m, tn), lambda j: (0, j))] + [pl.BlockSpec((k, tn), lambda j: (0, j))] * emit,
        out_shape=[jax.ShapeDtypeStruct((m, n), F32)] + [jax.ShapeDtypeStruct((k, n), BF16)] * emit,
        compiler_params=pltpu.CompilerParams(
            dimension_semantics=("arbitrary",),
            vmem_limit_bytes=_vmem_limit(vmem)),
        name="matmul_residual",
    )(a, w, res)
    return tuple(outs) if emit else (outs[0], w)


_MLP_TILE = 1024


def _mlp_start(first, x_refs, g_ref, o_ref, hn_ref):
    @pl.when(first)
    def _():
        row = 0
        for x_ref in x_refs:
            x = x_ref[...]
            rows = slice(row, row + x.shape[0])
            hn_ref[rows, :] = _rms(x, g_ref[...]).astype(BF16)
            o_ref[rows, :] = x
            row += x.shape[0]


def _mlp_accumulate(wg, wu, wd, o_ref, hn_ref):
    hn = hn_ref[...]
    act = (_silu(_dot(hn, wg)) * _dot(hn, wu)).astype(BF16)
    o_ref[...] += _dot(act, wd)


def _mlp_finish(last, fg_ref, o_ref, final_norm):
    if final_norm:
        @pl.when(last)
        def _():
            o_ref[...] = _rms(o_ref[...], fg_ref[...])


def _mlp_cast_kernel(xs_ref, xp_ref, g_ref, wg_ref, wu_ref, wd_ref, fg_ref,
                     o_ref, wgb_ref, wub_ref, wdb_ref, hn_ref, *, final_norm):
    f = pl.program_id(0)
    _mlp_start(f == 0, (xs_ref, xp_ref), g_ref, o_ref, hn_ref)
    wg, wu, wd = wg_ref[...].astype(BF16), wu_ref[...].astype(BF16), wd_ref[...].astype(BF16)
    wgb_ref[...] = wg
    wub_ref[...] = wu
    wdb_ref[...] = wd
    _mlp_accumulate(wg, wu, wd, o_ref, hn_ref)
    _mlp_finish(f == pl.num_programs(0) - 1, fg_ref, o_ref, final_norm)


def _mlp_rest_kernel(first_hbm, x_ref, g_ref, wg_ref, wu_ref, wd_ref, fg_ref, o_ref, hn_ref, sem,
                     *, nf, skip_rows, final_norm):
    s = pl.program_id(0)
    tm = o_ref.shape[0]

    @pl.when(s == 0)
    def _():
        copy = pltpu.make_async_copy(first_hbm.at[pl.ds(skip_rows, tm), :], o_ref, sem)
        copy.start()
        copy.wait()

    @pl.when(s > 0)
    def _():
        f = (s - 1) % nf
        _mlp_start(f == 0, (x_ref,), g_ref, o_ref, hn_ref)
        _mlp_accumulate(wg_ref[...], wu_ref[...], wd_ref[...], o_ref, hn_ref)
        _mlp_finish(f == nf - 1, fg_ref, o_ref, final_norm)


def mlp_cast(xs, xp, g, w_gate_up, w_down, final_g, layer, *, tp, tf, final_norm):
    ns, d = xs.shape
    m = ns + tp
    nf = D_FF // tf
    once = pl.Buffered(1)
    vmem = 2 * m * d * 4 + m * d * 2 + 2 * 3 * d * tf * (4 + 2) + 3 * m * tf * 4
    outs = pl.pallas_call(
        functools.partial(_mlp_cast_kernel, final_norm=final_norm),
        grid=(nf,),
        in_specs=[
            pl.BlockSpec((ns, d), lambda f: (0, 0), pipeline_mode=once),
            pl.BlockSpec((tp, d), lambda f: (0, 0), pipeline_mode=once),
            pl.BlockSpec((None, 1, d), lambda f: (layer, 0, 0)),
            pl.BlockSpec((None, d, tf), lambda f: (layer, 0, f)),
            pl.BlockSpec((None, d, tf), lambda f: (layer, 0, nf + f)),
            pl.BlockSpec((None, tf, d), lambda f: (layer, f, 0)),
            pl.BlockSpec((1, d), lambda f: (0, 0)),
        ],
        out_specs=[
            pl.BlockSpec((m, d), lambda f: (0, 0), pipeline_mode=once),
            pl.BlockSpec((d, tf), lambda f: (0, f)),
            pl.BlockSpec((d, tf), lambda f: (0, f)),
            pl.BlockSpec((tf, d), lambda f: (f, 0)),
        ],
        out_shape=[jax.ShapeDtypeStruct((m, d), F32),
                   jax.ShapeDtypeStruct((d, D_FF), BF16), jax.ShapeDtypeStruct((d, D_FF), BF16),
                   jax.ShapeDtypeStruct((D_FF, d), BF16)],
        scratch_shapes=[pltpu.VMEM((m, d), BF16)],
        compiler_params=pltpu.CompilerParams(
            dimension_semantics=("arbitrary",),
            vmem_limit_bytes=_vmem_limit(vmem)),
        name="mlp_cast",
    )(xs, xp, g, w_gate_up, w_gate_up, w_down, final_g)
    return outs[0], tuple(outs[1:])


def mlp_rest(first, x, g, weights, final_g, layer, *, skip_rows, tm, tf, final_norm):
    m, d = x.shape
    nf = D_FF // tf
    step = lambda s: jnp.maximum(s - 1, 0)
    x_tile = lambda s: 1 + step(s) // nf
    f_tile = lambda s: step(s) % nf
    vmem = 2 * 2 * tm * d * 4 + tm * d * 2 + 2 * 3 * d * tf * 2 + 3 * tm * tf * 4
    return pl.pallas_call(
        functools.partial(_mlp_rest_kernel, nf=nf, skip_rows=skip_rows, final_norm=final_norm),
        grid=(1 + (m // tm - 1) * nf,),
        in_specs=[
            pl.BlockSpec(memory_space=pl.ANY),
            pl.BlockSpec((tm, d), lambda s: (x_tile(s), 0)),
            pl.BlockSpec((None, 1, d), lambda s: (layer, 0, 0)),
            pl.BlockSpec((d, tf), lambda s: (0, f_tile(s))),
            pl.BlockSpec((d, tf), lambda s: (0, f_tile(s))),
            pl.BlockSpec((tf, d), lambda s: (f_tile(s), 0)),
            pl.BlockSpec((1, d), lambda s: (0, 0)),
        ],
        out_specs=pl.BlockSpec((tm, d), lambda s: (jnp.where(s == 0, 0, x_tile(s)), 0)),
        out_shape=jax.ShapeDtypeStruct((m, d), F32),
        scratch_shapes=[pltpu.VMEM((tm, d), BF16), pltpu.SemaphoreType.DMA(())],
        compiler_params=pltpu.CompilerParams(
            dimension_semantics=("arbitrary",),
            vmem_limit_bytes=_vmem_limit(vmem)),
        name="mlp_rest",
    )(first, x, g, *weights, final_g)


_POOL_TOP = 16
_CONV_TOP = 8


def _block_prompt_kernel(x_ref, g1_ref, win_ref, wout_ref, cos_ref, sin_ref, dmask_ref, qdec_ref, kdec_ref,
                         poolw_ref, pscale_ref, convw_ref, retg_ref, sgug_ref, sguw_ref, sgub_ref,
                         *rest, start, tm, cast_next):
    if cast_next:
        win_next_ref, wout_next_ref, *rest = rest
        *rest, win_next_out, wout_next_out, pool_ext, conv_ext = rest
        win_next_out[...] = win_next_ref[...].astype(BF16)
        wout_next_out[...] = wout_next_ref[...].astype(BF16)
    else:
        *rest, pool_ext, conv_ext = rest
    h_out, pool_out, conv_out, ret_out, sgu_out = rest
    c = pl.program_id(1)
    T = CHUNK
    n_chunks = tm // T

    @pl.when(c == 0)
    def _():
        pool_ext[0:_POOL_TOP, :] = jnp.zeros((_POOL_TOP, GROUP), F32)
        conv_ext[0:_CONV_TOP, :] = jnp.zeros((_CONV_TOP, GROUP), F32)
        ret_out[...] = jnp.zeros(ret_out.shape, F32)

    x = x_ref[0]
    xn = _rms(x, g1_ref[...]).astype(BF16)

    def proj(col):
        return _dot(xn, win_ref[:, col:col + GROUP])

    def out_proj(y, col):
        return _dot(y, wout_ref[col:col + GROUP, :])

    head_lanes = [slice(hd * HEAD_DIM, (hd + 1) * HEAD_DIM) for hd in range(HEADS)]
    chunk_rows = [slice(j * T, (j + 1) * T) for j in range(n_chunks)]

    def pool_mixer(a_proj):
        pool_ext[_POOL_TOP:_POOL_TOP + tm, :] = a_proj
        pos1 = lax.broadcasted_iota(jnp.int32, (tm, HEAD_DIM), 0) + (start + 1) + c * tm
        y = []
        for gi, win in enumerate(POOL_WINDOWS):
            lanes = head_lanes[gi]
            a = pool_ext[_POOL_TOP:_POOL_TOP + tm, lanes]
            wsum = a
            for j in range(1, win):
                wsum = wsum + pool_ext[_POOL_TOP - j:_POOL_TOP - j + tm, lanes]
            cnt = jnp.minimum(pos1, win).astype(F32)
            d = (wsum / cnt - a).astype(BF16)
            y.append((_dot(d, poolw_ref[gi]) * pscale_ref[:, lanes]).astype(BF16))
        pool_out[0] = pool_ext[_POOL_TOP + tm - POOL_BUF:_POOL_TOP + tm, :]
        pool_ext[_POOL_TOP - POOL_BUF:_POOL_TOP, :] = pool_ext[_POOL_TOP + tm - POOL_BUF:_POOL_TOP + tm, :]
        return jnp.concatenate(y, axis=-1)

    def conv_mixer(bg, cg, hh):
        z = cg * hh
        conv_ext[_CONV_TOP:_CONV_TOP + tm, :] = z
        acc = (conv_ext[_CONV_TOP - 2:_CONV_TOP - 2 + tm, :] * convw_ref[0:1, :]
               + conv_ext[_CONV_TOP - 1:_CONV_TOP - 1 + tm, :] * convw_ref[1:2, :]
               + z * convw_ref[2:3, :])
        conv_out[0] = conv_ext[_CONV_TOP + tm - 2:_CONV_TOP + tm, :]
        conv_ext[_CONV_TOP - 2:_CONV_TOP, :] = conv_ext[_CONV_TOP + tm - 2:_CONV_TOP + tm, :]
        return (bg * acc).astype(BF16)

    def retention(q_all, k_all, v_all, g_all, j, hd):
        rows, lanes = chunk_rows[j], head_lanes[hd]
        cos = cos_ref[rows, :]
        sin = sin_ref[rows, :]
        q = _rotate(q_all[rows, lanes], cos, sin)
        k = _rotate(k_all[rows, lanes], cos, sin) * (HEAD_DIM ** -0.5)
        v = v_all[rows, lanes].astype(BF16)
        s_prev = ret_out[0, hd]
        scores = lax.dot_general(q.astype(BF16), k.astype(BF16), (((1,), (1,)), ((), ())),
                                 preferred_element_type=F32) * dmask_ref[hd]
        o = _dot(scores.astype(BF16), v) + _dot((q * qdec_ref[hd]).astype(BF16), s_prev.astype(BF16))
        kv = lax.dot_general((k * kdec_ref[hd]).astype(BF16), v, (((0,), (0,)), ((), ())),
                             preferred_element_type=F32)
        ret_out[0, hd] = qdec_ref[hd, T - 1:T, :] * s_prev + kv
        on = _layernorm(o) * retg_ref[:, lanes]
        return (_silu(g_all[rows, lanes]) * on).astype(BF16)

    def sgu_mixer(u_all, vn, j):
        rows = chunk_rows[j]
        tri = lax.broadcasted_iota(jnp.int32, (T, T), 0) >= lax.broadcasted_iota(jnp.int32, (T, T), 1)
        y = []
        for hd in range(HEADS):
            lanes = head_lanes[hd]
            w = jnp.where(tri, sguw_ref[hd], 0.0).astype(BF16)
            mixed = _dot(w, vn[rows, lanes].astype(BF16)) + sgub_ref[hd]
            y.append((u_all[rows, lanes] * mixed).astype(BF16))
        return jnp.concatenate(y, axis=-1)

    q_all, k_all, v_all, g_all = proj(_Q), proj(_K), proj(_V), proj(_G)
    ret = functools.partial(retention, q_all, k_all, v_all, g_all)
    y_c = [[None] * HEADS for _ in range(n_chunks)]
    y_c[0][0] = ret(0, 0)
    a_proj = proj(_A)
    y_c[0][1] = ret(0, 1)
    cg = proj(_CG)
    y_c[0][2] = ret(0, 2)
    hh = proj(_HH)
    y_c[0][3] = ret(0, 3)
    bg = proj(_BG)
    later = iter([lambda: proj(_VV), lambda: proj(_U), lambda: pool_mixer(a_proj),
                  lambda: conv_mixer(bg, cg, hh)])
    fills = []
    for j in range(1, n_chunks):
        for hd in range(HEADS):
            y_c[j][hd] = ret(j, hd)
            nxt = next(later, None)
            if nxt is not None:
                fills.append(nxt())
    fills += [f() for f in later]
    vv, u_all, y_a, y_b = fills
    h = x + out_proj(y_a, _YA)
    vn = _layernorm(vv) * sgug_ref[...]
    y_d = [sgu_mixer(u_all, vn, 0)]
    h = h + out_proj(y_b, _YB)
    y_d += [sgu_mixer(u_all, vn, j) for j in range(1, n_chunks)]
    h = h + out_proj(jnp.concatenate([jnp.concatenate(r, axis=-1) for r in y_c], axis=0), _YC)
    h_out[0] = h + out_proj(jnp.concatenate(y_d, axis=0), _YD)
    sgu_out[0] = vn[tm - T:tm, :]


def block_prompt(x, norm_g, w_in, w_out, tables, params, layer, next_weights, *, start, tm):
    b, l, d = x.shape
    nc = l // tm
    cos, sin, dmask, qdec, kdec = tables
    poolw, pscale, convw, retg, sgug, sguw, sgub = params
    const3 = lambda shape: pl.BlockSpec(shape, lambda i, c: (0, 0, 0))
    lay3 = lambda shape: pl.BlockSpec((None,) + shape, lambda i, c: (layer, 0, 0))
    lay4 = lambda shape: pl.BlockSpec((None,) + shape, lambda i, c: (layer, 0, 0, 0))
    resident = lambda shape: pl.BlockSpec(shape, lambda i, c: (0, 0), pipeline_mode=pl.Buffered(1))
    cast_next = next_weights is not None
    next_in_specs, next_out_specs, next_out_shape, next_args = [], [], [], ()
    next_bytes = 0
    if cast_next:
        for w in next_weights:
            rows, cols = w.shape[1] // (b * nc), w.shape[2]
            assert rows * b * nc == w.shape[1] and rows % 16 == 0
            next_in_specs.append(pl.BlockSpec((None, rows, cols), lambda i, c: (layer + 1, i * nc + c, 0)))
            next_out_specs.append(pl.BlockSpec((rows, cols), lambda i, c: (i * nc + c, 0)))
            next_out_shape.append(jax.ShapeDtypeStruct(w.shape[1:], BF16))
            next_bytes += 2 * rows * cols * (4 + 2)
        next_args = tuple(next_weights)
    vmem = (d * IN_WIDTH + 4 * GROUP * d) * 2 + 2 * 2 * tm * d * 4 + 2 * tm * IN_WIDTH * 4 + next_bytes
    outs = pl.pallas_call(
        functools.partial(_block_prompt_kernel, start=start, tm=tm, cast_next=cast_next),
        grid=(b, nc),
        in_specs=[
            pl.BlockSpec((1, tm, d), lambda i, c: (i, c, 0)),
            lay3((1, d)),
            resident((d, IN_WIDTH)),
            resident((4 * GROUP, d)),
            pl.BlockSpec((tm, HEAD_DIM), lambda i, c: (c, 0)),
            pl.BlockSpec((tm, HEAD_DIM), lambda i, c: (c, 0)),
            const3((HEADS, CHUNK, CHUNK)),
            const3((HEADS, CHUNK, HEAD_DIM)),
            const3((HEADS, CHUNK, HEAD_DIM)),
            lay4((len(POOL_WINDOWS), HEAD_DIM, HEAD_DIM)),
            lay3((1, GROUP)),
            lay3((3, GROUP)),
            lay3((1, GROUP)),
            lay3((1, GROUP)),
            lay4((HEADS, CHUNK, CHUNK)),
            lay4((HEADS, CHUNK, HEAD_DIM)),
            *next_in_specs,
        ],
        out_specs=[
            pl.BlockSpec((1, tm, d), lambda i, c: (i, c, 0)),
            pl.BlockSpec((1, POOL_BUF, GROUP), lambda i, c: (i, 0, 0)),
            pl.BlockSpec((1, 2, GROUP), lambda i, c: (i, 0, 0)),
            pl.BlockSpec((1, HEADS, HEAD_DIM, HEAD_DIM), lambda i, c: (i, 0, 0, 0)),
            pl.BlockSpec((1, CHUNK, GROUP), lambda i, c: (i, 0, 0)),
            *next_out_specs,
        ],
        out_shape=[
            jax.ShapeDtypeStruct((b, l, d), F32),
            jax.ShapeDtypeStruct((b, POOL_BUF, GROUP), F32),
            jax.ShapeDtypeStruct((b, 2, GROUP), F32),
            jax.ShapeDtypeStruct((b, HEADS, HEAD_DIM, HEAD_DIM), F32),
            jax.ShapeDtypeStruct((b, CHUNK, GROUP), F32),
            *next_out_shape,
        ],
        scratch_shapes=[
            pltpu.VMEM((_POOL_TOP + tm, GROUP), F32),
            pltpu.VMEM((_CONV_TOP + tm, GROUP), F32),
        ],
        compiler_params=pltpu.CompilerParams(
            dimension_semantics=("arbitrary", "arbitrary"),
            vmem_limit_bytes=_vmem_limit(vmem)),
        name="block_prompt",
    )(x, norm_g, w_in, w_out, cos, sin, dmask, qdec, kdec, poolw, pscale, convw, retg, sgug, sguw, sgub,
      *next_args)
    return outs[0], tuple(outs[1:5]), (tuple(outs[5:]) if cast_next else None)


_SEQ_BLOCK = 16


_N_SAMPLE_INPUTS = 14


def _mixer_sample_kernel(*refs, start, layer, all_layers):
    (p_ref, pool_ref, conv_ref, ret_ref, cos_ref, sin_ref, gam_ref,
     poolw_ref, pscale_ref, convw_ref, retg_ref, sgug_ref, sguw_ref, sgub_ref) = refs[:_N_SAMPLE_INPUTS]
    mix_ref, pool_out, conv_out, ret_out, sgu_out, qg_s, k_s, v_s, os_s = refs[-9:]
    TB = _SEQ_BLOCK

    if all_layers:
        stacked = (pool_out, conv_out, ret_out, sgu_out)
        for ref in stacked:
            for other in range(DEPTH):
                if other != layer:
                    ref[other] = jnp.zeros(ref.shape[1:], F32)
        pool_out, conv_out, ret_out, sgu_out = (ref.at[layer] for ref in stacked)

    a_all = p_ref[:, _A:_A + GROUP]
    for gi, win in enumerate(POOL_WINDOWS):
        lanes = slice(gi * HEAD_DIM, (gi + 1) * HEAD_DIM)
        a = a_all[:, lanes]
        wsum = a
        for j in range(1, win):
            wsum = wsum + pool_ref[POOL_BUF - j, :, lanes]
        cnt = float(min(start + 1, win))
        d = (wsum / cnt - a).astype(BF16)
        y = _dot(d, poolw_ref[gi]) * pscale_ref[:, lanes]
        mix_ref[:, _YA + gi * HEAD_DIM:_YA + (gi + 1) * HEAD_DIM] = y.astype(BF16)
    pool_out[0:POOL_BUF - 1] = pool_ref[1:POOL_BUF]
    pool_out[POOL_BUF - 1] = a_all

    z = p_ref[:, _CG:_CG + GROUP] * p_ref[:, _HH:_HH + GROUP]
    acc = (conv_ref[:, 0, :] * convw_ref[0:1, :] + conv_ref[:, 1, :] * convw_ref[1:2, :]
           + z * convw_ref[2:3, :])
    mix_ref[:, _YB:_YB + GROUP] = (p_ref[:, _BG:_BG + GROUP] * acc).astype(BF16)
    conv_out[:, 0, :] = conv_ref[:, 1, :]
    conv_out[:, 1, :] = z

    cos = cos_ref[...]
    sin = sin_ref[...]
    scores = []
    for h in range(HEADS):
        lanes = slice(h * HEAD_DIM, (h + 1) * HEAD_DIM)
        q = _rotate(p_ref[:, _Q + h * HEAD_DIM:_Q + (h + 1) * HEAD_DIM], cos, sin)
        k = _rotate(p_ref[:, _K + h * HEAD_DIM:_K + (h + 1) * HEAD_DIM], cos, sin) * (HEAD_DIM ** -0.5)
        scores.append(jnp.sum(q * k, axis=-1, keepdims=True))
        qg_s[:, lanes] = q * gam_ref[:, lanes]
        k_s[:, lanes] = k
    v_s[...] = p_ref[:, _V:_V + GROUP]

    first_row = lax.broadcasted_iota(jnp.int32, (8, HEAD_DIM), 0) == 0

    for b in range(TB):
        for h in range(HEADS):
            lanes = slice(h * HEAD_DIM, (h + 1) * HEAD_DIM)
            s_prev = ret_ref[b, h]
            q8 = jnp.broadcast_to(qg_s[b:b + 1, lanes], (8, HEAD_DIM))
            os_s[b:b + 1, lanes] = _dot(q8.astype(BF16), s_prev.astype(BF16))[0:1, :]
            k8 = jnp.where(first_row, jnp.broadcast_to(k_s[b:b + 1, lanes], (8, HEAD_DIM)), 0.0)
            v8 = jnp.broadcast_to(v_s[b:b + 1, lanes], (8, HEAD_DIM))
            kv = lax.dot_general(k8.astype(BF16), v8.astype(BF16), (((0,), (0,)), ((), ())),
                                 preferred_element_type=F32)
            ret_out[b, h] = gam_ref[:, lanes] * s_prev + kv

    for h in range(HEADS):
        lanes = slice(h * HEAD_DIM, (h + 1) * HEAD_DIM)
        o = scores[h] * v_s[:, lanes] + os_s[:, lanes]
        on = _layernorm(o) * retg_ref[:, lanes]
        g = p_ref[:, _G + h * HEAD_DIM:_G + (h + 1) * HEAD_DIM]
        mix_ref[:, _YC + h * HEAD_DIM:_YC + (h + 1) * HEAD_DIM] = (_silu(g) * on).astype(BF16)

    vn = _layernorm(p_ref[:, _VV:_VV + GROUP]) * sgug_ref[...]
    mix_ref[:, _YD:_YD + GROUP] = (p_ref[:, _U:_U + GROUP] * (sguw_ref[...] * vn + sgub_ref[...])).astype(BF16)
    sgu_out[:, 0, :] = vn


def mixer_sample(p, pool_state, conv_state, ret_state, tables, params, layer, stacked, *, start):
    n = p.shape[0]
    tb = _SEQ_BLOCK
    cos, sin, gam = tables
    poolw, pscale, convw, retg, sgug, sguw, sgub = params
    rows = lambda width: pl.BlockSpec((tb, width), lambda i: (i, 0))
    lay_rows = lambda r: pl.BlockSpec((None, tb, r, GROUP), lambda i: (layer, i, 0, 0))
    lay_pool = pl.BlockSpec((None, POOL_BUF, tb, GROUP), lambda i: (layer, 0, i, 0))
    lay_ret = pl.BlockSpec((None, tb, HEADS, HEAD_DIM, HEAD_DIM), lambda i: (layer, i, 0, 0, 0))
    const2 = lambda shape: pl.BlockSpec(shape, lambda i: (0, 0))
    lay3 = lambda shape: pl.BlockSpec((None,) + shape, lambda i: (layer, 0, 0))
    lay4 = lambda shape: pl.BlockSpec((None,) + shape, lambda i: (layer, 0, 0, 0))
    in_specs = [
        rows(IN_WIDTH),
        lay_pool,
        lay_rows(2),
        lay_ret,
        const2((1, HEAD_DIM)),
        const2((1, HEAD_DIM)),
        const2((1, GROUP)),
        lay4((len(POOL_WINDOWS), HEAD_DIM, HEAD_DIM)),
        lay3((1, GROUP)),
        lay3((3, GROUP)),
        lay3((1, GROUP)),
        lay3((1, GROUP)),
        lay3((1, GROUP)),
        lay3((1, GROUP)),
    ]
    args = (p, pool_state, conv_state, ret_state, cos, sin, gam,
            poolw, pscale, convw, retg, sgug, sguw, sgub)
    assert len(args) == _N_SAMPLE_INPUTS
    aliases = {}
    first = stacked is None
    state_block = tb * (16 + 8 + 8 + HEAD_DIM) * GROUP * 4
    vmem = 2 * (tb * IN_WIDTH * 4 + state_block * (1 + (DEPTH if first else 1)))
    if first:
        all_rows = lambda r: pl.BlockSpec((DEPTH, tb, r, GROUP), lambda i: (0, i, 0, 0))
        state_specs = [pl.BlockSpec((DEPTH, POOL_BUF, tb, GROUP), lambda i: (0, 0, i, 0)), all_rows(2),
                       pl.BlockSpec((DEPTH, tb, HEADS, HEAD_DIM, HEAD_DIM), lambda i: (0, i, 0, 0, 0)),
                       all_rows(1)]
    else:
        state_specs = [lay_pool, lay_rows(2), lay_ret, lay_rows(1)]
        aliases = {len(args) + k: 1 + k for k in range(len(stacked))}
        in_specs = in_specs + [pl.BlockSpec(memory_space=pl.ANY)] * len(stacked)
        args = args + tuple(stacked)
    outs = pl.pallas_call(
        functools.partial(_mixer_sample_kernel, start=start, layer=layer, all_layers=first),
        grid=(n // tb,),
        in_specs=in_specs,
        out_specs=[rows(4 * GROUP)] + state_specs,
        out_shape=[
            jax.ShapeDtypeStruct((n, 4 * GROUP), BF16),
            jax.ShapeDtypeStruct((DEPTH, POOL_BUF, n, GROUP), F32),
            jax.ShapeDtypeStruct((DEPTH, n, 2, GROUP), F32),
            jax.ShapeDtypeStruct((DEPTH, n, HEADS, HEAD_DIM, HEAD_DIM), F32),
            jax.ShapeDtypeStruct((DEPTH, n, 1, GROUP), F32),
        ],
        scratch_shapes=[pltpu.VMEM((tb, GROUP), F32)] * 4,
        input_output_aliases=aliases,
        compiler_params=pltpu.CompilerParams(
            dimension_semantics=("arbitrary",),
            vmem_limit_bytes=_vmem_limit(vmem)),
        name="mixer_sample",
    )(*args)
    return outs[0], tuple(outs[1:])


def _rope_tables(start, length):
    half = HEAD_DIM // 2
    inv = ROPE_BASE ** (-jnp.arange(half, dtype=F32) / half)
    pos = start + jnp.arange(length)
    ang = pos.astype(F32)[:, None] * inv[None, :]
    cos, sin = jnp.cos(ang), jnp.sin(ang)
    return jnp.concatenate([cos, cos], axis=-1), jnp.concatenate([-sin, sin], axis=-1)


def _log_gamma():
    return jnp.log1p(-(2.0 ** (-5.0 - jnp.arange(HEADS, dtype=F32))))


def _decay_tables(c):
    lg = _log_gamma()
    idx = jnp.arange(c, dtype=F32)
    diff = idx[:, None] - idx[None, :]
    dmask = jnp.where(diff >= 0, jnp.exp(jnp.maximum(diff, 0.0)[None] * lg[:, None, None]), 0.0)
    qdec = jnp.exp((idx + 1.0)[None, :] * lg[:, None])[:, :, None]
    kdec = jnp.exp((c - 1.0 - idx)[None, :] * lg[:, None])[:, :, None]
    wide = lambda t: jnp.broadcast_to(t, (HEADS, c, HEAD_DIM))
    return dmask, wide(qdec), wide(kdec)


def _per_head_lanes(t):
    return jnp.repeat(t, HEAD_DIM, axis=-1)[..., None, :]


def kernel(x_prompt, x_sample, state_pool, state_conv, state_ret, norm1_g, w_in, pool_w, pool_scale,
           conv_w, ret_norm_g, sgu_norm_g, sgu_w, sgu_b, w_out, norm2_g, w_gate_up, w_down,
           final_norm_g):
    bp, lp, _ = x_prompt.shape
    ns = x_sample.shape[0]

    pool_w_b = pool_w.astype(BF16)
    row3 = lambda t: t[:, None, :]
    norm1 = row3(norm1_g)
    norm2 = row3(norm2_g)
    final_g = final_norm_g[None, :]

    cos_p, sin_p = _rope_tables(0, lp)
    prompt_tables = (cos_p, sin_p) + _decay_tables(CHUNK)
    sgu_b_wide = jnp.broadcast_to(sgu_b[:, :, :, None], sgu_b.shape + (HEAD_DIM,))
    prompt_params = (pool_w_b, row3(pool_scale), conv_w, row3(ret_norm_g), row3(sgu_norm_g), sgu_w, sgu_b_wide)

    cos_s, sin_s = _rope_tables(PAST_LEN, 1)
    gam = _per_head_lanes(jnp.exp(_log_gamma()))
    sample_tables = (cos_s, sin_s, gam)
    sample_params = (pool_w_b, row3(pool_scale), conv_w, row3(ret_norm_g), row3(sgu_norm_g),
                     _per_head_lanes(sgu_w[:, :, 0, 0]), _per_head_lanes(sgu_b[:, :, 0]))
    pool_rows = jnp.transpose(state_pool, (0, 2, 1, 3))
    hp = x_prompt
    hs = x_sample.reshape(ns, D_MODEL)
    prompt_states = [[] for _ in range(4)]
    sample_states = None
    next_b = None
    for l in range(DEPTH):
        fin = l == DEPTH - 1
        w_in_l, w_out_l = (w_in, w_out) if next_b is None else next_b
        p, w_in_b = norm_matmul(hs, norm1, w_in_l, l, tn=512)
        mix, sample_states = mixer_sample(p, pool_rows, state_conv, state_ret, sample_tables,
                                          sample_params, l, sample_states, start=PAST_LEN)
        hs, w_out_b = matmul_residual(mix, w_out_l, hs, l, tn=1024)
        hp, states, next_b = block_prompt(hp, norm1, w_in_b, w_out_b, prompt_tables, prompt_params, l,
                                          None if fin else (w_in, w_out), start=0, tm=256)
        for lst, val in zip(prompt_states, states):
            lst.append(val)
        hp = hp.reshape(bp * lp, D_MODEL)
        first, w_mlp_b = mlp_cast(hs, hp, norm2, w_gate_up, w_down, final_g, l, tp=_MLP_TILE, tf=256,
                                  final_norm=fin)
        hs = first[:ns]
        hp = mlp_rest(first, hp, norm2, w_mlp_b, final_g, l, skip_rows=ns, tm=_MLP_TILE, tf=512,
                      final_norm=fin).reshape(bp, lp, D_MODEL)
    pool_p, conv_p, ret_p, sgu_p = (jnp.stack(s) for s in prompt_states)
    pool_s, conv_s, ret_s, sgu_s = sample_states
    pool_s = jnp.transpose(pool_s, (0, 2, 1, 3))
    return (hp, hs.reshape(ns, 1, D_MODEL), pool_p, pool_s, conv_p, conv_s, ret_p, ret_s, sgu_p, sgu_s)
```

```python
import functools

import jax
import jax.numpy as jnp
from jax import lax
from jax.experimental import pallas as pl
from jax.experimental.pallas import tpu as pltpu

F32 = jnp.float32
BF16 = jnp.bfloat16

D_MODEL = 2048
DEPTH = 2
GROUP = 512
N_SPLITS = 10
IN_WIDTH = N_SPLITS * GROUP
POOL_WINDOWS = (2, 4, 8, 16)
POOL_BUF = 15
HEADS = 4
HEAD_DIM = 128
CHUNK = 128
ROPE_BASE = 10000.0
D_FF = 5632
NORM_EPS = 1e-6
PAST_LEN = 16384

_A, _BG, _CG, _HH, _Q, _K, _V, _G, _U, _VV = (i * GROUP for i in range(N_SPLITS))
_YA, _YB, _YC, _YD = (i * GROUP for i in range(4))

MIB = 1024 * 1024
VMEM_BYTES_V7X = 64 * MIB


VMEM_LIMIT_MAX = VMEM_BYTES_V7X - 4 * MIB


def _vmem_limit(estimate):
    return min(estimate + 4 * MIB, VMEM_LIMIT_MAX)


def _rms(x, g):
    ms = jnp.mean(x * x, axis=-1, keepdims=True)
    return x * lax.rsqrt(ms + NORM_EPS) * g


def _layernorm(x):
    mu = jnp.mean(x, axis=-1, keepdims=True)
    xc = x - mu
    return xc * lax.rsqrt(jnp.mean(xc * xc, axis=-1, keepdims=True) + NORM_EPS)


def _silu(x):
    return x * jax.nn.sigmoid(x)


def _dot(a, b):
    return jnp.dot(a, b, preferred_element_type=F32)


def _rotate(x, cos, sin_signed):
    return x * cos + pltpu.roll(x, HEAD_DIM // 2, 1) * sin_signed


def _weight_specs(w, layer, k, tn):
    if w.ndim == 3:
        return pl.BlockSpec((None, k, tn), lambda j: (layer, 0, j)), True
    return pl.BlockSpec((k, tn), lambda j: (0, j)), False


def _in_hbm(w):
    return pltpu.with_memory_space_constraint(w, pltpu.HBM)


def _norm_matmul_kernel(x_ref, g_ref, w_ref, o_ref, *rest, emit):
    xn_ref = rest[-1]

    @pl.when(pl.program_id(0) == 0)
    def _():
        xn_ref[...] = _rms(x_ref[...], g_ref[...]).astype(BF16)

    wb = w_ref[...].astype(BF16)
    if emit:
        rest[0][...] = wb
    o_ref[...] = _dot(xn_ref[...], wb)


def norm_matmul(x, g, w, layer, *, tn):
    m, k = x.shape
    n = w.shape[-1]
    w_spec, emit = _weight_specs(w, layer, k, tn)
    vmem = 2 * k * tn * (4 + 2) + 2 * m * (k + tn) * 4 + m * k * 2
    outs = pl.pallas_call(
        functools.partial(_norm_matmul_kernel, emit=emit),
        grid=(n // tn,),
        in_specs=[
            pl.BlockSpec((m, k), lambda j: (0, 0)),
            pl.BlockSpec((None, 1, k), lambda j: (layer, 0, 0)),
            w_spec,
        ],
        out_specs=[pl.BlockSpec((m, tn), lambda j: (0, j))] + [pl.BlockSpec((k, tn), lambda j: (0, j))] * emit,
        out_shape=[jax.ShapeDtypeStruct((m, n), F32)] + [jax.ShapeDtypeStruct((k, n), BF16)] * emit,
        scratch_shapes=[pltpu.VMEM((m, k), BF16)],
        compiler_params=pltpu.CompilerParams(
            dimension_semantics=("arbitrary",),
            vmem_limit_bytes=_vmem_limit(vmem)),
        name="norm_matmul",
    )(x, g, w if emit else _in_hbm(w))
    return tuple(outs) if emit else (outs[0], w)


def _matmul_residual_kernel(a_ref, w_ref, r_ref, o_ref, *rest, emit):
    wb = w_ref[...].astype(BF16)
    if emit:
        rest[0][...] = wb
    o_ref[...] = r_ref[...] + _dot(a_ref[...], wb)


def matmul_residual(a, w, res, layer, *, tn):
    m, k = a.shape
    n = w.shape[-1]
    w_spec, emit = _weight_specs(w, layer, k, tn)
    vmem = 2 * k * tn * (4 + 2) + 2 * m * (k * 2 + 2 * tn * 4)
    outs = pl.pallas_call(
        functools.partial(_matmul_residual_kernel, emit=emit),
        grid=(n // tn,),
        in_specs=[
            pl.BlockSpec((m, k), lambda j: (0, 0)),
            w_spec,
            pl.BlockSpec((m, tn), lambda j: (0, j)),
        ],
        out_specs=[pl.BlockSpec((m, tn), lambda j: (0, j))] + [pl.BlockSpec((k, tn), lambda j: (0, j))] * emit,
        out_shape=[jax.ShapeDtypeStruct((m, n), F32)] + [jax.ShapeDtypeStruct((k, n), BF16)] * emit,
        compiler_params=pltpu.CompilerParams(
            dimension_semantics=("arbitrary",),
            vmem_limit_bytes=_vmem_limit(vmem)),
        name="matmul_residual",
    )(a, w if emit else _in_hbm(w), res)
    return tuple(outs) if emit else (outs[0], w)


_MLP_TILE = 1024


def _mlp_start(first, x_refs, g_ref, o_ref, hn_ref):
    @pl.when(first)
    def _():
        row = 0
        for x_ref in x_refs:
            x = x_ref[...]
            rows = slice(row, row + x.shape[0])
            hn_ref[rows, :] = _rms(x, g_ref[...]).astype(BF16)
            o_ref[rows, :] = x
            row += x.shape[0]


def _mlp_accumulate(wg, wu, wd, o_ref, hn_ref):
    hn = hn_ref[...]
    act = (_silu(_dot(hn, wg)) * _dot(hn, wu)).astype(BF16)
    o_ref[...] += _dot(act, wd)


def _mlp_finish(last, fg_ref, o_ref, final_norm):
    if final_norm:
        @pl.when(last)
        def _():
            o_ref[...] = _rms(o_ref[...], fg_ref[...])


def _mlp_cast_kernel(xs_ref, xp_ref, g_ref, wg_ref, wu_ref, wd_ref, fg_ref,
                     o_ref, wgb_ref, wub_ref, wdb_ref, hn_ref, *, final_norm):
    f = pl.program_id(0)
    _mlp_start(f == 0, (xs_ref, xp_ref), g_ref, o_ref, hn_ref)
    wg, wu, wd = wg_ref[...].astype(BF16), wu_ref[...].astype(BF16), wd_ref[...].astype(BF16)
    wgb_ref[...] = wg
    wub_ref[...] = wu
    wdb_ref[...] = wd
    _mlp_accumulate(wg, wu, wd, o_ref, hn_ref)
    _mlp_finish(f == pl.num_programs(0) - 1, fg_ref, o_ref, final_norm)


def _mlp_rest_kernel(first_hbm, x_ref, g_ref, wg_ref, wu_ref, wd_ref, fg_ref, o_ref, hn_ref, sem,
                     *, nf, skip_rows, final_norm):
    s = pl.program_id(0)
    tm = o_ref.shape[0]

    @pl.when(s == 0)
    def _():
        copy = pltpu.make_async_copy(first_hbm.at[pl.ds(skip_rows, tm), :], o_ref, sem)
        copy.start()
        copy.wait()

    @pl.when(s > 0)
    def _():
        f = (s - 1) % nf
        _mlp_start(f == 0, (x_ref,), g_ref, o_ref, hn_ref)
        _mlp_accumulate(wg_ref[...], wu_ref[...], wd_ref[...], o_ref, hn_ref)
        _mlp_finish(f == nf - 1, fg_ref, o_ref, final_norm)


def mlp_cast(xs, xp, g, w_gate_up, w_down, final_g, layer, *, tp, tf, final_norm):
    ns, d = xs.shape
    m = ns + tp
    nf = D_FF // tf
    once = pl.Buffered(1)
    vmem = 2 * m * d * 4 + m * d * 2 + 2 * 3 * d * tf * (4 + 2) + 3 * m * tf * 4
    outs = pl.pallas_call(
        functools.partial(_mlp_cast_kernel, final_norm=final_norm),
        grid=(nf,),
        in_specs=[
            pl.BlockSpec((ns, d), lambda f: (0, 0), pipeline_mode=once),
            pl.BlockSpec((tp, d), lambda f: (0, 0), pipeline_mode=once),
            pl.BlockSpec((None, 1, d), lambda f: (layer, 0, 0)),
            pl.BlockSpec((None, d, tf), lambda f: (layer, 0, f)),
            pl.BlockSpec((None, d, tf), lambda f: (layer, 0, nf + f)),
            pl.BlockSpec((None, tf, d), lambda f: (layer, f, 0)),
            pl.BlockSpec((1, d), lambda f: (0, 0)),
        ],
        out_specs=[
            pl.BlockSpec((m, d), lambda f: (0, 0), pipeline_mode=once),
            pl.BlockSpec((d, tf), lambda f: (0, f)),
            pl.BlockSpec((d, tf), lambda f: (0, f)),
            pl.BlockSpec((tf, d), lambda f: (f, 0)),
        ],
        out_shape=[jax.ShapeDtypeStruct((m, d), F32),
                   jax.ShapeDtypeStruct((d, D_FF), BF16), jax.ShapeDtypeStruct((d, D_FF), BF16),
                   jax.ShapeDtypeStruct((D_FF, d), BF16)],
        scratch_shapes=[pltpu.VMEM((m, d), BF16)],
        compiler_params=pltpu.CompilerParams(
            dimension_semantics=("arbitrary",),
            vmem_limit_bytes=_vmem_limit(vmem)),
        name="mlp_cast",
    )(xs, xp, g, w_gate_up, w_gate_up, w_down, final_g)
    return outs[0], tuple(outs[1:])


def mlp_rest(first, x, g, weights, final_g, layer, *, skip_rows, tm, tf, final_norm):
    m, d = x.shape
    nf = D_FF // tf
    step = lambda s: jnp.maximum(s - 1, 0)
    x_tile = lambda s: 1 + step(s) // nf
    f_tile = lambda s: step(s) % nf
    vmem = 2 * 2 * tm * d * 4 + tm * d * 2 + 2 * 3 * d * tf * 2 + 3 * tm * tf * 4
    return pl.pallas_call(
        functools.partial(_mlp_rest_kernel, nf=nf, skip_rows=skip_rows, final_norm=final_norm),
        grid=(1 + (m // tm - 1) * nf,),
        in_specs=[
            pl.BlockSpec(memory_space=pl.ANY),
            pl.BlockSpec((tm, d), lambda s: (x_tile(s), 0)),
            pl.BlockSpec((None, 1, d), lambda s: (layer, 0, 0)),
            pl.BlockSpec((d, tf), lambda s: (0, f_tile(s))),
            pl.BlockSpec((d, tf), lambda s: (0, f_tile(s))),
            pl.BlockSpec((tf, d), lambda s: (f_tile(s), 0)),
            pl.BlockSpec((1, d), lambda s: (0, 0)),
        ],
        out_specs=pl.BlockSpec((tm, d), lambda s: (jnp.where(s == 0, 0, x_tile(s)), 0)),
        out_shape=jax.ShapeDtypeStruct((m, d), F32),
        scratch_shapes=[pltpu.VMEM((tm, d), BF16), pltpu.SemaphoreType.DMA(())],
        compiler_params=pltpu.CompilerParams(
            dimension_semantics=("arbitrary",),
            vmem_limit_bytes=_vmem_limit(vmem)),
        name="mlp_rest",
    )(first, x, g, *weights, final_g)


_POOL_TOP = 16
_CONV_TOP = 8


def _block_prompt_kernel(x_ref, g1_ref, win_ref, wout_ref, cos_ref, sin_ref, dmask_ref, qdec_ref, kdec_ref,
                         poolw_ref, pscale_ref, convw_ref, retg_ref, sgug_ref, sguw_ref, sgub_ref,
                         *rest, start, tm, cast_next):
    if cast_next:
        win_next_ref, wout_next_ref, *rest = rest
        *rest, win_next_out, wout_next_out, pool_ext, conv_ext = rest
        win_next_out[...] = win_next_ref[...].astype(BF16)
        wout_next_out[...] = wout_next_ref[...].astype(BF16)
    else:
        *rest, pool_ext, conv_ext = rest
    h_out, pool_out, conv_out, ret_out, sgu_out = rest
    c = pl.program_id(1)
    T = CHUNK
    n_chunks = tm // T

    @pl.when(c == 0)
    def _():
        pool_ext[0:_POOL_TOP, :] = jnp.zeros((_POOL_TOP, GROUP), F32)
        conv_ext[0:_CONV_TOP, :] = jnp.zeros((_CONV_TOP, GROUP), F32)
        ret_out[...] = jnp.zeros(ret_out.shape, F32)

    x = x_ref[0]
    xn = _rms(x, g1_ref[...]).astype(BF16)

    def proj(col):
        return _dot(xn, win_ref[:, col:col + GROUP])

    def out_proj(y, col):
        return _dot(y, wout_ref[col:col + GROUP, :])

    head_lanes = [slice(hd * HEAD_DIM, (hd + 1) * HEAD_DIM) for hd in range(HEADS)]
    chunk_rows = [slice(j * T, (j + 1) * T) for j in range(n_chunks)]

    def pool_mixer(a_proj):
        pool_ext[_POOL_TOP:_POOL_TOP + tm, :] = a_proj
        pos1 = lax.broadcasted_iota(jnp.int32, (tm, HEAD_DIM), 0) + (start + 1) + c * tm
        y = []
        for gi, win in enumerate(POOL_WINDOWS):
            lanes = head_lanes[gi]
            a = pool_ext[_POOL_TOP:_POOL_TOP + tm, lanes]
            wsum = a
            for j in range(1, win):
                wsum = wsum + pool_ext[_POOL_TOP - j:_POOL_TOP - j + tm, lanes]
            cnt = jnp.minimum(pos1, win).astype(F32)
            d = (wsum / cnt - a).astype(BF16)
            y.append((_dot(d, poolw_ref[gi]) * pscale_ref[:, lanes]).astype(BF16))
        pool_out[0] = pool_ext[_POOL_TOP + tm - POOL_BUF:_POOL_TOP + tm, :]
        pool_ext[_POOL_TOP - POOL_BUF:_POOL_TOP, :] = pool_ext[_POOL_TOP + tm - POOL_BUF:_POOL_TOP + tm, :]
        return jnp.concatenate(y, axis=-1)

    def conv_mixer(bg, cg, hh):
        z = cg * hh
        conv_ext[_CONV_TOP:_CONV_TOP + tm, :] = z
        acc = (conv_ext[_CONV_TOP - 2:_CONV_TOP - 2 + tm, :] * convw_ref[0:1, :]
               + conv_ext[_CONV_TOP - 1:_CONV_TOP - 1 + tm, :] * convw_ref[1:2, :]
               + z * convw_ref[2:3, :])
        conv_out[0] = conv_ext[_CONV_TOP + tm - 2:_CONV_TOP + tm, :]
        conv_ext[_CONV_TOP - 2:_CONV_TOP, :] = conv_ext[_CONV_TOP + tm - 2:_CONV_TOP + tm, :]
        return (bg * acc).astype(BF16)

    def retention(q_all, k_all, v_all, g_all, j, hd):
        rows, lanes = chunk_rows[j], head_lanes[hd]
        cos = cos_ref[rows, :]
        sin = sin_ref[rows, :]
        q = _rotate(q_all[rows, lanes], cos, sin)
        k = _rotate(k_all[rows, lanes], cos, sin) * (HEAD_DIM ** -0.5)
        v = v_all[rows, lanes].astype(BF16)
        s_prev = ret_out[0, hd]
        scores = lax.dot_general(q.astype(BF16), k.astype(BF16), (((1,), (1,)), ((), ())),
                                 preferred_element_type=F32) * dmask_ref[hd]
        o = _dot(scores.astype(BF16), v) + _dot((q * qdec_ref[hd]).astype(BF16), s_prev.astype(BF16))
        kv = lax.dot_general((k * kdec_ref[hd]).astype(BF16), v, (((0,), (0,)), ((), ())),
                             preferred_element_type=F32)
        ret_out[0, hd] = qdec_ref[hd, T - 1:T, :] * s_prev + kv
        on = _layernorm(o) * retg_ref[:, lanes]
        return (_silu(g_all[rows, lanes]) * on).astype(BF16)

    def sgu_mixer(u_all, vn, j):
        rows = chunk_rows[j]
        tri = lax.broadcasted_iota(jnp.int32, (T, T), 0) >= lax.broadcasted_iota(jnp.int32, (T, T), 1)
        y = []
        for hd in range(HEADS):
            lanes = head_lanes[hd]
            w = jnp.where(tri, sguw_ref[hd], 0.0).astype(BF16)
            mixed = _dot(w, vn[rows, lanes].astype(BF16)) + sgub_ref[hd]
            y.append((u_all[rows, lanes] * mixed).astype(BF16))
        return jnp.concatenate(y, axis=-1)

    q_all, k_all, v_all, g_all = proj(_Q), proj(_K), proj(_V), proj(_G)
    ret = functools.partial(retention, q_all, k_all, v_all, g_all)
    y_c = [[None] * HEADS for _ in range(n_chunks)]
    y_c[0][0] = ret(0, 0)
    a_proj = proj(_A)
    y_c[0][1] = ret(0, 1)
    cg = proj(_CG)
    y_c[0][2] = ret(0, 2)
    hh = proj(_HH)
    y_c[0][3] = ret(0, 3)
    bg = proj(_BG)
    later = iter([lambda: proj(_VV), lambda: proj(_U), lambda: pool_mixer(a_proj),
                  lambda: conv_mixer(bg, cg, hh)])
    fills = []
    for j in range(1, n_chunks):
        for hd in range(HEADS):
            y_c[j][hd] = ret(j, hd)
            nxt = next(later, None)
            if nxt is not None:
                fills.append(nxt())
    fills += [f() for f in later]
    vv, u_all, y_a, y_b = fills
    h = x + out_proj(y_a, _YA)
    vn = _layernorm(vv) * sgug_ref[...]
    y_d = [sgu_mixer(u_all, vn, 0)]
    h = h + out_proj(y_b, _YB)
    y_d += [sgu_mixer(u_all, vn, j) for j in range(1, n_chunks)]
    h = h + out_proj(jnp.concatenate([jnp.concatenate(r, axis=-1) for r in y_c], axis=0), _YC)
    h_out[0] = h + out_proj(jnp.concatenate(y_d, axis=0), _YD)
    sgu_out[0] = vn[tm - T:tm, :]


def block_prompt(x, norm_g, w_in, w_out, tables, params, layer, next_weights, *, start, tm):
    b, l, d = x.shape
    nc = l // tm
    cos, sin, dmask, qdec, kdec = tables
    poolw, pscale, convw, retg, sgug, sguw, sgub = params
    const3 = lambda shape: pl.BlockSpec(shape, lambda i, c: (0, 0, 0))
    lay3 = lambda shape: pl.BlockSpec((None,) + shape, lambda i, c: (layer, 0, 0))
    lay4 = lambda shape: pl.BlockSpec((None,) + shape, lambda i, c: (layer, 0, 0, 0))
    resident = lambda shape: pl.BlockSpec(shape, lambda i, c: (0, 0), pipeline_mode=pl.Buffered(1))
    cast_next = next_weights is not None
    next_in_specs, next_out_specs, next_out_shape, next_args = [], [], [], ()
    next_bytes = 0
    if cast_next:
        for w in next_weights:
            rows, cols = w.shape[1] // (b * nc), w.shape[2]
            assert rows * b * nc == w.shape[1] and rows % 16 == 0
            next_in_specs.append(pl.BlockSpec((None, rows, cols), lambda i, c: (layer + 1, i * nc + c, 0)))
            next_out_specs.append(pl.BlockSpec((rows, cols), lambda i, c: (i * nc + c, 0)))
            next_out_shape.append(jax.ShapeDtypeStruct(w.shape[1:], BF16))
            next_bytes += 2 * rows * cols * (4 + 2)
        next_args = tuple(next_weights)
    vmem = (d * IN_WIDTH + 4 * GROUP * d) * 2 + 2 * 2 * tm * d * 4 + 2 * tm * IN_WIDTH * 4 + next_bytes
    outs = pl.pallas_call(
        functools.partial(_block_prompt_kernel, start=start, tm=tm, cast_next=cast_next),
        grid=(b, nc),
        in_specs=[
            pl.BlockSpec((1, tm, d), lambda i, c: (i, c, 0)),
            lay3((1, d)),
            resident((d, IN_WIDTH)),
            resident((4 * GROUP, d)),
            pl.BlockSpec((tm, HEAD_DIM), lambda i, c: (c, 0)),
            pl.BlockSpec((tm, HEAD_DIM), lambda i, c: (c, 0)),
            const3((HEADS, CHUNK, CHUNK)),
            const3((HEADS, CHUNK, HEAD_DIM)),
            const3((HEADS, CHUNK, HEAD_DIM)),
            lay4((len(POOL_WINDOWS), HEAD_DIM, HEAD_DIM)),
            lay3((1, GROUP)),
            lay3((3, GROUP)),
            lay3((1, GROUP)),
            lay3((1, GROUP)),
            lay4((HEADS, CHUNK, CHUNK)),
            lay4((HEADS, CHUNK, HEAD_DIM)),
            *next_in_specs,
        ],
        out_specs=[
            pl.BlockSpec((1, tm, d), lambda i, c: (i, c, 0)),
            pl.BlockSpec((1, POOL_BUF, GROUP), lambda i, c: (i, 0, 0)),
            pl.BlockSpec((1, 2, GROUP), lambda i, c: (i, 0, 0)),
            pl.BlockSpec((1, HEADS, HEAD_DIM, HEAD_DIM), lambda i, c: (i, 0, 0, 0)),
            pl.BlockSpec((1, CHUNK, GROUP), lambda i, c: (i, 0, 0)),
            *next_out_specs,
        ],
        out_shape=[
            jax.ShapeDtypeStruct((b, l, d), F32),
            jax.ShapeDtypeStruct((b, POOL_BUF, GROUP), F32),
            jax.ShapeDtypeStruct((b, 2, GROUP), F32),
            jax.ShapeDtypeStruct((b, HEADS, HEAD_DIM, HEAD_DIM), F32),
            jax.ShapeDtypeStruct((b, CHUNK, GROUP), F32),
            *next_out_shape,
        ],
        scratch_shapes=[
            pltpu.VMEM((_POOL_TOP + tm, GROUP), F32),
            pltpu.VMEM((_CONV_TOP + tm, GROUP), F32),
        ],
        compiler_params=pltpu.CompilerParams(
            dimension_semantics=("arbitrary", "arbitrary"),
            vmem_limit_bytes=_vmem_limit(vmem)),
        name="block_prompt",
    )(x, norm_g, _in_hbm(w_in), _in_hbm(w_out), cos, sin, dmask, qdec, kdec, poolw, pscale, convw, retg,
      sgug, sguw, sgub, *next_args)
    return outs[0], tuple(outs[1:5]), (tuple(outs[5:]) if cast_next else None)


_SEQ_BLOCK = 16


_N_SAMPLE_INPUTS = 14


def _mixer_sample_kernel(*refs, start, layer, all_layers):
    (p_ref, pool_ref, conv_ref, ret_ref, cos_ref, sin_ref, gam_ref,
     poolw_ref, pscale_ref, convw_ref, retg_ref, sgug_ref, sguw_ref, sgub_ref) = refs[:_N_SAMPLE_INPUTS]
    mix_ref, pool_out, conv_out, ret_out, sgu_out, qg_s, k_s, v_s, os_s = refs[-9:]
    TB = _SEQ_BLOCK

    if all_layers:
        stacked = (pool_out, conv_out, ret_out, sgu_out)
        for ref in stacked:
            for other in range(DEPTH):
                if other != layer:
                    ref[other] = jnp.zeros(ref.shape[1:], F32)
        pool_out, conv_out, ret_out, sgu_out = (ref.at[layer] for ref in stacked)

    a_all = p_ref[:, _A:_A + GROUP]
    for gi, win in enumerate(POOL_WINDOWS):
        lanes = slice(gi * HEAD_DIM, (gi + 1) * HEAD_DIM)
        a = a_all[:, lanes]
        wsum = a
        for j in range(1, win):
            wsum = wsum + pool_ref[POOL_BUF - j, :, lanes]
        cnt = float(min(start + 1, win))
        d = (wsum / cnt - a).astype(BF16)
        y = _dot(d, poolw_ref[gi]) * pscale_ref[:, lanes]
        mix_ref[:, _YA + gi * HEAD_DIM:_YA + (gi + 1) * HEAD_DIM] = y.astype(BF16)
    pool_out[0:POOL_BUF - 1] = pool_ref[1:POOL_BUF]
    pool_out[POOL_BUF - 1] = a_all

    z = p_ref[:, _CG:_CG + GROUP] * p_ref[:, _HH:_HH + GROUP]
    acc = (conv_ref[:, 0, :] * convw_ref[0:1, :] + conv_ref[:, 1, :] * convw_ref[1:2, :]
           + z * convw_ref[2:3, :])
    mix_ref[:, _YB:_YB + GROUP] = (p_ref[:, _BG:_BG + GROUP] * acc).astype(BF16)
    conv_out[:, 0, :] = conv_ref[:, 1, :]
    conv_out[:, 1, :] = z

    cos = cos_ref[...]
    sin = sin_ref[...]
    scores = []
    for h in range(HEADS):
        lanes = slice(h * HEAD_DIM, (h + 1) * HEAD_DIM)
        q = _rotate(p_ref[:, _Q + h * HEAD_DIM:_Q + (h + 1) * HEAD_DIM], cos, sin)
        k = _rotate(p_ref[:, _K + h * HEAD_DIM:_K + (h + 1) * HEAD_DIM], cos, sin) * (HEAD_DIM ** -0.5)
        scores.append(jnp.sum(q * k, axis=-1, keepdims=True))
        qg_s[:, lanes] = q * gam_ref[:, lanes]
        k_s[:, lanes] = k
    v_s[...] = p_ref[:, _V:_V + GROUP]

    first_row = lax.broadcasted_iota(jnp.int32, (8, HEAD_DIM), 0) == 0

    for b in range(TB):
        for h in range(HEADS):
            lanes = slice(h * HEAD_DIM, (h + 1) * HEAD_DIM)
            s_prev = ret_ref[b, h]
            q8 = jnp.broadcast_to(qg_s[b:b + 1, lanes], (8, HEAD_DIM))
            os_s[b:b + 1, lanes] = _dot(q8.astype(BF16), s_prev.astype(BF16))[0:1, :]
            k8 = jnp.where(first_row, jnp.broadcast_to(k_s[b:b + 1, lanes], (8, HEAD_DIM)), 0.0)
            v8 = jnp.broadcast_to(v_s[b:b + 1, lanes], (8, HEAD_DIM))
            kv = lax.dot_general(k8.astype(BF16), v8.astype(BF16), (((0,), (0,)), ((), ())),
                                 preferred_element_type=F32)
            ret_out[b, h] = gam_ref[:, lanes] * s_prev + kv

    for h in range(HEADS):
        lanes = slice(h * HEAD_DIM, (h + 1) * HEAD_DIM)
        o = scores[h] * v_s[:, lanes] + os_s[:, lanes]
        on = _layernorm(o) * retg_ref[:, lanes]
        g = p_ref[:, _G + h * HEAD_DIM:_G + (h + 1) * HEAD_DIM]
        mix_ref[:, _YC + h * HEAD_DIM:_YC + (h + 1) * HEAD_DIM] = (_silu(g) * on).astype(BF16)

    vn = _layernorm(p_ref[:, _VV:_VV + GROUP]) * sgug_ref[...]
    mix_ref[:, _YD:_YD + GROUP] = (p_ref[:, _U:_U + GROUP] * (sguw_ref[...] * vn + sgub_ref[...])).astype(BF16)
    sgu_out[:, 0, :] = vn


def mixer_sample(p, pool_state, conv_state, ret_state, tables, params, layer, stacked, *, start):
    n = p.shape[0]
    tb = _SEQ_BLOCK
    cos, sin, gam = tables
    poolw, pscale, convw, retg, sgug, sguw, sgub = params
    rows = lambda width: pl.BlockSpec((tb, width), lambda i: (i, 0))
    lay_rows = lambda r: pl.BlockSpec((None, tb, r, GROUP), lambda i: (layer, i, 0, 0))
    lay_pool = pl.BlockSpec((None, POOL_BUF, tb, GROUP), lambda i: (layer, 0, i, 0))
    lay_ret = pl.BlockSpec((None, tb, HEADS, HEAD_DIM, HEAD_DIM), lambda i: (layer, i, 0, 0, 0))
    const2 = lambda shape: pl.BlockSpec(shape, lambda i: (0, 0))
    lay3 = lambda shape: pl.BlockSpec((None,) + shape, lambda i: (layer, 0, 0))
    lay4 = lambda shape: pl.BlockSpec((None,) + shape, lambda i: (layer, 0, 0, 0))
    in_specs = [
        rows(IN_WIDTH),
        lay_pool,
        lay_rows(2),
        lay_ret,
        const2((1, HEAD_DIM)),
        const2((1, HEAD_DIM)),
        const2((1, GROUP)),
        lay4((len(POOL_WINDOWS), HEAD_DIM, HEAD_DIM)),
        lay3((1, GROUP)),
        lay3((3, GROUP)),
        lay3((1, GROUP)),
        lay3((1, GROUP)),
        lay3((1, GROUP)),
        lay3((1, GROUP)),
    ]
    args = (p, pool_state, conv_state, ret_state, cos, sin, gam,
            poolw, pscale, convw, retg, sgug, sguw, sgub)
    assert len(args) == _N_SAMPLE_INPUTS
    aliases = {}
    first = stacked is None
    state_block = tb * (16 + 8 + 8 + HEAD_DIM) * GROUP * 4
    vmem = 2 * (tb * IN_WIDTH * 4 + state_block * (1 + (DEPTH if first else 1)))
    if first:
        all_rows = lambda r: pl.BlockSpec((DEPTH, tb, r, GROUP), lambda i: (0, i, 0, 0))
        state_specs = [pl.BlockSpec((DEPTH, POOL_BUF, tb, GROUP), lambda i: (0, 0, i, 0)), all_rows(2),
                       pl.BlockSpec((DEPTH, tb, HEADS, HEAD_DIM, HEAD_DIM), lambda i: (0, i, 0, 0, 0)),
                       all_rows(1)]
    else:
        state_specs = [lay_pool, lay_rows(2), lay_ret, lay_rows(1)]
        aliases = {len(args) + k: 1 + k for k in range(len(stacked))}
        in_specs = in_specs + [pl.BlockSpec(memory_space=pl.ANY)] * len(stacked)
        args = args + tuple(stacked)
    outs = pl.pallas_call(
        functools.partial(_mixer_sample_kernel, start=start, layer=layer, all_layers=first),
        grid=(n // tb,),
        in_specs=in_specs,
        out_specs=[rows(4 * GROUP)] + state_specs,
        out_shape=[
            jax.ShapeDtypeStruct((n, 4 * GROUP), BF16),
            jax.ShapeDtypeStruct((DEPTH, POOL_BUF, n, GROUP), F32),
            jax.ShapeDtypeStruct((DEPTH, n, 2, GROUP), F32),
            jax.ShapeDtypeStruct((DEPTH, n, HEADS, HEAD_DIM, HEAD_DIM), F32),
            jax.ShapeDtypeStruct((DEPTH, n, 1, GROUP), F32),
        ],
        scratch_shapes=[pltpu.VMEM((tb, GROUP), F32)] * 4,
        input_output_aliases=aliases,
        compiler_params=pltpu.CompilerParams(
            dimension_semantics=("arbitrary",),
            vmem_limit_bytes=_vmem_limit(vmem)),
        name="mixer_sample",
    )(*args)
    return outs[0], tuple(outs[1:])


def _rope_tables(start, length):
    half = HEAD_DIM // 2
    inv = ROPE_BASE ** (-jnp.arange(half, dtype=F32) / half)
    pos = start + jnp.arange(length)
    ang = pos.astype(F32)[:, None] * inv[None, :]
    cos, sin = jnp.cos(ang), jnp.sin(ang)
    return jnp.concatenate([cos, cos], axis=-1), jnp.concatenate([-sin, sin], axis=-1)


def _log_gamma():
    return jnp.log1p(-(2.0 ** (-5.0 - jnp.arange(HEADS, dtype=F32))))


def _decay_tables(c):
    lg = _log_gamma()
    idx = jnp.arange(c, dtype=F32)
    diff = idx[:, None] - idx[None, :]
    dmask = jnp.where(diff >= 0, jnp.exp(jnp.maximum(diff, 0.0)[None] * lg[:, None, None]), 0.0)
    qdec = jnp.exp((idx + 1.0)[None, :] * lg[:, None])[:, :, None]
    kdec = jnp.exp((c - 1.0 - idx)[None, :] * lg[:, None])[:, :, None]
    wide = lambda t: jnp.broadcast_to(t, (HEADS, c, HEAD_DIM))
    return dmask, wide(qdec), wide(kdec)


def _per_head_lanes(t):
    return jnp.repeat(t, HEAD_DIM, axis=-1)[..., None, :]


def kernel(x_prompt, x_sample, state_pool, state_conv, state_ret, norm1_g, w_in, pool_w, pool_scale,
           conv_w, ret_norm_g, sgu_norm_g, sgu_w, sgu_b, w_out, norm2_g, w_gate_up, w_down,
           final_norm_g):
    bp, lp, _ = x_prompt.shape
    ns = x_sample.shape[0]

    pool_w_b = pool_w.astype(BF16)
    row3 = lambda t: t[:, None, :]
    norm1 = row3(norm1_g)
    norm2 = row3(norm2_g)
    final_g = final_norm_g[None, :]

    cos_p, sin_p = _rope_tables(0, lp)
    prompt_tables = (cos_p, sin_p) + _decay_tables(CHUNK)
    sgu_b_wide = jnp.broadcast_to(sgu_b[:, :, :, None], sgu_b.shape + (HEAD_DIM,))
    prompt_params = (pool_w_b, row3(pool_scale), conv_w, row3(ret_norm_g), row3(sgu_norm_g), sgu_w, sgu_b_wide)

    cos_s, sin_s = _rope_tables(PAST_LEN, 1)
    gam = _per_head_lanes(jnp.exp(_log_gamma()))
    sample_tables = (cos_s, sin_s, gam)
    sample_params = (pool_w_b, row3(pool_scale), conv_w, row3(ret_norm_g), row3(sgu_norm_g),
                     _per_head_lanes(sgu_w[:, :, 0, 0]), _per_head_lanes(sgu_b[:, :, 0]))
    pool_rows = jnp.transpose(state_pool, (0, 2, 1, 3))
    hp = x_prompt
    hs = x_sample.reshape(ns, D_MODEL)
    prompt_states = [[] for _ in range(4)]
    sample_states = None
    next_b = None
    for l in range(DEPTH):
        fin = l == DEPTH - 1
        w_in_l, w_out_l = (w_in, w_out) if next_b is None else next_b
        p, w_in_b = norm_matmul(hs, norm1, w_in_l, l, tn=512)
        mix, sample_states = mixer_sample(p, pool_rows, state_conv, state_ret, sample_tables,
                                          sample_params, l, sample_states, start=PAST_LEN)
        hs, w_out_b = matmul_residual(mix, w_out_l, hs, l, tn=1024)
        hp, states, next_b = block_prompt(hp, norm1, w_in_b, w_out_b, prompt_tables, prompt_params, l,
                                          None if fin else (w_in, w_out), start=0, tm=256)
        for lst, val in zip(prompt_states, states):
            lst.append(val)
        hp = hp.reshape(bp * lp, D_MODEL)
        first, w_mlp_b = mlp_cast(hs, hp, norm2, w_gate_up, w_down, final_g, l, tp=_MLP_TILE, tf=256,
                                  final_norm=fin)
        hs = first[:ns]
        hp = mlp_rest(first, hp, norm2, w_mlp_b, final_g, l, skip_rows=ns, tm=_MLP_TILE, tf=512,
                      final_norm=fin).reshape(bp, lp, D_MODEL)
    pool_p, conv_p, ret_p, sgu_p = (jnp.stack(s) for s in prompt_states)
    pool_s, conv_s, ret_s, sgu_s = sample_states
    pool_s = jnp.transpose(pool_s, (0, 2, 1, 3))
    return (hp, hs.reshape(ns, 1, D_MODEL), pool_p, pool_s, conv_p, conv_s, ret_p, ret_s, sgu_p, sgu_s)
```

```python
import functools

import jax
import jax.numpy as jnp
from jax import lax
from jax.experimental import pallas as pl
from jax.experimental.pallas import tpu as pltpu

F32 = jnp.float32
BF16 = jnp.bfloat16

D_MODEL = 2048
DEPTH = 2
GROUP = 512
N_SPLITS = 10
IN_WIDTH = N_SPLITS * GROUP
POOL_WINDOWS = (2, 4, 8, 16)
POOL_BUF = 15
HEADS = 4
HEAD_DIM = 128
CHUNK = 128
ROPE_BASE = 10000.0
D_FF = 5632
NORM_EPS = 1e-6
PAST_LEN = 16384

_A, _BG, _CG, _HH, _Q, _K, _V, _G, _U, _VV = (i * GROUP for i in range(N_SPLITS))
_YA, _YB, _YC, _YD = (i * GROUP for i in range(4))

MIB = 1024 * 1024
VMEM_BYTES_V7X = 64 * MIB


VMEM_LIMIT_MAX = VMEM_BYTES_V7X - 4 * MIB


def _vmem_limit(estimate):
    return min(estimate + 4 * MIB, VMEM_LIMIT_MAX)


def _rms(x, g):
    ms = jnp.mean(x * x, axis=-1, keepdims=True)
    return x * lax.rsqrt(ms + NORM_EPS) * g


def _layernorm(x):
    mu = jnp.mean(x, axis=-1, keepdims=True)
    xc = x - mu
    return xc * lax.rsqrt(jnp.mean(xc * xc, axis=-1, keepdims=True) + NORM_EPS)


def _silu(x):
    return x * jax.nn.sigmoid(x)


def _dot(a, b):
    return jnp.dot(a, b, preferred_element_type=F32)


def _rotate(x, cos, sin_signed):
    return x * cos + pltpu.roll(x, HEAD_DIM // 2, 1) * sin_signed


def _weight_specs(w, layer, k, tn):
    if w.ndim == 3:
        return pl.BlockSpec((None, k, tn), lambda j: (layer, 0, j)), True
    return pl.BlockSpec((k, tn), lambda j: (0, j)), False


def _norm_matmul_kernel(x_ref, g_ref, w_ref, o_ref, *rest, emit):
    xn_ref = rest[-1]

    @pl.when(pl.program_id(0) == 0)
    def _():
        xn_ref[...] = _rms(x_ref[...], g_ref[...]).astype(BF16)

    wb = w_ref[...].astype(BF16)
    if emit:
        rest[0][...] = wb
    o_ref[...] = _dot(xn_ref[...], wb)


def norm_matmul(x, g, w, layer, *, tn):
    m, k = x.shape
    n = w.shape[-1]
    w_spec, emit = _weight_specs(w, layer, k, tn)
    vmem = 2 * k * tn * (4 + 2) + 2 * m * (k + tn) * 4 + m * k * 2
    outs = pl.pallas_call(
        functools.partial(_norm_matmul_kernel, emit=emit),
        grid=(n // tn,),
        in_specs=[
            pl.BlockSpec((m, k), lambda j: (0, 0)),
            pl.BlockSpec((None, 1, k), lambda j: (layer, 0, 0)),
            w_spec,
        ],
        out_specs=[pl.BlockSpec((m, tn), lambda j: (0, j))] + [pl.BlockSpec((k, tn), lambda j: (0, j))] * emit,
        out_shape=[jax.ShapeDtypeStruct((m, n), F32)] + [jax.ShapeDtypeStruct((k, n), BF16)] * emit,
        scratch_shapes=[pltpu.VMEM((m, k), BF16)],
        compiler_params=pltpu.CompilerParams(
            dimension_semantics=("arbitrary",),
            vmem_limit_bytes=_vmem_limit(vmem)),
        name="norm_matmul",
    )(x, g, w)
    return tuple(outs) if emit else (outs[0], w)


def _matmul_residual_kernel(a_ref, w_ref, r_ref, o_ref, *rest, emit):
    wb = w_ref[...].astype(BF16)
    if emit:
        rest[0][...] = wb
    o_ref[...] = r_ref[...] + _dot(a_ref[...], wb)


def matmul_residual(a, w, res, layer, *, tn):
    m, k = a.shape
    n = w.shape[-1]
    w_spec, emit = _weight_specs(w, layer, k, tn)
    vmem = 2 * k * tn * (4 + 2) + 2 * m * (k * 2 + 2 * tn * 4)
    outs = pl.pallas_call(
        functools.partial(_matmul_residual_kernel, emit=emit),
        grid=(n // tn,),
        in_specs=[
            pl.BlockSpec((m, k), lambda j: (0, 0)),
            w_spec,
            pl.BlockSpec((m, tn), lambda j: (0, j)),
        ],
        out_specs=[pl.BlockSpec((m, tn), lambda j: (0, j))] + [pl.BlockSpec((k, tn), lambda j: (0, j))] * emit,
        out_shape=[jax.ShapeDtypeStruct((m, n), F32)] + [jax.ShapeDtypeStruct((k, n), BF16)] * emit,
        compiler_params=pltpu.CompilerParams(
            dimension_semantics=("arbitrary",),
            vmem_limit_bytes=_vmem_limit(vmem)),
        name="matmul_residual",
    )(a, w, res)
    return tuple(outs) if emit else (outs[0], w)


_MLP_TILE = 1024


def _mlp_start(first, x_refs, g_ref, o_ref, hn_ref):
    @pl.when(first)
    def _():
        row = 0
        for x_ref in x_refs:
            x = x_ref[...]
            rows = slice(row, row + x.shape[0])
            hn_ref[rows, :] = _rms(x, g_ref[...]).astype(BF16)
            o_ref[rows, :] = x
            row += x.shape[0]


def _mlp_accumulate(wg, wu, wd, o_ref, hn_ref):
    hn = hn_ref[...]
    act = (_silu(_dot(hn, wg)) * _dot(hn, wu)).astype(BF16)
    o_ref[...] += _dot(act, wd)


def _mlp_finish(last, fg_ref, o_ref, final_norm):
    if final_norm:
        @pl.when(last)
        def _():
            o_ref[...] = _rms(o_ref[...], fg_ref[...])


def _mlp_cast_kernel(xs_ref, xp_ref, g_ref, wg_ref, wu_ref, wd_ref, fg_ref,
                     o_ref, wgb_ref, wub_ref, wdb_ref, hn_ref, *, final_norm):
    f = pl.program_id(0)
    _mlp_start(f == 0, (xs_ref, xp_ref), g_ref, o_ref, hn_ref)
    wg, wu, wd = wg_ref[...].astype(BF16), wu_ref[...].astype(BF16), wd_ref[...].astype(BF16)
    wgb_ref[...] = wg
    wub_ref[...] = wu
    wdb_ref[...] = wd
    _mlp_accumulate(wg, wu, wd, o_ref, hn_ref)
    _mlp_finish(f == pl.num_programs(0) - 1, fg_ref, o_ref, final_norm)


def _mlp_rest_kernel(first_hbm, x_ref, g_ref, wg_ref, wu_ref, wd_ref, fg_ref, o_ref, hn_ref, sem,
                     *, nf, skip_rows, final_norm):
    s = pl.program_id(0)
    tm = o_ref.shape[0]

    @pl.when(s == 0)
    def _():
        copy = pltpu.make_async_copy(first_hbm.at[pl.ds(skip_rows, tm), :], o_ref, sem)
        copy.start()
        copy.wait()

    @pl.when(s > 0)
    def _():
        f = (s - 1) % nf
        _mlp_start(f == 0, (x_ref,), g_ref, o_ref, hn_ref)
        _mlp_accumulate(wg_ref[...], wu_ref[...], wd_ref[...], o_ref, hn_ref)
        _mlp_finish(f == nf - 1, fg_ref, o_ref, final_norm)


def mlp_cast(xs, xp, g, w_gate_up, w_down, final_g, layer, *, tp, tf, final_norm):
    ns, d = xs.shape
    m = ns + tp
    nf = D_FF // tf
    once = pl.Buffered(1)
    vmem = 2 * m * d * 4 + m * d * 2 + 2 * 3 * d * tf * (4 + 2) + 3 * m * tf * 4
    outs = pl.pallas_call(
        functools.partial(_mlp_cast_kernel, final_norm=final_norm),
        grid=(nf,),
        in_specs=[
            pl.BlockSpec((ns, d), lambda f: (0, 0), pipeline_mode=once),
            pl.BlockSpec((tp, d), lambda f: (0, 0), pipeline_mode=once),
            pl.BlockSpec((None, 1, d), lambda f: (layer, 0, 0)),
            pl.BlockSpec((None, d, tf), lambda f: (layer, 0, f)),
            pl.BlockSpec((None, d, tf), lambda f: (layer, 0, nf + f)),
            pl.BlockSpec((None, tf, d), lambda f: (layer, f, 0)),
            pl.BlockSpec((1, d), lambda f: (0, 0)),
        ],
        out_specs=[
            pl.BlockSpec((m, d), lambda f: (0, 0), pipeline_mode=once),
            pl.BlockSpec((d, tf), lambda f: (0, f)),
            pl.BlockSpec((d, tf), lambda f: (0, f)),
            pl.BlockSpec((tf, d), lambda f: (f, 0)),
        ],
        out_shape=[jax.ShapeDtypeStruct((m, d), F32),
                   jax.ShapeDtypeStruct((d, D_FF), BF16), jax.ShapeDtypeStruct((d, D_FF), BF16),
                   jax.ShapeDtypeStruct((D_FF, d), BF16)],
        scratch_shapes=[pltpu.VMEM((m, d), BF16)],
        compiler_params=pltpu.CompilerParams(
            dimension_semantics=("arbitrary",),
            vmem_limit_bytes=_vmem_limit(vmem)),
        name="mlp_cast",
    )(xs, xp, g, w_gate_up, w_gate_up, w_down, final_g)
    return outs[0], tuple(outs[1:])


def mlp_rest(first, x, g, weights, final_g, layer, *, skip_rows, tm, tf, final_norm):
    m, d = x.shape
    nf = D_FF // tf
    step = lambda s: jnp.maximum(s - 1, 0)
    x_tile = lambda s: 1 + step(s) // nf
    f_tile = lambda s: step(s) % nf
    vmem = 2 * 2 * tm * d * 4 + tm * d * 2 + 2 * 3 * d * tf * 2 + 3 * tm * tf * 4
    return pl.pallas_call(
        functools.partial(_mlp_rest_kernel, nf=nf, skip_rows=skip_rows, final_norm=final_norm),
        grid=(1 + (m // tm - 1) * nf,),
        in_specs=[
            pl.BlockSpec(memory_space=pl.ANY),
            pl.BlockSpec((tm, d), lambda s: (x_tile(s), 0)),
            pl.BlockSpec((None, 1, d), lambda s: (layer, 0, 0)),
            pl.BlockSpec((d, tf), lambda s: (0, f_tile(s))),
            pl.BlockSpec((d, tf), lambda s: (0, f_tile(s))),
            pl.BlockSpec((tf, d), lambda s: (f_tile(s), 0)),
            pl.BlockSpec((1, d), lambda s: (0, 0)),
        ],
        out_specs=pl.BlockSpec((tm, d), lambda s: (jnp.where(s == 0, 0, x_tile(s)), 0)),
        out_shape=jax.ShapeDtypeStruct((m, d), F32),
        scratch_shapes=[pltpu.VMEM((tm, d), BF16), pltpu.SemaphoreType.DMA(())],
        compiler_params=pltpu.CompilerParams(
            dimension_semantics=("arbitrary",),
            vmem_limit_bytes=_vmem_limit(vmem)),
        name="mlp_rest",
    )(first, x, g, *weights, final_g)


_POOL_TOP = 16
_CONV_TOP = 8


def _block_prompt_kernel(x_ref, g1_ref, win_ref, wout_ref, cos_ref, sin_ref, dmask_ref, qdec_ref, kdec_ref,
                         poolw_ref, pscale_ref, convw_ref, retg_ref, sgug_ref, sguw_ref, sgub_ref,
                         *rest, start, tm, cast_next, layer, all_layers):
    *rest, pool_ext, conv_ext = rest
    if cast_next:
        win_next_ref, wout_next_ref, *rest, win_next_out, wout_next_out = rest
        win_next_out[...] = win_next_ref[...].astype(BF16)
        wout_next_out[...] = wout_next_ref[...].astype(BF16)
    h_out, pool_out, conv_out, ret_out, sgu_out = rest[-5:]
    stacked = (pool_out, conv_out, ret_out, sgu_out)
    if all_layers:
        pool_out, conv_out, ret_out, sgu_out = (ref.at[layer] for ref in stacked)
    c = pl.program_id(1)
    T = CHUNK
    n_chunks = tm // T

    @pl.when(c == 0)
    def _():
        pool_ext[0:_POOL_TOP, :] = jnp.zeros((_POOL_TOP, GROUP), F32)
        conv_ext[0:_CONV_TOP, :] = jnp.zeros((_CONV_TOP, GROUP), F32)
        ret_out[...] = jnp.zeros(ret_out.shape, F32)
        if all_layers:
            for ref in stacked:
                for other in range(DEPTH):
                    if other != layer:
                        ref[other] = jnp.zeros(ref.shape[1:], F32)

    x = x_ref[0]
    xn = _rms(x, g1_ref[...]).astype(BF16)

    def proj(col):
        return _dot(xn, win_ref[:, col:col + GROUP])

    def out_proj(y, col):
        return _dot(y, wout_ref[col:col + GROUP, :])

    head_lanes = [slice(hd * HEAD_DIM, (hd + 1) * HEAD_DIM) for hd in range(HEADS)]
    chunk_rows = [slice(j * T, (j + 1) * T) for j in range(n_chunks)]

    def pool_mixer(a_proj):
        pool_ext[_POOL_TOP:_POOL_TOP + tm, :] = a_proj
        pos1 = lax.broadcasted_iota(jnp.int32, (tm, HEAD_DIM), 0) + (start + 1) + c * tm
        y = []
        for gi, win in enumerate(POOL_WINDOWS):
            lanes = head_lanes[gi]
            a = pool_ext[_POOL_TOP:_POOL_TOP + tm, lanes]
            wsum = a
            for j in range(1, win):
                wsum = wsum + pool_ext[_POOL_TOP - j:_POOL_TOP - j + tm, lanes]
            cnt = jnp.minimum(pos1, win).astype(F32)
            d = (wsum / cnt - a).astype(BF16)
            y.append((_dot(d, poolw_ref[gi]) * pscale_ref[:, lanes]).astype(BF16))
        pool_out[0] = pool_ext[_POOL_TOP + tm - POOL_BUF:_POOL_TOP + tm, :]
        pool_ext[_POOL_TOP - POOL_BUF:_POOL_TOP, :] = pool_ext[_POOL_TOP + tm - POOL_BUF:_POOL_TOP + tm, :]
        return jnp.concatenate(y, axis=-1)

    def conv_mixer(bg, cg, hh):
        z = cg * hh
        conv_ext[_CONV_TOP:_CONV_TOP + tm, :] = z
        acc = (conv_ext[_CONV_TOP - 2:_CONV_TOP - 2 + tm, :] * convw_ref[0:1, :]
               + conv_ext[_CONV_TOP - 1:_CONV_TOP - 1 + tm, :] * convw_ref[1:2, :]
               + z * convw_ref[2:3, :])
        conv_out[0] = conv_ext[_CONV_TOP + tm - 2:_CONV_TOP + tm, :]
        conv_ext[_CONV_TOP - 2:_CONV_TOP, :] = conv_ext[_CONV_TOP + tm - 2:_CONV_TOP + tm, :]
        return (bg * acc).astype(BF16)

    def retention(q_all, k_all, v_all, g_all, j, hd):
        rows, lanes = chunk_rows[j], head_lanes[hd]
        cos = cos_ref[rows, :]
        sin = sin_ref[rows, :]
        q = _rotate(q_all[rows, lanes], cos, sin)
        k = _rotate(k_all[rows, lanes], cos, sin) * (HEAD_DIM ** -0.5)
        v = v_all[rows, lanes].astype(BF16)
        s_prev = ret_out[0, hd]
        scores = lax.dot_general(q.astype(BF16), k.astype(BF16), (((1,), (1,)), ((), ())),
                                 preferred_element_type=F32) * dmask_ref[hd]
        o = _dot(scores.astype(BF16), v) + _dot((q * qdec_ref[hd]).astype(BF16), s_prev.astype(BF16))
        kv = lax.dot_general((k * kdec_ref[hd]).astype(BF16), v, (((0,), (0,)), ((), ())),
                             preferred_element_type=F32)
        ret_out[0, hd] = qdec_ref[hd, T - 1:T, :] * s_prev + kv
        on = _layernorm(o) * retg_ref[:, lanes]
        return (_silu(g_all[rows, lanes]) * on).astype(BF16)

    def sgu_mixer(u_all, vn, j):
        rows = chunk_rows[j]
        tri = lax.broadcasted_iota(jnp.int32, (T, T), 0) >= lax.broadcasted_iota(jnp.int32, (T, T), 1)
        y = []
        for hd in range(HEADS):
            lanes = head_lanes[hd]
            w = jnp.where(tri, sguw_ref[hd], 0.0).astype(BF16)
            mixed = _dot(w, vn[rows, lanes].astype(BF16)) + sgub_ref[hd]
            y.append((u_all[rows, lanes] * mixed).astype(BF16))
        return jnp.concatenate(y, axis=-1)

    q_all, k_all, v_all, g_all = proj(_Q), proj(_K), proj(_V), proj(_G)
    ret = functools.partial(retention, q_all, k_all, v_all, g_all)
    y_c = [[None] * HEADS for _ in range(n_chunks)]
    y_c[0][0] = ret(0, 0)
    a_proj = proj(_A)
    y_c[0][1] = ret(0, 1)
    cg = proj(_CG)
    y_c[0][2] = ret(0, 2)
    hh = proj(_HH)
    y_c[0][3] = ret(0, 3)
    bg = proj(_BG)
    later = iter([lambda: proj(_VV), lambda: proj(_U), lambda: pool_mixer(a_proj),
                  lambda: conv_mixer(bg, cg, hh)])
    fills = []
    for j in range(1, n_chunks):
        for hd in range(HEADS):
            y_c[j][hd] = ret(j, hd)
            nxt = next(later, None)
            if nxt is not None:
                fills.append(nxt())
    fills += [f() for f in later]
    vv, u_all, y_a, y_b = fills
    h = x + out_proj(y_a, _YA)
    vn = _layernorm(vv) * sgug_ref[...]
    y_d = [sgu_mixer(u_all, vn, 0)]
    h = h + out_proj(y_b, _YB)
    y_d += [sgu_mixer(u_all, vn, j) for j in range(1, n_chunks)]
    h = h + out_proj(jnp.concatenate([jnp.concatenate(r, axis=-1) for r in y_c], axis=0), _YC)
    h_out[0] = h + out_proj(jnp.concatenate(y_d, axis=0), _YD)
    sgu_out[0] = vn[tm - T:tm, :]


_N_BLOCK_INPUTS = 16


def block_prompt(x, norm_g, w_in, w_out, tables, params, layer, next_weights, stacked, *, start, tm):
    b, l, d = x.shape
    nc = l // tm
    cos, sin, dmask, qdec, kdec = tables
    poolw, pscale, convw, retg, sgug, sguw, sgub = params
    const3 = lambda shape: pl.BlockSpec(shape, lambda i, c: (0, 0, 0))
    lay3 = lambda shape: pl.BlockSpec((None,) + shape, lambda i, c: (layer, 0, 0))
    lay4 = lambda shape: pl.BlockSpec((None,) + shape, lambda i, c: (layer, 0, 0, 0))
    resident = lambda shape: pl.BlockSpec(shape, lambda i, c: (0, 0), pipeline_mode=pl.Buffered(1))
    cast_next = next_weights is not None
    next_in_specs, next_out_specs, next_out_shape, next_args = [], [], [], ()
    next_bytes = 0
    if cast_next:
        for w in next_weights:
            rows, cols = w.shape[1] // (b * nc), w.shape[2]
            assert rows * b * nc == w.shape[1] and rows % 16 == 0
            next_in_specs.append(pl.BlockSpec((None, rows, cols), lambda i, c: (layer + 1, i * nc + c, 0)))
            next_out_specs.append(pl.BlockSpec((rows, cols), lambda i, c: (i * nc + c, 0)))
            next_out_shape.append(jax.ShapeDtypeStruct(w.shape[1:], BF16))
            next_bytes += 2 * rows * cols * (4 + 2)
        next_args = tuple(next_weights)
    state_dims = [(POOL_BUF, GROUP), (2, GROUP), (HEADS, HEAD_DIM, HEAD_DIM), (CHUNK, GROUP)]
    first = stacked is None
    if first:
        state_specs = [pl.BlockSpec((DEPTH, 1) + dims, lambda i, c, n=len(dims): (0, i) + (0,) * n)
                       for dims in state_dims]
        alias_specs, alias_args, aliases = [], (), {}
    else:
        state_specs = [pl.BlockSpec((None, 1) + dims, lambda i, c, n=len(dims): (layer, i) + (0,) * n)
                       for dims in state_dims]
        alias_specs = [pl.BlockSpec(memory_space=pl.ANY)] * len(stacked)
        alias_args = tuple(stacked)
        n_in = _N_BLOCK_INPUTS + len(next_in_specs)
        aliases = {n_in + k: 1 + k for k in range(len(stacked))}
    vmem = (d * IN_WIDTH + 4 * GROUP * d) * 2 + 2 * 2 * tm * d * 4 + 2 * tm * IN_WIDTH * 4 + next_bytes
    outs = pl.pallas_call(
        functools.partial(_block_prompt_kernel, start=start, tm=tm, cast_next=cast_next, layer=layer,
                          all_layers=first),
        grid=(b, nc),
        in_specs=[
            pl.BlockSpec((1, tm, d), lambda i, c: (i, c, 0)),
            lay3((1, d)),
            resident((d, IN_WIDTH)),
            resident((4 * GROUP, d)),
            pl.BlockSpec((tm, HEAD_DIM), lambda i, c: (c, 0)),
            pl.BlockSpec((tm, HEAD_DIM), lambda i, c: (c, 0)),
            const3((HEADS, CHUNK, CHUNK)),
            const3((HEADS, CHUNK, HEAD_DIM)),
            const3((HEADS, CHUNK, HEAD_DIM)),
            lay4((len(POOL_WINDOWS), HEAD_DIM, HEAD_DIM)),
            lay3((1, GROUP)),
            lay3((3, GROUP)),
            lay3((1, GROUP)),
            lay3((1, GROUP)),
            lay4((HEADS, CHUNK, CHUNK)),
            lay4((HEADS, CHUNK, HEAD_DIM)),
            *next_in_specs,
            *alias_specs,
        ],
        out_specs=[pl.BlockSpec((1, tm, d), lambda i, c: (i, c, 0)), *state_specs, *next_out_specs],
        out_shape=[
            jax.ShapeDtypeStruct((b, l, d), F32),
            *[jax.ShapeDtypeStruct((DEPTH, b) + dims, F32) for dims in state_dims],
            *next_out_shape,
        ],
        scratch_shapes=[
            pltpu.VMEM((_POOL_TOP + tm, GROUP), F32),
            pltpu.VMEM((_CONV_TOP + tm, GROUP), F32),
        ],
        input_output_aliases=aliases,
        compiler_params=pltpu.CompilerParams(
            dimension_semantics=("arbitrary", "arbitrary"),
            vmem_limit_bytes=_vmem_limit(vmem)),
        name="block_prompt",
    )(x, norm_g, w_in, w_out, cos, sin, dmask, qdec, kdec, poolw, pscale, convw, retg, sgug, sguw, sgub,
      *next_args, *alias_args)
    return outs[0], tuple(outs[1:5]), (tuple(outs[5:]) if cast_next else None)


_SEQ_BLOCK = 16


_N_SAMPLE_INPUTS = 14


def _mixer_sample_kernel(*refs, start, layer, all_layers):
    (p_ref, pool_ref, conv_ref, ret_ref, cos_ref, sin_ref, gam_ref,
     poolw_ref, pscale_ref, convw_ref, retg_ref, sgug_ref, sguw_ref, sgub_ref) = refs[:_N_SAMPLE_INPUTS]
    mix_ref, pool_out, conv_out, ret_out, sgu_out, qg_s, k_s, v_s, os_s = refs[-9:]
    TB = _SEQ_BLOCK

    if all_layers:
        stacked = (pool_out, conv_out, ret_out, sgu_out)
        for ref in stacked:
            for other in range(DEPTH):
                if other != layer:
                    ref[other] = jnp.zeros(ref.shape[1:], F32)
        pool_out, conv_out, ret_out, sgu_out = (ref.at[layer] for ref in stacked)

    a_all = p_ref[:, _A:_A + GROUP]
    for gi, win in enumerate(POOL_WINDOWS):
        lanes = slice(gi * HEAD_DIM, (gi + 1) * HEAD_DIM)
        a = a_all[:, lanes]
        wsum = a
        for j in range(1, win):
            wsum = wsum + pool_ref[POOL_BUF - j, :, lanes]
        cnt = float(min(start + 1, win))
        d = (wsum / cnt - a).astype(BF16)
        y = _dot(d, poolw_ref[gi]) * pscale_ref[:, lanes]
        mix_ref[:, _YA + gi * HEAD_DIM:_YA + (gi + 1) * HEAD_DIM] = y.astype(BF16)
    pool_out[0:POOL_BUF - 1] = pool_ref[1:POOL_BUF]
    pool_out[POOL_BUF - 1] = a_all

    z = p_ref[:, _CG:_CG + GROUP] * p_ref[:, _HH:_HH + GROUP]
    acc = (conv_ref[:, 0, :] * convw_ref[0:1, :] + conv_ref[:, 1, :] * convw_ref[1:2, :]
           + z * convw_ref[2:3, :])
    mix_ref[:, _YB:_YB + GROUP] = (p_ref[:, _BG:_BG + GROUP] * acc).astype(BF16)
    conv_out[:, 0, :] = conv_ref[:, 1, :]
    conv_out[:, 1, :] = z

    cos = cos_ref[...]
    sin = sin_ref[...]
    scores = []
    for h in range(HEADS):
        lanes = slice(h * HEAD_DIM, (h + 1) * HEAD_DIM)
        q = _rotate(p_ref[:, _Q + h * HEAD_DIM:_Q + (h + 1) * HEAD_DIM], cos, sin)
        k = _rotate(p_ref[:, _K + h * HEAD_DIM:_K + (h + 1) * HEAD_DIM], cos, sin) * (HEAD_DIM ** -0.5)
        scores.append(jnp.sum(q * k, axis=-1, keepdims=True))
        qg_s[:, lanes] = q * gam_ref[:, lanes]
        k_s[:, lanes] = k
    v_s[...] = p_ref[:, _V:_V + GROUP]

    first_row = lax.broadcasted_iota(jnp.int32, (8, HEAD_DIM), 0) == 0

    for b in range(TB):
        for h in range(HEADS):
            lanes = slice(h * HEAD_DIM, (h + 1) * HEAD_DIM)
            s_prev = ret_ref[b, h]
            q8 = jnp.broadcast_to(qg_s[b:b + 1, lanes], (8, HEAD_DIM))
            os_s[b:b + 1, lanes] = _dot(q8.astype(BF16), s_prev.astype(BF16))[0:1, :]
            k8 = jnp.where(first_row, jnp.broadcast_to(k_s[b:b + 1, lanes], (8, HEAD_DIM)), 0.0)
            v8 = jnp.broadcast_to(v_s[b:b + 1, lanes], (8, HEAD_DIM))
            kv = lax.dot_general(k8.astype(BF16), v8.astype(BF16), (((0,), (0,)), ((), ())),
                                 preferred_element_type=F32)
            ret_out[b, h] = gam_ref[:, lanes] * s_prev + kv

    for h in range(HEADS):
        lanes = slice(h * HEAD_DIM, (h + 1) * HEAD_DIM)
        o = scores[h] * v_s[:, lanes] + os_s[:, lanes]
        on = _layernorm(o) * retg_ref[:, lanes]
        g = p_ref[:, _G + h * HEAD_DIM:_G + (h + 1) * HEAD_DIM]
        mix_ref[:, _YC + h * HEAD_DIM:_YC + (h + 1) * HEAD_DIM] = (_silu(g) * on).astype(BF16)

    vn = _layernorm(p_ref[:, _VV:_VV + GROUP]) * sgug_ref[...]
    mix_ref[:, _YD:_YD + GROUP] = (p_ref[:, _U:_U + GROUP] * (sguw_ref[...] * vn + sgub_ref[...])).astype(BF16)
    sgu_out[:, 0, :] = vn


def mixer_sample(p, pool_state, conv_state, ret_state, tables, params, layer, stacked, *, start):
    n = p.shape[0]
    tb = _SEQ_BLOCK
    cos, sin, gam = tables
    poolw, pscale, convw, retg, sgug, sguw, sgub = params
    rows = lambda width: pl.BlockSpec((tb, width), lambda i: (i, 0))
    lay_rows = lambda r: pl.BlockSpec((None, tb, r, GROUP), lambda i: (layer, i, 0, 0))
    lay_pool = pl.BlockSpec((None, POOL_BUF, tb, GROUP), lambda i: (layer, 0, i, 0))
    lay_ret = pl.BlockSpec((None, tb, HEADS, HEAD_DIM, HEAD_DIM), lambda i: (layer, i, 0, 0, 0))
    const2 = lambda shape: pl.BlockSpec(shape, lambda i: (0, 0))
    lay3 = lambda shape: pl.BlockSpec((None,) + shape, lambda i: (layer, 0, 0))
    lay4 = lambda shape: pl.BlockSpec((None,) + shape, lambda i: (layer, 0, 0, 0))
    in_specs = [
        rows(IN_WIDTH),
        lay_pool,
        lay_rows(2),
        lay_ret,
        const2((1, HEAD_DIM)),
        const2((1, HEAD_DIM)),
        const2((1, GROUP)),
        lay4((len(POOL_WINDOWS), HEAD_DIM, HEAD_DIM)),
        lay3((1, GROUP)),
        lay3((3, GROUP)),
        lay3((1, GROUP)),
        lay3((1, GROUP)),
        lay3((1, GROUP)),
        lay3((1, GROUP)),
    ]
    args = (p, pool_state, conv_state, ret_state, cos, sin, gam,
            poolw, pscale, convw, retg, sgug, sguw, sgub)
    assert len(args) == _N_SAMPLE_INPUTS
    aliases = {}
    first = stacked is None
    state_block = tb * (16 + 8 + 8 + HEAD_DIM) * GROUP * 4
    vmem = 2 * (tb * IN_WIDTH * 4 + state_block * (1 + (DEPTH if first else 1)))
    if first:
        all_rows = lambda r: pl.BlockSpec((DEPTH, tb, r, GROUP), lambda i: (0, i, 0, 0))
        state_specs = [pl.BlockSpec((DEPTH, POOL_BUF, tb, GROUP), lambda i: (0, 0, i, 0)), all_rows(2),
                       pl.BlockSpec((DEPTH, tb, HEADS, HEAD_DIM, HEAD_DIM), lambda i: (0, i, 0, 0, 0)),
                       all_rows(1)]
    else:
        state_specs = [lay_pool, lay_rows(2), lay_ret, lay_rows(1)]
        aliases = {len(args) + k: 1 + k for k in range(len(stacked))}
        in_specs = in_specs + [pl.BlockSpec(memory_space=pl.ANY)] * len(stacked)
        args = args + tuple(stacked)
    outs = pl.pallas_call(
        functools.partial(_mixer_sample_kernel, start=start, layer=layer, all_layers=first),
        grid=(n // tb,),
        in_specs=in_specs,
        out_specs=[rows(4 * GROUP)] + state_specs,
        out_shape=[
            jax.ShapeDtypeStruct((n, 4 * GROUP), BF16),
            jax.ShapeDtypeStruct((DEPTH, POOL_BUF, n, GROUP), F32),
            jax.ShapeDtypeStruct((DEPTH, n, 2, GROUP), F32),
            jax.ShapeDtypeStruct((DEPTH, n, HEADS, HEAD_DIM, HEAD_DIM), F32),
            jax.ShapeDtypeStruct((DEPTH, n, 1, GROUP), F32),
        ],
        scratch_shapes=[pltpu.VMEM((tb, GROUP), F32)] * 4,
        input_output_aliases=aliases,
        compiler_params=pltpu.CompilerParams(
            dimension_semantics=("arbitrary",),
            vmem_limit_bytes=_vmem_limit(vmem)),
        name="mixer_sample",
    )(*args)
    return outs[0], tuple(outs[1:])


def _rope_tables(start, length):
    half = HEAD_DIM // 2
    inv = ROPE_BASE ** (-jnp.arange(half, dtype=F32) / half)
    pos = start + jnp.arange(length)
    ang = pos.astype(F32)[:, None] * inv[None, :]
    cos, sin = jnp.cos(ang), jnp.sin(ang)
    return jnp.concatenate([cos, cos], axis=-1), jnp.concatenate([-sin, sin], axis=-1)


def _log_gamma():
    return jnp.log1p(-(2.0 ** (-5.0 - jnp.arange(HEADS, dtype=F32))))


def _decay_tables(c):
    lg = _log_gamma()
    idx = jnp.arange(c, dtype=F32)
    diff = idx[:, None] - idx[None, :]
    dmask = jnp.where(diff >= 0, jnp.exp(jnp.maximum(diff, 0.0)[None] * lg[:, None, None]), 0.0)
    qdec = jnp.exp((idx + 1.0)[None, :] * lg[:, None])[:, :, None]
    kdec = jnp.exp((c - 1.0 - idx)[None, :] * lg[:, None])[:, :, None]
    wide = lambda t: jnp.broadcast_to(t, (HEADS, c, HEAD_DIM))
    return dmask, wide(qdec), wide(kdec)


def _per_head_lanes(t):
    return jnp.repeat(t, HEAD_DIM, axis=-1)[..., None, :]


def kernel(x_prompt, x_sample, state_pool, state_conv, state_ret, norm1_g, w_in, pool_w, pool_scale,
           conv_w, ret_norm_g, sgu_norm_g, sgu_w, sgu_b, w_out, norm2_g, w_gate_up, w_down,
           final_norm_g):
    bp, lp, _ = x_prompt.shape
    ns = x_sample.shape[0]

    pool_w_b = pool_w.astype(BF16)
    row3 = lambda t: t[:, None, :]
    norm1 = row3(norm1_g)
    norm2 = row3(norm2_g)
    final_g = final_norm_g[None, :]

    cos_p, sin_p = _rope_tables(0, lp)
    prompt_tables = (cos_p, sin_p) + _decay_tables(CHUNK)
    sgu_b_wide = jnp.broadcast_to(sgu_b[:, :, :, None], sgu_b.shape + (HEAD_DIM,))
    prompt_params = (pool_w_b, row3(pool_scale), conv_w, row3(ret_norm_g), row3(sgu_norm_g), sgu_w, sgu_b_wide)

    cos_s, sin_s = _rope_tables(PAST_LEN, 1)
    gam = _per_head_lanes(jnp.exp(_log_gamma()))
    sample_tables = (cos_s, sin_s, gam)
    sample_params = (pool_w_b, row3(pool_scale), conv_w, row3(ret_norm_g), row3(sgu_norm_g),
                     _per_head_lanes(sgu_w[:, :, 0, 0]), _per_head_lanes(sgu_b[:, :, 0]))
    pool_rows = jnp.transpose(state_pool, (0, 2, 1, 3))
    hp = x_prompt
    hs = x_sample.reshape(ns, D_MODEL)
    prompt_states = None
    sample_states = None
    next_b = None
    for l in range(DEPTH):
        fin = l == DEPTH - 1
        w_in_l, w_out_l = (w_in, w_out) if next_b is None else next_b
        p, w_in_b = norm_matmul(hs, norm1, w_in_l, l, tn=512)
        mix, sample_states = mixer_sample(p, pool_rows, state_conv, state_ret, sample_tables,
                                          sample_params, l, sample_states, start=PAST_LEN)
        hs, w_out_b = matmul_residual(mix, w_out_l, hs, l, tn=1024)
        hp, prompt_states, next_b = block_prompt(hp, norm1, w_in_b, w_out_b, prompt_tables, prompt_params,
                                                 l, None if fin else (w_in, w_out), prompt_states,
                                                 start=0, tm=256)
        hp = hp.reshape(bp * lp, D_MODEL)
        first, w_mlp_b = mlp_cast(hs, hp, norm2, w_gate_up, w_down, final_g, l, tp=_MLP_TILE, tf=256,
                                  final_norm=fin)
        hs = first[:ns]
        hp = mlp_rest(first, hp, norm2, w_mlp_b, final_g, l, skip_rows=ns, tm=_MLP_TILE, tf=512,
                      final_norm=fin).reshape(bp, lp, D_MODEL)
    pool_p, conv_p, ret_p, sgu_p = prompt_states
    pool_s, conv_s, ret_s, sgu_s = sample_states
    pool_s = jnp.transpose(pool_s, (0, 2, 1, 3))
    return (hp, hs.reshape(ns, 1, D_MODEL), pool_p, pool_s, conv_p, conv_s, ret_p, ret_s, sgu_p, sgu_s)
```

```python
import functools

import jax
import jax.numpy as jnp
from jax import lax
from jax.experimental import pallas as pl
from jax.experimental.pallas import tpu as pltpu

F32 = jnp.float32
BF16 = jnp.bfloat16

D_MODEL = 2048
DEPTH = 2
GROUP = 512
N_SPLITS = 10
IN_WIDTH = N_SPLITS * GROUP
POOL_WINDOWS = (2, 4, 8, 16)
POOL_BUF = 15
HEADS = 4
HEAD_DIM = 128
CHUNK = 128
ROPE_BASE = 10000.0
D_FF = 5632
NORM_EPS = 1e-6
PAST_LEN = 16384

_A, _BG, _CG, _HH, _Q, _K, _V, _G, _U, _VV = (i * GROUP for i in range(N_SPLITS))
_YA, _YB, _YC, _YD = (i * GROUP for i in range(4))

MIB = 1024 * 1024
VMEM_BYTES_V7X = 64 * MIB


VMEM_LIMIT_MAX = VMEM_BYTES_V7X - 4 * MIB


def _vmem_limit(estimate):
    return min(estimate + 4 * MIB, VMEM_LIMIT_MAX)


def _rms(x, g):
    ms = jnp.mean(x * x, axis=-1, keepdims=True)
    return x * lax.rsqrt(ms + NORM_EPS) * g


def _layernorm(x):
    mu = jnp.mean(x, axis=-1, keepdims=True)
    xc = x - mu
    return xc * lax.rsqrt(jnp.mean(xc * xc, axis=-1, keepdims=True) + NORM_EPS)


def _silu(x):
    return x * jax.nn.sigmoid(x)


def _dot(a, b):
    return jnp.dot(a, b, preferred_element_type=F32)


def _rotate(x, cos, sin_signed):
    return x * cos + pltpu.roll(x, HEAD_DIM // 2, 1) * sin_signed


def _weight_specs(w, layer, k, tn):
    if w.ndim == 3:
        return pl.BlockSpec((None, k, tn), lambda j: (layer, 0, j)), True
    return pl.BlockSpec((k, tn), lambda j: (0, j)), False


def _norm_matmul_kernel(x_ref, g_ref, w_ref, o_ref, *rest, emit):
    xn_ref = rest[-1]

    @pl.when(pl.program_id(0) == 0)
    def _():
        xn_ref[...] = _rms(x_ref[...], g_ref[...]).astype(BF16)

    wb = w_ref[...].astype(BF16)
    if emit:
        rest[0][...] = wb
    o_ref[...] = _dot(xn_ref[...], wb)


def norm_matmul(x, g, w, layer, *, tn):
    m, k = x.shape
    n = w.shape[-1]
    w_spec, emit = _weight_specs(w, layer, k, tn)
    vmem = 2 * k * tn * (4 + 2) + 2 * m * (k + tn) * 4 + m * k * 2
    outs = pl.pallas_call(
        functools.partial(_norm_matmul_kernel, emit=emit),
        grid=(n // tn,),
        in_specs=[
            pl.BlockSpec((m, k), lambda j: (0, 0)),
            pl.BlockSpec((None, 1, k), lambda j: (layer, 0, 0)),
            w_spec,
        ],
        out_specs=[pl.BlockSpec((m, tn), lambda j: (0, j))] + [pl.BlockSpec((k, tn), lambda j: (0, j))] * emit,
        out_shape=[jax.ShapeDtypeStruct((m, n), F32)] + [jax.ShapeDtypeStruct((k, n), BF16)] * emit,
        scratch_shapes=[pltpu.VMEM((m, k), BF16)],
        compiler_params=pltpu.CompilerParams(
            dimension_semantics=("arbitrary",),
            vmem_limit_bytes=_vmem_limit(vmem)),
        name="norm_matmul",
    )(x, g, w)
    return tuple(outs) if emit else (outs[0], w)


def _matmul_residual_kernel(a_ref, w_ref, r_ref, o_ref, *rest, emit):
    wb = w_ref[...].astype(BF16)
    if emit:
        rest[0][...] = wb
    o_ref[...] = r_ref[...] + _dot(a_ref[...], wb)


def matmul_residual(a, w, res, layer, *, tn):
    m, k = a.shape
    n = w.shape[-1]
    w_spec, emit = _weight_specs(w, layer, k, tn)
    vmem = 2 * k * tn * (4 + 2) + 2 * m * (k * 2 + 2 * tn * 4)
    outs = pl.pallas_call(
        functools.partial(_matmul_residual_kernel, emit=emit),
        grid=(n // tn,),
        in_specs=[
            pl.BlockSpec((m, k), lambda j: (0, 0)),
            w_spec,
            pl.BlockSpec((m, tn), lambda j: (0, j)),
        ],
        out_specs=[pl.BlockSpec((m, tn), lambda j: (0, j))] + [pl.BlockSpec((k, tn), lambda j: (0, j))] * emit,
        out_shape=[jax.ShapeDtypeStruct((m, n), F32)] + [jax.ShapeDtypeStruct((k, n), BF16)] * emit,
        compiler_params=pltpu.CompilerParams(
            dimension_semantics=("arbitrary",),
            vmem_limit_bytes=_vmem_limit(vmem)),
        name="matmul_residual",
    )(a, w, res)
    return tuple(outs) if emit else (outs[0], w)


_MLP_TILE = 1024


def _mlp_step(first, last, x_refs, g_ref, weights, fg_ref, o_ref, hn_ref, final_norm):
    def delta():
        wg, wu, wd = weights()
        hn = hn_ref[...]
        act = (_silu(_dot(hn, wg)) * _dot(hn, wu)).astype(BF16)
        return _dot(act, wd)

    @pl.when(first)
    def _():
        row = 0
        for x_ref in x_refs:
            rows = slice(row, row + x_ref.shape[0])
            hn_ref[rows, :] = _rms(x_ref[...], g_ref[...]).astype(BF16)
            row += x_ref.shape[0]
        d = delta()
        row = 0
        for x_ref in x_refs:
            rows = slice(row, row + x_ref.shape[0])
            o_ref[rows, :] = x_ref[...] + d[rows, :]
            row += x_ref.shape[0]

    @pl.when(jnp.logical_not(first))
    def _():
        o_ref[...] += delta()

    if final_norm:
        @pl.when(last)
        def _():
            o_ref[...] = _rms(o_ref[...], fg_ref[...])


def _mlp_cast_kernel(xs_ref, xp_ref, g_ref, wg_ref, wu_ref, wd_ref, fg_ref,
                     o_ref, wgb_ref, wub_ref, wdb_ref, hn_ref, *, final_norm):
    f = pl.program_id(0)

    def cast_weights():
        wg, wu, wd = wg_ref[...].astype(BF16), wu_ref[...].astype(BF16), wd_ref[...].astype(BF16)
        wgb_ref[...] = wg
        wub_ref[...] = wu
        wdb_ref[...] = wd
        return wg, wu, wd

    _mlp_step(f == 0, f == pl.num_programs(0) - 1, (xs_ref, xp_ref), g_ref, cast_weights, fg_ref,
              o_ref, hn_ref, final_norm)


def _mlp_rest_kernel(first_hbm, x_ref, g_ref, wg_ref, wu_ref, wd_ref, fg_ref, o_ref, hn_ref, sem,
                     *, nf, skip_rows, final_norm):
    s = pl.program_id(0)
    tm = o_ref.shape[0]

    @pl.when(s == 0)
    def _():
        copy = pltpu.make_async_copy(first_hbm.at[pl.ds(skip_rows, tm), :], o_ref, sem)
        copy.start()
        copy.wait()

    @pl.when(s > 0)
    def _():
        f = (s - 1) % nf
        _mlp_step(f == 0, f == nf - 1, (x_ref,), g_ref, lambda: (wg_ref[...], wu_ref[...], wd_ref[...]),
                  fg_ref, o_ref, hn_ref, final_norm)


def mlp_cast(xs, xp, g, w_gate_up, w_down, final_g, layer, *, tp, tf, final_norm):
    ns, d = xs.shape
    m = ns + tp
    nf = D_FF // tf
    once = pl.Buffered(1)
    vmem = 2 * m * d * 4 + m * d * 2 + 2 * 3 * d * tf * (4 + 2) + 3 * m * tf * 4
    outs = pl.pallas_call(
        functools.partial(_mlp_cast_kernel, final_norm=final_norm),
        grid=(nf,),
        in_specs=[
            pl.BlockSpec((ns, d), lambda f: (0, 0), pipeline_mode=once),
            pl.BlockSpec((tp, d), lambda f: (0, 0), pipeline_mode=once),
            pl.BlockSpec((None, 1, d), lambda f: (layer, 0, 0)),
            pl.BlockSpec((None, d, tf), lambda f: (layer, 0, f)),
            pl.BlockSpec((None, d, tf), lambda f: (layer, 0, nf + f)),
            pl.BlockSpec((None, tf, d), lambda f: (layer, f, 0)),
            pl.BlockSpec((1, d), lambda f: (0, 0)),
        ],
        out_specs=[
            pl.BlockSpec((m, d), lambda f: (0, 0), pipeline_mode=once),
            pl.BlockSpec((d, tf), lambda f: (0, f)),
            pl.BlockSpec((d, tf), lambda f: (0, f)),
            pl.BlockSpec((tf, d), lambda f: (f, 0)),
        ],
        out_shape=[jax.ShapeDtypeStruct((m, d), F32),
                   jax.ShapeDtypeStruct((d, D_FF), BF16), jax.ShapeDtypeStruct((d, D_FF), BF16),
                   jax.ShapeDtypeStruct((D_FF, d), BF16)],
        scratch_shapes=[pltpu.VMEM((m, d), BF16)],
        compiler_params=pltpu.CompilerParams(
            dimension_semantics=("arbitrary",),
            vmem_limit_bytes=_vmem_limit(vmem)),
        name="mlp_cast",
    )(xs, xp, g, w_gate_up, w_gate_up, w_down, final_g)
    return outs[0], tuple(outs[1:])


def mlp_rest(first, x, g, weights, final_g, layer, *, skip_rows, tm, tf, final_norm):
    m, d = x.shape
    nf = D_FF // tf
    step = lambda s: jnp.maximum(s - 1, 0)
    x_tile = lambda s: 1 + step(s) // nf
    f_tile = lambda s: step(s) % nf
    vmem = 2 * 2 * tm * d * 4 + tm * d * 2 + 2 * 3 * d * tf * 2 + 3 * tm * tf * 4
    return pl.pallas_call(
        functools.partial(_mlp_rest_kernel, nf=nf, skip_rows=skip_rows, final_norm=final_norm),
        grid=(1 + (m // tm - 1) * nf,),
        in_specs=[
            pl.BlockSpec(memory_space=pl.ANY),
            pl.BlockSpec((tm, d), lambda s: (x_tile(s), 0)),
            pl.BlockSpec((None, 1, d), lambda s: (layer, 0, 0)),
            pl.BlockSpec((d, tf), lambda s: (0, f_tile(s))),
            pl.BlockSpec((d, tf), lambda s: (0, f_tile(s))),
            pl.BlockSpec((tf, d), lambda s: (f_tile(s), 0)),
            pl.BlockSpec((1, d), lambda s: (0, 0)),
        ],
        out_specs=pl.BlockSpec((tm, d), lambda s: (jnp.where(s == 0, 0, x_tile(s)), 0)),
        out_shape=jax.ShapeDtypeStruct((m, d), F32),
        scratch_shapes=[pltpu.VMEM((tm, d), BF16), pltpu.SemaphoreType.DMA(())],
        compiler_params=pltpu.CompilerParams(
            dimension_semantics=("arbitrary",),
            vmem_limit_bytes=_vmem_limit(vmem)),
        name="mlp_rest",
    )(first, x, g, *weights, final_g)


_POOL_TOP = 16
_CONV_TOP = 8


def _block_prompt_kernel(x_ref, g1_ref, win_ref, wout_ref, cos_ref, sin_ref, dmask_ref, qdec_ref, kdec_ref,
                         poolw_ref, pscale_ref, convw_ref, retg_ref, sgug_ref, sguw_ref, sgub_ref,
                         *rest, start, tm, cast_next, layer, all_layers):
    *rest, pool_ext, conv_ext = rest
    if cast_next:
        win_next_ref, wout_next_ref, *rest, win_next_out, wout_next_out = rest
        win_next_out[...] = win_next_ref[...].astype(BF16)
        wout_next_out[...] = wout_next_ref[...].astype(BF16)
    h_out, pool_out, conv_out, ret_out, sgu_out = rest[-5:]
    stacked = (pool_out, conv_out, ret_out, sgu_out)
    if all_layers:
        pool_out, conv_out, ret_out, sgu_out = (ref.at[layer] for ref in stacked)
    c = pl.program_id(1)
    T = CHUNK
    n_chunks = tm // T

    @pl.when(c == 0)
    def _():
        pool_ext[0:_POOL_TOP, :] = jnp.zeros((_POOL_TOP, GROUP), F32)
        conv_ext[0:_CONV_TOP, :] = jnp.zeros((_CONV_TOP, GROUP), F32)
        ret_out[...] = jnp.zeros(ret_out.shape, F32)
        if all_layers:
            for ref in stacked:
                for other in range(DEPTH):
                    if other != layer:
                        ref[other] = jnp.zeros(ref.shape[1:], F32)

    x = x_ref[0]
    xn = _rms(x, g1_ref[...]).astype(BF16)

    def proj(col):
        return _dot(xn, win_ref[:, col:col + GROUP])

    def out_proj(y, col):
        return _dot(y, wout_ref[col:col + GROUP, :])

    head_lanes = [slice(hd * HEAD_DIM, (hd + 1) * HEAD_DIM) for hd in range(HEADS)]
    chunk_rows = [slice(j * T, (j + 1) * T) for j in range(n_chunks)]

    def pool_mixer(a_proj):
        pool_ext[_POOL_TOP:_POOL_TOP + tm, :] = a_proj
        pos1 = lax.broadcasted_iota(jnp.int32, (tm, HEAD_DIM), 0) + (start + 1) + c * tm
        y = []
        for gi, win in enumerate(POOL_WINDOWS):
            lanes = head_lanes[gi]
            a = pool_ext[_POOL_TOP:_POOL_TOP + tm, lanes]
            wsum = a
            for j in range(1, win):
                wsum = wsum + pool_ext[_POOL_TOP - j:_POOL_TOP - j + tm, lanes]
            cnt = jnp.minimum(pos1, win).astype(F32)
            d = (wsum / cnt - a).astype(BF16)
            y.append((_dot(d, poolw_ref[gi]) * pscale_ref[:, lanes]).astype(BF16))
        pool_out[0] = pool_ext[_POOL_TOP + tm - POOL_BUF:_POOL_TOP + tm, :]
        pool_ext[_POOL_TOP - POOL_BUF:_POOL_TOP, :] = pool_ext[_POOL_TOP + tm - POOL_BUF:_POOL_TOP + tm, :]
        return jnp.concatenate(y, axis=-1)

    def conv_mixer(bg, cg, hh):
        z = cg * hh
        conv_ext[_CONV_TOP:_CONV_TOP + tm, :] = z
        acc = (conv_ext[_CONV_TOP - 2:_CONV_TOP - 2 + tm, :] * convw_ref[0:1, :]
               + conv_ext[_CONV_TOP - 1:_CONV_TOP - 1 + tm, :] * convw_ref[1:2, :]
               + z * convw_ref[2:3, :])
        conv_out[0] = conv_ext[_CONV_TOP + tm - 2:_CONV_TOP + tm, :]
        conv_ext[_CONV_TOP - 2:_CONV_TOP, :] = conv_ext[_CONV_TOP + tm - 2:_CONV_TOP + tm, :]
        return (bg * acc).astype(BF16)

    def retention(q_all, k_all, v_all, g_all, j, hd):
        rows, lanes = chunk_rows[j], head_lanes[hd]
        cos = cos_ref[rows, :]
        sin = sin_ref[rows, :]
        q = _rotate(q_all[rows, lanes], cos, sin)
        k = _rotate(k_all[rows, lanes], cos, sin) * (HEAD_DIM ** -0.5)
        v = v_all[rows, lanes].astype(BF16)
        s_prev = ret_out[0, hd]
        scores = lax.dot_general(q.astype(BF16), k.astype(BF16), (((1,), (1,)), ((), ())),
                                 preferred_element_type=F32) * dmask_ref[hd]
        o = _dot(scores.astype(BF16), v) + _dot((q * qdec_ref[hd]).astype(BF16), s_prev.astype(BF16))
        kv = lax.dot_general((k * kdec_ref[hd]).astype(BF16), v, (((0,), (0,)), ((), ())),
                             preferred_element_type=F32)
        ret_out[0, hd] = qdec_ref[hd, T - 1:T, :] * s_prev + kv
        on = _layernorm(o) * retg_ref[:, lanes]
        return (_silu(g_all[rows, lanes]) * on).astype(BF16)

    def sgu_mixer(u_all, vn, j):
        rows = chunk_rows[j]
        tri = lax.broadcasted_iota(jnp.int32, (T, T), 0) >= lax.broadcasted_iota(jnp.int32, (T, T), 1)
        y = []
        for hd in range(HEADS):
            lanes = head_lanes[hd]
            w = jnp.where(tri, sguw_ref[hd], 0.0).astype(BF16)
            mixed = _dot(w, vn[rows, lanes].astype(BF16)) + sgub_ref[hd]
            y.append((u_all[rows, lanes] * mixed).astype(BF16))
        return jnp.concatenate(y, axis=-1)

    q_all, k_all, v_all, g_all = proj(_Q), proj(_K), proj(_V), proj(_G)
    ret = functools.partial(retention, q_all, k_all, v_all, g_all)
    y_c = [[None] * HEADS for _ in range(n_chunks)]
    y_c[0][0] = ret(0, 0)
    a_proj = proj(_A)
    y_c[0][1] = ret(0, 1)
    cg = proj(_CG)
    y_c[0][2] = ret(0, 2)
    hh = proj(_HH)
    y_c[0][3] = ret(0, 3)
    bg = proj(_BG)
    later = iter([lambda: proj(_VV), lambda: proj(_U), lambda: pool_mixer(a_proj),
                  lambda: conv_mixer(bg, cg, hh)])
    fills = []
    for j in range(1, n_chunks):
        for hd in range(HEADS):
            y_c[j][hd] = ret(j, hd)
            nxt = next(later, None)
            if nxt is not None:
                fills.append(nxt())
    fills += [f() for f in later]
    vv, u_all, y_a, y_b = fills
    h = x + out_proj(y_a, _YA)
    vn = _layernorm(vv) * sgug_ref[...]
    y_d = [sgu_mixer(u_all, vn, 0)]
    h = h + out_proj(y_b, _YB)
    y_d += [sgu_mixer(u_all, vn, j) for j in range(1, n_chunks)]
    h = h + out_proj(jnp.concatenate([jnp.concatenate(r, axis=-1) for r in y_c], axis=0), _YC)
    h_out[0] = h + out_proj(jnp.concatenate(y_d, axis=0), _YD)
    sgu_out[0] = vn[tm - T:tm, :]


_N_BLOCK_INPUTS = 16


def block_prompt(x, norm_g, w_in, w_out, tables, params, layer, next_weights, stacked, *, start, tm):
    b, l, d = x.shape
    nc = l // tm
    cos, sin, dmask, qdec, kdec = tables
    poolw, pscale, convw, retg, sgug, sguw, sgub = params
    const3 = lambda shape: pl.BlockSpec(shape, lambda i, c: (0, 0, 0))
    lay3 = lambda shape: pl.BlockSpec((None,) + shape, lambda i, c: (layer, 0, 0))
    lay4 = lambda shape: pl.BlockSpec((None,) + shape, lambda i, c: (layer, 0, 0, 0))
    resident = lambda shape: pl.BlockSpec(shape, lambda i, c: (0, 0), pipeline_mode=pl.Buffered(1))
    cast_next = next_weights is not None
    next_in_specs, next_out_specs, next_out_shape, next_args = [], [], [], ()
    next_bytes = 0
    if cast_next:
        for w in next_weights:
            rows, cols = w.shape[1] // (b * nc), w.shape[2]
            assert rows * b * nc == w.shape[1] and rows % 16 == 0
            next_in_specs.append(pl.BlockSpec((None, rows, cols), lambda i, c: (layer + 1, i * nc + c, 0)))
            next_out_specs.append(pl.BlockSpec((rows, cols), lambda i, c: (i * nc + c, 0)))
            next_out_shape.append(jax.ShapeDtypeStruct(w.shape[1:], BF16))
            next_bytes += 2 * rows * cols * (4 + 2)
        next_args = tuple(next_weights)
    state_dims = [(POOL_BUF, GROUP), (2, GROUP), (HEADS, HEAD_DIM, HEAD_DIM), (CHUNK, GROUP)]
    first = stacked is None
    if first:
        state_specs = [pl.BlockSpec((DEPTH, 1) + dims, lambda i, c, n=len(dims): (0, i) + (0,) * n)
                       for dims in state_dims]
        alias_specs, alias_args, aliases = [], (), {}
    else:
        state_specs = [pl.BlockSpec((None, 1) + dims, lambda i, c, n=len(dims): (layer, i) + (0,) * n)
                       for dims in state_dims]
        alias_specs = [pl.BlockSpec(memory_space=pl.ANY)] * len(stacked)
        alias_args = tuple(stacked)
        n_in = _N_BLOCK_INPUTS + len(next_in_specs)
        aliases = {n_in + k: 1 + k for k in range(len(stacked))}
    vmem = (d * IN_WIDTH + 4 * GROUP * d) * 2 + 2 * 2 * tm * d * 4 + 2 * tm * IN_WIDTH * 4 + next_bytes
    outs = pl.pallas_call(
        functools.partial(_block_prompt_kernel, start=start, tm=tm, cast_next=cast_next, layer=layer,
                          all_layers=first),
        grid=(b, nc),
        in_specs=[
            pl.BlockSpec((1, tm, d), lambda i, c: (i, c, 0)),
            lay3((1, d)),
            resident((d, IN_WIDTH)),
            resident((4 * GROUP, d)),
            pl.BlockSpec((tm, HEAD_DIM), lambda i, c: (c, 0)),
            pl.BlockSpec((tm, HEAD_DIM), lambda i, c: (c, 0)),
            const3((HEADS, CHUNK, CHUNK)),
            const3((HEADS, CHUNK, HEAD_DIM)),
            const3((HEADS, CHUNK, HEAD_DIM)),
            lay4((len(POOL_WINDOWS), HEAD_DIM, HEAD_DIM)),
            lay3((1, GROUP)),
            lay3((3, GROUP)),
            lay3((1, GROUP)),
            lay3((1, GROUP)),
            lay4((HEADS, CHUNK, CHUNK)),
            lay4((HEADS, CHUNK, HEAD_DIM)),
            *next_in_specs,
            *alias_specs,
        ],
        out_specs=[pl.BlockSpec((1, tm, d), lambda i, c: (i, c, 0)), *state_specs, *next_out_specs],
        out_shape=[
            jax.ShapeDtypeStruct((b, l, d), F32),
            *[jax.ShapeDtypeStruct((DEPTH, b) + dims, F32) for dims in state_dims],
            *next_out_shape,
        ],
        scratch_shapes=[
            pltpu.VMEM((_POOL_TOP + tm, GROUP), F32),
            pltpu.VMEM((_CONV_TOP + tm, GROUP), F32),
        ],
        input_output_aliases=aliases,
        compiler_params=pltpu.CompilerParams(
            dimension_semantics=("arbitrary", "arbitrary"),
            vmem_limit_bytes=_vmem_limit(vmem)),
        name="block_prompt",
    )(x, norm_g, w_in, w_out, cos, sin, dmask, qdec, kdec, poolw, pscale, convw, retg, sgug, sguw, sgub,
      *next_args, *alias_args)
    return outs[0], tuple(outs[1:5]), (tuple(outs[5:]) if cast_next else None)


_SEQ_BLOCK = 16


_N_SAMPLE_INPUTS = 14


def _mixer_sample_kernel(*refs, start, layer, all_layers):
    (p_ref, pool_ref, conv_ref, ret_ref, cos_ref, sin_ref, gam_ref,
     poolw_ref, pscale_ref, convw_ref, retg_ref, sgug_ref, sguw_ref, sgub_ref) = refs[:_N_SAMPLE_INPUTS]
    mix_ref, pool_out, conv_out, ret_out, sgu_out, qg_s, k_s, v_s, os_s = refs[-9:]
    TB = _SEQ_BLOCK

    if all_layers:
        stacked = (pool_out, conv_out, ret_out, sgu_out)
        for ref in stacked:
            for other in range(DEPTH):
                if other != layer:
                    ref[other] = jnp.zeros(ref.shape[1:], F32)
        pool_out, conv_out, ret_out, sgu_out = (ref.at[layer] for ref in stacked)

    a_all = p_ref[:, _A:_A + GROUP]
    for gi, win in enumerate(POOL_WINDOWS):
        lanes = slice(gi * HEAD_DIM, (gi + 1) * HEAD_DIM)
        a = a_all[:, lanes]
        wsum = a
        for j in range(1, win):
            wsum = wsum + pool_ref[POOL_BUF - j, :, lanes]
        cnt = float(min(start + 1, win))
        d = (wsum / cnt - a).astype(BF16)
        y = _dot(d, poolw_ref[gi]) * pscale_ref[:, lanes]
        mix_ref[:, _YA + gi * HEAD_DIM:_YA + (gi + 1) * HEAD_DIM] = y.astype(BF16)
    pool_out[0:POOL_BUF - 1] = pool_ref[1:POOL_BUF]
    pool_out[POOL_BUF - 1] = a_all

    z = p_ref[:, _CG:_CG + GROUP] * p_ref[:, _HH:_HH + GROUP]
    acc = (conv_ref[:, 0, :] * convw_ref[0:1, :] + conv_ref[:, 1, :] * convw_ref[1:2, :]
           + z * convw_ref[2:3, :])
    mix_ref[:, _YB:_YB + GROUP] = (p_ref[:, _BG:_BG + GROUP] * acc).astype(BF16)
    conv_out[:, 0, :] = conv_ref[:, 1, :]
    conv_out[:, 1, :] = z

    cos = cos_ref[...]
    sin = sin_ref[...]
    scores = []
    for h in range(HEADS):
        lanes = slice(h * HEAD_DIM, (h + 1) * HEAD_DIM)
        q = _rotate(p_ref[:, _Q + h * HEAD_DIM:_Q + (h + 1) * HEAD_DIM], cos, sin)
        k = _rotate(p_ref[:, _K + h * HEAD_DIM:_K + (h + 1) * HEAD_DIM], cos, sin) * (HEAD_DIM ** -0.5)
        scores.append(jnp.sum(q * k, axis=-1, keepdims=True))
        qg_s[:, lanes] = q * gam_ref[:, lanes]
        k_s[:, lanes] = k
    v_s[...] = p_ref[:, _V:_V + GROUP]

    first_row = lax.broadcasted_iota(jnp.int32, (8, HEAD_DIM), 0) == 0

    for b in range(TB):
        for h in range(HEADS):
            lanes = slice(h * HEAD_DIM, (h + 1) * HEAD_DIM)
            s_prev = ret_ref[b, h]
            q8 = jnp.broadcast_to(qg_s[b:b + 1, lanes], (8, HEAD_DIM))
            os_s[b:b + 1, lanes] = _dot(q8.astype(BF16), s_prev.astype(BF16))[0:1, :]
            k8 = jnp.where(first_row, jnp.broadcast_to(k_s[b:b + 1, lanes], (8, HEAD_DIM)), 0.0)
            v8 = jnp.broadcast_to(v_s[b:b + 1, lanes], (8, HEAD_DIM))
            kv = lax.dot_general(k8.astype(BF16), v8.astype(BF16), (((0,), (0,)), ((), ())),
                                 preferred_element_type=F32)
            ret_out[b, h] = gam_ref[:, lanes] * s_prev + kv

    for h in range(HEADS):
        lanes = slice(h * HEAD_DIM, (h + 1) * HEAD_DIM)
        o = scores[h] * v_s[:, lanes] + os_s[:, lanes]
        on = _layernorm(o) * retg_ref[:, lanes]
        g = p_ref[:, _G + h * HEAD_DIM:_G + (h + 1) * HEAD_DIM]
        mix_ref[:, _YC + h * HEAD_DIM:_YC + (h + 1) * HEAD_DIM] = (_silu(g) * on).astype(BF16)

    vn = _layernorm(p_ref[:, _VV:_VV + GROUP]) * sgug_ref[...]
    mix_ref[:, _YD:_YD + GROUP] = (p_ref[:, _U:_U + GROUP] * (sguw_ref[...] * vn + sgub_ref[...])).astype(BF16)
    sgu_out[:, 0, :] = vn


def mixer_sample(p, pool_state, conv_state, ret_state, tables, params, layer, stacked, *, start):
    n = p.shape[0]
    tb = _SEQ_BLOCK
    cos, sin, gam = tables
    poolw, pscale, convw, retg, sgug, sguw, sgub = params
    rows = lambda width: pl.BlockSpec((tb, width), lambda i: (i, 0))
    lay_rows = lambda r: pl.BlockSpec((None, tb, r, GROUP), lambda i: (layer, i, 0, 0))
    lay_pool = pl.BlockSpec((None, POOL_BUF, tb, GROUP), lambda i: (layer, 0, i, 0))
    lay_ret = pl.BlockSpec((None, tb, HEADS, HEAD_DIM, HEAD_DIM), lambda i: (layer, i, 0, 0, 0))
    const2 = lambda shape: pl.BlockSpec(shape, lambda i: (0, 0))
    lay3 = lambda shape: pl.BlockSpec((None,) + shape, lambda i: (layer, 0, 0))
    lay4 = lambda shape: pl.BlockSpec((None,) + shape, lambda i: (layer, 0, 0, 0))
    in_specs = [
        rows(IN_WIDTH),
        lay_pool,
        lay_rows(2),
        lay_ret,
        const2((1, HEAD_DIM)),
        const2((1, HEAD_DIM)),
        const2((1, GROUP)),
        lay4((len(POOL_WINDOWS), HEAD_DIM, HEAD_DIM)),
        lay3((1, GROUP)),
        lay3((3, GROUP)),
        lay3((1, GROUP)),
        lay3((1, GROUP)),
        lay3((1, GROUP)),
        lay3((1, GROUP)),
    ]
    args = (p, pool_state, conv_state, ret_state, cos, sin, gam,
            poolw, pscale, convw, retg, sgug, sguw, sgub)
    assert len(args) == _N_SAMPLE_INPUTS
    aliases = {}
    first = stacked is None
    state_block = tb * (16 + 8 + 8 + HEAD_DIM) * GROUP * 4
    vmem = 2 * (tb * IN_WIDTH * 4 + state_block * (1 + (DEPTH if first else 1)))
    if first:
        all_rows = lambda r: pl.BlockSpec((DEPTH, tb, r, GROUP), lambda i: (0, i, 0, 0))
        state_specs = [pl.BlockSpec((DEPTH, POOL_BUF, tb, GROUP), lambda i: (0, 0, i, 0)), all_rows(2),
                       pl.BlockSpec((DEPTH, tb, HEADS, HEAD_DIM, HEAD_DIM), lambda i: (0, i, 0, 0, 0)),
                       all_rows(1)]
    else:
        state_specs = [lay_pool, lay_rows(2), lay_ret, lay_rows(1)]
        aliases = {len(args) + k: 1 + k for k in range(len(stacked))}
        in_specs = in_specs + [pl.BlockSpec(memory_space=pl.ANY)] * len(stacked)
        args = args + tuple(stacked)
    outs = pl.pallas_call(
        functools.partial(_mixer_sample_kernel, start=start, layer=layer, all_layers=first),
        grid=(n // tb,),
        in_specs=in_specs,
        out_specs=[rows(4 * GROUP)] + state_specs,
        out_shape=[
            jax.ShapeDtypeStruct((n, 4 * GROUP), BF16),
            jax.ShapeDtypeStruct((DEPTH, POOL_BUF, n, GROUP), F32),
            jax.ShapeDtypeStruct((DEPTH, n, 2, GROUP), F32),
            jax.ShapeDtypeStruct((DEPTH, n, HEADS, HEAD_DIM, HEAD_DIM), F32),
            jax.ShapeDtypeStruct((DEPTH, n, 1, GROUP), F32),
        ],
        scratch_shapes=[pltpu.VMEM((tb, GROUP), F32)] * 4,
        input_output_aliases=aliases,
        compiler_params=pltpu.CompilerParams(
            dimension_semantics=("arbitrary",),
            vmem_limit_bytes=_vmem_limit(vmem)),
        name="mixer_sample",
    )(*args)
    return outs[0], tuple(outs[1:])


def _rope_tables(start, length):
    half = HEAD_DIM // 2
    inv = ROPE_BASE ** (-jnp.arange(half, dtype=F32) / half)
    pos = start + jnp.arange(length)
    ang = pos.astype(F32)[:, None] * inv[None, :]
    cos, sin = jnp.cos(ang), jnp.sin(ang)
    return jnp.concatenate([cos, cos], axis=-1), jnp.concatenate([-sin, sin], axis=-1)


def _log_gamma():
    return jnp.log1p(-(2.0 ** (-5.0 - jnp.arange(HEADS, dtype=F32))))


def _decay_tables(c):
    lg = _log_gamma()
    idx = jnp.arange(c, dtype=F32)
    diff = idx[:, None] - idx[None, :]
    dmask = jnp.where(diff >= 0, jnp.exp(jnp.maximum(diff, 0.0)[None] * lg[:, None, None]), 0.0)
    qdec = jnp.exp((idx + 1.0)[None, :] * lg[:, None])[:, :, None]
    kdec = jnp.exp((c - 1.0 - idx)[None, :] * lg[:, None])[:, :, None]
    wide = lambda t: jnp.broadcast_to(t, (HEADS, c, HEAD_DIM))
    return dmask, wide(qdec), wide(kdec)


def _per_head_lanes(t):
    return jnp.repeat(t, HEAD_DIM, axis=-1)[..., None, :]


def kernel(x_prompt, x_sample, state_pool, state_conv, state_ret, norm1_g, w_in, pool_w, pool_scale,
           conv_w, ret_norm_g, sgu_norm_g, sgu_w, sgu_b, w_out, norm2_g, w_gate_up, w_down,
           final_norm_g):
    bp, lp, _ = x_prompt.shape
    ns = x_sample.shape[0]

    pool_w_b = pool_w.astype(BF16)
    row3 = lambda t: t[:, None, :]
    norm1 = row3(norm1_g)
    norm2 = row3(norm2_g)
    final_g = final_norm_g[None, :]

    cos_p, sin_p = _rope_tables(0, lp)
    prompt_tables = (cos_p, sin_p) + _decay_tables(CHUNK)
    sgu_b_wide = jnp.broadcast_to(sgu_b[:, :, :, None], sgu_b.shape + (HEAD_DIM,))
    prompt_params = (pool_w_b, row3(pool_scale), conv_w, row3(ret_norm_g), row3(sgu_norm_g), sgu_w, sgu_b_wide)

    cos_s, sin_s = _rope_tables(PAST_LEN, 1)
    gam = _per_head_lanes(jnp.exp(_log_gamma()))
    sample_tables = (cos_s, sin_s, gam)
    sample_params = (pool_w_b, row3(pool_scale), conv_w, row3(ret_norm_g), row3(sgu_norm_g),
                     _per_head_lanes(sgu_w[:, :, 0, 0]), _per_head_lanes(sgu_b[:, :, 0]))
    pool_rows = jnp.transpose(state_pool, (0, 2, 1, 3))
    hp = x_prompt
    hs = x_sample.reshape(ns, D_MODEL)
    prompt_states = None
    sample_states = None
    next_b = None
    for l in range(DEPTH):
        fin = l == DEPTH - 1
        w_in_l, w_out_l = (w_in, w_out) if next_b is None else next_b
        p, w_in_b = norm_matmul(hs, norm1, w_in_l, l, tn=512)
        mix, sample_states = mixer_sample(p, pool_rows, state_conv, state_ret, sample_tables,
                                          sample_params, l, sample_states, start=PAST_LEN)
        hs, w_out_b = matmul_residual(mix, w_out_l, hs, l, tn=1024)
        hp, prompt_states, next_b = block_prompt(hp, norm1, w_in_b, w_out_b, prompt_tables, prompt_params,
                                                 l, None if fin else (w_in, w_out), prompt_states,
                                                 start=0, tm=256)
        hp = hp.reshape(bp * lp, D_MODEL)
        first, w_mlp_b = mlp_cast(hs, hp, norm2, w_gate_up, w_down, final_g, l, tp=_MLP_TILE, tf=256,
                                  final_norm=fin)
        hs = first[:ns]
        hp = mlp_rest(first, hp, norm2, w_mlp_b, final_g, l, skip_rows=ns, tm=_MLP_TILE, tf=512,
                      final_norm=fin).reshape(bp, lp, D_MODEL)
    pool_p, conv_p, ret_p, sgu_p = prompt_states
    pool_s, conv_s, ret_s, sgu_s = sample_states
    pool_s = jnp.transpose(pool_s, (0, 2, 1, 3))
    return (hp, hs.reshape(ns, 1, D_MODEL), pool_p, pool_s, conv_p, conv_s, ret_p, ret_s, sgu_p, sgu_s)
```

```python
import functools

import jax
import jax.numpy as jnp
from jax import lax
from jax.experimental import pallas as pl
from jax.experimental.pallas import tpu as pltpu

F32 = jnp.float32
BF16 = jnp.bfloat16

D_MODEL = 2048
DEPTH = 2
GROUP = 512
N_SPLITS = 10
IN_WIDTH = N_SPLITS * GROUP
POOL_WINDOWS = (2, 4, 8, 16)
POOL_BUF = 15
HEADS = 4
HEAD_DIM = 128
CHUNK = 128
ROPE_BASE = 10000.0
D_FF = 5632
NORM_EPS = 1e-6
PAST_LEN = 16384

_A, _BG, _CG, _HH, _Q, _K, _V, _G, _U, _VV = (i * GROUP for i in range(N_SPLITS))
_YA, _YB, _YC, _YD = (i * GROUP for i in range(4))

MIB = 1024 * 1024
VMEM_BYTES_V7X = 64 * MIB


VMEM_LIMIT_MAX = VMEM_BYTES_V7X - 4 * MIB


def _vmem_limit(estimate):
    return min(estimate + 4 * MIB, VMEM_LIMIT_MAX)


def _rms(x, g):
    ms = jnp.mean(x * x, axis=-1, keepdims=True)
    return x * lax.rsqrt(ms + NORM_EPS) * g


def _layernorm(x):
    mu = jnp.mean(x, axis=-1, keepdims=True)
    xc = x - mu
    return xc * lax.rsqrt(jnp.mean(xc * xc, axis=-1, keepdims=True) + NORM_EPS)


def _silu(x):
    return x * jax.nn.sigmoid(x)


def _dot(a, b):
    return jnp.dot(a, b, preferred_element_type=F32)


def _rotate(x, cos, sin_signed):
    return x * cos + pltpu.roll(x, HEAD_DIM // 2, 1) * sin_signed


def _weight_specs(w, layer, k, tn):
    if w.ndim == 3:
        return pl.BlockSpec((None, k, tn), lambda j: (layer, 0, j)), True
    return pl.BlockSpec((k, tn), lambda j: (0, j)), False


def _norm_matmul_kernel(x_ref, g_ref, w_ref, o_ref, *rest, emit):
    xn_ref = rest[-1]
    j = pl.program_id(0)
    tk = xn_ref.shape[-1]

    @pl.when(j == 0)
    def _():
        xn = _rms(x_ref[...], g_ref[...]).astype(BF16)
        for kb in range(xn_ref.shape[0]):
            xn_ref[kb] = xn[:, kb * tk:(kb + 1) * tk]

    wb = w_ref[...].astype(BF16)
    if emit:
        rest[0][...] = wb
    part = _dot(xn_ref[j], wb)

    @pl.when(j == 0)
    def _():
        o_ref[...] = part

    @pl.when(j > 0)
    def _():
        o_ref[...] += part


def norm_matmul(x, g, w, layer, *, tk):
    m, k = x.shape
    n = w.shape[-1]
    emit = w.ndim == 3
    w_spec = (pl.BlockSpec((None, tk, n), lambda j: (layer, j, 0)) if emit
              else pl.BlockSpec((tk, n), lambda j: (j, 0)))
    vmem = 2 * tk * n * (4 + 2) + 2 * m * (k + n) * 4 + m * k * 2
    outs = pl.pallas_call(
        functools.partial(_norm_matmul_kernel, emit=emit),
        grid=(k // tk,),
        in_specs=[
            pl.BlockSpec((m, k), lambda j: (0, 0)),
            pl.BlockSpec((None, 1, k), lambda j: (layer, 0, 0)),
            w_spec,
        ],
        out_specs=[pl.BlockSpec((m, n), lambda j: (0, 0))] + [pl.BlockSpec((tk, n), lambda j: (j, 0))] * emit,
        out_shape=[jax.ShapeDtypeStruct((m, n), F32)] + [jax.ShapeDtypeStruct((k, n), BF16)] * emit,
        scratch_shapes=[pltpu.VMEM((k // tk, m, tk), BF16)],
        compiler_params=pltpu.CompilerParams(
            dimension_semantics=("arbitrary",),
            vmem_limit_bytes=_vmem_limit(vmem)),
        name="norm_matmul",
    )(x, g, w)
    return tuple(outs) if emit else (outs[0], w)


def _matmul_residual_kernel(a_ref, w_ref, r_ref, o_ref, *rest, emit):
    wb = w_ref[...].astype(BF16)
    if emit:
        rest[0][...] = wb
    o_ref[...] = r_ref[...] + _dot(a_ref[...], wb)


def matmul_residual(a, w, res, layer, *, tn):
    m, k = a.shape
    n = w.shape[-1]
    w_spec, emit = _weight_specs(w, layer, k, tn)
    vmem = 2 * k * tn * (4 + 2) + 2 * m * (k * 2 + 2 * tn * 4)
    outs = pl.pallas_call(
        functools.partial(_matmul_residual_kernel, emit=emit),
        grid=(n // tn,),
        in_specs=[
            pl.BlockSpec((m, k), lambda j: (0, 0)),
            w_spec,
            pl.BlockSpec((m, tn), lambda j: (0, j)),
        ],
        out_specs=[pl.BlockSpec((m, tn), lambda j: (0, j))] + [pl.BlockSpec((k, tn), lambda j: (0, j))] * emit,
        out_shape=[jax.ShapeDtypeStruct((m, n), F32)] + [jax.ShapeDtypeStruct((k, n), BF16)] * emit,
        compiler_params=pltpu.CompilerParams(
            dimension_semantics=("arbitrary",),
            vmem_limit_bytes=_vmem_limit(vmem)),
        name="matmul_residual",
    )(a, w, res)
    return tuple(outs) if emit else (outs[0], w)


_MLP_TILE = 1024


def _mlp_step(first, last, x_refs, g_ref, weights, fg_ref, o_ref, hn_ref, final_norm):
    def delta():
        wg, wu, wd = weights()
        hn = hn_ref[...]
        act = (_silu(_dot(hn, wg)) * _dot(hn, wu)).astype(BF16)
        return _dot(act, wd)

    @pl.when(first)
    def _():
        row = 0
        for x_ref in x_refs:
            rows = slice(row, row + x_ref.shape[0])
            hn_ref[rows, :] = _rms(x_ref[...], g_ref[...]).astype(BF16)
            row += x_ref.shape[0]
        d = delta()
        row = 0
        for x_ref in x_refs:
            rows = slice(row, row + x_ref.shape[0])
            o_ref[rows, :] = x_ref[...] + d[rows, :]
            row += x_ref.shape[0]

    @pl.when(jnp.logical_not(first))
    def _():
        o_ref[...] += delta()

    if final_norm:
        @pl.when(last)
        def _():
            o_ref[...] = _rms(o_ref[...], fg_ref[...])


def _mlp_cast_kernel(xs_ref, xp_ref, g_ref, wg_ref, wu_ref, wd_ref, fg_ref,
                     o_ref, wgb_ref, wub_ref, wdb_ref, hn_ref, *, final_norm):
    f = pl.program_id(0)

    def cast_weights():
        wg, wu, wd = wg_ref[...].astype(BF16), wu_ref[...].astype(BF16), wd_ref[...].astype(BF16)
        wgb_ref[...] = wg
        wub_ref[...] = wu
        wdb_ref[...] = wd
        return wg, wu, wd

    _mlp_step(f == 0, f == pl.num_programs(0) - 1, (xs_ref, xp_ref), g_ref, cast_weights, fg_ref,
              o_ref, hn_ref, final_norm)


def _mlp_rest_kernel(first_hbm, x_ref, g_ref, wg_ref, wu_ref, wd_ref, fg_ref, o_ref, hn_ref, sem,
                     *, nf, skip_rows, final_norm):
    s = pl.program_id(0)
    tm = o_ref.shape[0]

    @pl.when(s == 0)
    def _():
        copy = pltpu.make_async_copy(first_hbm.at[pl.ds(skip_rows, tm), :], o_ref, sem)
        copy.start()
        copy.wait()

    @pl.when(s > 0)
    def _():
        f = (s - 1) % nf
        _mlp_step(f == 0, f == nf - 1, (x_ref,), g_ref, lambda: (wg_ref[...], wu_ref[...], wd_ref[...]),
                  fg_ref, o_ref, hn_ref, final_norm)


def mlp_cast(xs, xp, g, w_gate_up, w_down, final_g, layer, *, tp, tf, final_norm):
    ns, d = xs.shape
    m = ns + tp
    nf = D_FF // tf
    once = pl.Buffered(1)
    vmem = 2 * m * d * 4 + m * d * 2 + 2 * 3 * d * tf * (4 + 2) + 3 * m * tf * 4
    outs = pl.pallas_call(
        functools.partial(_mlp_cast_kernel, final_norm=final_norm),
        grid=(nf,),
        in_specs=[
            pl.BlockSpec((ns, d), lambda f: (0, 0), pipeline_mode=once),
            pl.BlockSpec((tp, d), lambda f: (0, 0), pipeline_mode=once),
            pl.BlockSpec((None, 1, d), lambda f: (layer, 0, 0)),
            pl.BlockSpec((None, d, tf), lambda f: (layer, 0, f)),
            pl.BlockSpec((None, d, tf), lambda f: (layer, 0, nf + f)),
            pl.BlockSpec((None, tf, d), lambda f: (layer, f, 0)),
            pl.BlockSpec((1, d), lambda f: (0, 0)),
        ],
        out_specs=[
            pl.BlockSpec((m, d), lambda f: (0, 0), pipeline_mode=once),
            pl.BlockSpec((d, tf), lambda f: (0, f)),
            pl.BlockSpec((d, tf), lambda f: (0, f)),
            pl.BlockSpec((tf, d), lambda f: (f, 0)),
        ],
        out_shape=[jax.ShapeDtypeStruct((m, d), F32),
                   jax.ShapeDtypeStruct((d, D_FF), BF16), jax.ShapeDtypeStruct((d, D_FF), BF16),
                   jax.ShapeDtypeStruct((D_FF, d), BF16)],
        scratch_shapes=[pltpu.VMEM((m, d), BF16)],
        compiler_params=pltpu.CompilerParams(
            dimension_semantics=("arbitrary",),
            vmem_limit_bytes=_vmem_limit(vmem)),
        name="mlp_cast",
    )(xs, xp, g, w_gate_up, w_gate_up, w_down, final_g)
    return outs[0], tuple(outs[1:])


def mlp_rest(first, x, g, weights, final_g, layer, *, skip_rows, tm, tf, final_norm):
    m, d = x.shape
    nf = D_FF // tf
    step = lambda s: jnp.maximum(s - 1, 0)
    x_tile = lambda s: 1 + step(s) // nf
    f_tile = lambda s: step(s) % nf
    vmem = 2 * 2 * tm * d * 4 + tm * d * 2 + 2 * 3 * d * tf * 2 + 3 * tm * tf * 4
    return pl.pallas_call(
        functools.partial(_mlp_rest_kernel, nf=nf, skip_rows=skip_rows, final_norm=final_norm),
        grid=(1 + (m // tm - 1) * nf,),
        in_specs=[
            pl.BlockSpec(memory_space=pl.ANY),
            pl.BlockSpec((tm, d), lambda s: (x_tile(s), 0)),
            pl.BlockSpec((None, 1, d), lambda s: (layer, 0, 0)),
            pl.BlockSpec((d, tf), lambda s: (0, f_tile(s))),
            pl.BlockSpec((d, tf), lambda s: (0, f_tile(s))),
            pl.BlockSpec((tf, d), lambda s: (f_tile(s), 0)),
            pl.BlockSpec((1, d), lambda s: (0, 0)),
        ],
        out_specs=pl.BlockSpec((tm, d), lambda s: (jnp.where(s == 0, 0, x_tile(s)), 0)),
        out_shape=jax.ShapeDtypeStruct((m, d), F32),
        scratch_shapes=[pltpu.VMEM((tm, d), BF16), pltpu.SemaphoreType.DMA(())],
        compiler_params=pltpu.CompilerParams(
            dimension_semantics=("arbitrary",),
            vmem_limit_bytes=_vmem_limit(vmem)),
        name="mlp_rest",
    )(first, x, g, *weights, final_g)


_POOL_TOP = 16
_CONV_TOP = 8


def _block_prompt_kernel(x_ref, g1_ref, win_ref, wout_ref, cos_ref, sin_ref, dmask_ref, qdec_ref, kdec_ref,
                         poolw_ref, pscale_ref, convw_ref, retg_ref, sgug_ref, sguw_ref, sgub_ref,
                         *rest, start, tm, cast_next, layer, all_layers):
    *rest, pool_ext, conv_ext = rest
    if cast_next:
        win_next_ref, wout_next_ref, *rest, win_next_out, wout_next_out = rest
    h_out, pool_out, conv_out, ret_out, sgu_out = rest[-5:]
    stacked = (pool_out, conv_out, ret_out, sgu_out)
    if all_layers:
        pool_out, conv_out, ret_out, sgu_out = (ref.at[layer] for ref in stacked)
    c = pl.program_id(1)
    T = CHUNK
    n_chunks = tm // T

    @pl.when(c == 0)
    def _():
        pool_ext[0:_POOL_TOP, :] = jnp.zeros((_POOL_TOP, GROUP), F32)
        conv_ext[0:_CONV_TOP, :] = jnp.zeros((_CONV_TOP, GROUP), F32)
        ret_out[...] = jnp.zeros(ret_out.shape, F32)
        if all_layers:
            for ref in stacked:
                for other in range(DEPTH):
                    if other != layer:
                        ref[other] = jnp.zeros(ref.shape[1:], F32)

    x = x_ref[0]
    xn = _rms(x, g1_ref[...]).astype(BF16)

    def proj(col):
        return _dot(xn, win_ref[:, col:col + GROUP])

    def out_proj(y, col):
        return _dot(y, wout_ref[col:col + GROUP, :])

    head_lanes = [slice(hd * HEAD_DIM, (hd + 1) * HEAD_DIM) for hd in range(HEADS)]
    chunk_rows = [slice(j * T, (j + 1) * T) for j in range(n_chunks)]

    def pool_mixer(a_proj):
        pool_ext[_POOL_TOP:_POOL_TOP + tm, :] = a_proj
        pos1 = lax.broadcasted_iota(jnp.int32, (tm, HEAD_DIM), 0) + (start + 1) + c * tm
        y = []
        for gi, win in enumerate(POOL_WINDOWS):
            lanes = head_lanes[gi]
            a = pool_ext[_POOL_TOP:_POOL_TOP + tm, lanes]
            wsum = a
            for j in range(1, win):
                wsum = wsum + pool_ext[_POOL_TOP - j:_POOL_TOP - j + tm, lanes]
            cnt = jnp.minimum(pos1, win).astype(F32)
            d = (wsum / cnt - a).astype(BF16)
            y.append((_dot(d, poolw_ref[gi]) * pscale_ref[:, lanes]).astype(BF16))
        pool_out[0] = pool_ext[_POOL_TOP + tm - POOL_BUF:_POOL_TOP + tm, :]
        pool_ext[_POOL_TOP - POOL_BUF:_POOL_TOP, :] = pool_ext[_POOL_TOP + tm - POOL_BUF:_POOL_TOP + tm, :]
        return jnp.concatenate(y, axis=-1)

    def conv_mixer(bg, cg, hh):
        z = cg * hh
        conv_ext[_CONV_TOP:_CONV_TOP + tm, :] = z
        acc = (conv_ext[_CONV_TOP - 2:_CONV_TOP - 2 + tm, :] * convw_ref[0:1, :]
               + conv_ext[_CONV_TOP - 1:_CONV_TOP - 1 + tm, :] * convw_ref[1:2, :]
               + z * convw_ref[2:3, :])
        conv_out[0] = conv_ext[_CONV_TOP + tm - 2:_CONV_TOP + tm, :]
        conv_ext[_CONV_TOP - 2:_CONV_TOP, :] = conv_ext[_CONV_TOP + tm - 2:_CONV_TOP + tm, :]
        return (bg * acc).astype(BF16)

    def retention(q_all, k_all, v_all, g_all, j, hd):
        rows, lanes = chunk_rows[j], head_lanes[hd]
        cos = cos_ref[rows, :]
        sin = sin_ref[rows, :]
        q = _rotate(q_all[rows, lanes], cos, sin)
        k = _rotate(k_all[rows, lanes], cos, sin) * (HEAD_DIM ** -0.5)
        v = v_all[rows, lanes].astype(BF16)
        s_prev = ret_out[0, hd]
        scores = lax.dot_general(q.astype(BF16), k.astype(BF16), (((1,), (1,)), ((), ())),
                                 preferred_element_type=F32) * dmask_ref[hd]
        o = _dot(scores.astype(BF16), v) + _dot((q * qdec_ref[hd]).astype(BF16), s_prev.astype(BF16))
        kv = lax.dot_general((k * kdec_ref[hd]).astype(BF16), v, (((0,), (0,)), ((), ())),
                             preferred_element_type=F32)
        ret_out[0, hd] = qdec_ref[hd, T - 1:T, :] * s_prev + kv
        on = _layernorm(o) * retg_ref[:, lanes]
        return (_silu(g_all[rows, lanes]) * on).astype(BF16)

    def sgu_mixer(u_all, vn):
        tri = lax.broadcasted_iota(jnp.int32, (T, T), 0) >= lax.broadcasted_iota(jnp.int32, (T, T), 1)
        y = [[None] * HEADS for _ in range(n_chunks)]
        for hd in range(HEADS):
            lanes = head_lanes[hd]
            w = jnp.where(tri, sguw_ref[hd], 0.0).astype(BF16)
            cols = jnp.concatenate([vn[rows, lanes] for rows in chunk_rows], axis=-1).astype(BF16)
            mixed = _dot(w, cols)
            for j, rows in enumerate(chunk_rows):
                m = mixed[:, j * T:(j + 1) * T] + sgub_ref[hd]
                y[j][hd] = (u_all[rows, lanes] * m).astype(BF16)
        return jnp.concatenate([jnp.concatenate(r, axis=-1) for r in y], axis=0)

    q_all, k_all, v_all, g_all = proj(_Q), proj(_K), proj(_V), proj(_G)
    ret = functools.partial(retention, q_all, k_all, v_all, g_all)
    y_c = [[None] * HEADS for _ in range(n_chunks)]
    y_c[0][0] = ret(0, 0)
    a_proj = proj(_A)
    y_c[0][1] = ret(0, 1)
    cg = proj(_CG)
    y_c[0][2] = ret(0, 2)
    hh = proj(_HH)
    y_c[0][3] = ret(0, 3)
    bg = proj(_BG)
    later = iter([lambda: proj(_VV), lambda: proj(_U), lambda: pool_mixer(a_proj),
                  lambda: conv_mixer(bg, cg, hh)])
    fills = []
    for j in range(1, n_chunks):
        for hd in range(HEADS):
            y_c[j][hd] = ret(j, hd)
            nxt = next(later, None)
            if nxt is not None:
                fills.append(nxt())
    fills += [f() for f in later]
    vv, u_all, y_a, y_b = fills
    h = x + out_proj(y_a, _YA)
    vn = _layernorm(vv) * sgug_ref[...]
    y_d = sgu_mixer(u_all, vn)
    h = h + out_proj(y_b, _YB)
    h = h + out_proj(jnp.concatenate([jnp.concatenate(r, axis=-1) for r in y_c], axis=0), _YC)
    h_out[0] = h + out_proj(y_d, _YD)
    sgu_out[0] = vn[tm - T:tm, :]

    if cast_next:
        win_next_out[...] = win_next_ref[...].astype(BF16)
        wout_next_out[...] = wout_next_ref[...].astype(BF16)


_N_BLOCK_INPUTS = 16


def block_prompt(x, norm_g, w_in, w_out, tables, params, layer, next_weights, stacked, *, start, tm):
    b, l, d = x.shape
    nc = l // tm
    cos, sin, dmask, qdec, kdec = tables
    poolw, pscale, convw, retg, sgug, sguw, sgub = params
    const3 = lambda shape: pl.BlockSpec(shape, lambda i, c: (0, 0, 0))
    lay3 = lambda shape: pl.BlockSpec((None,) + shape, lambda i, c: (layer, 0, 0))
    lay4 = lambda shape: pl.BlockSpec((None,) + shape, lambda i, c: (layer, 0, 0, 0))
    resident = lambda shape: pl.BlockSpec(shape, lambda i, c: (0, 0), pipeline_mode=pl.Buffered(1))
    cast_next = next_weights is not None
    next_in_specs, next_out_specs, next_out_shape, next_args = [], [], [], ()
    next_bytes = 0
    if cast_next:
        for w in next_weights:
            rows, cols = w.shape[1] // (b * nc), w.shape[2]
            assert rows * b * nc == w.shape[1] and rows % 16 == 0
            next_in_specs.append(pl.BlockSpec((None, rows, cols), lambda i, c: (layer + 1, i * nc + c, 0)))
            next_out_specs.append(pl.BlockSpec((rows, cols), lambda i, c: (i * nc + c, 0)))
            next_out_shape.append(jax.ShapeDtypeStruct(w.shape[1:], BF16))
            next_bytes += 2 * rows * cols * (4 + 2)
        next_args = tuple(next_weights)
    state_dims = [(POOL_BUF, GROUP), (2, GROUP), (HEADS, HEAD_DIM, HEAD_DIM), (CHUNK, GROUP)]
    first = stacked is None
    if first:
        state_specs = [pl.BlockSpec((DEPTH, 1) + dims, lambda i, c, n=len(dims): (0, i) + (0,) * n)
                       for dims in state_dims]
        alias_specs, alias_args, aliases = [], (), {}
    else:
        state_specs = [pl.BlockSpec((None, 1) + dims, lambda i, c, n=len(dims): (layer, i) + (0,) * n)
                       for dims in state_dims]
        alias_specs = [pl.BlockSpec(memory_space=pl.ANY)] * len(stacked)
        alias_args = tuple(stacked)
        n_in = _N_BLOCK_INPUTS + len(next_in_specs)
        aliases = {n_in + k: 1 + k for k in range(len(stacked))}
    vmem = (d * IN_WIDTH + 4 * GROUP * d) * 2 + 2 * 2 * tm * d * 4 + 2 * tm * IN_WIDTH * 4 + next_bytes
    outs = pl.pallas_call(
        functools.partial(_block_prompt_kernel, start=start, tm=tm, cast_next=cast_next, layer=layer,
                          all_layers=first),
        grid=(b, nc),
        in_specs=[
            pl.BlockSpec((1, tm, d), lambda i, c: (i, c, 0)),
            lay3((1, d)),
            resident((d, IN_WIDTH)),
            resident((4 * GROUP, d)),
            pl.BlockSpec((tm, HEAD_DIM), lambda i, c: (c, 0)),
            pl.BlockSpec((tm, HEAD_DIM), lambda i, c: (c, 0)),
            const3((HEADS, CHUNK, CHUNK)),
            const3((HEADS, CHUNK, HEAD_DIM)),
            const3((HEADS, CHUNK, HEAD_DIM)),
            lay4((len(POOL_WINDOWS), HEAD_DIM, HEAD_DIM)),
            lay3((1, GROUP)),
            lay3((3, GROUP)),
            lay3((1, GROUP)),
            lay3((1, GROUP)),
            lay4((HEADS, CHUNK, CHUNK)),
            lay4((HEADS, CHUNK, HEAD_DIM)),
            *next_in_specs,
            *alias_specs,
        ],
        out_specs=[pl.BlockSpec((1, tm, d), lambda i, c: (i, c, 0)), *state_specs, *next_out_specs],
        out_shape=[
            jax.ShapeDtypeStruct((b, l, d), F32),
            *[jax.ShapeDtypeStruct((DEPTH, b) + dims, F32) for dims in state_dims],
            *next_out_shape,
        ],
        scratch_shapes=[
            pltpu.VMEM((_POOL_TOP + tm, GROUP), F32),
            pltpu.VMEM((_CONV_TOP + tm, GROUP), F32),
        ],
        input_output_aliases=aliases,
        compiler_params=pltpu.CompilerParams(
            dimension_semantics=("arbitrary", "arbitrary"),
            vmem_limit_bytes=_vmem_limit(vmem)),
        name="block_prompt",
    )(x, norm_g, w_in, w_out, cos, sin, dmask, qdec, kdec, poolw, pscale, convw, retg, sgug, sguw, sgub,
      *next_args, *alias_args)
    return outs[0], tuple(outs[1:5]), (tuple(outs[5:]) if cast_next else None)


_SEQ_BLOCK = 16


_N_SAMPLE_INPUTS = 14


def _mixer_sample_kernel(*refs, start, layer, all_layers):
    (p_ref, pool_ref, conv_ref, ret_ref, cos_ref, sin_ref, gam_ref,
     poolw_ref, pscale_ref, convw_ref, retg_ref, sgug_ref, sguw_ref, sgub_ref) = refs[:_N_SAMPLE_INPUTS]
    mix_ref, pool_out, conv_out, ret_out, sgu_out, qg_s, k_s, v_s, os_s = refs[-9:]
    TB = _SEQ_BLOCK

    if all_layers:
        stacked = (pool_out, conv_out, ret_out, sgu_out)
        for ref in stacked:
            for other in range(DEPTH):
                if other != layer:
                    ref[other] = jnp.zeros(ref.shape[1:], F32)
        pool_out, conv_out, ret_out, sgu_out = (ref.at[layer] for ref in stacked)

    a_all = p_ref[:, _A:_A + GROUP]
    for gi, win in enumerate(POOL_WINDOWS):
        lanes = slice(gi * HEAD_DIM, (gi + 1) * HEAD_DIM)
        a = a_all[:, lanes]
        wsum = a
        for j in range(1, win):
            wsum = wsum + pool_ref[POOL_BUF - j, :, lanes]
        cnt = float(min(start + 1, win))
        d = (wsum / cnt - a).astype(BF16)
        y = _dot(d, poolw_ref[gi]) * pscale_ref[:, lanes]
        mix_ref[:, _YA + gi * HEAD_DIM:_YA + (gi + 1) * HEAD_DIM] = y.astype(BF16)
    pool_out[0:POOL_BUF - 1] = pool_ref[1:POOL_BUF]
    pool_out[POOL_BUF - 1] = a_all

    z = p_ref[:, _CG:_CG + GROUP] * p_ref[:, _HH:_HH + GROUP]
    acc = (conv_ref[:, 0, :] * convw_ref[0:1, :] + conv_ref[:, 1, :] * convw_ref[1:2, :]
           + z * convw_ref[2:3, :])
    mix_ref[:, _YB:_YB + GROUP] = (p_ref[:, _BG:_BG + GROUP] * acc).astype(BF16)
    conv_out[:, 0, :] = conv_ref[:, 1, :]
    conv_out[:, 1, :] = z

    cos = cos_ref[...]
    sin = sin_ref[...]
    scores = []
    for h in range(HEADS):
        lanes = slice(h * HEAD_DIM, (h + 1) * HEAD_DIM)
        q = _rotate(p_ref[:, _Q + h * HEAD_DIM:_Q + (h + 1) * HEAD_DIM], cos, sin)
        k = _rotate(p_ref[:, _K + h * HEAD_DIM:_K + (h + 1) * HEAD_DIM], cos, sin) * (HEAD_DIM ** -0.5)
        scores.append(jnp.sum(q * k, axis=-1, keepdims=True))
        qg_s[:, lanes] = q * gam_ref[:, lanes]
        k_s[:, lanes] = k
    v_s[...] = p_ref[:, _V:_V + GROUP]

    first_row = lax.broadcasted_iota(jnp.int32, (8, HEAD_DIM), 0) == 0

    for b in range(TB):
        for h in range(HEADS):
            lanes = slice(h * HEAD_DIM, (h + 1) * HEAD_DIM)
            s_prev = ret_ref[b, h]
            q8 = jnp.broadcast_to(qg_s[b:b + 1, lanes], (8, HEAD_DIM))
            os_s[b:b + 1, lanes] = _dot(q8.astype(BF16), s_prev.astype(BF16))[0:1, :]
            k8 = jnp.where(first_row, jnp.broadcast_to(k_s[b:b + 1, lanes], (8, HEAD_DIM)), 0.0)
            v8 = jnp.broadcast_to(v_s[b:b + 1, lanes], (8, HEAD_DIM))
            kv = lax.dot_general(k8.astype(BF16), v8.astype(BF16), (((0,), (0,)), ((), ())),
                                 preferred_element_type=F32)
            ret_out[b, h] = gam_ref[:, lanes] * s_prev + kv

    for h in range(HEADS):
        lanes = slice(h * HEAD_DIM, (h + 1) * HEAD_DIM)
        o = scores[h] * v_s[:, lanes] + os_s[:, lanes]
        on = _layernorm(o) * retg_ref[:, lanes]
        g = p_ref[:, _G + h * HEAD_DIM:_G + (h + 1) * HEAD_DIM]
        mix_ref[:, _YC + h * HEAD_DIM:_YC + (h + 1) * HEAD_DIM] = (_silu(g) * on).astype(BF16)

    vn = _layernorm(p_ref[:, _VV:_VV + GROUP]) * sgug_ref[...]
    mix_ref[:, _YD:_YD + GROUP] = (p_ref[:, _U:_U + GROUP] * (sguw_ref[...] * vn + sgub_ref[...])).astype(BF16)
    sgu_out[:, 0, :] = vn


def mixer_sample(p, pool_state, conv_state, ret_state, tables, params, layer, stacked, *, start):
    n = p.shape[0]
    tb = _SEQ_BLOCK
    cos, sin, gam = tables
    poolw, pscale, convw, retg, sgug, sguw, sgub = params
    rows = lambda width: pl.BlockSpec((tb, width), lambda i: (i, 0))
    lay_rows = lambda r: pl.BlockSpec((None, tb, r, GROUP), lambda i: (layer, i, 0, 0))
    lay_pool = pl.BlockSpec((None, POOL_BUF, tb, GROUP), lambda i: (layer, 0, i, 0))
    lay_ret = pl.BlockSpec((None, tb, HEADS, HEAD_DIM, HEAD_DIM), lambda i: (layer, i, 0, 0, 0))
    const2 = lambda shape: pl.BlockSpec(shape, lambda i: (0, 0))
    lay3 = lambda shape: pl.BlockSpec((None,) + shape, lambda i: (layer, 0, 0))
    lay4 = lambda shape: pl.BlockSpec((None,) + shape, lambda i: (layer, 0, 0, 0))
    in_specs = [
        rows(IN_WIDTH),
        lay_pool,
        lay_rows(2),
        lay_ret,
        const2((1, HEAD_DIM)),
        const2((1, HEAD_DIM)),
        const2((1, GROUP)),
        lay4((len(POOL_WINDOWS), HEAD_DIM, HEAD_DIM)),
        lay3((1, GROUP)),
        lay3((3, GROUP)),
        lay3((1, GROUP)),
        lay3((1, GROUP)),
        lay3((1, GROUP)),
        lay3((1, GROUP)),
    ]
    args = (p, pool_state, conv_state, ret_state, cos, sin, gam,
            poolw, pscale, convw, retg, sgug, sguw, sgub)
    assert len(args) == _N_SAMPLE_INPUTS
    aliases = {}
    first = stacked is None
    state_block = tb * (16 + 8 + 8 + HEAD_DIM) * GROUP * 4
    vmem = 2 * (tb * IN_WIDTH * 4 + state_block * (1 + (DEPTH if first else 1)))
    if first:
        all_rows = lambda r: pl.BlockSpec((DEPTH, tb, r, GROUP), lambda i: (0, i, 0, 0))
        state_specs = [pl.BlockSpec((DEPTH, POOL_BUF, tb, GROUP), lambda i: (0, 0, i, 0)), all_rows(2),
                       pl.BlockSpec((DEPTH, tb, HEADS, HEAD_DIM, HEAD_DIM), lambda i: (0, i, 0, 0, 0)),
                       all_rows(1)]
    else:
        state_specs = [lay_pool, lay_rows(2), lay_ret, lay_rows(1)]
        aliases = {len(args) + k: 1 + k for k in range(len(stacked))}
        in_specs = in_specs + [pl.BlockSpec(memory_space=pl.ANY)] * len(stacked)
        args = args + tuple(stacked)
    outs = pl.pallas_call(
        functools.partial(_mixer_sample_kernel, start=start, layer=layer, all_layers=first),
        grid=(n // tb,),
        in_specs=in_specs,
        out_specs=[rows(4 * GROUP)] + state_specs,
        out_shape=[
            jax.ShapeDtypeStruct((n, 4 * GROUP), BF16),
            jax.ShapeDtypeStruct((DEPTH, POOL_BUF, n, GROUP), F32),
            jax.ShapeDtypeStruct((DEPTH, n, 2, GROUP), F32),
            jax.ShapeDtypeStruct((DEPTH, n, HEADS, HEAD_DIM, HEAD_DIM), F32),
            jax.ShapeDtypeStruct((DEPTH, n, 1, GROUP), F32),
        ],
        scratch_shapes=[pltpu.VMEM((tb, GROUP), F32)] * 4,
        input_output_aliases=aliases,
        compiler_params=pltpu.CompilerParams(
            dimension_semantics=("arbitrary",),
            vmem_limit_bytes=_vmem_limit(vmem)),
        name="mixer_sample",
    )(*args)
    return outs[0], tuple(outs[1:])


def _rope_tables(start, length):
    half = HEAD_DIM // 2
    inv = ROPE_BASE ** (-jnp.arange(half, dtype=F32) / half)
    pos = start + jnp.arange(length)
    ang = pos.astype(F32)[:, None] * inv[None, :]
    cos, sin = jnp.cos(ang), jnp.sin(ang)
    return jnp.concatenate([cos, cos], axis=-1), jnp.concatenate([-sin, sin], axis=-1)


def _log_gamma():
    return jnp.log1p(-(2.0 ** (-5.0 - jnp.arange(HEADS, dtype=F32))))


def _decay_tables(c):
    lg = _log_gamma()
    idx = jnp.arange(c, dtype=F32)
    diff = idx[:, None] - idx[None, :]
    dmask = jnp.where(diff >= 0, jnp.exp(jnp.maximum(diff, 0.0)[None] * lg[:, None, None]), 0.0)
    qdec = jnp.exp((idx + 1.0)[None, :] * lg[:, None])[:, :, None]
    kdec = jnp.exp((c - 1.0 - idx)[None, :] * lg[:, None])[:, :, None]
    wide = lambda t: jnp.broadcast_to(t, (HEADS, c, HEAD_DIM))
    return dmask, wide(qdec), wide(kdec)


def _per_head_lanes(t):
    return jnp.repeat(t, HEAD_DIM, axis=-1)[..., None, :]


def kernel(x_prompt, x_sample, state_pool, state_conv, state_ret, norm1_g, w_in, pool_w, pool_scale,
           conv_w, ret_norm_g, sgu_norm_g, sgu_w, sgu_b, w_out, norm2_g, w_gate_up, w_down,
           final_norm_g):
    bp, lp, _ = x_prompt.shape
    ns = x_sample.shape[0]

    pool_w_b = pool_w.astype(BF16)
    row3 = lambda t: t[:, None, :]
    norm1 = row3(norm1_g)
    norm2 = row3(norm2_g)
    final_g = final_norm_g[None, :]

    cos_p, sin_p = _rope_tables(0, lp)
    prompt_tables = (cos_p, sin_p) + _decay_tables(CHUNK)
    sgu_b_wide = jnp.broadcast_to(sgu_b[:, :, :, None], sgu_b.shape + (HEAD_DIM,))
    prompt_params = (pool_w_b, row3(pool_scale), conv_w, row3(ret_norm_g), row3(sgu_norm_g), sgu_w, sgu_b_wide)

    cos_s, sin_s = _rope_tables(PAST_LEN, 1)
    gam = _per_head_lanes(jnp.exp(_log_gamma()))
    sample_tables = (cos_s, sin_s, gam)
    sample_params = (pool_w_b, row3(pool_scale), conv_w, row3(ret_norm_g), row3(sgu_norm_g),
                     _per_head_lanes(sgu_w[:, :, 0, 0]), _per_head_lanes(sgu_b[:, :, 0]))
    pool_rows = jnp.transpose(state_pool, (0, 2, 1, 3))
    hp = x_prompt
    hs = x_sample.reshape(ns, D_MODEL)
    prompt_states = None
    sample_states = None
    next_b = None
    for l in range(DEPTH):
        fin = l == DEPTH - 1
        w_in_l, w_out_l = (w_in, w_out) if next_b is None else next_b
        p, w_in_b = norm_matmul(hs, norm1, w_in_l, l, tk=256)
        mix, sample_states = mixer_sample(p, pool_rows, state_conv, state_ret, sample_tables,
                                          sample_params, l, sample_states, start=PAST_LEN)
        hs, w_out_b = matmul_residual(mix, w_out_l, hs, l, tn=1024)
        hp, prompt_states, next_b = block_prompt(hp, norm1, w_in_b, w_out_b, prompt_tables, prompt_params,
                                                 l, None if fin else (w_in, w_out), prompt_states,
                                                 start=0, tm=256)
        hp = hp.reshape(bp * lp, D_MODEL)
        first, w_mlp_b = mlp_cast(hs, hp, norm2, w_gate_up, w_down, final_g, l, tp=_MLP_TILE, tf=256,
                                  final_norm=fin)
        hs = first[:ns]
        hp = mlp_rest(first, hp, norm2, w_mlp_b, final_g, l, skip_rows=ns, tm=_MLP_TILE, tf=512,
                      final_norm=fin).reshape(bp, lp, D_MODEL)
    pool_p, conv_p, ret_p, sgu_p = prompt_states
    pool_s, conv_s, ret_s, sgu_s = sample_states
    pool_s = jnp.transpose(pool_s, (0, 2, 1, 3))
    return (hp, hs.reshape(ns, 1, D_MODEL), pool_p, pool_s, conv_p, conv_s, ret_p, ret_s, sgu_p, sgu_s)
```

```python
import functools

import jax
import jax.numpy as jnp
from jax import lax
from jax.experimental import pallas as pl
from jax.experimental.pallas import tpu as pltpu

F32 = jnp.float32
BF16 = jnp.bfloat16

D_MODEL = 2048
DEPTH = 2
GROUP = 512
N_SPLITS = 10
IN_WIDTH = N_SPLITS * GROUP
POOL_WINDOWS = (2, 4, 8, 16)
POOL_BUF = 15
HEADS = 4
HEAD_DIM = 128
CHUNK = 128
ROPE_BASE = 10000.0
D_FF = 5632
NORM_EPS = 1e-6
PAST_LEN = 16384

_A, _BG, _CG, _HH, _Q, _K, _V, _G, _U, _VV = (i * GROUP for i in range(N_SPLITS))
_YA, _YB, _YC, _YD = (i * GROUP for i in range(4))

MIB = 1024 * 1024
VMEM_BYTES_V7X = 64 * MIB


VMEM_LIMIT_MAX = VMEM_BYTES_V7X - 4 * MIB


def _vmem_limit(estimate):
    return min(estimate + 4 * MIB, VMEM_LIMIT_MAX)


def _rms(x, g):
    ms = jnp.mean(x * x, axis=-1, keepdims=True)
    return x * lax.rsqrt(ms + NORM_EPS) * g


def _layernorm(x):
    mu = jnp.mean(x, axis=-1, keepdims=True)
    xc = x - mu
    return xc * lax.rsqrt(jnp.mean(xc * xc, axis=-1, keepdims=True) + NORM_EPS)


def _silu(x):
    return x * jax.nn.sigmoid(x)


def _dot(a, b):
    return jnp.dot(a, b, preferred_element_type=F32)


def _rotate(x, cos, sin_signed):
    return x * cos + pltpu.roll(x, HEAD_DIM // 2, 1) * sin_signed


def _weight_specs(w, layer, k, tn):
    if w.ndim == 3:
        return pl.BlockSpec((None, k, tn), lambda j: (layer, 0, j)), True
    return pl.BlockSpec((k, tn), lambda j: (0, j)), False


def _norm_matmul_kernel(x_ref, g_ref, w_ref, o_ref, *rest, emit):
    xn_ref = rest[-1]
    j = pl.program_id(0)
    tk = xn_ref.shape[-1]

    @pl.when(j == 0)
    def _():
        xn = _rms(x_ref[...], g_ref[...]).astype(BF16)
        for kb in range(xn_ref.shape[0]):
            xn_ref[kb] = xn[:, kb * tk:(kb + 1) * tk]

    wb = w_ref[...].astype(BF16)
    if emit:
        rest[0][...] = wb
    part = _dot(xn_ref[j], wb)

    @pl.when(j == 0)
    def _():
        o_ref[...] = part

    @pl.when(j > 0)
    def _():
        o_ref[...] += part


def norm_matmul(x, g, w, layer, *, tk):
    m, k = x.shape
    n = w.shape[-1]
    emit = w.ndim == 3
    w_spec = (pl.BlockSpec((None, tk, n), lambda j: (layer, j, 0)) if emit
              else pl.BlockSpec((tk, n), lambda j: (j, 0)))
    outs = pl.pallas_call(
        functools.partial(_norm_matmul_kernel, emit=emit),
        grid=(k // tk,),
        in_specs=[
            pl.BlockSpec((m, k), lambda j: (0, 0)),
            pl.BlockSpec((None, 1, k), lambda j: (layer, 0, 0)),
            w_spec,
        ],
        out_specs=[pl.BlockSpec((m, n), lambda j: (0, 0))] + [pl.BlockSpec((tk, n), lambda j: (j, 0))] * emit,
        out_shape=[jax.ShapeDtypeStruct((m, n), F32)] + [jax.ShapeDtypeStruct((k, n), BF16)] * emit,
        scratch_shapes=[pltpu.VMEM((k // tk, m, tk), BF16)],
        compiler_params=pltpu.CompilerParams(
            dimension_semantics=("arbitrary",),
            vmem_limit_bytes=VMEM_LIMIT_MAX),
        name="norm_matmul",
    )(x, g, w)
    return tuple(outs) if emit else (outs[0], w)


def _matmul_residual_kernel(a_ref, w_ref, r_ref, o_ref, *rest, emit):
    wb = w_ref[...].astype(BF16)
    if emit:
        rest[0][...] = wb
    o_ref[...] = r_ref[...] + _dot(a_ref[...], wb)


def matmul_residual(a, w, res, layer, *, tn):
    m, k = a.shape
    n = w.shape[-1]
    w_spec, emit = _weight_specs(w, layer, k, tn)
    vmem = 2 * k * tn * (4 + 2) + 2 * m * (k * 2 + 2 * tn * 4)
    outs = pl.pallas_call(
        functools.partial(_matmul_residual_kernel, emit=emit),
        grid=(n // tn,),
        in_specs=[
            pl.BlockSpec((m, k), lambda j: (0, 0)),
            w_spec,
            pl.BlockSpec((m, tn), lambda j: (0, j)),
        ],
        out_specs=[pl.BlockSpec((m, tn), lambda j: (0, j))] + [pl.BlockSpec((k, tn), lambda j: (0, j))] * emit,
        out_shape=[jax.ShapeDtypeStruct((m, n), F32)] + [jax.ShapeDtypeStruct((k, n), BF16)] * emit,
        compiler_params=pltpu.CompilerParams(
            dimension_semantics=("arbitrary",),
            vmem_limit_bytes=_vmem_limit(vmem)),
        name="matmul_residual",
    )(a, w, res)
    return tuple(outs) if emit else (outs[0], w)


_MLP_TILE = 1024


def _mlp_step(first, last, x_refs, g_ref, weights, fg_ref, o_ref, hn_ref, final_norm):
    def delta():
        wg, wu, wd = weights()
        hn = hn_ref[...]
        act = (_silu(_dot(hn, wg)) * _dot(hn, wu)).astype(BF16)
        return _dot(act, wd)

    @pl.when(first)
    def _():
        row = 0
        for x_ref in x_refs:
            rows = slice(row, row + x_ref.shape[0])
            hn_ref[rows, :] = _rms(x_ref[...], g_ref[...]).astype(BF16)
            row += x_ref.shape[0]
        d = delta()
        row = 0
        for x_ref in x_refs:
            rows = slice(row, row + x_ref.shape[0])
            o_ref[rows, :] = x_ref[...] + d[rows, :]
            row += x_ref.shape[0]

    @pl.when(jnp.logical_not(first))
    def _():
        o_ref[...] += delta()

    if final_norm:
        @pl.when(last)
        def _():
            o_ref[...] = _rms(o_ref[...], fg_ref[...])


def _mlp_cast_kernel(xs_ref, xp_ref, g_ref, wg_ref, wu_ref, wd_ref, fg_ref,
                     o_ref, wgb_ref, wub_ref, wdb_ref, hn_ref, *, final_norm):
    f = pl.program_id(0)

    def cast_weights():
        wg, wu, wd = wg_ref[...].astype(BF16), wu_ref[...].astype(BF16), wd_ref[...].astype(BF16)
        wgb_ref[...] = wg
        wub_ref[...] = wu
        wdb_ref[...] = wd
        return wg, wu, wd

    _mlp_step(f == 0, f == pl.num_programs(0) - 1, (xs_ref, xp_ref), g_ref, cast_weights, fg_ref,
              o_ref, hn_ref, final_norm)


def _mlp_rest_kernel(first_hbm, x_ref, g_ref, wg_ref, wu_ref, wd_ref, fg_ref, o_ref, hn_ref, sem,
                     *, nf, skip_rows, final_norm):
    s = pl.program_id(0)
    tm = o_ref.shape[0]

    @pl.when(s == 0)
    def _():
        copy = pltpu.make_async_copy(first_hbm.at[pl.ds(skip_rows, tm), :], o_ref, sem)
        copy.start()
        copy.wait()

    @pl.when(s > 0)
    def _():
        f = (s - 1) % nf
        _mlp_step(f == 0, f == nf - 1, (x_ref,), g_ref, lambda: (wg_ref[...], wu_ref[...], wd_ref[...]),
                  fg_ref, o_ref, hn_ref, final_norm)


def mlp_cast(xs, xp, g, w_gate_up, w_down, final_g, layer, *, tp, tf, final_norm):
    ns, d = xs.shape
    m = ns + tp
    nf = D_FF // tf
    once = pl.Buffered(1)
    vmem = 2 * m * d * 4 + m * d * 2 + 2 * 3 * d * tf * (4 + 2) + 3 * m * tf * 4
    outs = pl.pallas_call(
        functools.partial(_mlp_cast_kernel, final_norm=final_norm),
        grid=(nf,),
        in_specs=[
            pl.BlockSpec((ns, d), lambda f: (0, 0), pipeline_mode=once),
            pl.BlockSpec((tp, d), lambda f: (0, 0), pipeline_mode=once),
            pl.BlockSpec((None, 1, d), lambda f: (layer, 0, 0)),
            pl.BlockSpec((None, d, tf), lambda f: (layer, 0, f)),
            pl.BlockSpec((None, d, tf), lambda f: (layer, 0, nf + f)),
            pl.BlockSpec((None, tf, d), lambda f: (layer, f, 0)),
            pl.BlockSpec((1, d), lambda f: (0, 0)),
        ],
        out_specs=[
            pl.BlockSpec((m, d), lambda f: (0, 0), pipeline_mode=once),
            pl.BlockSpec((d, tf), lambda f: (0, f)),
            pl.BlockSpec((d, tf), lambda f: (0, f)),
            pl.BlockSpec((tf, d), lambda f: (f, 0)),
        ],
        out_shape=[jax.ShapeDtypeStruct((m, d), F32),
                   jax.ShapeDtypeStruct((d, D_FF), BF16), jax.ShapeDtypeStruct((d, D_FF), BF16),
                   jax.ShapeDtypeStruct((D_FF, d), BF16)],
        scratch_shapes=[pltpu.VMEM((m, d), BF16)],
        compiler_params=pltpu.CompilerParams(
            dimension_semantics=("arbitrary",),
            vmem_limit_bytes=_vmem_limit(vmem)),
        name="mlp_cast",
    )(xs, xp, g, w_gate_up, w_gate_up, w_down, final_g)
    return outs[0], tuple(outs[1:])


def mlp_rest(first, x, g, weights, final_g, layer, *, skip_rows, tm, tf, final_norm):
    m, d = x.shape
    nf = D_FF // tf
    step = lambda s: jnp.maximum(s - 1, 0)
    x_tile = lambda s: 1 + step(s) // nf
    f_tile = lambda s: step(s) % nf
    vmem = 2 * 2 * tm * d * 4 + tm * d * 2 + 2 * 3 * d * tf * 2 + 3 * tm * tf * 4
    return pl.pallas_call(
        functools.partial(_mlp_rest_kernel, nf=nf, skip_rows=skip_rows, final_norm=final_norm),
        grid=(1 + (m // tm - 1) * nf,),
        in_specs=[
            pl.BlockSpec(memory_space=pl.ANY),
            pl.BlockSpec((tm, d), lambda s: (x_tile(s), 0)),
            pl.BlockSpec((None, 1, d), lambda s: (layer, 0, 0)),
            pl.BlockSpec((d, tf), lambda s: (0, f_tile(s))),
            pl.BlockSpec((d, tf), lambda s: (0, f_tile(s))),
            pl.BlockSpec((tf, d), lambda s: (f_tile(s), 0)),
            pl.BlockSpec((1, d), lambda s: (0, 0)),
        ],
        out_specs=pl.BlockSpec((tm, d), lambda s: (jnp.where(s == 0, 0, x_tile(s)), 0)),
        out_shape=jax.ShapeDtypeStruct((m, d), F32),
        scratch_shapes=[pltpu.VMEM((tm, d), BF16), pltpu.SemaphoreType.DMA(())],
        compiler_params=pltpu.CompilerParams(
            dimension_semantics=("arbitrary",),
            vmem_limit_bytes=_vmem_limit(vmem)),
        name="mlp_rest",
    )(first, x, g, *weights, final_g)


_POOL_TOP = 16
_CONV_TOP = 8


def _block_prompt_kernel(x_ref, g1_ref, win_ref, wout_ref, cos_ref, sin_ref, dmask_ref, qdec_ref, kdec_ref,
                         poolw_ref, pscale_ref, convw_ref, retg_ref, sgug_ref, sguw_ref, sgub_ref,
                         *rest, start, tm, cast_next, layer, all_layers):
    *rest, pool_ext, conv_ext = rest
    if cast_next:
        win_next_ref, wout_next_ref, *rest, win_next_out, wout_next_out = rest
    h_out, pool_out, conv_out, ret_out, sgu_out = rest[-5:]
    stacked = (pool_out, conv_out, ret_out, sgu_out)
    if all_layers:
        pool_out, conv_out, ret_out, sgu_out = (ref.at[layer] for ref in stacked)
    c = pl.program_id(1)
    T = CHUNK
    n_chunks = tm // T

    @pl.when(c == 0)
    def _():
        pool_ext[0:_POOL_TOP, :] = jnp.zeros((_POOL_TOP, GROUP), F32)
        conv_ext[0:_CONV_TOP, :] = jnp.zeros((_CONV_TOP, GROUP), F32)
        ret_out[...] = jnp.zeros(ret_out.shape, F32)
        if all_layers:
            for ref in stacked:
                for other in range(DEPTH):
                    if other != layer:
                        ref[other] = jnp.zeros(ref.shape[1:], F32)

    x = x_ref[0]
    xn = _rms(x, g1_ref[...]).astype(BF16)

    def proj(col):
        return _dot(xn, win_ref[:, col:col + GROUP])

    def out_proj(y, col):
        return _dot(y, wout_ref[col:col + GROUP, :])

    head_lanes = [slice(hd * HEAD_DIM, (hd + 1) * HEAD_DIM) for hd in range(HEADS)]
    chunk_rows = [slice(j * T, (j + 1) * T) for j in range(n_chunks)]

    def pool_mixer(a_proj):
        pool_ext[_POOL_TOP:_POOL_TOP + tm, :] = a_proj
        pos1 = lax.broadcasted_iota(jnp.int32, (tm, HEAD_DIM), 0) + (start + 1) + c * tm
        y = []
        for gi, win in enumerate(POOL_WINDOWS):
            lanes = head_lanes[gi]
            a = pool_ext[_POOL_TOP:_POOL_TOP + tm, lanes]
            wsum = a
            for j in range(1, win):
                wsum = wsum + pool_ext[_POOL_TOP - j:_POOL_TOP - j + tm, lanes]
            cnt = jnp.minimum(pos1, win).astype(F32)
            d = (wsum / cnt - a).astype(BF16)
            y.append((_dot(d, poolw_ref[gi]) * pscale_ref[:, lanes]).astype(BF16))
        pool_out[0] = pool_ext[_POOL_TOP + tm - POOL_BUF:_POOL_TOP + tm, :]
        pool_ext[_POOL_TOP - POOL_BUF:_POOL_TOP, :] = pool_ext[_POOL_TOP + tm - POOL_BUF:_POOL_TOP + tm, :]
        return jnp.concatenate(y, axis=-1)

    def conv_mixer(bg, cg, hh):
        z = cg * hh
        conv_ext[_CONV_TOP:_CONV_TOP + tm, :] = z
        acc = (conv_ext[_CONV_TOP - 2:_CONV_TOP - 2 + tm, :] * convw_ref[0:1, :]
               + conv_ext[_CONV_TOP - 1:_CONV_TOP - 1 + tm, :] * convw_ref[1:2, :]
               + z * convw_ref[2:3, :])
        conv_out[0] = conv_ext[_CONV_TOP + tm - 2:_CONV_TOP + tm, :]
        conv_ext[_CONV_TOP - 2:_CONV_TOP, :] = conv_ext[_CONV_TOP + tm - 2:_CONV_TOP + tm, :]
        return (bg * acc).astype(BF16)

    def retention(q_all, k_all, v_all, g_all, j, hd):
        rows, lanes = chunk_rows[j], head_lanes[hd]
        cos = cos_ref[rows, :]
        sin = sin_ref[rows, :]
        q = _rotate(q_all[rows, lanes], cos, sin)
        k = _rotate(k_all[rows, lanes], cos, sin) * (HEAD_DIM ** -0.5)
        v = v_all[rows, lanes].astype(BF16)
        s_prev = ret_out[0, hd]
        scores = lax.dot_general(q.astype(BF16), k.astype(BF16), (((1,), (1,)), ((), ())),
                                 preferred_element_type=F32) * dmask_ref[hd]
        o = _dot(scores.astype(BF16), v) + _dot((q * qdec_ref[hd]).astype(BF16), s_prev.astype(BF16))
        kv = lax.dot_general((k * kdec_ref[hd]).astype(BF16), v, (((0,), (0,)), ((), ())),
                             preferred_element_type=F32)
        ret_out[0, hd] = qdec_ref[hd, T - 1:T, :] * s_prev + kv
        on = _layernorm(o) * retg_ref[:, lanes]
        return (_silu(g_all[rows, lanes]) * on).astype(BF16)

    def sgu_mixer(u_all, vn):
        tri = lax.broadcasted_iota(jnp.int32, (T, T), 0) >= lax.broadcasted_iota(jnp.int32, (T, T), 1)
        y = [[None] * HEADS for _ in range(n_chunks)]
        for hd in range(HEADS):
            lanes = head_lanes[hd]
            w = jnp.where(tri, sguw_ref[hd], 0.0).astype(BF16)
            cols = jnp.concatenate([vn[rows, lanes] for rows in chunk_rows], axis=-1).astype(BF16)
            mixed = _dot(w, cols)
            for j, rows in enumerate(chunk_rows):
                m = mixed[:, j * T:(j + 1) * T] + sgub_ref[hd]
                y[j][hd] = (u_all[rows, lanes] * m).astype(BF16)
        return jnp.concatenate([jnp.concatenate(r, axis=-1) for r in y], axis=0)

    q_all, k_all, v_all, g_all = proj(_Q), proj(_K), proj(_V), proj(_G)
    ret = functools.partial(retention, q_all, k_all, v_all, g_all)
    y_c = [[None] * HEADS for _ in range(n_chunks)]
    y_c[0][0] = ret(0, 0)
    a_proj = proj(_A)
    y_c[0][1] = ret(0, 1)
    cg = proj(_CG)
    y_c[0][2] = ret(0, 2)
    hh = proj(_HH)
    y_c[0][3] = ret(0, 3)
    bg = proj(_BG)
    later = iter([lambda: proj(_VV), lambda: proj(_U), lambda: pool_mixer(a_proj),
                  lambda: conv_mixer(bg, cg, hh)])
    fills = []
    for j in range(1, n_chunks):
        for hd in range(HEADS):
            y_c[j][hd] = ret(j, hd)
            nxt = next(later, None)
            if nxt is not None:
                fills.append(nxt())
    fills += [f() for f in later]
    vv, u_all, y_a, y_b = fills
    h = x + out_proj(y_a, _YA)
    vn = _layernorm(vv) * sgug_ref[...]
    y_d = sgu_mixer(u_all, vn)
    h = h + out_proj(y_b, _YB)
    h = h + out_proj(jnp.concatenate([jnp.concatenate(r, axis=-1) for r in y_c], axis=0), _YC)
    h_out[0] = h + out_proj(y_d, _YD)
    sgu_out[0] = vn[tm - T:tm, :]

    if cast_next:
        win_next_out[...] = win_next_ref[...].astype(BF16)
        wout_next_out[...] = wout_next_ref[...].astype(BF16)


_N_BLOCK_INPUTS = 16


def block_prompt(x, norm_g, w_in, w_out, tables, params, layer, next_weights, stacked, *, start, tm):
    b, l, d = x.shape
    nc = l // tm
    cos, sin, dmask, qdec, kdec = tables
    poolw, pscale, convw, retg, sgug, sguw, sgub = params
    const3 = lambda shape: pl.BlockSpec(shape, lambda i, c: (0, 0, 0))
    lay3 = lambda shape: pl.BlockSpec((None,) + shape, lambda i, c: (layer, 0, 0))
    lay4 = lambda shape: pl.BlockSpec((None,) + shape, lambda i, c: (layer, 0, 0, 0))
    resident = lambda shape: pl.BlockSpec(shape, lambda i, c: (0, 0), pipeline_mode=pl.Buffered(1))
    cast_next = next_weights is not None
    next_in_specs, next_out_specs, next_out_shape, next_args = [], [], [], ()
    next_bytes = 0
    if cast_next:
        for w in next_weights:
            rows, cols = w.shape[1] // (b * nc), w.shape[2]
            assert rows * b * nc == w.shape[1] and rows % 16 == 0
            next_in_specs.append(pl.BlockSpec((None, rows, cols), lambda i, c: (layer + 1, i * nc + c, 0)))
            next_out_specs.append(pl.BlockSpec((rows, cols), lambda i, c: (i * nc + c, 0)))
            next_out_shape.append(jax.ShapeDtypeStruct(w.shape[1:], BF16))
            next_bytes += 2 * rows * cols * (4 + 2)
        next_args = tuple(next_weights)
    state_dims = [(POOL_BUF, GROUP), (2, GROUP), (HEADS, HEAD_DIM, HEAD_DIM), (CHUNK, GROUP)]
    first = stacked is None
    if first:
        state_specs = [pl.BlockSpec((DEPTH, 1) + dims, lambda i, c, n=len(dims): (0, i) + (0,) * n)
                       for dims in state_dims]
        alias_specs, alias_args, aliases = [], (), {}
    else:
        state_specs = [pl.BlockSpec((None, 1) + dims, lambda i, c, n=len(dims): (layer, i) + (0,) * n)
                       for dims in state_dims]
        alias_specs = [pl.BlockSpec(memory_space=pl.ANY)] * len(stacked)
        alias_args = tuple(stacked)
        n_in = _N_BLOCK_INPUTS + len(next_in_specs)
        aliases = {n_in + k: 1 + k for k in range(len(stacked))}
    vmem = (d * IN_WIDTH + 4 * GROUP * d) * 2 + 2 * 2 * tm * d * 4 + 2 * tm * IN_WIDTH * 4 + next_bytes
    outs = pl.pallas_call(
        functools.partial(_block_prompt_kernel, start=start, tm=tm, cast_next=cast_next, layer=layer,
                          all_layers=first),
        grid=(b, nc),
        in_specs=[
            pl.BlockSpec((1, tm, d), lambda i, c: (i, c, 0)),
            lay3((1, d)),
            resident((d, IN_WIDTH)),
            resident((4 * GROUP, d)),
            pl.BlockSpec((tm, HEAD_DIM), lambda i, c: (c, 0)),
            pl.BlockSpec((tm, HEAD_DIM), lambda i, c: (c, 0)),
            const3((HEADS, CHUNK, CHUNK)),
            const3((HEADS, CHUNK, HEAD_DIM)),
            const3((HEADS, CHUNK, HEAD_DIM)),
            lay4((len(POOL_WINDOWS), HEAD_DIM, HEAD_DIM)),
            lay3((1, GROUP)),
            lay3((3, GROUP)),
            lay3((1, GROUP)),
            lay3((1, GROUP)),
            lay4((HEADS, CHUNK, CHUNK)),
            lay4((HEADS, CHUNK, HEAD_DIM)),
            *next_in_specs,
            *alias_specs,
        ],
        out_specs=[pl.BlockSpec((1, tm, d), lambda i, c: (i, c, 0)), *state_specs, *next_out_specs],
        out_shape=[
            jax.ShapeDtypeStruct((b, l, d), F32),
            *[jax.ShapeDtypeStruct((DEPTH, b) + dims, F32) for dims in state_dims],
            *next_out_shape,
        ],
        scratch_shapes=[
            pltpu.VMEM((_POOL_TOP + tm, GROUP), F32),
            pltpu.VMEM((_CONV_TOP + tm, GROUP), F32),
        ],
        input_output_aliases=aliases,
        compiler_params=pltpu.CompilerParams(
            dimension_semantics=("arbitrary", "arbitrary"),
            vmem_limit_bytes=_vmem_limit(vmem)),
        name="block_prompt",
    )(x, norm_g, w_in, w_out, cos, sin, dmask, qdec, kdec, poolw, pscale, convw, retg, sgug, sguw, sgub,
      *next_args, *alias_args)
    return outs[0], tuple(outs[1:5]), (tuple(outs[5:]) if cast_next else None)


_SEQ_BLOCK = 16


_N_SAMPLE_INPUTS = 14


def _mixer_sample_kernel(*refs, start, layer, all_layers):
    (p_ref, pool_ref, conv_ref, ret_ref, cos_ref, sin_ref, gam_ref,
     poolw_ref, pscale_ref, convw_ref, retg_ref, sgug_ref, sguw_ref, sgub_ref) = refs[:_N_SAMPLE_INPUTS]
    mix_ref, pool_out, conv_out, ret_out, sgu_out, qg_s, k_s, v_s, os_s = refs[-9:]
    TB = _SEQ_BLOCK

    if all_layers:
        stacked = (pool_out, conv_out, ret_out, sgu_out)
        for ref in stacked:
            for other in range(DEPTH):
                if other != layer:
                    ref[other] = jnp.zeros(ref.shape[1:], F32)
        pool_out, conv_out, ret_out, sgu_out = (ref.at[layer] for ref in stacked)

    a_all = p_ref[:, _A:_A + GROUP]
    for gi, win in enumerate(POOL_WINDOWS):
        lanes = slice(gi * HEAD_DIM, (gi + 1) * HEAD_DIM)
        a = a_all[:, lanes]
        wsum = a
        for j in range(1, win):
            wsum = wsum + pool_ref[POOL_BUF - j, :, lanes]
        cnt = float(min(start + 1, win))
        d = (wsum / cnt - a).astype(BF16)
        y = _dot(d, poolw_ref[gi]) * pscale_ref[:, lanes]
        mix_ref[:, _YA + gi * HEAD_DIM:_YA + (gi + 1) * HEAD_DIM] = y.astype(BF16)
    pool_out[0:POOL_BUF - 1] = pool_ref[1:POOL_BUF]
    pool_out[POOL_BUF - 1] = a_all

    z = p_ref[:, _CG:_CG + GROUP] * p_ref[:, _HH:_HH + GROUP]
    acc = (conv_ref[:, 0, :] * convw_ref[0:1, :] + conv_ref[:, 1, :] * convw_ref[1:2, :]
           + z * convw_ref[2:3, :])
    mix_ref[:, _YB:_YB + GROUP] = (p_ref[:, _BG:_BG + GROUP] * acc).astype(BF16)
    conv_out[:, 0, :] = conv_ref[:, 1, :]
    conv_out[:, 1, :] = z

    cos = cos_ref[...]
    sin = sin_ref[...]
    scores = []
    for h in range(HEADS):
        lanes = slice(h * HEAD_DIM, (h + 1) * HEAD_DIM)
        q = _rotate(p_ref[:, _Q + h * HEAD_DIM:_Q + (h + 1) * HEAD_DIM], cos, sin)
        k = _rotate(p_ref[:, _K + h * HEAD_DIM:_K + (h + 1) * HEAD_DIM], cos, sin) * (HEAD_DIM ** -0.5)
        scores.append(jnp.sum(q * k, axis=-1, keepdims=True))
        qg_s[:, lanes] = q * gam_ref[:, lanes]
        k_s[:, lanes] = k
    v_s[...] = p_ref[:, _V:_V + GROUP]

    first_row = lax.broadcasted_iota(jnp.int32, (8, HEAD_DIM), 0) == 0

    for b in range(TB):
        for h in range(HEADS):
            lanes = slice(h * HEAD_DIM, (h + 1) * HEAD_DIM)
            s_prev = ret_ref[b, h]
            q8 = jnp.broadcast_to(qg_s[b:b + 1, lanes], (8, HEAD_DIM))
            os_s[b:b + 1, lanes] = _dot(q8.astype(BF16), s_prev.astype(BF16))[0:1, :]
            k8 = jnp.where(first_row, jnp.broadcast_to(k_s[b:b + 1, lanes], (8, HEAD_DIM)), 0.0)
            v8 = jnp.broadcast_to(v_s[b:b + 1, lanes], (8, HEAD_DIM))
            kv = lax.dot_general(k8.astype(BF16), v8.astype(BF16), (((0,), (0,)), ((), ())),
                                 preferred_element_type=F32)
            ret_out[b, h] = gam_ref[:, lanes] * s_prev + kv

    for h in range(HEADS):
        lanes = slice(h * HEAD_DIM, (h + 1) * HEAD_DIM)
        o = scores[h] * v_s[:, lanes] + os_s[:, lanes]
        on = _layernorm(o) * retg_ref[:, lanes]
        g = p_ref[:, _G + h * HEAD_DIM:_G + (h + 1) * HEAD_DIM]
        mix_ref[:, _YC + h * HEAD_DIM:_YC + (h + 1) * HEAD_DIM] = (_silu(g) * on).astype(BF16)

    vn = _layernorm(p_ref[:, _VV:_VV + GROUP]) * sgug_ref[...]
    mix_ref[:, _YD:_YD + GROUP] = (p_ref[:, _U:_U + GROUP] * (sguw_ref[...] * vn + sgub_ref[...])).astype(BF16)
    sgu_out[:, 0, :] = vn


def mixer_sample(p, pool_state, conv_state, ret_state, tables, params, layer, stacked, *, start):
    n = p.shape[0]
    tb = _SEQ_BLOCK
    cos, sin, gam = tables
    poolw, pscale, convw, retg, sgug, sguw, sgub = params
    rows = lambda width: pl.BlockSpec((tb, width), lambda i: (i, 0))
    lay_rows = lambda r: pl.BlockSpec((None, tb, r, GROUP), lambda i: (layer, i, 0, 0))
    lay_pool = pl.BlockSpec((None, POOL_BUF, tb, GROUP), lambda i: (layer, 0, i, 0))
    lay_ret = pl.BlockSpec((None, tb, HEADS, HEAD_DIM, HEAD_DIM), lambda i: (layer, i, 0, 0, 0))
    const2 = lambda shape: pl.BlockSpec(shape, lambda i: (0, 0))
    lay3 = lambda shape: pl.BlockSpec((None,) + shape, lambda i: (layer, 0, 0))
    lay4 = lambda shape: pl.BlockSpec((None,) + shape, lambda i: (layer, 0, 0, 0))
    in_specs = [
        rows(IN_WIDTH),
        lay_pool,
        lay_rows(2),
        lay_ret,
        const2((1, HEAD_DIM)),
        const2((1, HEAD_DIM)),
        const2((1, GROUP)),
        lay4((len(POOL_WINDOWS), HEAD_DIM, HEAD_DIM)),
        lay3((1, GROUP)),
        lay3((3, GROUP)),
        lay3((1, GROUP)),
        lay3((1, GROUP)),
        lay3((1, GROUP)),
        lay3((1, GROUP)),
    ]
    args = (p, pool_state, conv_state, ret_state, cos, sin, gam,
            poolw, pscale, convw, retg, sgug, sguw, sgub)
    assert len(args) == _N_SAMPLE_INPUTS
    aliases = {}
    first = stacked is None
    state_block = tb * (16 + 8 + 8 + HEAD_DIM) * GROUP * 4
    vmem = 2 * (tb * IN_WIDTH * 4 + state_block * (1 + (DEPTH if first else 1)))
    if first:
        all_rows = lambda r: pl.BlockSpec((DEPTH, tb, r, GROUP), lambda i: (0, i, 0, 0))
        state_specs = [pl.BlockSpec((DEPTH, POOL_BUF, tb, GROUP), lambda i: (0, 0, i, 0)), all_rows(2),
                       pl.BlockSpec((DEPTH, tb, HEADS, HEAD_DIM, HEAD_DIM), lambda i: (0, i, 0, 0, 0)),
                       all_rows(1)]
    else:
        state_specs = [lay_pool, lay_rows(2), lay_ret, lay_rows(1)]
        aliases = {len(args) + k: 1 + k for k in range(len(stacked))}
        in_specs = in_specs + [pl.BlockSpec(memory_space=pl.ANY)] * len(stacked)
        args = args + tuple(stacked)
    outs = pl.pallas_call(
        functools.partial(_mixer_sample_kernel, start=start, layer=layer, all_layers=first),
        grid=(n // tb,),
        in_specs=in_specs,
        out_specs=[rows(4 * GROUP)] + state_specs,
        out_shape=[
            jax.ShapeDtypeStruct((n, 4 * GROUP), BF16),
            jax.ShapeDtypeStruct((DEPTH, POOL_BUF, n, GROUP), F32),
            jax.ShapeDtypeStruct((DEPTH, n, 2, GROUP), F32),
            jax.ShapeDtypeStruct((DEPTH, n, HEADS, HEAD_DIM, HEAD_DIM), F32),
            jax.ShapeDtypeStruct((DEPTH, n, 1, GROUP), F32),
        ],
        scratch_shapes=[pltpu.VMEM((tb, GROUP), F32)] * 4,
        input_output_aliases=aliases,
        compiler_params=pltpu.CompilerParams(
            dimension_semantics=("arbitrary",),
            vmem_limit_bytes=_vmem_limit(vmem)),
        name="mixer_sample",
    )(*args)
    return outs[0], tuple(outs[1:])


def _rope_tables(start, length):
    half = HEAD_DIM // 2
    inv = ROPE_BASE ** (-jnp.arange(half, dtype=F32) / half)
    pos = start + jnp.arange(length)
    ang = pos.astype(F32)[:, None] * inv[None, :]
    cos, sin = jnp.cos(ang), jnp.sin(ang)
    return jnp.concatenate([cos, cos], axis=-1), jnp.concatenate([-sin, sin], axis=-1)


def _log_gamma():
    return jnp.log1p(-(2.0 ** (-5.0 - jnp.arange(HEADS, dtype=F32))))


def _decay_tables(c):
    lg = _log_gamma()
    idx = jnp.arange(c, dtype=F32)
    diff = idx[:, None] - idx[None, :]
    dmask = jnp.where(diff >= 0, jnp.exp(jnp.maximum(diff, 0.0)[None] * lg[:, None, None]), 0.0)
    qdec = jnp.exp((idx + 1.0)[None, :] * lg[:, None])[:, :, None]
    kdec = jnp.exp((c - 1.0 - idx)[None, :] * lg[:, None])[:, :, None]
    wide = lambda t: jnp.broadcast_to(t, (HEADS, c, HEAD_DIM))
    return dmask, wide(qdec), wide(kdec)


def _per_head_lanes(t):
    return jnp.repeat(t, HEAD_DIM, axis=-1)[..., None, :]


def kernel(x_prompt, x_sample, state_pool, state_conv, state_ret, norm1_g, w_in, pool_w, pool_scale,
           conv_w, ret_norm_g, sgu_norm_g, sgu_w, sgu_b, w_out, norm2_g, w_gate_up, w_down,
           final_norm_g):
    bp, lp, _ = x_prompt.shape
    ns = x_sample.shape[0]

    pool_w_b = pool_w.astype(BF16)
    row3 = lambda t: t[:, None, :]
    norm1 = row3(norm1_g)
    norm2 = row3(norm2_g)
    final_g = final_norm_g[None, :]

    cos_p, sin_p = _rope_tables(0, lp)
    prompt_tables = (cos_p, sin_p) + _decay_tables(CHUNK)
    sgu_b_wide = jnp.broadcast_to(sgu_b[:, :, :, None], sgu_b.shape + (HEAD_DIM,))
    prompt_params = (pool_w_b, row3(pool_scale), conv_w, row3(ret_norm_g), row3(sgu_norm_g), sgu_w, sgu_b_wide)

    cos_s, sin_s = _rope_tables(PAST_LEN, 1)
    gam = _per_head_lanes(jnp.exp(_log_gamma()))
    sample_tables = (cos_s, sin_s, gam)
    sample_params = (pool_w_b, row3(pool_scale), conv_w, row3(ret_norm_g), row3(sgu_norm_g),
                     _per_head_lanes(sgu_w[:, :, 0, 0]), _per_head_lanes(sgu_b[:, :, 0]))
    pool_rows = jnp.transpose(state_pool, (0, 2, 1, 3))
    hp = x_prompt
    hs = x_sample.reshape(ns, D_MODEL)
    prompt_states = None
    sample_states = None
    next_b = None
    for l in range(DEPTH):
        fin = l == DEPTH - 1
        w_in_l, w_out_l = (w_in, w_out) if next_b is None else next_b
        p, w_in_b = norm_matmul(hs, norm1, w_in_l, l, tk=256)
        mix, sample_states = mixer_sample(p, pool_rows, state_conv, state_ret, sample_tables,
                                          sample_params, l, sample_states, start=PAST_LEN)
        hs, w_out_b = matmul_residual(mix, w_out_l, hs, l, tn=1024)
        hp, prompt_states, next_b = block_prompt(hp, norm1, w_in_b, w_out_b, prompt_tables, prompt_params,
                                                 l, None if fin else (w_in, w_out), prompt_states,
                                                 start=0, tm=256)
        hp = hp.reshape(bp * lp, D_MODEL)
        first, w_mlp_b = mlp_cast(hs, hp, norm2, w_gate_up, w_down, final_g, l, tp=_MLP_TILE, tf=256,
                                  final_norm=fin)
        hs = first[:ns]
        hp = mlp_rest(first, hp, norm2, w_mlp_b, final_g, l, skip_rows=ns, tm=_MLP_TILE, tf=512,
                      final_norm=fin).reshape(bp, lp, D_MODEL)
    pool_p, conv_p, ret_p, sgu_p = prompt_states
    pool_s, conv_s, ret_s, sgu_s = sample_states
    pool_s = jnp.transpose(pool_s, (0, 2, 1, 3))
    return (hp, hs.reshape(ns, 1, D_MODEL), pool_p, pool_s, conv_p, conv_s, ret_p, ret_s, sgu_p, sgu_s)
```

```python
import functools

import jax
import jax.numpy as jnp
from jax import lax
from jax.experimental import pallas as pl
from jax.experimental.pallas import tpu as pltpu

F32 = jnp.float32
BF16 = jnp.bfloat16

D_MODEL = 2048
DEPTH = 2
GROUP = 512
N_SPLITS = 10
IN_WIDTH = N_SPLITS * GROUP
POOL_WINDOWS = (2, 4, 8, 16)
POOL_BUF = 15
HEADS = 4
HEAD_DIM = 128
CHUNK = 128
ROPE_BASE = 10000.0
D_FF = 5632
NORM_EPS = 1e-6
PAST_LEN = 16384

_A, _BG, _CG, _HH, _Q, _K, _V, _G, _U, _VV = (i * GROUP for i in range(N_SPLITS))
_YA, _YB, _YC, _YD = (i * GROUP for i in range(4))

MIB = 1024 * 1024
VMEM_BYTES_V7X = 64 * MIB


VMEM_LIMIT_MAX = VMEM_BYTES_V7X - 4 * MIB


def _vmem_limit(estimate):
    return min(estimate + 4 * MIB, VMEM_LIMIT_MAX)


def _rms(x, g):
    ms = jnp.mean(x * x, axis=-1, keepdims=True)
    return x * lax.rsqrt(ms + NORM_EPS) * g


def _layernorm(x):
    mu = jnp.mean(x, axis=-1, keepdims=True)
    xc = x - mu
    return xc * lax.rsqrt(jnp.mean(xc * xc, axis=-1, keepdims=True) + NORM_EPS)


def _silu(x):
    return x * jax.nn.sigmoid(x)


def _dot(a, b):
    return jnp.dot(a, b, preferred_element_type=F32)


def _rotate(x, cos, sin_signed):
    return x * cos + pltpu.roll(x, HEAD_DIM // 2, 1) * sin_signed


def _weight_specs(w, layer, k, tn):
    if w.ndim == 3:
        return pl.BlockSpec((None, k, tn), lambda j: (layer, 0, j)), True
    return pl.BlockSpec((k, tn), lambda j: (0, j)), False


def _norm_matmul_kernel(x_ref, g_ref, w_ref, o_ref, *rest, emit):
    xn_ref = rest[-1]
    j = pl.program_id(0)
    tk = xn_ref.shape[-1]

    @pl.when(j == 0)
    def _():
        xn = _rms(x_ref[...], g_ref[...]).astype(BF16)
        for kb in range(xn_ref.shape[0]):
            xn_ref[kb] = xn[:, kb * tk:(kb + 1) * tk]

    wb = w_ref[...].astype(BF16)
    if emit:
        rest[0][...] = wb
    part = _dot(xn_ref[j], wb)

    @pl.when(j == 0)
    def _():
        o_ref[...] = part

    @pl.when(j > 0)
    def _():
        o_ref[...] += part


def norm_matmul(x, g, w, layer, *, tk):
    m, k = x.shape
    n = w.shape[-1]
    emit = w.ndim == 3
    w_spec = (pl.BlockSpec((None, tk, n), lambda j: (layer, j, 0)) if emit
              else pl.BlockSpec((tk, n), lambda j: (j, 0)))
    vmem = 2 * tk * n * (4 + 2) + 2 * m * (k + n) * 4 + m * k * 2
    outs = pl.pallas_call(
        functools.partial(_norm_matmul_kernel, emit=emit),
        grid=(k // tk,),
        in_specs=[
            pl.BlockSpec((m, k), lambda j: (0, 0)),
            pl.BlockSpec((None, 1, k), lambda j: (layer, 0, 0)),
            w_spec,
        ],
        out_specs=[pl.BlockSpec((m, n), lambda j: (0, 0))] + [pl.BlockSpec((tk, n), lambda j: (j, 0))] * emit,
        out_shape=[jax.ShapeDtypeStruct((m, n), F32)] + [jax.ShapeDtypeStruct((k, n), BF16)] * emit,
        scratch_shapes=[pltpu.VMEM((k // tk, m, tk), BF16)],
        compiler_params=pltpu.CompilerParams(
            dimension_semantics=("arbitrary",),
            vmem_limit_bytes=_vmem_limit(vmem)),
        name="norm_matmul",
    )(x, g, w)
    return tuple(outs) if emit else (outs[0], w)


def _matmul_residual_kernel(a_ref, w_ref, r_ref, o_ref, *rest, emit):
    wb = w_ref[...].astype(BF16)
    if emit:
        rest[0][...] = wb
    o_ref[...] = r_ref[...] + _dot(a_ref[...], wb)


def matmul_residual(a, w, res, layer, *, tn):
    m, k = a.shape
    n = w.shape[-1]
    w_spec, emit = _weight_specs(w, layer, k, tn)
    vmem = 2 * k * tn * (4 + 2) + 2 * m * (k * 2 + 2 * tn * 4)
    outs = pl.pallas_call(
        functools.partial(_matmul_residual_kernel, emit=emit),
        grid=(n // tn,),
        in_specs=[
            pl.BlockSpec((m, k), lambda j: (0, 0)),
            w_spec,
            pl.BlockSpec((m, tn), lambda j: (0, j)),
        ],
        out_specs=[pl.BlockSpec((m, tn), lambda j: (0, j))] + [pl.BlockSpec((k, tn), lambda j: (0, j))] * emit,
        out_shape=[jax.ShapeDtypeStruct((m, n), F32)] + [jax.ShapeDtypeStruct((k, n), BF16)] * emit,
        compiler_params=pltpu.CompilerParams(
            dimension_semantics=("arbitrary",),
            vmem_limit_bytes=_vmem_limit(vmem)),
        name="matmul_residual",
    )(a, w, res)
    return tuple(outs) if emit else (outs[0], w)


_MLP_TILE = 1024


def _mlp_step(first, last, x_refs, g_ref, weights, fg_ref, o_ref, hn_ref, final_norm):
    def delta():
        wg, wu, wd = weights()
        hn = hn_ref[...]
        act = (_silu(_dot(hn, wg)) * _dot(hn, wu)).astype(BF16)
        return _dot(act, wd)

    @pl.when(first)
    def _():
        row = 0
        for x_ref in x_refs:
            rows = slice(row, row + x_ref.shape[0])
            hn_ref[rows, :] = _rms(x_ref[...], g_ref[...]).astype(BF16)
            row += x_ref.shape[0]
        d = delta()
        row = 0
        for x_ref in x_refs:
            rows = slice(row, row + x_ref.shape[0])
            o_ref[rows, :] = x_ref[...] + d[rows, :]
            row += x_ref.shape[0]

    @pl.when(jnp.logical_not(first))
    def _():
        o_ref[...] += delta()

    if final_norm:
        @pl.when(last)
        def _():
            o_ref[...] = _rms(o_ref[...], fg_ref[...])


def _mlp_cast_kernel(*refs, final_norm, fuse_proj):
    refs = list(refs)
    if fuse_proj:
        mix_ref, wo_ref = refs[:2]
        xs_scr = refs[-1]
        refs = refs[2:-1]
    (xs_ref, xp_ref, g_ref, wg_ref, wu_ref, wd_ref, fg_ref,
     o_ref, wgb_ref, wub_ref, wdb_ref, hn_ref) = refs
    f = pl.program_id(0)
    if fuse_proj:
        @pl.when(f == 0)
        def _():
            xs_scr[...] = xs_ref[...] + _dot(mix_ref[...], wo_ref[...])
        xs_ref = xs_scr

    def cast_weights():
        wg, wu, wd = wg_ref[...].astype(BF16), wu_ref[...].astype(BF16), wd_ref[...].astype(BF16)
        wgb_ref[...] = wg
        wub_ref[...] = wu
        wdb_ref[...] = wd
        return wg, wu, wd

    _mlp_step(f == 0, f == pl.num_programs(0) - 1, (xs_ref, xp_ref), g_ref, cast_weights, fg_ref,
              o_ref, hn_ref, final_norm)


def _mlp_rest_kernel(first_hbm, x_ref, g_ref, wg_ref, wu_ref, wd_ref, fg_ref, o_ref, hn_ref, sem,
                     *, nf, skip_rows, final_norm):
    s = pl.program_id(0)
    tm = o_ref.shape[0]

    @pl.when(s == 0)
    def _():
        copy = pltpu.make_async_copy(first_hbm.at[pl.ds(skip_rows, tm), :], o_ref, sem)
        copy.start()
        copy.wait()

    @pl.when(s > 0)
    def _():
        f = (s - 1) % nf
        _mlp_step(f == 0, f == nf - 1, (x_ref,), g_ref, lambda: (wg_ref[...], wu_ref[...], wd_ref[...]),
                  fg_ref, o_ref, hn_ref, final_norm)


def mlp_cast(xs, xp, g, w_gate_up, w_down, final_g, layer, proj, *, tp, tf, final_norm):
    ns, d = xs.shape
    m = ns + tp
    nf = D_FF // tf
    once = pl.Buffered(1)
    fuse = proj is not None
    proj_specs, proj_scratch, proj_bytes = [], [], 0
    if fuse:
        proj_specs = [pl.BlockSpec(a.shape, lambda f: (0, 0), pipeline_mode=once) for a in proj]
        proj_scratch = [pltpu.VMEM((ns, d), F32)]
        proj_bytes = sum(a.size * 2 for a in proj) + ns * d * 4
    vmem = 2 * m * d * 4 + m * d * 2 + 2 * 3 * d * tf * (4 + 2) + 3 * m * tf * 4 + proj_bytes
    outs = pl.pallas_call(
        functools.partial(_mlp_cast_kernel, final_norm=final_norm, fuse_proj=fuse),
        grid=(nf,),
        in_specs=[
            *proj_specs,
            pl.BlockSpec((ns, d), lambda f: (0, 0), pipeline_mode=once),
            pl.BlockSpec((tp, d), lambda f: (0, 0), pipeline_mode=once),
            pl.BlockSpec((None, 1, d), lambda f: (layer, 0, 0)),
            pl.BlockSpec((None, d, tf), lambda f: (layer, 0, f)),
            pl.BlockSpec((None, d, tf), lambda f: (layer, 0, nf + f)),
            pl.BlockSpec((None, tf, d), lambda f: (layer, f, 0)),
            pl.BlockSpec((1, d), lambda f: (0, 0)),
        ],
        out_specs=[
            pl.BlockSpec((m, d), lambda f: (0, 0), pipeline_mode=once),
            pl.BlockSpec((d, tf), lambda f: (0, f)),
            pl.BlockSpec((d, tf), lambda f: (0, f)),
            pl.BlockSpec((tf, d), lambda f: (f, 0)),
        ],
        out_shape=[jax.ShapeDtypeStruct((m, d), F32),
                   jax.ShapeDtypeStruct((d, D_FF), BF16), jax.ShapeDtypeStruct((d, D_FF), BF16),
                   jax.ShapeDtypeStruct((D_FF, d), BF16)],
        scratch_shapes=[pltpu.VMEM((m, d), BF16), *proj_scratch],
        compiler_params=pltpu.CompilerParams(
            dimension_semantics=("arbitrary",),
            vmem_limit_bytes=_vmem_limit(vmem)),
        name="mlp_cast",
    )(*(proj or ()), xs, xp, g, w_gate_up, w_gate_up, w_down, final_g)
    return outs[0], tuple(outs[1:])


def mlp_rest(first, x, g, weights, final_g, layer, *, skip_rows, tm, tf, final_norm):
    m, d = x.shape
    nf = D_FF // tf
    step = lambda s: jnp.maximum(s - 1, 0)
    x_tile = lambda s: 1 + step(s) // nf
    f_tile = lambda s: step(s) % nf
    vmem = 2 * 2 * tm * d * 4 + tm * d * 2 + 2 * 3 * d * tf * 2 + 3 * tm * tf * 4
    return pl.pallas_call(
        functools.partial(_mlp_rest_kernel, nf=nf, skip_rows=skip_rows, final_norm=final_norm),
        grid=(1 + (m // tm - 1) * nf,),
        in_specs=[
            pl.BlockSpec(memory_space=pl.ANY),
            pl.BlockSpec((tm, d), lambda s: (x_tile(s), 0)),
            pl.BlockSpec((None, 1, d), lambda s: (layer, 0, 0)),
            pl.BlockSpec((d, tf), lambda s: (0, f_tile(s))),
            pl.BlockSpec((d, tf), lambda s: (0, f_tile(s))),
            pl.BlockSpec((tf, d), lambda s: (f_tile(s), 0)),
            pl.BlockSpec((1, d), lambda s: (0, 0)),
        ],
        out_specs=pl.BlockSpec((tm, d), lambda s: (jnp.where(s == 0, 0, x_tile(s)), 0)),
        out_shape=jax.ShapeDtypeStruct((m, d), F32),
        scratch_shapes=[pltpu.VMEM((tm, d), BF16), pltpu.SemaphoreType.DMA(())],
        compiler_params=pltpu.CompilerParams(
            dimension_semantics=("arbitrary",),
            vmem_limit_bytes=_vmem_limit(vmem)),
        name="mlp_rest",
    )(first, x, g, *weights, final_g)


_POOL_TOP = 16
_CONV_TOP = 8


def _block_prompt_kernel(x_ref, g1_ref, win_ref, wout_ref, cos_ref, sin_ref, dmask_ref, qdec_ref, kdec_ref,
                         poolw_ref, pscale_ref, convw_ref, retg_ref, sgug_ref, sguw_ref, sgub_ref,
                         *rest, start, tm, cast_next, layer, all_layers):
    *rest, pool_ext, conv_ext = rest
    if cast_next:
        win_next_ref, wout_next_ref, *rest, win_next_out, wout_next_out = rest
    h_out, pool_out, conv_out, ret_out, sgu_out = rest[-5:]
    stacked = (pool_out, conv_out, ret_out, sgu_out)
    if all_layers:
        pool_out, conv_out, ret_out, sgu_out = (ref.at[layer] for ref in stacked)
    c = pl.program_id(1)
    T = CHUNK
    n_chunks = tm // T

    @pl.when(c == 0)
    def _():
        pool_ext[0:_POOL_TOP, :] = jnp.zeros((_POOL_TOP, GROUP), F32)
        conv_ext[0:_CONV_TOP, :] = jnp.zeros((_CONV_TOP, GROUP), F32)
        ret_out[...] = jnp.zeros(ret_out.shape, F32)
        if all_layers:
            for ref in stacked:
                for other in range(DEPTH):
                    if other != layer:
                        ref[other] = jnp.zeros(ref.shape[1:], F32)

    x = x_ref[0]
    xn = _rms(x, g1_ref[...]).astype(BF16)

    def proj(col):
        return _dot(xn, win_ref[:, col:col + GROUP])

    def out_proj(y, col):
        return _dot(y, wout_ref[col:col + GROUP, :])

    head_lanes = [slice(hd * HEAD_DIM, (hd + 1) * HEAD_DIM) for hd in range(HEADS)]
    chunk_rows = [slice(j * T, (j + 1) * T) for j in range(n_chunks)]

    def pool_mixer(a_proj):
        pool_ext[_POOL_TOP:_POOL_TOP + tm, :] = a_proj
        pos1 = lax.broadcasted_iota(jnp.int32, (tm, HEAD_DIM), 0) + (start + 1) + c * tm
        y = []
        for gi, win in enumerate(POOL_WINDOWS):
            lanes = head_lanes[gi]
            a = pool_ext[_POOL_TOP:_POOL_TOP + tm, lanes]
            wsum = a
            for j in range(1, win):
                wsum = wsum + pool_ext[_POOL_TOP - j:_POOL_TOP - j + tm, lanes]
            cnt = jnp.minimum(pos1, win).astype(F32)
            d = (wsum / cnt - a).astype(BF16)
            y.append((_dot(d, poolw_ref[gi]) * pscale_ref[:, lanes]).astype(BF16))
        pool_out[0] = pool_ext[_POOL_TOP + tm - POOL_BUF:_POOL_TOP + tm, :]
        pool_ext[_POOL_TOP - POOL_BUF:_POOL_TOP, :] = pool_ext[_POOL_TOP + tm - POOL_BUF:_POOL_TOP + tm, :]
        return jnp.concatenate(y, axis=-1)

    def conv_mixer(bg, cg, hh):
        z = cg * hh
        conv_ext[_CONV_TOP:_CONV_TOP + tm, :] = z
        acc = (conv_ext[_CONV_TOP - 2:_CONV_TOP - 2 + tm, :] * convw_ref[0:1, :]
               + conv_ext[_CONV_TOP - 1:_CONV_TOP - 1 + tm, :] * convw_ref[1:2, :]
               + z * convw_ref[2:3, :])
        conv_out[0] = conv_ext[_CONV_TOP + tm - 2:_CONV_TOP + tm, :]
        conv_ext[_CONV_TOP - 2:_CONV_TOP, :] = conv_ext[_CONV_TOP + tm - 2:_CONV_TOP + tm, :]
        return (bg * acc).astype(BF16)

    def retention(q_all, k_all, v_all, g_all, j, hd):
        rows, lanes = chunk_rows[j], head_lanes[hd]
        cos = cos_ref[rows, :]
        sin = sin_ref[rows, :]
        q = _rotate(q_all[rows, lanes], cos, sin)
        k = _rotate(k_all[rows, lanes], cos, sin) * (HEAD_DIM ** -0.5)
        v = v_all[rows, lanes].astype(BF16)
        s_prev = ret_out[0, hd]
        scores = lax.dot_general(q.astype(BF16), k.astype(BF16), (((1,), (1,)), ((), ())),
                                 preferred_element_type=F32) * dmask_ref[hd]
        o = _dot(scores.astype(BF16), v) + _dot((q * qdec_ref[hd]).astype(BF16), s_prev.astype(BF16))
        kv = lax.dot_general((k * kdec_ref[hd]).astype(BF16), v, (((0,), (0,)), ((), ())),
                             preferred_element_type=F32)
        ret_out[0, hd] = qdec_ref[hd, T - 1:T, :] * s_prev + kv
        on = _layernorm(o) * retg_ref[:, lanes]
        return (_silu(g_all[rows, lanes]) * on).astype(BF16)

    def sgu_mixer(u_all, vn):
        tri = lax.broadcasted_iota(jnp.int32, (T, T), 0) >= lax.broadcasted_iota(jnp.int32, (T, T), 1)
        y = [[None] * HEADS for _ in range(n_chunks)]
        for hd in range(HEADS):
            lanes = head_lanes[hd]
            w = jnp.where(tri, sguw_ref[hd], 0.0).astype(BF16)
            cols = jnp.concatenate([vn[rows, lanes] for rows in chunk_rows], axis=-1).astype(BF16)
            mixed = _dot(w, cols)
            for j, rows in enumerate(chunk_rows):
                m = mixed[:, j * T:(j + 1) * T] + sgub_ref[hd]
                y[j][hd] = (u_all[rows, lanes] * m).astype(BF16)
        return jnp.concatenate([jnp.concatenate(r, axis=-1) for r in y], axis=0)

    q_all, k_all, v_all, g_all = proj(_Q), proj(_K), proj(_V), proj(_G)
    ret = functools.partial(retention, q_all, k_all, v_all, g_all)
    y_c = [[None] * HEADS for _ in range(n_chunks)]
    y_c[0][0] = ret(0, 0)
    a_proj = proj(_A)
    y_c[0][1] = ret(0, 1)
    cg = proj(_CG)
    y_c[0][2] = ret(0, 2)
    hh = proj(_HH)
    y_c[0][3] = ret(0, 3)
    bg = proj(_BG)
    later = iter([lambda: proj(_VV), lambda: proj(_U), lambda: pool_mixer(a_proj),
                  lambda: conv_mixer(bg, cg, hh)])
    fills = []
    for j in range(1, n_chunks):
        for hd in range(HEADS):
            y_c[j][hd] = ret(j, hd)
            nxt = next(later, None)
            if nxt is not None:
                fills.append(nxt())
    fills += [f() for f in later]
    vv, u_all, y_a, y_b = fills
    h = x + out_proj(y_a, _YA)
    vn = _layernorm(vv) * sgug_ref[...]
    y_d = sgu_mixer(u_all, vn)
    h = h + out_proj(y_b, _YB)
    h = h + out_proj(jnp.concatenate([jnp.concatenate(r, axis=-1) for r in y_c], axis=0), _YC)
    h_out[0] = h + out_proj(y_d, _YD)
    sgu_out[0] = vn[tm - T:tm, :]

    if cast_next:
        win_next_out[...] = win_next_ref[...].astype(BF16)
        wout_next_out[...] = wout_next_ref[...].astype(BF16)


_N_BLOCK_INPUTS = 16


def block_prompt(x, norm_g, w_in, w_out, tables, params, layer, next_weights, stacked, *, start, tm):
    b, l, d = x.shape
    nc = l // tm
    cos, sin, dmask, qdec, kdec = tables
    poolw, pscale, convw, retg, sgug, sguw, sgub = params
    const3 = lambda shape: pl.BlockSpec(shape, lambda i, c: (0, 0, 0))
    lay3 = lambda shape: pl.BlockSpec((None,) + shape, lambda i, c: (layer, 0, 0))
    lay4 = lambda shape: pl.BlockSpec((None,) + shape, lambda i, c: (layer, 0, 0, 0))
    resident = lambda shape: pl.BlockSpec(shape, lambda i, c: (0, 0), pipeline_mode=pl.Buffered(1))
    cast_next = next_weights is not None
    next_in_specs, next_out_specs, next_out_shape, next_args = [], [], [], ()
    next_bytes = 0
    if cast_next:
        for w in next_weights:
            rows, cols = w.shape[1] // (b * nc), w.shape[2]
            assert rows * b * nc == w.shape[1] and rows % 16 == 0
            next_in_specs.append(pl.BlockSpec((None, rows, cols), lambda i, c: (layer + 1, i * nc + c, 0)))
            next_out_specs.append(pl.BlockSpec((rows, cols), lambda i, c: (i * nc + c, 0)))
            next_out_shape.append(jax.ShapeDtypeStruct(w.shape[1:], BF16))
            next_bytes += 2 * rows * cols * (4 + 2)
        next_args = tuple(next_weights)
    state_dims = [(POOL_BUF, GROUP), (2, GROUP), (HEADS, HEAD_DIM, HEAD_DIM), (CHUNK, GROUP)]
    first = stacked is None
    if first:
        state_specs = [pl.BlockSpec((DEPTH, 1) + dims, lambda i, c, n=len(dims): (0, i) + (0,) * n)
                       for dims in state_dims]
        alias_specs, alias_args, aliases = [], (), {}
    else:
        state_specs = [pl.BlockSpec((None, 1) + dims, lambda i, c, n=len(dims): (layer, i) + (0,) * n)
                       for dims in state_dims]
        alias_specs = [pl.BlockSpec(memory_space=pl.ANY)] * len(stacked)
        alias_args = tuple(stacked)
        n_in = _N_BLOCK_INPUTS + len(next_in_specs)
        aliases = {n_in + k: 1 + k for k in range(len(stacked))}
    vmem = (d * IN_WIDTH + 4 * GROUP * d) * 2 + 2 * 2 * tm * d * 4 + 2 * tm * IN_WIDTH * 4 + next_bytes
    outs = pl.pallas_call(
        functools.partial(_block_prompt_kernel, start=start, tm=tm, cast_next=cast_next, layer=layer,
                          all_layers=first),
        grid=(b, nc),
        in_specs=[
            pl.BlockSpec((1, tm, d), lambda i, c: (i, c, 0)),
            lay3((1, d)),
            resident((d, IN_WIDTH)),
            resident((4 * GROUP, d)),
            pl.BlockSpec((tm, HEAD_DIM), lambda i, c: (c, 0)),
            pl.BlockSpec((tm, HEAD_DIM), lambda i, c: (c, 0)),
            const3((HEADS, CHUNK, CHUNK)),
            const3((HEADS, CHUNK, HEAD_DIM)),
            const3((HEADS, CHUNK, HEAD_DIM)),
            lay4((len(POOL_WINDOWS), HEAD_DIM, HEAD_DIM)),
            lay3((1, GROUP)),
            lay3((3, GROUP)),
            lay3((1, GROUP)),
            lay3((1, GROUP)),
            lay4((HEADS, CHUNK, CHUNK)),
            lay4((HEADS, CHUNK, HEAD_DIM)),
            *next_in_specs,
            *alias_specs,
        ],
        out_specs=[pl.BlockSpec((1, tm, d), lambda i, c: (i, c, 0)), *state_specs, *next_out_specs],
        out_shape=[
            jax.ShapeDtypeStruct((b, l, d), F32),
            *[jax.ShapeDtypeStruct((DEPTH, b) + dims, F32) for dims in state_dims],
            *next_out_shape,
        ],
        scratch_shapes=[
            pltpu.VMEM((_POOL_TOP + tm, GROUP), F32),
            pltpu.VMEM((_CONV_TOP + tm, GROUP), F32),
        ],
        input_output_aliases=aliases,
        compiler_params=pltpu.CompilerParams(
            dimension_semantics=("arbitrary", "arbitrary"),
            vmem_limit_bytes=_vmem_limit(vmem)),
        name="block_prompt",
    )(x, norm_g, w_in, w_out, cos, sin, dmask, qdec, kdec, poolw, pscale, convw, retg, sgug, sguw, sgub,
      *next_args, *alias_args)
    return outs[0], tuple(outs[1:5]), (tuple(outs[5:]) if cast_next else None)


_SEQ_BLOCK = 16


_N_SAMPLE_INPUTS = 14


def _mixer_sample_kernel(*refs, start, layer, all_layers):
    (p_ref, pool_ref, conv_ref, ret_ref, cos_ref, sin_ref, gam_ref,
     poolw_ref, pscale_ref, convw_ref, retg_ref, sgug_ref, sguw_ref, sgub_ref) = refs[:_N_SAMPLE_INPUTS]
    mix_ref, pool_out, conv_out, ret_out, sgu_out, qg_s, k_s, v_s, os_s = refs[-9:]
    TB = _SEQ_BLOCK

    if all_layers:
        stacked = (pool_out, conv_out, ret_out, sgu_out)
        for ref in stacked:
            for other in range(DEPTH):
                if other != layer:
                    ref[other] = jnp.zeros(ref.shape[1:], F32)
        pool_out, conv_out, ret_out, sgu_out = (ref.at[layer] for ref in stacked)

    a_all = p_ref[:, _A:_A + GROUP]
    for gi, win in enumerate(POOL_WINDOWS):
        lanes = slice(gi * HEAD_DIM, (gi + 1) * HEAD_DIM)
        a = a_all[:, lanes]
        wsum = a
        for j in range(1, win):
            wsum = wsum + pool_ref[POOL_BUF - j, :, lanes]
        cnt = float(min(start + 1, win))
        d = (wsum / cnt - a).astype(BF16)
        y = _dot(d, poolw_ref[gi]) * pscale_ref[:, lanes]
        mix_ref[:, _YA + gi * HEAD_DIM:_YA + (gi + 1) * HEAD_DIM] = y.astype(BF16)
    pool_out[0:POOL_BUF - 1] = pool_ref[1:POOL_BUF]
    pool_out[POOL_BUF - 1] = a_all

    z = p_ref[:, _CG:_CG + GROUP] * p_ref[:, _HH:_HH + GROUP]
    acc = (conv_ref[:, 0, :] * convw_ref[0:1, :] + conv_ref[:, 1, :] * convw_ref[1:2, :]
           + z * convw_ref[2:3, :])
    mix_ref[:, _YB:_YB + GROUP] = (p_ref[:, _BG:_BG + GROUP] * acc).astype(BF16)
    conv_out[:, 0, :] = conv_ref[:, 1, :]
    conv_out[:, 1, :] = z

    cos = cos_ref[...]
    sin = sin_ref[...]
    scores = []
    for h in range(HEADS):
        lanes = slice(h * HEAD_DIM, (h + 1) * HEAD_DIM)
        q = _rotate(p_ref[:, _Q + h * HEAD_DIM:_Q + (h + 1) * HEAD_DIM], cos, sin)
        k = _rotate(p_ref[:, _K + h * HEAD_DIM:_K + (h + 1) * HEAD_DIM], cos, sin) * (HEAD_DIM ** -0.5)
        scores.append(jnp.sum(q * k, axis=-1, keepdims=True))
        qg_s[:, lanes] = q * gam_ref[:, lanes]
        k_s[:, lanes] = k
    v_s[...] = p_ref[:, _V:_V + GROUP]

    first_row = lax.broadcasted_iota(jnp.int32, (8, HEAD_DIM), 0) == 0

    for b in range(TB):
        for h in range(HEADS):
            lanes = slice(h * HEAD_DIM, (h + 1) * HEAD_DIM)
            s_prev = ret_ref[b, h]
            q8 = jnp.broadcast_to(qg_s[b:b + 1, lanes], (8, HEAD_DIM))
            os_s[b:b + 1, lanes] = _dot(q8.astype(BF16), s_prev.astype(BF16))[0:1, :]
            k8 = jnp.where(first_row, jnp.broadcast_to(k_s[b:b + 1, lanes], (8, HEAD_DIM)), 0.0)
            v8 = jnp.broadcast_to(v_s[b:b + 1, lanes], (8, HEAD_DIM))
            kv = lax.dot_general(k8.astype(BF16), v8.astype(BF16), (((0,), (0,)), ((), ())),
                                 preferred_element_type=F32)
            ret_out[b, h] = gam_ref[:, lanes] * s_prev + kv

    for h in range(HEADS):
        lanes = slice(h * HEAD_DIM, (h + 1) * HEAD_DIM)
        o = scores[h] * v_s[:, lanes] + os_s[:, lanes]
        on = _layernorm(o) * retg_ref[:, lanes]
        g = p_ref[:, _G + h * HEAD_DIM:_G + (h + 1) * HEAD_DIM]
        mix_ref[:, _YC + h * HEAD_DIM:_YC + (h + 1) * HEAD_DIM] = (_silu(g) * on).astype(BF16)

    vn = _layernorm(p_ref[:, _VV:_VV + GROUP]) * sgug_ref[...]
    mix_ref[:, _YD:_YD + GROUP] = (p_ref[:, _U:_U + GROUP] * (sguw_ref[...] * vn + sgub_ref[...])).astype(BF16)
    sgu_out[:, 0, :] = vn


def mixer_sample(p, pool_state, conv_state, ret_state, tables, params, layer, stacked, *, start):
    n = p.shape[0]
    tb = _SEQ_BLOCK
    cos, sin, gam = tables
    poolw, pscale, convw, retg, sgug, sguw, sgub = params
    rows = lambda width: pl.BlockSpec((tb, width), lambda i: (i, 0))
    lay_rows = lambda r: pl.BlockSpec((None, tb, r, GROUP), lambda i: (layer, i, 0, 0))
    lay_pool = pl.BlockSpec((None, POOL_BUF, tb, GROUP), lambda i: (layer, 0, i, 0))
    lay_ret = pl.BlockSpec((None, tb, HEADS, HEAD_DIM, HEAD_DIM), lambda i: (layer, i, 0, 0, 0))
    const2 = lambda shape: pl.BlockSpec(shape, lambda i: (0, 0))
    lay3 = lambda shape: pl.BlockSpec((None,) + shape, lambda i: (layer, 0, 0))
    lay4 = lambda shape: pl.BlockSpec((None,) + shape, lambda i: (layer, 0, 0, 0))
    in_specs = [
        rows(IN_WIDTH),
        lay_pool,
        lay_rows(2),
        lay_ret,
        const2((1, HEAD_DIM)),
        const2((1, HEAD_DIM)),
        const2((1, GROUP)),
        lay4((len(POOL_WINDOWS), HEAD_DIM, HEAD_DIM)),
        lay3((1, GROUP)),
        lay3((3, GROUP)),
        lay3((1, GROUP)),
        lay3((1, GROUP)),
        lay3((1, GROUP)),
        lay3((1, GROUP)),
    ]
    args = (p, pool_state, conv_state, ret_state, cos, sin, gam,
            poolw, pscale, convw, retg, sgug, sguw, sgub)
    assert len(args) == _N_SAMPLE_INPUTS
    aliases = {}
    first = stacked is None
    state_block = tb * (16 + 8 + 8 + HEAD_DIM) * GROUP * 4
    vmem = 2 * (tb * IN_WIDTH * 4 + state_block * (1 + (DEPTH if first else 1)))
    if first:
        all_rows = lambda r: pl.BlockSpec((DEPTH, tb, r, GROUP), lambda i: (0, i, 0, 0))
        state_specs = [pl.BlockSpec((DEPTH, POOL_BUF, tb, GROUP), lambda i: (0, 0, i, 0)), all_rows(2),
                       pl.BlockSpec((DEPTH, tb, HEADS, HEAD_DIM, HEAD_DIM), lambda i: (0, i, 0, 0, 0)),
                       all_rows(1)]
    else:
        state_specs = [lay_pool, lay_rows(2), lay_ret, lay_rows(1)]
        aliases = {len(args) + k: 1 + k for k in range(len(stacked))}
        in_specs = in_specs + [pl.BlockSpec(memory_space=pl.ANY)] * len(stacked)
        args = args + tuple(stacked)
    outs = pl.pallas_call(
        functools.partial(_mixer_sample_kernel, start=start, layer=layer, all_layers=first),
        grid=(n // tb,),
        in_specs=in_specs,
        out_specs=[rows(4 * GROUP)] + state_specs,
        out_shape=[
            jax.ShapeDtypeStruct((n, 4 * GROUP), BF16),
            jax.ShapeDtypeStruct((DEPTH, POOL_BUF, n, GROUP), F32),
            jax.ShapeDtypeStruct((DEPTH, n, 2, GROUP), F32),
            jax.ShapeDtypeStruct((DEPTH, n, HEADS, HEAD_DIM, HEAD_DIM), F32),
            jax.ShapeDtypeStruct((DEPTH, n, 1, GROUP), F32),
        ],
        scratch_shapes=[pltpu.VMEM((tb, GROUP), F32)] * 4,
        input_output_aliases=aliases,
        compiler_params=pltpu.CompilerParams(
            dimension_semantics=("arbitrary",),
            vmem_limit_bytes=_vmem_limit(vmem)),
        name="mixer_sample",
    )(*args)
    return outs[0], tuple(outs[1:])


def _rope_tables(start, length):
    half = HEAD_DIM // 2
    inv = ROPE_BASE ** (-jnp.arange(half, dtype=F32) / half)
    pos = start + jnp.arange(length)
    ang = pos.astype(F32)[:, None] * inv[None, :]
    cos, sin = jnp.cos(ang), jnp.sin(ang)
    return jnp.concatenate([cos, cos], axis=-1), jnp.concatenate([-sin, sin], axis=-1)


def _log_gamma():
    return jnp.log1p(-(2.0 ** (-5.0 - jnp.arange(HEADS, dtype=F32))))


def _decay_tables(c):
    lg = _log_gamma()
    idx = jnp.arange(c, dtype=F32)
    diff = idx[:, None] - idx[None, :]
    dmask = jnp.where(diff >= 0, jnp.exp(jnp.maximum(diff, 0.0)[None] * lg[:, None, None]), 0.0)
    qdec = jnp.exp((idx + 1.0)[None, :] * lg[:, None])[:, :, None]
    kdec = jnp.exp((c - 1.0 - idx)[None, :] * lg[:, None])[:, :, None]
    wide = lambda t: jnp.broadcast_to(t, (HEADS, c, HEAD_DIM))
    return dmask, wide(qdec), wide(kdec)


def _per_head_lanes(t):
    return jnp.repeat(t, HEAD_DIM, axis=-1)[..., None, :]


def kernel(x_prompt, x_sample, state_pool, state_conv, state_ret, norm1_g, w_in, pool_w, pool_scale,
           conv_w, ret_norm_g, sgu_norm_g, sgu_w, sgu_b, w_out, norm2_g, w_gate_up, w_down,
           final_norm_g):
    bp, lp, _ = x_prompt.shape
    ns = x_sample.shape[0]

    pool_w_b = pool_w.astype(BF16)
    row3 = lambda t: t[:, None, :]
    norm1 = row3(norm1_g)
    norm2 = row3(norm2_g)
    final_g = final_norm_g[None, :]

    cos_p, sin_p = _rope_tables(0, lp)
    prompt_tables = (cos_p, sin_p) + _decay_tables(CHUNK)
    sgu_b_wide = jnp.broadcast_to(sgu_b[:, :, :, None], sgu_b.shape + (HEAD_DIM,))
    prompt_params = (pool_w_b, row3(pool_scale), conv_w, row3(ret_norm_g), row3(sgu_norm_g), sgu_w, sgu_b_wide)

    cos_s, sin_s = _rope_tables(PAST_LEN, 1)
    gam = _per_head_lanes(jnp.exp(_log_gamma()))
    sample_tables = (cos_s, sin_s, gam)
    sample_params = (pool_w_b, row3(pool_scale), conv_w, row3(ret_norm_g), row3(sgu_norm_g),
                     _per_head_lanes(sgu_w[:, :, 0, 0]), _per_head_lanes(sgu_b[:, :, 0]))
    pool_rows = jnp.transpose(state_pool, (0, 2, 1, 3))
    hp = x_prompt
    hs = x_sample.reshape(ns, D_MODEL)
    prompt_states = None
    sample_states = None
    next_b = None
    for l in range(DEPTH):
        fin = l == DEPTH - 1
        w_in_l, w_out_l = (w_in, w_out) if next_b is None else next_b
        p, w_in_b = norm_matmul(hs, norm1, w_in_l, l, tk=256)
        mix, sample_states = mixer_sample(p, pool_rows, state_conv, state_ret, sample_tables,
                                          sample_params, l, sample_states, start=PAST_LEN)
        if next_b is None:
            hs, w_out_b = matmul_residual(mix, w_out_l, hs, l, tn=1024)
            proj = None
        else:
            w_out_b, proj = w_out_l, (mix, w_out_l)
        hp, prompt_states, next_b = block_prompt(hp, norm1, w_in_b, w_out_b, prompt_tables, prompt_params,
                                                 l, None if fin else (w_in, w_out), prompt_states,
                                                 start=0, tm=256)
        hp = hp.reshape(bp * lp, D_MODEL)
        first, w_mlp_b = mlp_cast(hs, hp, norm2, w_gate_up, w_down, final_g, l, proj, tp=_MLP_TILE,
                                  tf=256, final_norm=fin)
        hs = first[:ns]
        hp = mlp_rest(first, hp, norm2, w_mlp_b, final_g, l, skip_rows=ns, tm=_MLP_TILE, tf=512,
                      final_norm=fin).reshape(bp, lp, D_MODEL)
    pool_p, conv_p, ret_p, sgu_p = prompt_states
    pool_s, conv_s, ret_s, sgu_s = sample_states
    pool_s = jnp.transpose(pool_s, (0, 2, 1, 3))
    return (hp, hs.reshape(ns, 1, D_MODEL), pool_p, pool_s, conv_p, conv_s, ret_p, ret_s, sgu_p, sgu_s)
```

```python
import functools

import jax
import jax.numpy as jnp
from jax import lax
from jax.experimental import pallas as pl
from jax.experimental.pallas import tpu as pltpu

F32 = jnp.float32
BF16 = jnp.bfloat16

D_MODEL = 2048
DEPTH = 2
GROUP = 512
N_SPLITS = 10
IN_WIDTH = N_SPLITS * GROUP
POOL_WINDOWS = (2, 4, 8, 16)
POOL_BUF = 15
HEADS = 4
HEAD_DIM = 128
CHUNK = 128
ROPE_BASE = 10000.0
D_FF = 5632
NORM_EPS = 1e-6
PAST_LEN = 16384

_A, _BG, _CG, _HH, _Q, _K, _V, _G, _U, _VV = (i * GROUP for i in range(N_SPLITS))
_YA, _YB, _YC, _YD = (i * GROUP for i in range(4))

MIB = 1024 * 1024
VMEM_BYTES_V7X = 64 * MIB


VMEM_LIMIT_MAX = VMEM_BYTES_V7X - 4 * MIB


def _vmem_limit(estimate):
    return min(estimate + 4 * MIB, VMEM_LIMIT_MAX)


def _rms(x, g):
    ms = jnp.mean(x * x, axis=-1, keepdims=True)
    return x * lax.rsqrt(ms + NORM_EPS) * g


def _layernorm(x):
    mu = jnp.mean(x, axis=-1, keepdims=True)
    xc = x - mu
    return xc * lax.rsqrt(jnp.mean(xc * xc, axis=-1, keepdims=True) + NORM_EPS)


def _silu(x):
    return x * jax.nn.sigmoid(x)


def _dot(a, b):
    return jnp.dot(a, b, preferred_element_type=F32)


def _rotate(x, cos, sin_signed):
    return x * cos + pltpu.roll(x, HEAD_DIM // 2, 1) * sin_signed


def _weight_specs(w, layer, k, tn):
    if w.ndim == 3:
        return pl.BlockSpec((None, k, tn), lambda j: (layer, 0, j)), True
    return pl.BlockSpec((k, tn), lambda j: (0, j)), False


def _norm_matmul_kernel(x_ref, g_ref, w_ref, o_ref, *rest, emit):
    xn_ref = rest[-1]
    j = pl.program_id(0)
    tk = xn_ref.shape[-1]

    @pl.when(j == 0)
    def _():
        xn = _rms(x_ref[...], g_ref[...]).astype(BF16)
        for kb in range(xn_ref.shape[0]):
            xn_ref[kb] = xn[:, kb * tk:(kb + 1) * tk]

    wb = w_ref[...].astype(BF16)
    if emit:
        rest[0][...] = wb
    part = _dot(xn_ref[j], wb)

    @pl.when(j == 0)
    def _():
        o_ref[...] = part

    @pl.when(j > 0)
    def _():
        o_ref[...] += part


def norm_matmul(x, g, w, layer, *, tk):
    m, k = x.shape
    n = w.shape[-1]
    emit = w.ndim == 3
    w_spec = (pl.BlockSpec((None, tk, n), lambda j: (layer, j, 0)) if emit
              else pl.BlockSpec((tk, n), lambda j: (j, 0)))
    vmem = 2 * tk * n * (4 + 2) + 2 * m * (k + n) * 4 + m * k * 2
    outs = pl.pallas_call(
        functools.partial(_norm_matmul_kernel, emit=emit),
        grid=(k // tk,),
        in_specs=[
            pl.BlockSpec((m, k), lambda j: (0, 0)),
            pl.BlockSpec((None, 1, k), lambda j: (layer, 0, 0)),
            w_spec,
        ],
        out_specs=[pl.BlockSpec((m, n), lambda j: (0, 0))] + [pl.BlockSpec((tk, n), lambda j: (j, 0))] * emit,
        out_shape=[jax.ShapeDtypeStruct((m, n), F32)] + [jax.ShapeDtypeStruct((k, n), BF16)] * emit,
        scratch_shapes=[pltpu.VMEM((k // tk, m, tk), BF16)],
        compiler_params=pltpu.CompilerParams(
            dimension_semantics=("arbitrary",),
            vmem_limit_bytes=_vmem_limit(vmem)),
        name="norm_matmul",
    )(x, g, w)
    return tuple(outs) if emit else (outs[0], w)


def _matmul_residual_kernel(a_ref, w_ref, r_ref, o_ref, *rest, emit):
    wb = w_ref[...].astype(BF16)
    if emit:
        rest[0][...] = wb
    o_ref[...] = r_ref[...] + _dot(a_ref[...], wb)


def matmul_residual(a, w, res, layer, *, tn):
    m, k = a.shape
    n = w.shape[-1]
    w_spec, emit = _weight_specs(w, layer, k, tn)
    vmem = 2 * k * tn * (4 + 2) + 2 * m * (k * 2 + 2 * tn * 4)
    outs = pl.pallas_call(
        functools.partial(_matmul_residual_kernel, emit=emit),
        grid=(n // tn,),
        in_specs=[
            pl.BlockSpec((m, k), lambda j: (0, 0)),
            w_spec,
            pl.BlockSpec((m, tn), lambda j: (0, j)),
        ],
        out_specs=[pl.BlockSpec((m, tn), lambda j: (0, j))] + [pl.BlockSpec((k, tn), lambda j: (0, j))] * emit,
        out_shape=[jax.ShapeDtypeStruct((m, n), F32)] + [jax.ShapeDtypeStruct((k, n), BF16)] * emit,
        compiler_params=pltpu.CompilerParams(
            dimension_semantics=("arbitrary",),
            vmem_limit_bytes=_vmem_limit(vmem)),
        name="matmul_residual",
    )(a, w, res)
    return tuple(outs) if emit else (outs[0], w)


_MLP_TILE = 1024


def _mlp_step(first, last, x_refs, g_ref, weights, fg_ref, o_ref, hn_ref, final_norm):
    def delta():
        wg, wu, wd = weights()
        hn = hn_ref[...]
        act = (_silu(_dot(hn, wg)) * _dot(hn, wu)).astype(BF16)
        return _dot(act, wd)

    @pl.when(first)
    def _():
        row = 0
        for x_ref in x_refs:
            rows = slice(row, row + x_ref.shape[0])
            hn_ref[rows, :] = _rms(x_ref[...], g_ref[...]).astype(BF16)
            row += x_ref.shape[0]
        d = delta()
        row = 0
        for x_ref in x_refs:
            rows = slice(row, row + x_ref.shape[0])
            o_ref[rows, :] = x_ref[...] + d[rows, :]
            row += x_ref.shape[0]

    @pl.when(jnp.logical_not(first))
    def _():
        o_ref[...] += delta()

    if final_norm:
        @pl.when(last)
        def _():
            o_ref[...] = _rms(o_ref[...], fg_ref[...])


def _mlp_cast_kernel(xs_ref, xp_ref, g_ref, wg_ref, wu_ref, wd_ref, fg_ref,
                     o_ref, wgb_ref, wub_ref, wdb_ref, hn_ref, *, final_norm):
    f = pl.program_id(0)

    def cast_weights():
        wg, wu, wd = wg_ref[...].astype(BF16), wu_ref[...].astype(BF16), wd_ref[...].astype(BF16)
        wgb_ref[...] = wg
        wub_ref[...] = wu
        wdb_ref[...] = wd
        return wg, wu, wd

    _mlp_step(f == 0, f == pl.num_programs(0) - 1, (xs_ref, xp_ref), g_ref, cast_weights, fg_ref,
              o_ref, hn_ref, final_norm)


def _mlp_rest_kernel(first_hbm, x_ref, g_ref, wg_ref, wu_ref, wd_ref, fg_ref, o_ref, hn_ref, sem,
                     *, nf, skip_rows, final_norm):
    s = pl.program_id(0)
    tm = o_ref.shape[0]

    @pl.when(s == 0)
    def _():
        copy = pltpu.make_async_copy(first_hbm.at[pl.ds(skip_rows, tm), :], o_ref, sem)
        copy.start()
        copy.wait()

    @pl.when(s > 0)
    def _():
        f = (s - 1) % nf
        _mlp_step(f == 0, f == nf - 1, (x_ref,), g_ref, lambda: (wg_ref[...], wu_ref[...], wd_ref[...]),
                  fg_ref, o_ref, hn_ref, final_norm)


def mlp_cast(xs, xp, g, w_gate_up, w_down, final_g, layer, *, tp, tf, final_norm):
    ns, d = xs.shape
    m = ns + tp
    nf = D_FF // tf
    once = pl.Buffered(1)
    vmem = 2 * m * d * 4 + m * d * 2 + 2 * 3 * d * tf * (4 + 2) + 3 * m * tf * 4
    outs = pl.pallas_call(
        functools.partial(_mlp_cast_kernel, final_norm=final_norm),
        grid=(nf,),
        in_specs=[
            pl.BlockSpec((ns, d), lambda f: (0, 0), pipeline_mode=once),
            pl.BlockSpec((tp, d), lambda f: (0, 0), pipeline_mode=once),
            pl.BlockSpec((None, 1, d), lambda f: (layer, 0, 0)),
            pl.BlockSpec((None, d, tf), lambda f: (layer, 0, f)),
            pl.BlockSpec((None, d, tf), lambda f: (layer, 0, nf + f)),
            pl.BlockSpec((None, tf, d), lambda f: (layer, f, 0)),
            pl.BlockSpec((1, d), lambda f: (0, 0)),
        ],
        out_specs=[
            pl.BlockSpec((m, d), lambda f: (0, 0), pipeline_mode=once),
            pl.BlockSpec((d, tf), lambda f: (0, f)),
            pl.BlockSpec((d, tf), lambda f: (0, f)),
            pl.BlockSpec((tf, d), lambda f: (f, 0)),
        ],
        out_shape=[jax.ShapeDtypeStruct((m, d), F32),
                   jax.ShapeDtypeStruct((d, D_FF), BF16), jax.ShapeDtypeStruct((d, D_FF), BF16),
                   jax.ShapeDtypeStruct((D_FF, d), BF16)],
        scratch_shapes=[pltpu.VMEM((m, d), BF16)],
        compiler_params=pltpu.CompilerParams(
            dimension_semantics=("arbitrary",),
            vmem_limit_bytes=_vmem_limit(vmem)),
        name="mlp_cast",
    )(xs, xp, g, w_gate_up, w_gate_up, w_down, final_g)
    return outs[0], tuple(outs[1:])


def mlp_rest(first, x, g, weights, final_g, layer, *, skip_rows, tm, tf, final_norm):
    m, d = x.shape
    nf = D_FF // tf
    step = lambda s: jnp.maximum(s - 1, 0)
    x_tile = lambda s: 1 + step(s) // nf
    f_tile = lambda s: step(s) % nf
    vmem = 2 * 2 * tm * d * 4 + tm * d * 2 + 2 * 3 * d * tf * 2 + 3 * tm * tf * 4
    return pl.pallas_call(
        functools.partial(_mlp_rest_kernel, nf=nf, skip_rows=skip_rows, final_norm=final_norm),
        grid=(1 + (m // tm - 1) * nf,),
        in_specs=[
            pl.BlockSpec(memory_space=pl.ANY),
            pl.BlockSpec((tm, d), lambda s: (x_tile(s), 0)),
            pl.BlockSpec((None, 1, d), lambda s: (layer, 0, 0)),
            pl.BlockSpec((d, tf), lambda s: (0, f_tile(s))),
            pl.BlockSpec((d, tf), lambda s: (0, f_tile(s))),
            pl.BlockSpec((tf, d), lambda s: (f_tile(s), 0)),
            pl.BlockSpec((1, d), lambda s: (0, 0)),
        ],
        out_specs=pl.BlockSpec((tm, d), lambda s: (jnp.where(s == 0, 0, x_tile(s)), 0)),
        out_shape=jax.ShapeDtypeStruct((m, d), F32),
        scratch_shapes=[pltpu.VMEM((tm, d), BF16), pltpu.SemaphoreType.DMA(())],
        compiler_params=pltpu.CompilerParams(
            dimension_semantics=("arbitrary",),
            vmem_limit_bytes=_vmem_limit(vmem)),
        name="mlp_rest",
    )(first, x, g, *weights, final_g)


_POOL_TOP = 16
_CONV_TOP = 8


def _block_prompt_kernel(x_ref, g1_ref, win_ref, wout_ref, cos_ref, sin_ref, dmask_ref, qdec_ref, kdec_ref,
                         poolw_ref, pscale_ref, convw_ref, retg_ref, sgug_ref, sguw_ref, sgub_ref,
                         *rest, start, tm, cast_next, layer, all_layers):
    *rest, pool_ext, conv_ext = rest
    if cast_next:
        win_next_ref, wout_next_ref, *rest, win_next_out, wout_next_out = rest
    h_out, pool_out, conv_out, ret_out, sgu_out = rest[-5:]
    stacked = (pool_out, conv_out, ret_out, sgu_out)
    if all_layers:
        pool_out, conv_out, ret_out, sgu_out = (ref.at[layer] for ref in stacked)
    c = pl.program_id(1)
    T = CHUNK
    n_chunks = tm // T

    @pl.when(c == 0)
    def _():
        pool_ext[0:_POOL_TOP, :] = jnp.zeros((_POOL_TOP, GROUP), F32)
        conv_ext[0:_CONV_TOP, :] = jnp.zeros((_CONV_TOP, GROUP), F32)
        ret_out[...] = jnp.zeros(ret_out.shape, F32)
        if all_layers:
            for ref in stacked:
                for other in range(DEPTH):
                    if other != layer:
                        ref[other] = jnp.zeros(ref.shape[1:], F32)

    x = x_ref[0]
    xn = _rms(x, g1_ref[...]).astype(BF16)

    def proj(col):
        return _dot(xn, win_ref[:, col:col + GROUP])

    def out_proj(y, col):
        return _dot(y, wout_ref[col:col + GROUP, :])

    head_lanes = [slice(hd * HEAD_DIM, (hd + 1) * HEAD_DIM) for hd in range(HEADS)]
    chunk_rows = [slice(j * T, (j + 1) * T) for j in range(n_chunks)]

    def pool_mixer(a_proj):
        pool_ext[_POOL_TOP:_POOL_TOP + tm, :] = a_proj
        pos1 = lax.broadcasted_iota(jnp.int32, (tm, HEAD_DIM), 0) + (start + 1) + c * tm
        y = []
        for gi, win in enumerate(POOL_WINDOWS):
            lanes = head_lanes[gi]
            a = pool_ext[_POOL_TOP:_POOL_TOP + tm, lanes]
            wsum = a
            for j in range(1, win):
                wsum = wsum + pool_ext[_POOL_TOP - j:_POOL_TOP - j + tm, lanes]
            cnt = jnp.minimum(pos1, win).astype(F32)
            d = (wsum / cnt - a).astype(BF16)
            y.append((_dot(d, poolw_ref[gi]) * pscale_ref[:, lanes]).astype(BF16))
        pool_out[0] = pool_ext[_POOL_TOP + tm - POOL_BUF:_POOL_TOP + tm, :]
        pool_ext[_POOL_TOP - POOL_BUF:_POOL_TOP, :] = pool_ext[_POOL_TOP + tm - POOL_BUF:_POOL_TOP + tm, :]
        return jnp.concatenate(y, axis=-1)

    def conv_mixer(bg, cg, hh):
        z = cg * hh
        conv_ext[_CONV_TOP:_CONV_TOP + tm, :] = z
        acc = (conv_ext[_CONV_TOP - 2:_CONV_TOP - 2 + tm, :] * convw_ref[0:1, :]
               + conv_ext[_CONV_TOP - 1:_CONV_TOP - 1 + tm, :] * convw_ref[1:2, :]
               + z * convw_ref[2:3, :])
        conv_out[0] = conv_ext[_CONV_TOP + tm - 2:_CONV_TOP + tm, :]
        conv_ext[_CONV_TOP - 2:_CONV_TOP, :] = conv_ext[_CONV_TOP + tm - 2:_CONV_TOP + tm, :]
        return (bg * acc).astype(BF16)

    def retention(q_all, k_all, v_all, g_all, j, hd):
        rows, lanes = chunk_rows[j], head_lanes[hd]
        cos = cos_ref[rows, :]
        sin = sin_ref[rows, :]
        q = _rotate(q_all[rows, lanes], cos, sin)
        k = _rotate(k_all[rows, lanes], cos, sin) * (HEAD_DIM ** -0.5)
        v = v_all[rows, lanes].astype(BF16)
        s_prev = ret_out[0, hd]
        scores = lax.dot_general(q.astype(BF16), k.astype(BF16), (((1,), (1,)), ((), ())),
                                 preferred_element_type=F32) * dmask_ref[hd]
        o = _dot(scores.astype(BF16), v) + _dot((q * qdec_ref[hd]).astype(BF16), s_prev.astype(BF16))
        kv = lax.dot_general((k * kdec_ref[hd]).astype(BF16), v, (((0,), (0,)), ((), ())),
                             preferred_element_type=F32)
        ret_out[0, hd] = qdec_ref[hd, T - 1:T, :] * s_prev + kv
        on = _layernorm(o) * retg_ref[:, lanes]
        return (_silu(g_all[rows, lanes]) * on).astype(BF16)

    def sgu_mixer(u_all, vn):
        tri = lax.broadcasted_iota(jnp.int32, (T, T), 0) >= lax.broadcasted_iota(jnp.int32, (T, T), 1)
        y = [[None] * HEADS for _ in range(n_chunks)]
        for hd in range(HEADS):
            lanes = head_lanes[hd]
            w = jnp.where(tri, sguw_ref[hd], 0.0).astype(BF16)
            cols = jnp.concatenate([vn[rows, lanes] for rows in chunk_rows], axis=-1).astype(BF16)
            mixed = _dot(w, cols)
            for j, rows in enumerate(chunk_rows):
                m = mixed[:, j * T:(j + 1) * T] + sgub_ref[hd]
                y[j][hd] = (u_all[rows, lanes] * m).astype(BF16)
        return jnp.concatenate([jnp.concatenate(r, axis=-1) for r in y], axis=0)

    q_all, k_all, v_all, g_all = proj(_Q), proj(_K), proj(_V), proj(_G)
    ret = functools.partial(retention, q_all, k_all, v_all, g_all)
    y_c = [[None] * HEADS for _ in range(n_chunks)]
    y_c[0][0] = ret(0, 0)
    a_proj = proj(_A)
    y_c[0][1] = ret(0, 1)
    cg = proj(_CG)
    y_c[0][2] = ret(0, 2)
    hh = proj(_HH)
    y_c[0][3] = ret(0, 3)
    bg = proj(_BG)
    later = iter([lambda: proj(_VV), lambda: proj(_U), lambda: pool_mixer(a_proj),
                  lambda: conv_mixer(bg, cg, hh)])
    fills = []
    for j in range(1, n_chunks):
        for hd in range(HEADS):
            y_c[j][hd] = ret(j, hd)
            nxt = next(later, None)
            if nxt is not None:
                fills.append(nxt())
    fills += [f() for f in later]
    vv, u_all, y_a, y_b = fills
    h = x + out_proj(y_a, _YA)
    vn = _layernorm(vv) * sgug_ref[...]
    y_d = sgu_mixer(u_all, vn)
    h = h + out_proj(y_b, _YB)
    h = h + out_proj(jnp.concatenate([jnp.concatenate(r, axis=-1) for r in y_c], axis=0), _YC)
    h_out[0] = h + out_proj(y_d, _YD)
    sgu_out[0] = vn[tm - T:tm, :]

    if cast_next:
        win_next_out[...] = win_next_ref[...].astype(BF16)
        wout_next_out[...] = wout_next_ref[...].astype(BF16)


_N_BLOCK_INPUTS = 16


def block_prompt(x, norm_g, w_in, w_out, tables, params, layer, next_weights, stacked, *, start, tm):
    b, l, d = x.shape
    nc = l // tm
    cos, sin, dmask, qdec, kdec = tables
    poolw, pscale, convw, retg, sgug, sguw, sgub = params
    const3 = lambda shape: pl.BlockSpec(shape, lambda i, c: (0, 0, 0))
    lay3 = lambda shape: pl.BlockSpec((None,) + shape, lambda i, c: (layer, 0, 0))
    lay4 = lambda shape: pl.BlockSpec((None,) + shape, lambda i, c: (layer, 0, 0, 0))
    resident = lambda shape: pl.BlockSpec(shape, lambda i, c: (0, 0), pipeline_mode=pl.Buffered(1))
    cast_next = next_weights is not None
    next_in_specs, next_out_specs, next_out_shape, next_args = [], [], [], ()
    next_bytes = 0
    if cast_next:
        for w in next_weights:
            rows, cols = w.shape[1] // (b * nc), w.shape[2]
            assert rows * b * nc == w.shape[1] and rows % 16 == 0
            next_in_specs.append(pl.BlockSpec((None, rows, cols), lambda i, c: (layer + 1, i * nc + c, 0)))
            next_out_specs.append(pl.BlockSpec((rows, cols), lambda i, c: (i * nc + c, 0)))
            next_out_shape.append(jax.ShapeDtypeStruct(w.shape[1:], BF16))
            next_bytes += 2 * rows * cols * (4 + 2)
        next_args = tuple(next_weights)
    state_dims = [(POOL_BUF, GROUP), (2, GROUP), (HEADS, HEAD_DIM, HEAD_DIM), (CHUNK, GROUP)]
    first = stacked is None
    if first:
        state_specs = [pl.BlockSpec((DEPTH, 1) + dims, lambda i, c, n=len(dims): (0, i) + (0,) * n)
                       for dims in state_dims]
        alias_specs, alias_args, aliases = [], (), {}
    else:
        state_specs = [pl.BlockSpec((None, 1) + dims, lambda i, c, n=len(dims): (layer, i) + (0,) * n)
                       for dims in state_dims]
        alias_specs = [pl.BlockSpec(memory_space=pl.ANY)] * len(stacked)
        alias_args = tuple(stacked)
        n_in = _N_BLOCK_INPUTS + len(next_in_specs)
        aliases = {n_in + k: 1 + k for k in range(len(stacked))}
    vmem = (d * IN_WIDTH + 4 * GROUP * d) * 2 + 2 * 2 * tm * d * 4 + 2 * tm * IN_WIDTH * 4 + next_bytes
    outs = pl.pallas_call(
        functools.partial(_block_prompt_kernel, start=start, tm=tm, cast_next=cast_next, layer=layer,
                          all_layers=first),
        grid=(b, nc),
        in_specs=[
            pl.BlockSpec((1, tm, d), lambda i, c: (i, c, 0)),
            lay3((1, d)),
            resident((d, IN_WIDTH)),
            resident((4 * GROUP, d)),
            pl.BlockSpec((tm, HEAD_DIM), lambda i, c: (c, 0)),
            pl.BlockSpec((tm, HEAD_DIM), lambda i, c: (c, 0)),
            const3((HEADS, CHUNK, CHUNK)),
            const3((HEADS, CHUNK, HEAD_DIM)),
            const3((HEADS, CHUNK, HEAD_DIM)),
            lay4((len(POOL_WINDOWS), HEAD_DIM, HEAD_DIM)),
            lay3((1, GROUP)),
            lay3((3, GROUP)),
            lay3((1, GROUP)),
            lay3((1, GROUP)),
            lay4((HEADS, CHUNK, CHUNK)),
            lay4((HEADS, CHUNK, HEAD_DIM)),
            *next_in_specs,
            *alias_specs,
        ],
        out_specs=[pl.BlockSpec((1, tm, d), lambda i, c: (i, c, 0)), *state_specs, *next_out_specs],
        out_shape=[
            jax.ShapeDtypeStruct((b, l, d), F32),
            *[jax.ShapeDtypeStruct((DEPTH, b) + dims, F32) for dims in state_dims],
            *next_out_shape,
        ],
        scratch_shapes=[
            pltpu.VMEM((_POOL_TOP + tm, GROUP), F32),
            pltpu.VMEM((_CONV_TOP + tm, GROUP), F32),
        ],
        input_output_aliases=aliases,
        compiler_params=pltpu.CompilerParams(
            dimension_semantics=("arbitrary", "arbitrary"),
            vmem_limit_bytes=_vmem_limit(vmem)),
        name="block_prompt",
    )(x, norm_g, w_in, w_out, cos, sin, dmask, qdec, kdec, poolw, pscale, convw, retg, sgug, sguw, sgub,
      *next_args, *alias_args)
    return outs[0], tuple(outs[1:5]), (tuple(outs[5:]) if cast_next else None)


_SEQ_BLOCK = 16


_N_SAMPLE_INPUTS = 14


def _mixer_sample_kernel(*refs, start, layer, all_layers):
    (p_ref, pool_ref, conv_ref, ret_ref, cos_ref, sin_ref, gam_ref,
     poolw_ref, pscale_ref, convw_ref, retg_ref, sgug_ref, sguw_ref, sgub_ref) = refs[:_N_SAMPLE_INPUTS]
    mix_ref, pool_out, conv_out, ret_out, sgu_out, qg_s, k_s, v_s, os_s = refs[-9:]
    TB = _SEQ_BLOCK

    if all_layers:
        stacked = (pool_out, conv_out, ret_out, sgu_out)
        for ref in stacked:
            for other in range(DEPTH):
                if other != layer:
                    ref[other] = jnp.zeros(ref.shape[1:], F32)
        pool_out, conv_out, ret_out, sgu_out = (ref.at[layer] for ref in stacked)

    a_all = p_ref[:, _A:_A + GROUP]
    for gi, win in enumerate(POOL_WINDOWS):
        lanes = slice(gi * HEAD_DIM, (gi + 1) * HEAD_DIM)
        a = a_all[:, lanes]
        wsum = a
        for j in range(1, win):
            wsum = wsum + pool_ref[POOL_BUF - j, :, lanes]
        cnt = float(min(start + 1, win))
        d = (wsum / cnt - a).astype(BF16)
        y = _dot(d, poolw_ref[gi]) * pscale_ref[:, lanes]
        mix_ref[:, _YA + gi * HEAD_DIM:_YA + (gi + 1) * HEAD_DIM] = y.astype(BF16)
    pool_out[0:POOL_BUF - 1] = pool_ref[1:POOL_BUF]
    pool_out[POOL_BUF - 1] = a_all

    z = p_ref[:, _CG:_CG + GROUP] * p_ref[:, _HH:_HH + GROUP]
    acc = (conv_ref[:, 0, :] * convw_ref[0:1, :] + conv_ref[:, 1, :] * convw_ref[1:2, :]
           + z * convw_ref[2:3, :])
    mix_ref[:, _YB:_YB + GROUP] = (p_ref[:, _BG:_BG + GROUP] * acc).astype(BF16)
    conv_out[:, 0, :] = conv_ref[:, 1, :]
    conv_out[:, 1, :] = z

    cos = cos_ref[...]
    sin = sin_ref[...]
    scores = []
    for h in range(HEADS):
        lanes = slice(h * HEAD_DIM, (h + 1) * HEAD_DIM)
        q = _rotate(p_ref[:, _Q + h * HEAD_DIM:_Q + (h + 1) * HEAD_DIM], cos, sin)
        k = _rotate(p_ref[:, _K + h * HEAD_DIM:_K + (h + 1) * HEAD_DIM], cos, sin) * (HEAD_DIM ** -0.5)
        scores.append(jnp.sum(q * k, axis=-1, keepdims=True))
        qg_s[:, lanes] = q * gam_ref[:, lanes]
        k_s[:, lanes] = k
    v_s[...] = p_ref[:, _V:_V + GROUP]

    first_row = lax.broadcasted_iota(jnp.int32, (8, HEAD_DIM), 0) == 0

    for b in range(TB):
        for h in range(HEADS):
            lanes = slice(h * HEAD_DIM, (h + 1) * HEAD_DIM)
            s_prev = ret_ref[b, h]
            q8 = jnp.broadcast_to(qg_s[b:b + 1, lanes], (8, HEAD_DIM))
            os_s[b:b + 1, lanes] = _dot(q8.astype(BF16), s_prev.astype(BF16))[0:1, :]
            k8 = jnp.where(first_row, jnp.broadcast_to(k_s[b:b + 1, lanes], (8, HEAD_DIM)), 0.0)
            v8 = jnp.broadcast_to(v_s[b:b + 1, lanes], (8, HEAD_DIM))
            kv = lax.dot_general(k8.astype(BF16), v8.astype(BF16), (((0,), (0,)), ((), ())),
                                 preferred_element_type=F32)
            ret_out[b, h] = gam_ref[:, lanes] * s_prev + kv

    for h in range(HEADS):
        lanes = slice(h * HEAD_DIM, (h + 1) * HEAD_DIM)
        o = scores[h] * v_s[:, lanes] + os_s[:, lanes]
        on = _layernorm(o) * retg_ref[:, lanes]
        g = p_ref[:, _G + h * HEAD_DIM:_G + (h + 1) * HEAD_DIM]
        mix_ref[:, _YC + h * HEAD_DIM:_YC + (h + 1) * HEAD_DIM] = (_silu(g) * on).astype(BF16)

    vn = _layernorm(p_ref[:, _VV:_VV + GROUP]) * sgug_ref[...]
    mix_ref[:, _YD:_YD + GROUP] = (p_ref[:, _U:_U + GROUP] * (sguw_ref[...] * vn + sgub_ref[...])).astype(BF16)
    sgu_out[:, 0, :] = vn


def mixer_sample(p, pool_state, conv_state, ret_state, tables, params, layer, stacked, *, start):
    n = p.shape[0]
    tb = _SEQ_BLOCK
    cos, sin, gam = tables
    poolw, pscale, convw, retg, sgug, sguw, sgub = params
    rows = lambda width: pl.BlockSpec((tb, width), lambda i: (i, 0))
    lay_rows = lambda r: pl.BlockSpec((None, tb, r, GROUP), lambda i: (layer, i, 0, 0))
    lay_pool = pl.BlockSpec((None, POOL_BUF, tb, GROUP), lambda i: (layer, 0, i, 0))
    lay_ret = pl.BlockSpec((None, tb, HEADS, HEAD_DIM, HEAD_DIM), lambda i: (layer, i, 0, 0, 0))
    const2 = lambda shape: pl.BlockSpec(shape, lambda i: (0, 0))
    lay3 = lambda shape: pl.BlockSpec((None,) + shape, lambda i: (layer, 0, 0))
    lay4 = lambda shape: pl.BlockSpec((None,) + shape, lambda i: (layer, 0, 0, 0))
    in_specs = [
        rows(IN_WIDTH),
        lay_pool,
        lay_rows(2),
        lay_ret,
        const2((1, HEAD_DIM)),
        const2((1, HEAD_DIM)),
        const2((1, GROUP)),
        lay4((len(POOL_WINDOWS), HEAD_DIM, HEAD_DIM)),
        lay3((1, GROUP)),
        lay3((3, GROUP)),
        lay3((1, GROUP)),
        lay3((1, GROUP)),
        lay3((1, GROUP)),
        lay3((1, GROUP)),
    ]
    args = (p, pool_state, conv_state, ret_state, cos, sin, gam,
            poolw, pscale, convw, retg, sgug, sguw, sgub)
    assert len(args) == _N_SAMPLE_INPUTS
    aliases = {}
    first = stacked is None
    state_block = tb * (16 + 8 + 8 + HEAD_DIM) * GROUP * 4
    vmem = 2 * (tb * IN_WIDTH * 4 + state_block * (1 + (DEPTH if first else 1)))
    if first:
        all_rows = lambda r: pl.BlockSpec((DEPTH, tb, r, GROUP), lambda i: (0, i, 0, 0))
        state_specs = [pl.BlockSpec((DEPTH, POOL_BUF, tb, GROUP), lambda i: (0, 0, i, 0)), all_rows(2),
                       pl.BlockSpec((DEPTH, tb, HEADS, HEAD_DIM, HEAD_DIM), lambda i: (0, i, 0, 0, 0)),
                       all_rows(1)]
    else:
        state_specs = [lay_pool, lay_rows(2), lay_ret, lay_rows(1)]
        aliases = {len(args) + k: 1 + k for k in range(len(stacked))}
        in_specs = in_specs + [pl.BlockSpec(memory_space=pl.ANY)] * len(stacked)
        args = args + tuple(stacked)
    outs = pl.pallas_call(
        functools.partial(_mixer_sample_kernel, start=start, layer=layer, all_layers=first),
        grid=(n // tb,),
        in_specs=in_specs,
        out_specs=[rows(4 * GROUP)] + state_specs,
        out_shape=[
            jax.ShapeDtypeStruct((n, 4 * GROUP), BF16),
            jax.ShapeDtypeStruct((DEPTH, POOL_BUF, n, GROUP), F32),
            jax.ShapeDtypeStruct((DEPTH, n, 2, GROUP), F32),
            jax.ShapeDtypeStruct((DEPTH, n, HEADS, HEAD_DIM, HEAD_DIM), F32),
            jax.ShapeDtypeStruct((DEPTH, n, 1, GROUP), F32),
        ],
        scratch_shapes=[pltpu.VMEM((tb, GROUP), F32)] * 4,
        input_output_aliases=aliases,
        compiler_params=pltpu.CompilerParams(
            dimension_semantics=("arbitrary",),
            vmem_limit_bytes=_vmem_limit(vmem)),
        name="mixer_sample",
    )(*args)
    return outs[0], tuple(outs[1:])


def _rope_tables(start, length):
    half = HEAD_DIM // 2
    inv = ROPE_BASE ** (-jnp.arange(half, dtype=F32) / half)
    pos = start + jnp.arange(length)
    ang = pos.astype(F32)[:, None] * inv[None, :]
    cos, sin = jnp.cos(ang), jnp.sin(ang)
    return jnp.concatenate([cos, cos], axis=-1), jnp.concatenate([-sin, sin], axis=-1)


def _log_gamma():
    return jnp.log1p(-(2.0 ** (-5.0 - jnp.arange(HEADS, dtype=F32))))


def _decay_tables(c):
    lg = _log_gamma()
    idx = jnp.arange(c, dtype=F32)
    diff = idx[:, None] - idx[None, :]
    dmask = jnp.where(diff >= 0, jnp.exp(jnp.maximum(diff, 0.0)[None] * lg[:, None, None]), 0.0)
    qdec = jnp.exp((idx + 1.0)[None, :] * lg[:, None])[:, :, None]
    kdec = jnp.exp((c - 1.0 - idx)[None, :] * lg[:, None])[:, :, None]
    wide = lambda t: jnp.broadcast_to(t, (HEADS, c, HEAD_DIM))
    return dmask, wide(qdec), wide(kdec)


def _per_head_lanes(t):
    return jnp.repeat(t, HEAD_DIM, axis=-1)[..., None, :]


def kernel(x_prompt, x_sample, state_pool, state_conv, state_ret, norm1_g, w_in, pool_w, pool_scale,
           conv_w, ret_norm_g, sgu_norm_g, sgu_w, sgu_b, w_out, norm2_g, w_gate_up, w_down,
           final_norm_g):
    bp, lp, _ = x_prompt.shape
    ns = x_sample.shape[0]

    pool_w_b = pool_w.astype(BF16)
    row3 = lambda t: t[:, None, :]
    norm1 = row3(norm1_g)
    norm2 = row3(norm2_g)
    final_g = final_norm_g[None, :]

    cos_p, sin_p = _rope_tables(0, lp)
    prompt_tables = (cos_p, sin_p) + _decay_tables(CHUNK)
    sgu_b_wide = jnp.broadcast_to(sgu_b[:, :, :, None], sgu_b.shape + (HEAD_DIM,))
    prompt_params = (pool_w_b, row3(pool_scale), conv_w, row3(ret_norm_g), row3(sgu_norm_g), sgu_w, sgu_b_wide)

    cos_s, sin_s = _rope_tables(PAST_LEN, 1)
    gam = _per_head_lanes(jnp.exp(_log_gamma()))
    sample_tables = (cos_s, sin_s, gam)
    sample_params = (pool_w_b, row3(pool_scale), conv_w, row3(ret_norm_g), row3(sgu_norm_g),
                     _per_head_lanes(sgu_w[:, :, 0, 0]), _per_head_lanes(sgu_b[:, :, 0]))
    pool_rows = jnp.transpose(state_pool, (0, 2, 1, 3))
    hp = x_prompt
    hs = x_sample.reshape(ns, D_MODEL)
    prompt_states = None
    sample_states = None
    next_b = None
    for l in range(DEPTH):
        fin = l == DEPTH - 1
        w_in_l, w_out_l = (w_in, w_out) if next_b is None else next_b
        p, w_in_b = norm_matmul(hs, norm1, w_in_l, l, tk=512)
        mix, sample_states = mixer_sample(p, pool_rows, state_conv, state_ret, sample_tables,
                                          sample_params, l, sample_states, start=PAST_LEN)
        hs, w_out_b = matmul_residual(mix, w_out_l, hs, l, tn=1024)
        hp, prompt_states, next_b = block_prompt(hp, norm1, w_in_b, w_out_b, prompt_tables, prompt_params,
                                                 l, None if fin else (w_in, w_out), prompt_states,
                                                 start=0, tm=256)
        hp = hp.reshape(bp * lp, D_MODEL)
        first, w_mlp_b = mlp_cast(hs, hp, norm2, w_gate_up, w_down, final_g, l, tp=_MLP_TILE, tf=256,
                                  final_norm=fin)
        hs = first[:ns]
        hp = mlp_rest(first, hp, norm2, w_mlp_b, final_g, l, skip_rows=ns, tm=_MLP_TILE, tf=512,
                      final_norm=fin).reshape(bp, lp, D_MODEL)
    pool_p, conv_p, ret_p, sgu_p = prompt_states
    pool_s, conv_s, ret_s, sgu_s = sample_states
    pool_s = jnp.transpose(pool_s, (0, 2, 1, 3))
    return (hp, hs.reshape(ns, 1, D_MODEL), pool_p, pool_s, conv_p, conv_s, ret_p, ret_s, sgu_p, sgu_s)
```
